```python
import jax, jax.numpy as jnp
from jax import lax
import numpy as np

D_MODEL = 1024
BATCH = 4
SEQ = 4096
DEPTH = 1

CTX_LEN = 256
GRID_W = 64
MIX_WIDTH = D_MODEL
MLSTM_HEADS = 4
MLSTM_WIDTH = MIX_WIDTH // 2
MLSTM_HEAD_DIM = MLSTM_WIDTH // MLSTM_HEADS
RET_HEADS = 4
RET_WIDTH = MIX_WIDTH - MLSTM_WIDTH
RET_HEAD_DIM = RET_WIDTH // RET_HEADS
N_GATE_COLS = 4 * MLSTM_HEADS
SPLIT_SIZES = (MLSTM_WIDTH, MLSTM_WIDTH, MLSTM_WIDTH, MLSTM_WIDTH, N_GATE_COLS,
               RET_WIDTH, RET_WIDTH, RET_WIDTH, RET_WIDTH)
IN_COLS = sum(SPLIT_SIZES)
CHUNK = 128
N_EXPERTS = 16
EC_CAPACITY_FACTOR = 2
EXPERT_FF = 2816
ROPE_BASE = 10000.0
EPS = 1e-6

kernel_name = "hybrid_mlstm_retention_ecmoe_dit"


def rmsnorm(x, w):
    xf = x.astype(jnp.float32)
    y = xf * lax.rsqrt(jnp.mean(xf * xf, axis=-1, keepdims=True) + EPS)
    return (y * w.astype(jnp.float32)).astype(x.dtype)


def modulate(x, w, shift, scale):
    return rmsnorm(x, w) * (1 + scale) + shift


def heads(t, n_heads):
    B, T, _ = t.shape
    return t.reshape(B, T, n_heads, -1).transpose(0, 2, 1, 3)


def head_rmsnorm(h, w):
    y = h * lax.rsqrt(jnp.mean(h * h, axis=-1, keepdims=True) + EPS)
    B, H, T, dh = y.shape
    return y.transpose(0, 2, 1, 3).reshape(B, T, H * dh) * w.astype(jnp.float32)


def to_chunks(t):
    B, H, T = t.shape[:3]
    t = t.reshape((B, H, T // CHUNK, CHUNK) + t.shape[3:])
    return jnp.moveaxis(t, 2, 0)


def from_chunks(h):
    N, B, H, L, d = h.shape
    return jnp.moveaxis(h, 0, 2).reshape(B, H, N * L, d)


def flip_if(t, rev):
    return jnp.flip(t, axis=2) if rev else t


def mlstm_scan(q, k, v, ig, lf, state, with_out):
    tril = jnp.tril(jnp.ones((CHUNK, CHUNK), dtype=bool))

    def step(carry, inp):
        C, n, m = carry
        qc, kc, vc, ic, fc = inp
        b = jnp.cumsum(fc, axis=-1)
        bL = b[..., -1]
        g = bL[..., None] - b + ic
        m_new = jnp.maximum(bL + m, jnp.max(g, axis=-1))
        w = jnp.exp(g - m_new[..., None])
        decay = jnp.exp(bL + m - m_new)
        C_new = decay[..., None, None] * C + jnp.einsum('bhs,bhsv,bhsk->bhvk', w, vc, kc)
        n_new = decay[..., None] * n + jnp.einsum('bhs,bhsk->bhk', w, kc)
        if with_out:
            d_log = jnp.where(tril, b[..., :, None] - b[..., None, :] + ic[..., None, :], -jnp.inf)
            inter = b + m[..., None]
            m_t = jnp.maximum(inter, jnp.max(d_log, axis=-1))
            d_w = jnp.exp(d_log - m_t[..., None])
            i_w = jnp.exp(inter - m_t)
            s = jnp.einsum('bhtk,bhsk->bhts', qc, kc) * d_w
            num = jnp.einsum('bhts,bhsv->bhtv', s, vc) + i_w[..., None] * jnp.einsum('bhvk,bhtk->bhtv', C, qc)
            den = jnp.sum(s, axis=-1) + i_w * jnp.einsum('bhk,bhtk->bht', n, qc)
            h = num / jnp.maximum(jnp.abs(den), jnp.exp(-m_t))[..., None]
        else:
            h = None
        return (C_new, n_new, m_new), h

    xs = (to_chunks(q), to_chunks(k), to_chunks(v), to_chunks(ig), to_chunks(lf))
    carry, hs = lax.scan(step, state, xs)
    return (from_chunks(hs) if with_out else None), carry


def retention_scan(q, k, v, lg, state, with_out):
    pos = jnp.arange(CHUNK, dtype=jnp.float32)
    rel = pos[:, None] - pos[None, :]
    d_mat = jnp.where(rel >= 0, jnp.exp(jnp.maximum(rel, 0.0)[None] * lg[:, None, None]), 0.0)
    inter = jnp.exp((pos + 1.0)[None] * lg[:, None])
    k_w = jnp.exp((CHUNK - 1.0 - pos)[None] * lg[:, None])
    d_chunk = jnp.exp(CHUNK * lg)

    def step(R, inp):
        qc, kc, vc = inp
        R_new = d_chunk[None, :, None, None] * R + jnp.einsum('hs,bhsv,bhsk->bhvk', k_w, vc, kc)
        if with_out:
            s = jnp.einsum('bhtk,bhsk->bhts', qc, kc) * d_mat
            out = jnp.einsum('bhts,bhsv->bhtv', s, vc) + inter[None, :, :, None] * jnp.einsum('bhvk,bhtk->bhtv', R, qc)
        else:
            out = None
        return R_new, out

    R_fin, outs = lax.scan(step, state, (to_chunks(q), to_chunks(k), to_chunks(v)))
    return (from_chunks(outs) if with_out else None), R_fin


def rotate_pairs(t, cos, sin):
    half = t.shape[-1] // 2
    t1, t2 = t[..., :half], t[..., half:]
    return jnp.concatenate([t1 * cos - t2 * sin, t1 * sin + t2 * cos], axis=-1)


def axial_rope(t, rope_cs):
    cr, sr, cc, sc = rope_cs
    half = t.shape[-1] // 2
    return jnp.concatenate([rotate_pairs(t[..., :half], cr, sr),
                            rotate_pairs(t[..., half:], cc, sc)], axis=-1)


def split_proj(p):
    idx = np.cumsum(SPLIT_SIZES)[:-1].tolist()
    return jnp.split(p, idx, axis=-1)


def mlstm_inputs(parts, gate_bias):
    mq, mk, mv, _, mg = parts[:5]
    B, T, _ = mq.shape
    q = heads(mq, MLSTM_HEADS).astype(jnp.float32)
    k = heads(mk, MLSTM_HEADS).astype(jnp.float32) * (MLSTM_HEAD_DIM ** -0.5)
    v = heads(mv, MLSTM_HEADS).astype(jnp.float32)
    g = (mg + gate_bias).astype(jnp.float32).reshape(B, T, 4, MLSTM_HEADS).transpose(2, 0, 3, 1)
    return q, k, v, g[:2], jax.nn.log_sigmoid(g[2:])


def retention_inputs(parts, rope_cs):
    rq, rk, rv = parts[5:8]
    q = heads(rq, RET_HEADS).astype(jnp.float32)
    k = heads(rk, RET_HEADS).astype(jnp.float32) * (RET_HEAD_DIM ** -0.5)
    if rope_cs is not None:
        q, k = axial_rope(q, rope_cs), axial_rope(k, rope_cs)
    return q, k, heads(rv, RET_HEADS).astype(jnp.float32)


def token_mix(hc, hl, w_in, gate_bias, decay_logit, mlstm_norm_w, ret_norm_w, w_out, rope_cs, need_ctx_out):
    B = hl.shape[0]
    pc, pl = split_proj(hc @ w_in), split_proj(hl @ w_in)

    cq, ck, cv, cig, clf = mlstm_inputs(pc, gate_bias)
    lq, lk, lv, lig, llf = mlstm_inputs(pl, gate_bias)
    m_lat, m_ctx = 0.0, 0.0
    for d in range(2):
        rev = d == 1
        st0 = (jnp.zeros((B, MLSTM_HEADS, MLSTM_HEAD_DIM, MLSTM_HEAD_DIM), jnp.float32),
               jnp.zeros((B, MLSTM_HEADS, MLSTM_HEAD_DIM), jnp.float32),
               jnp.zeros((B, MLSTM_HEADS), jnp.float32))
        h_c, st = mlstm_scan(flip_if(cq, rev), flip_if(ck, rev), flip_if(cv, rev),
                             flip_if(cig[d], rev), flip_if(clf[d], rev), st0, need_ctx_out)
        h_l, _ = mlstm_scan(flip_if(lq, rev), flip_if(lk, rev), flip_if(lv, rev),
                            flip_if(lig[d], rev), flip_if(llf[d], rev), st, True)
        m_lat = m_lat + flip_if(h_l, rev)
        if need_ctx_out:
            m_ctx = m_ctx + flip_if(h_c, rev)

    lg = jax.nn.log_sigmoid(decay_logit.astype(jnp.float32))
    rcq, rck, rcv = retention_inputs(pc, None)
    rlq, rlk, rlv = retention_inputs(pl, rope_cs)
    r_lat, r_ctx = 0.0, 0.0
    for d in range(2):
        rev = d == 1
        R0 = jnp.zeros((B, RET_HEADS, RET_HEAD_DIM, RET_HEAD_DIM), jnp.float32)
        o_c, R = retention_scan(flip_if(rcq, rev), flip_if(rck, rev), flip_if(rcv, rev), lg[d], R0, need_ctx_out)
        o_l, _ = retention_scan(flip_if(rlq, rev), flip_if(rlk, rev), flip_if(rlv, rev), lg[d], R, True)
        r_lat = r_lat + flip_if(o_l, rev)
        if need_ctx_out:
            r_ctx = r_ctx + flip_if(o_c, rev)

    def merge(parts, m_sum, r_sum, dtype):
        y_m = jax.nn.sigmoid(parts[3].astype(jnp.float32)) * head_rmsnorm(m_sum, mlstm_norm_w)
        y_r = jax.nn.silu(parts[8].astype(jnp.float32)) * head_rmsnorm(r_sum, ret_norm_w)
        return jnp.concatenate([y_m, y_r], axis=-1).astype(dtype) @ w_out

    y_lat = merge(pl, m_lat, r_lat, hl.dtype)
    y_ctx = merge(pc, m_ctx, r_ctx, hc.dtype) if need_ctx_out else None
    return y_ctx, y_lat


def expert_choice_ffn(h, w_router, w_gate, w_up, w_down):
    B, T, D = h.shape
    cap = EC_CAPACITY_FACTOR * T // N_EXPERTS
    aff = jax.nn.softmax((h @ w_router).astype(jnp.float32), axis=-1)
    gate, idx = lax.top_k(aff.transpose(0, 2, 1), cap)
    xg = jax.vmap(lambda hb, ib: hb[ib])(h, idx)
    a = jnp.einsum('becd,edf->becf', xg, w_gate)
    u = jnp.einsum('becd,edf->becf', xg, w_up)
    y = jnp.einsum('becf,efd->becd', jax.nn.silu(a) * u, w_down) * gate[..., None].astype(h.dtype)
    return jax.vmap(lambda yb, ib: jnp.zeros((T, D), yb.dtype).at[ib.reshape(-1)].add(yb.reshape(-1, D)))(y, idx)


def setup_inputs(seed: int = 0) -> dict:
    key = jax.random.key(seed)
    ks = jax.random.split(key, 22)
    f32 = jnp.float32

    def nrm(k, shape, s):
        return jax.random.normal(k, shape, f32) * s

    x = nrm(ks[0], (BATCH, SEQ, D_MODEL), 1.0)
    c = nrm(ks[1], (BATCH, D_MODEL), 1.0)
    ctx = nrm(ks[2], (BATCH, CTX_LEN, D_MODEL), 1.0)
    c_ctx = nrm(ks[3], (D_MODEL,), 1.0)
    w_ada = nrm(ks[4], (DEPTH, D_MODEL, 6 * D_MODEL), 0.5 * D_MODEL ** -0.5)
    b_ada = nrm(ks[5], (DEPTH, 6 * D_MODEL), 0.02)
    norm1_w = 1.0 + nrm(ks[6], (DEPTH, D_MODEL), 0.02)
    norm2_w = 1.0 + nrm(ks[7], (DEPTH, D_MODEL), 0.02)
    w_in = nrm(ks[8], (DEPTH, D_MODEL, IN_COLS), D_MODEL ** -0.5)
    i_bias = nrm(ks[9], (DEPTH, 2 * MLSTM_HEADS), 0.1)
    f_bias = jnp.tile(jnp.linspace(3.0, 6.0, MLSTM_HEADS, dtype=f32), 2)[None] + nrm(ks[10], (DEPTH, 2 * MLSTM_HEADS), 0.1)
    mlstm_gate_bias = jnp.concatenate([i_bias, f_bias], axis=-1)
    base_logit = jnp.log(2.0 ** jnp.arange(5, 5 + RET_HEADS, dtype=f32) - 1.0)
    ret_decay_logit = base_logit[None, None] + nrm(ks[11], (DEPTH, 2, RET_HEADS), 0.05)
    mlstm_norm_w = 1.0 + nrm(ks[12], (DEPTH, MLSTM_WIDTH), 0.02)
    ret_norm_w = 1.0 + nrm(ks[13], (DEPTH, RET_WIDTH), 0.02)
    w_out = nrm(ks[14], (DEPTH, MIX_WIDTH, D_MODEL), MIX_WIDTH ** -0.5)
    w_router = nrm(ks[15], (DEPTH, D_MODEL, N_EXPERTS), D_MODEL ** -0.5)
    w_gate = nrm(ks[16], (DEPTH, N_EXPERTS, D_MODEL, EXPERT_FF), D_MODEL ** -0.5)
    w_up = nrm(ks[17], (DEPTH, N_EXPERTS, D_MODEL, EXPERT_FF), D_MODEL ** -0.5)
    w_down = nrm(ks[18], (DEPTH, N_EXPERTS, EXPERT_FF, D_MODEL), EXPERT_FF ** -0.5)
    final_norm_w = 1.0 + nrm(ks[19], (D_MODEL,), 0.02)
    return {"x": x, "c": c, "ctx": ctx, "c_ctx": c_ctx, "w_ada": w_ada, "b_ada": b_ada,
            "norm1_w": norm1_w, "norm2_w": norm2_w, "w_in": w_in, "mlstm_gate_bias": mlstm_gate_bias,
            "ret_decay_logit": ret_decay_logit, "mlstm_norm_w": mlstm_norm_w, "ret_norm_w": ret_norm_w,
            "w_out": w_out, "w_router": w_router, "w_gate": w_gate, "w_up": w_up, "w_down": w_down,
            "final_norm_w": final_norm_w}


def reference(x, c, ctx, c_ctx, w_ada, b_ada, norm1_w, norm2_w, w_in, mlstm_gate_bias,
              ret_decay_logit, mlstm_norm_w, ret_norm_w, w_out, w_router, w_gate, w_up, w_down,
              final_norm_w):
    T = x.shape[1]
    ROWS = T // GRID_W
    rows = jnp.repeat(jnp.arange(ROWS, dtype=jnp.float32), GRID_W)
    cols = jnp.tile(jnp.arange(GRID_W, dtype=jnp.float32), ROWS)
    n_freq = RET_HEAD_DIM // 4
    inv_freq = ROPE_BASE ** (-jnp.arange(n_freq, dtype=jnp.float32) / n_freq)
    ang_r, ang_c = rows[:, None] * inv_freq, cols[:, None] * inv_freq
    rope_cs = (jnp.cos(ang_r), jnp.sin(ang_r), jnp.cos(ang_c), jnp.sin(ang_c))

    xl, xc = x, ctx
    for l in range(DEPTH):
        last = l == DEPTH - 1
        mod_l = jax.nn.silu(c) @ w_ada[l] + b_ada[l]
        mod_c = jax.nn.silu(c_ctx) @ w_ada[l] + b_ada[l]
        sh1, sc1, g1, sh2, sc2, g2 = jnp.split(mod_l[:, None, :], 6, axis=-1)
        csh1, csc1, cg1, csh2, csc2, cg2 = jnp.split(mod_c, 6, axis=-1)

        hl = modulate(xl, norm1_w[l], sh1, sc1)
        hc = modulate(xc, norm1_w[l], csh1, csc1)
        y_ctx, y_lat = token_mix(hc, hl, w_in[l], mlstm_gate_bias[l], ret_decay_logit[l],
                                 mlstm_norm_w[l], ret_norm_w[l], w_out[l], rope_cs, not last)
        xl = xl + g1 * y_lat
        xl = xl + g2 * expert_choice_ffn(modulate(xl, norm2_w[l], sh2, sc2),
                                         w_router[l], w_gate[l], w_up[l], w_down[l])
        if not last:
            xc = xc + cg1 * y_ctx
            xc = xc + cg2 * expert_choice_ffn(modulate(xc, norm2_w[l], csh2, csc2),
                                              w_router[l], w_gate[l], w_up[l], w_down[l])
    return rmsnorm(xl, final_norm_w)
```

```python
import functools

import jax
import jax.numpy as jnp
from jax import lax
from jax.experimental import pallas as pl
from jax.experimental.pallas import tpu as pltpu

F32 = jnp.float32
BF16 = jnp.bfloat16

CHUNK = 128
GRID_W = 64
N_HEADS = 4
HEAD_DIM = 128
ROPE_BASE = 10000.0
EPS = 1e-6
EC_CAPACITY_FACTOR = 2
LANES = 128
VMEM_LIMIT_BYTES = 56 * 1024 * 1024


def _params(*sem):
    return pltpu.CompilerParams(dimension_semantics=sem, vmem_limit_bytes=VMEM_LIMIT_BYTES)


def _dot(a, b):
    return jnp.dot(a, b, preferred_element_type=F32)


def _dot_nt(a, b):
    return lax.dot_general(a, b, (((1,), (1,)), ((), ())), preferred_element_type=F32)


def _dot_tn(a, b):
    return lax.dot_general(a, b, (((0,), (0,)), ((), ())), preferred_element_type=F32)


def _sigmoid(x):
    return 1.0 / (1.0 + jnp.exp(-x))


def _log_sigmoid(x):
    return jnp.minimum(x, 0.0) - jnp.log1p(jnp.exp(-jnp.abs(x)))


def _tile(n, pref):
    t = min(n, pref)
    assert n % t == 0, (n, t)
    return t


def _ada_kernel(c_ref, w_ref, b_ref, o_ref):
    cv = c_ref[...]
    s = (cv * _sigmoid(cv)).astype(BF16)
    o_ref[...] = _dot(s, w_ref[...].astype(BF16)) + b_ref[...]


def _ada(cc, w, b):
    rows, d = cc.shape
    n = w.shape[1]
    tn = _tile(n, 1024)
    return pl.pallas_call(
        _ada_kernel,
        grid=(n // tn,),
        in_specs=[pl.BlockSpec((rows, d), lambda j: (0, 0)),
                  pl.BlockSpec((d, tn), lambda j: (0, j)),
                  pl.BlockSpec((1, tn), lambda j: (0, j))],
        out_specs=pl.BlockSpec((rows, tn), lambda j: (0, j)),
        out_shape=jax.ShapeDtypeStruct((rows, n), F32),
        compiler_params=_params("arbitrary"),
    )(cc, w, b.reshape(1, n))


def _rms_mod(x, nw, sh, sc):
    ms = jnp.mean(x * x, axis=-1, keepdims=True)
    return (x * lax.rsqrt(ms + EPS) * nw) * (1.0 + sc) + sh


def _inproj_kernel(x_ref, nw_ref, sh_ref, sc_ref, wm_ref, wg_ref, wr_ref, gb_ref, cos_ref, sin_ref,
                   pm_ref, g_ref, pr_ref, *, width, k_scale):
    hb = _rms_mod(x_ref[0], nw_ref[...], sh_ref[0], sc_ref[0]).astype(BF16)
    for j in range(4):
        cols = slice(j * width, (j + 1) * width)
        acc = _dot(hb, wm_ref[:, cols])
        if j == 1:
            acc = acc * k_scale
        pm_ref[0, :, cols] = acc.astype(BF16)
    g_ref[0] = _dot(hb, wg_ref[...]) + gb_ref[...]
    cos = cos_ref[...]
    sin = sin_ref[...]
    lane = lax.broadcasted_iota(jnp.int32, cos.shape, 1)
    even = ((lane // 32) % 2) == 0
    for j in range(4):
        cols = slice(j * width, (j + 1) * width)
        acc = _dot(hb, wr_ref[:, cols])
        if j == 1:
            acc = acc * k_scale
        if j < 2:
            swapped = jnp.where(even, pltpu.roll(acc, width - 32, 1), pltpu.roll(acc, 32, 1))
            acc = acc * cos + swapped * sin
        pr_ref[0, :, cols] = acc.astype(BF16)


def _inproj(x, nw, sh, sc, wm, wg, wr, gb, cos, sin):
    b, t, d = x.shape
    width = wm.shape[1] // 4
    tm = _tile(t, 512)
    kern = functools.partial(_inproj_kernel, width=width, k_scale=HEAD_DIM ** -0.5)
    const = lambda i, j: (0, 0)
    return pl.pallas_call(
        kern,
        grid=(b, t // tm),
        in_specs=[pl.BlockSpec((1, tm, d), lambda i, j: (i, j, 0)),
                  pl.BlockSpec((1, d), const),
                  pl.BlockSpec((1, 1, d), lambda i, j: (i, 0, 0)),
                  pl.BlockSpec((1, 1, d), lambda i, j: (i, 0, 0)),
                  pl.BlockSpec(wm.shape, const),
                  pl.BlockSpec(wg.shape, const),
                  pl.BlockSpec(wr.shape, const),
                  pl.BlockSpec((1, LANES), const),
                  pl.BlockSpec((tm, width), lambda i, j: (j, 0)),
                  pl.BlockSpec((tm, width), lambda i, j: (j, 0))],
        out_specs=[pl.BlockSpec((1, tm, 4 * width), lambda i, j: (i, j, 0)),
                   pl.BlockSpec((1, tm, LANES), lambda i, j: (i, j, 0)),
                   pl.BlockSpec((1, tm, 4 * width), lambda i, j: (i, j, 0))],
        out_shape=[jax.ShapeDtypeStruct((b, t, 4 * width), BF16),
                   jax.ShapeDtypeStruct((b, t, LANES), F32),
                   jax.ShapeDtypeStruct((b, t, 4 * width), BF16)],
        compiler_params=_params("arbitrary", "arbitrary"),
    )(x, nw, sh, sc, wm, wg, wr, gb, cos, sin)


def _scan_lanes(x, op, fill, reverse):
    lane = lax.broadcasted_iota(jnp.int32, x.shape, 1)
    sh = 1
    while sh < LANES:
        if reverse:
            x = op(x, jnp.where(lane < LANES - sh, pltpu.roll(x, LANES - sh, 1), fill))
        else:
            x = op(x, jnp.where(lane >= sh, pltpu.roll(x, sh, 1), fill))
        sh *= 2
    return x


def _gateprep_kernel(ig_ref, fg_ref, o_ref, *, n_chunks):
    rows = ig_ref.shape[1]
    half = rows // 2
    row = lax.broadcasted_iota(jnp.int32, (rows, LANES), 0)
    lane = lax.broadcasted_iota(jnp.int32, (rows, LANES), 1)
    is_fwd = row < half
    last = lane == jnp.where(is_fwd, LANES - 1, 0)

    def body(c, m):
        lf = _log_sigmoid(fg_ref[c])
        ig = ig_ref[c]
        b = jnp.where(is_fwd, _scan_lanes(lf, jnp.add, 0.0, False), _scan_lanes(lf, jnp.add, 0.0, True))
        b_tot = jnp.sum(jnp.where(last, b, 0.0), axis=1, keepdims=True)
        a = ig - b
        m_new = jnp.maximum(b_tot + m, jnp.max(b_tot + a, axis=1, keepdims=True))
        w = jnp.exp(b_tot + a - m_new)
        decay = jnp.exp(b_tot + m - m_new)
        run = jnp.where(is_fwd, _scan_lanes(a, jnp.maximum, -jnp.inf, False),
                        _scan_lanes(a, jnp.maximum, -jnp.inf, True))
        mm = jnp.maximum(m, run)
        o_ref[c, 0] = a
        o_ref[c, 1] = w
        o_ref[c, 2] = mm
        o_ref[c, 3] = jnp.exp(m - mm)
        o_ref[c, 4] = jnp.exp(-(b + mm))
        o_ref[c, 5] = jnp.broadcast_to(decay, (rows, LANES))
        return m_new

    lax.fori_loop(0, n_chunks, body, jnp.zeros((rows, 1), F32))


def _gateprep(ig, fg):
    n_chunks, rows, _ = ig.shape
    return pl.pallas_call(
        functools.partial(_gateprep_kernel, n_chunks=n_chunks),
        out_shape=jax.ShapeDtypeStruct((n_chunks, 6, rows, LANES), F32),
        compiler_params=pltpu.CompilerParams(vmem_limit_bytes=VMEM_LIMIT_BYTES),
    )(ig, fg)


def _head_norm_gate(h, gate, nw):
    return gate * (h * lax.rsqrt(jnp.mean(h * h, axis=-1, keepdims=True) + EPS)) * nw


def _mlstm_kernel(q_ref, k_ref, v_ref, o_ref, kc_ref, vc_ref, rowv_ref, colv_ref, nw_ref, y_ref,
                  sf_ref, sb_ref, cf_ref, cb_ref, *, n_lat, n_ctx):
    L = CHUNK
    ones = jnp.ones((L, HEAD_DIM), BF16)

    def update(state_ref, k, v, w_col, decay_row):
        wk = (k.astype(F32) * w_col).astype(BF16)
        u = _dot_tn(wk, jnp.concatenate([v, ones], axis=1))
        dec = jnp.concatenate([decay_row, decay_row], axis=1)
        state_ref[...] = dec * state_ref[...] + u

    cf_ref[...] = jnp.zeros_like(cf_ref)
    cb_ref[...] = jnp.zeros_like(cb_ref)
    for c in range(n_ctx):
        rows = slice(c * L, (c + 1) * L)
        update(cf_ref, kc_ref[0, rows, :], vc_ref[0, rows, :],
               colv_ref[0, 0, rows, 0:1], rowv_ref[0, 0, c, 2:3, :])
    for c in reversed(range(n_ctx)):
        rows = slice(c * L, (c + 1) * L)
        update(cb_ref, kc_ref[0, rows, :], vc_ref[0, rows, :],
               colv_ref[0, 0, rows, 1:2], rowv_ref[0, 0, c, 3:4, :])

    def state_body(i, carry):
        jf = i
        jb = n_lat - 1 - i
        rf = pl.ds(pl.multiple_of(jf * L, L), L)
        rb = pl.ds(pl.multiple_of(jb * L, L), L)
        cf_rows = pl.ds(pl.multiple_of((n_ctx + jf) * L, L), L)
        cb_rows = pl.ds(pl.multiple_of((n_ctx + jb) * L, L), L)
        sf_ref[jf] = cf_ref[...].astype(BF16)
        sb_ref[jb] = cb_ref[...].astype(BF16)
        update(cf_ref, k_ref[0, rf, :], v_ref[0, rf, :],
               colv_ref[0, 0, cf_rows, 0:1], rowv_ref[0, 0, n_ctx + jf, 2:3, :])
        update(cb_ref, k_ref[0, rb, :], v_ref[0, rb, :],
               colv_ref[0, 0, cb_rows, 1:2], rowv_ref[0, 0, n_ctx + jb, 3:4, :])
        return carry

    lax.fori_loop(0, n_lat, state_body, 0)

    r_i = lax.broadcasted_iota(jnp.int32, (L, L), 0)
    c_i = lax.broadcasted_iota(jnp.int32, (L, L), 1)
    tril = c_i <= r_i
    triu = c_i >= r_i
    nw = nw_ref[0]

    def out_body(j, carry):
        rows = pl.ds(pl.multiple_of(j * L, L), L)
        crow = pl.ds(pl.multiple_of((n_ctx + j) * L, L), L)
        q = q_ref[0, rows, :]
        k = k_ref[0, rows, :]
        v = v_ref[0, rows, :]
        rv = rowv_ref[0, 0, n_ctx + j]
        cv = colv_ref[0, 0, crow, :]
        s = _dot_nt(q, k)
        d_f = jnp.where(tril, jnp.exp(rv[0:1, :] - cv[:, 2:3]), 0.0)
        d_b = jnp.where(triu, jnp.exp(rv[1:2, :] - cv[:, 3:4]), 0.0)
        p = jnp.concatenate([s * d_f, s * d_b], axis=0).astype(BF16)
        tot = _dot(p, jnp.concatenate([v, ones], axis=1))
        inter = _dot(q, jnp.concatenate([sf_ref[j], sb_ref[j]], axis=1))
        w2 = HEAD_DIM + LANES
        t_f = tot[:L] + cv[:, 4:5] * inter[:, :w2]
        t_b = tot[L:] + cv[:, 5:6] * inter[:, w2:]
        h_f = t_f[:, :HEAD_DIM] / jnp.maximum(jnp.abs(t_f[:, HEAD_DIM:]), cv[:, 6:7])
        h_b = t_b[:, :HEAD_DIM] / jnp.maximum(jnp.abs(t_b[:, HEAD_DIM:]), cv[:, 7:8])
        gate = _sigmoid(o_ref[0, rows, :].astype(F32))
        y_ref[0, rows, :] = _head_norm_gate(h_f + h_b, gate, nw).astype(BF16)
        return carry

    lax.fori_loop(0, n_lat, out_body, 0)


def _mlstm(pm, pmc, rowv, colv, nw):
    b, t, _ = pm.shape
    tc = pmc.shape[1]
    n_lat, n_ctx = t // CHUNK, tc // CHUNK
    H = N_HEADS
    blk = lambda off: pl.BlockSpec((1, t, HEAD_DIM), lambda i, h: (i, 0, off + h))
    cblk = lambda off: pl.BlockSpec((1, tc, HEAD_DIM), lambda i, h: (i, 0, off + h))
    return pl.pallas_call(
        functools.partial(_mlstm_kernel, n_lat=n_lat, n_ctx=n_ctx),
        grid=(b, H),
        in_specs=[blk(0), blk(H), blk(2 * H), blk(3 * H), cblk(H), cblk(2 * H),
                  pl.BlockSpec((1, 1, n_ctx + n_lat, 8, LANES), lambda i, h: (i, h, 0, 0, 0)),
                  pl.BlockSpec((1, 1, tc + t, 8), lambda i, h: (i, h, 0, 0)),
                  pl.BlockSpec((1, 1, HEAD_DIM), lambda i, h: (h, 0, 0))],
        out_specs=pl.BlockSpec((1, t, HEAD_DIM), lambda i, h: (i, 0, h)),
        out_shape=jax.ShapeDtypeStruct((b, t, H * HEAD_DIM), BF16),
        scratch_shapes=[pltpu.VMEM((n_lat, HEAD_DIM, HEAD_DIM + LANES), BF16),
                        pltpu.VMEM((n_lat, HEAD_DIM, HEAD_DIM + LANES), BF16),
                        pltpu.VMEM((HEAD_DIM, HEAD_DIM + LANES), F32),
                        pltpu.VMEM((HEAD_DIM, HEAD_DIM + LANES), F32)],
        compiler_params=_params("arbitrary", "arbitrary"),
    )(pm, pm, pm, pm, pmc, pmc, rowv, colv, nw)


def _ret_kernel(q_ref, k_ref, v_ref, o_ref, kc_ref, vc_ref, dl_ref, nw_ref, y_ref,
                sf_ref, sb_ref, rf_ref, rb_ref, *, n_lat, n_ctx):
    L = CHUNK
    lg = _log_sigmoid(dl_ref[0])
    lg_f, lg_b = lg[0:1, :], lg[1:2, :]
    r_i = lax.broadcasted_iota(jnp.int32, (L, L), 0)
    c_i = lax.broadcasted_iota(jnp.int32, (L, L), 1)
    r_f = r_i.astype(F32)
    rel = (r_i - c_i).astype(F32)
    kw_f = jnp.exp((L - 1.0 - r_f) * lg_f)
    kw_b = jnp.exp(r_f * lg_b)
    in_f = jnp.exp((r_f + 1.0) * lg_f)
    in_b = jnp.exp((L - r_f) * lg_b)
    dch_f = jnp.exp(L * lg_f)
    dch_b = jnp.exp(L * lg_b)
    d_mat = (jnp.where(rel >= 0, jnp.exp(jnp.maximum(rel, 0.0) * lg_f), 0.0)
             + jnp.where(rel <= 0, jnp.exp(jnp.maximum(-rel, 0.0) * lg_b), 0.0))

    def update(state_ref, k, v, kw, dch):
        wk = (k.astype(F32) * kw).astype(BF16)
        state_ref[...] = dch * state_ref[...] + _dot_tn(wk, v)

    rf_ref[...] = jnp.zeros_like(rf_ref)
    rb_ref[...] = jnp.zeros_like(rb_ref)
    for c in range(n_ctx):
        rows = slice(c * L, (c + 1) * L)
        update(rf_ref, kc_ref[0, rows, :], vc_ref[0, rows, :], kw_f, dch_f)
    for c in reversed(range(n_ctx)):
        rows = slice(c * L, (c + 1) * L)
        update(rb_ref, kc_ref[0, rows, :], vc_ref[0, rows, :], kw_b, dch_b)

    def state_body(i, carry):
        jb = n_lat - 1 - i
        rowf = pl.ds(pl.multiple_of(i * L, L), L)
        rowb = pl.ds(pl.multiple_of(jb * L, L), L)
        sf_ref[i] = rf_ref[...].astype(BF16)
        sb_ref[jb] = rb_ref[...].astype(BF16)
        update(rf_ref, k_ref[0, rowf, :], v_ref[0, rowf, :], kw_f, dch_f)
        update(rb_ref, k_ref[0, rowb, :], v_ref[0, rowb, :], kw_b, dch_b)
        return carry

    lax.fori_loop(0, n_lat, state_body, 0)
    nw = nw_ref[0]

    def out_body(j, carry):
        rows = pl.ds(pl.multiple_of(j * L, L), L)
        q = q_ref[0, rows, :]
        s = _dot_nt(q, k_ref[0, rows, :])
        intra = _dot((s * d_mat).astype(BF16), v_ref[0, rows, :])
        inter = _dot(q, jnp.concatenate([sf_ref[j], sb_ref[j]], axis=1))
        out = intra + in_f * inter[:, :HEAD_DIM] + in_b * inter[:, HEAD_DIM:]
        og = o_ref[0, rows, :].astype(F32)
        y_ref[0, rows, :] = _head_norm_gate(out, og * _sigmoid(og), nw).astype(BF16)
        return carry

    lax.fori_loop(0, n_lat, out_body, 0)


def _ret(pr, prc, dl, nw):
    b, t, _ = pr.shape
    tc = prc.shape[1]
    n_lat, n_ctx = t // CHUNK, tc // CHUNK
    H = N_HEADS
    blk = lambda off: pl.BlockSpec((1, t, HEAD_DIM), lambda i, h: (i, 0, off + h))
    cblk = lambda off: pl.BlockSpec((1, tc, HEAD_DIM), lambda i, h: (i, 0, off + h))
    return pl.pallas_call(
        functools.partial(_ret_kernel, n_lat=n_lat, n_ctx=n_ctx),
        grid=(b, H),
        in_specs=[blk(0), blk(H), blk(2 * H), blk(3 * H), cblk(H), cblk(2 * H),
                  pl.BlockSpec((1, 2, LANES), lambda i, h: (h, 0, 0)),
                  pl.BlockSpec((1, 1, HEAD_DIM), lambda i, h: (h, 0, 0))],
        out_specs=pl.BlockSpec((1, t, HEAD_DIM), lambda i, h: (i, 0, h)),
        out_shape=jax.ShapeDtypeStruct((b, t, H * HEAD_DIM), BF16),
        scratch_shapes=[pltpu.VMEM((n_lat, HEAD_DIM, HEAD_DIM), BF16),
                        pltpu.VMEM((n_lat, HEAD_DIM, HEAD_DIM), BF16),
                        pltpu.VMEM((HEAD_DIM, HEAD_DIM), F32),
                        pltpu.VMEM((HEAD_DIM, HEAD_DIM), F32)],
        compiler_params=_params("arbitrary", "arbitrary"),
    )(pr, pr, pr, pr, prc, prc, dl, nw)


def _outproj_kernel(ym_ref, yr_ref, x_ref, wo_ref, g1_ref, nw_ref, sh_ref, sc_ref, wrt_ref,
                    xl_ref, h2_ref, aff_ref, *, n_experts, width):
    y = _dot(ym_ref[0], wo_ref[:width, :]) + _dot(yr_ref[0], wo_ref[width:, :])
    xl = x_ref[0] + g1_ref[0] * y
    xl_ref[0] = xl
    h2 = _rms_mod(xl, nw_ref[...], sh_ref[0], sc_ref[0])
    h2_ref[0] = h2.astype(BF16)
    logits = jnp.dot(h2, wrt_ref[...], preferred_element_type=F32, precision=lax.Precision.HIGHEST)
    lane = lax.broadcasted_iota(jnp.int32, logits.shape, 1)
    logits = jnp.where(lane < n_experts, logits, -jnp.inf)
    e = jnp.exp(logits - jnp.max(logits, axis=-1, keepdims=True))
    aff_ref[0] = e / jnp.sum(e, axis=-1, keepdims=True)


def _outproj(ym, yr, x, wo, g1, nw, sh, sc, wrt, n_experts):
    b, t, d = x.shape
    width = ym.shape[2]
    tm = _tile(t, 512)
    const = lambda i, j: (0, 0)
    per_b = lambda i, j: (i, 0, 0)
    tok = lambda i, j: (i, j, 0)
    return pl.pallas_call(
        functools.partial(_outproj_kernel, n_experts=n_experts, width=width),
        grid=(b, t // tm),
        in_specs=[pl.BlockSpec((1, tm, width), tok), pl.BlockSpec((1, tm, width), tok),
                  pl.BlockSpec((1, tm, d), tok), pl.BlockSpec(wo.shape, const),
                  pl.BlockSpec((1, 1, d), per_b), pl.BlockSpec((1, d), const),
                  pl.BlockSpec((1, 1, d), per_b), pl.BlockSpec((1, 1, d), per_b),
                  pl.BlockSpec(wrt.shape, const)],
        out_specs=[pl.BlockSpec((1, tm, d), tok), pl.BlockSpec((1, tm, d), tok),
                   pl.BlockSpec((1, tm, LANES), tok)],
        out_shape=[jax.ShapeDtypeStruct((b, t, d), F32), jax.ShapeDtypeStruct((b, t, d), BF16),
                   jax.ShapeDtypeStruct((b, t, LANES), F32)],
        compiler_params=_params("arbitrary", "arbitrary"),
    )(ym, yr, x, wo, g1, nw, sh, sc, wrt)


def _route_kernel(aff_ref, pos_ref, *, cap):
    a = aff_ref[0]
    n_e, n_b, _ = a.shape

    def count(mask):
        return jnp.sum(jnp.sum(mask.astype(jnp.int32), axis=2, keepdims=True), axis=1, keepdims=True)

    def search(i, thr):
        cand = thr | jnp.left_shift(jnp.int32(1), 30 - i)
        return jnp.where(count(a >= pltpu.bitcast(cand, F32)) >= cap, cand, thr)

    thr = lax.fori_loop(0, 31, search, jnp.zeros((n_e, 1, 1), jnp.int32))
    above = a >= pltpu.bitcast(thr + 1, F32)
    band = (a >= pltpu.bitcast(thr, F32)) & jnp.logical_not(above)
    need = cap - count(above)
    tok = (lax.broadcasted_iota(jnp.int32, a.shape, 1) * LANES
           + lax.broadcasted_iota(jnp.int32, a.shape, 2))

    def take_one(_, carry):
        sel, band, need = carry
        in_band = band > 0
        best = jnp.max(jnp.max(jnp.where(in_band, a, -1.0), axis=2, keepdims=True), axis=1, keepdims=True)
        cand = jnp.where(in_band & (a == best), tok, n_b * LANES)
        first = jnp.min(jnp.min(cand, axis=2, keepdims=True), axis=1, keepdims=True)
        take = ((tok == first) & (need > 0)).astype(jnp.int32)
        return sel | take, band - take, need - 1

    sel, _, _ = lax.fori_loop(0, jnp.max(need), take_one,
                              (above.astype(jnp.int32), band.astype(jnp.int32), need))
    sel = sel > 0

    rows = n_e * n_b
    u_i = lax.broadcasted_iota(jnp.int32, (LANES, LANES), 0)
    s_i = lax.broadcasted_iota(jnp.int32, (LANES, LANES), 1)
    incl = (u_i <= s_i).astype(BF16)
    ones = jnp.ones((LANES, LANES), BF16)
    r_i = lax.broadcasted_iota(jnp.int32, (rows, rows), 0)
    c_i = lax.broadcasted_iota(jnp.int32, (rows, rows), 1)
    before = ((r_i // n_b == c_i // n_b) & (c_i < r_i)).astype(BF16)

    x = sel.astype(F32).reshape(rows, LANES)
    xb = x.astype(BF16)
    block_tot = _dot(xb, ones).astype(BF16)
    pre = _dot(before, block_tot) + _dot(xb, incl) - x
    pos_ref[0] = jnp.where(sel, pre.astype(jnp.int32).reshape(n_e, n_b, LANES), -1)


def _route(aff_t, cap):
    b, n_e, n_b, _ = aff_t.shape
    spec = pl.BlockSpec((1, n_e, n_b, LANES), lambda i: (i, 0, 0, 0))
    return pl.pallas_call(
        functools.partial(_route_kernel, cap=cap),
        grid=(b,), in_specs=[spec], out_specs=spec,
        out_shape=jax.ShapeDtypeStruct(aff_t.shape, jnp.int32),
        compiler_params=_params("arbitrary"),
    )(aff_t)


def _gather_kernel(pos_ref, aff_ref, h_ref, xg_ref, gate_ref, *, cap, tk):
    t = h_ref.shape[1]
    slot = lax.broadcasted_iota(jnp.int32, (cap, tk), 0)
    acc = jnp.zeros((cap, h_ref.shape[2]), F32)
    gate = jnp.zeros((cap, 1), F32)
    for c in range(t // tk):
        cols = slice(c * tk, (c + 1) * tk)
        hit = pos_ref[0, 0, :, cols] == slot
        acc = acc + _dot(hit.astype(BF16), h_ref[0, cols, :])
        gate = gate + jnp.sum(jnp.where(hit, aff_ref[0, 0, :, cols], 0.0), axis=1, keepdims=True)
    xg_ref[0, 0] = acc.astype(BF16)
    gate_ref[0, 0] = gate


def _gather(pos_row, aff_row, h2, cap):
    b, n_e, _, t = pos_row.shape
    d = h2.shape[2]
    tk = _tile(t, 512)
    row = pl.BlockSpec((1, 1, 1, t), lambda i, e: (i, e, 0, 0))
    return pl.pallas_call(
        functools.partial(_gather_kernel, cap=cap, tk=tk),
        grid=(b, n_e),
        in_specs=[row, row, pl.BlockSpec((1, t, d), lambda i, e: (i, 0, 0))],
        out_specs=[pl.BlockSpec((1, 1, cap, d), lambda i, e: (e, i, 0, 0)),
                   pl.BlockSpec((1, 1, cap, 1), lambda i, e: (e, i, 0, 0))],
        out_shape=[jax.ShapeDtypeStruct((n_e, b, cap, d), BF16),
                   jax.ShapeDtypeStruct((n_e, b, cap, 1), F32)],
        compiler_params=_params("arbitrary", "arbitrary"),
    )(pos_row, aff_row, h2)


def _ffn_kernel(x_ref, gate_ref, wg_ref, wu_ref, wd_ref, y_ref, acc_ref):
    f = pl.program_id(1)

    @pl.when(f == 0)
    def _():
        acc_ref[...] = jnp.zeros_like(acc_ref)

    x = x_ref[0]
    a = _dot(x, wg_ref[0].astype(BF16))
    u = _dot(x, wu_ref[0].astype(BF16))
    mid = (a * _sigmoid(a) * u).astype(BF16)
    acc_ref[...] += _dot(mid, wd_ref[0].astype(BF16))

    @pl.when(f == pl.num_programs(1) - 1)
    def _():
        y_ref[0] = (acc_ref[...] * gate_ref[0]).astype(BF16)


def _ffn(xg, gate, wg, wu, wd):
    n_e, m, d = xg.shape
    ff = wg.shape[2]
    tf = _tile(ff, 256)
    return pl.pallas_call(
        _ffn_kernel,
        grid=(n_e, ff // tf),
        in_specs=[pl.BlockSpec((1, m, d), lambda e, f: (e, 0, 0)),
                  pl.BlockSpec((1, m, 1), lambda e, f: (e, 0, 0)),
                  pl.BlockSpec((1, d, tf), lambda e, f: (e, 0, f)),
                  pl.BlockSpec((1, d, tf), lambda e, f: (e, 0, f)),
                  pl.BlockSpec((1, tf, d), lambda e, f: (e, f, 0))],
        out_specs=pl.BlockSpec((1, m, d), lambda e, f: (e, 0, 0)),
        out_shape=jax.ShapeDtypeStruct((n_e, m, d), BF16),
        scratch_shapes=[pltpu.VMEM((m, d), F32)],
        compiler_params=_params("arbitrary", "arbitrary"),
    )(xg, gate, wg, wu, wd)


def _combine_kernel(y_ref, pos_ref, xl_ref, g2_ref, fw_ref, o_ref, *, n_experts, cap):
    tt = xl_ref.shape[1]
    slot = lax.broadcasted_iota(jnp.int32, (tt, cap), 1)
    pos = pos_ref[0]
    acc = jnp.zeros((tt, xl_ref.shape[2]), F32)
    for e in range(n_experts):
        hit = (pos[:, e:e + 1] == slot).astype(BF16)
        acc = acc + _dot(hit, y_ref[e, 0])
    xl = xl_ref[0] + g2_ref[0] * acc
    ms = jnp.mean(xl * xl, axis=-1, keepdims=True)
    o_ref[0] = xl * lax.rsqrt(ms + EPS) * fw_ref[...]


def _combine(y, pos_col, xl, g2, fw, cap):
    b, t, d = xl.shape
    n_e = y.shape[0]
    tt = _tile(t, 512)
    tok = lambda i, j: (i, j, 0)
    return pl.pallas_call(
        functools.partial(_combine_kernel, n_experts=n_e, cap=cap),
        grid=(b, t // tt),
        in_specs=[pl.BlockSpec((n_e, 1, cap, d), lambda i, j: (0, i, 0, 0)),
                  pl.BlockSpec((1, tt, n_e), tok),
                  pl.BlockSpec((1, tt, d), tok),
                  pl.BlockSpec((1, 1, d), lambda i, j: (i, 0, 0)),
                  pl.BlockSpec((1, d), lambda i, j: (0, 0))],
        out_specs=pl.BlockSpec((1, tt, d), tok),
        out_shape=jax.ShapeDtypeStruct((b, t, d), F32),
        compiler_params=_params("arbitrary", "arbitrary"),
    )(y, pos_col, xl, g2, fw)


def _rope_tables(t):
    n_freq = HEAD_DIM // 4
    rows = jnp.repeat(jnp.arange(t // GRID_W, dtype=F32), GRID_W)
    cols = jnp.tile(jnp.arange(GRID_W, dtype=F32), t // GRID_W)
    inv_freq = ROPE_BASE ** (-jnp.arange(n_freq, dtype=F32) / n_freq)
    ang_r, ang_c = rows[:, None] * inv_freq, cols[:, None] * inv_freq
    cos = jnp.concatenate([jnp.cos(ang_r)] * 2 + [jnp.cos(ang_c)] * 2, axis=-1)
    sin = jnp.concatenate([-jnp.sin(ang_r), jnp.sin(ang_r), -jnp.sin(ang_c), jnp.sin(ang_c)], axis=-1)
    return jnp.tile(cos, (1, N_HEADS)), jnp.tile(sin, (1, N_HEADS))


def _scan_order(lat, ctx, reverse):
    def chunks(a):
        b, h, t = a.shape[:3]
        a = jnp.moveaxis(a.reshape((b, h, t // CHUNK, CHUNK) + a.shape[3:]), 2, 0)
        return a[::-1] if reverse else a
    return jnp.concatenate([chunks(ctx), chunks(lat)], axis=0)


def _time_order(a, n_ctx, reverse):
    if not reverse:
        return a
    return jnp.concatenate([a[:n_ctx][::-1], a[n_ctx:][::-1]], axis=0)


def kernel(x, c, ctx, c_ctx, w_ada, b_ada, norm1_w, norm2_w, w_in, mlstm_gate_bias, ret_decay_logit,
           mlstm_norm_w, ret_norm_w, w_out, w_router, w_gate, w_up, w_down, final_norm_w):
    B, T, D = x.shape
    Tc = ctx.shape[1]
    H = N_HEADS
    width = H * HEAD_DIM
    assert D == 2 * width and w_ada.shape[0] == 1
    assert T % CHUNK == 0 and Tc % CHUNK == 0 and T % GRID_W == 0
    n_lat, n_ctx = T // CHUNK, Tc // CHUNK
    n_experts = w_router.shape[2]
    cap = EC_CAPACITY_FACTOR * T // n_experts
    n_gate = 4 * H

    pad = (-(B + 1)) % 8
    cc = jnp.concatenate([c, c_ctx[None], jnp.zeros((pad, D), F32)], axis=0)
    mod = _ada(cc, w_ada[0], b_ada[0])
    sh1, sc1, g1, sh2, sc2, g2 = [mod[:B, None, i * D:(i + 1) * D] for i in range(6)]
    csh1, csc1 = [jnp.broadcast_to(mod[B, i * D:(i + 1) * D], (B, 1, D)) for i in range(2)]

    w_in0 = w_in[0]
    wm = w_in0[:, :4 * width].astype(BF16)
    wg = jnp.pad(w_in0[:, 4 * width:4 * width + n_gate], ((0, 0), (0, LANES - n_gate))).astype(BF16)
    wr = w_in0[:, 4 * width + n_gate:].astype(BF16)
    gb = jnp.pad(mlstm_gate_bias[0], (0, LANES - n_gate)).reshape(1, LANES)
    nw1 = norm1_w[0].reshape(1, D)
    cos, sin = _rope_tables(T)
    pm, gl, pr = _inproj(x, nw1, sh1, sc1, wm, wg, wr, gb, cos, sin)
    pmc, gc, prc = _inproj(ctx, nw1, csh1, csc1, wm, wg, wr, gb,
                           jnp.ones((Tc, width), F32), jnp.zeros((Tc, width), F32))

    def gate_cols(g, lo):
        return jnp.swapaxes(g[:, :, lo:lo + H], 1, 2)
    def stream_rows(lo_f, lo_b):
        f = _scan_order(gate_cols(gl, lo_f), gate_cols(gc, lo_f), False)
        r = _scan_order(gate_cols(gl, lo_b), gate_cols(gc, lo_b), True)
        n = f.shape[0]
        return jnp.concatenate([f.reshape(n, B * H, CHUNK), r.reshape(n, B * H, CHUNK)], axis=1)
    stats = _gateprep(stream_rows(0, H), stream_rows(2 * H, 3 * H))
    n_all = n_ctx + n_lat
    st_f = stats[:, :, :B * H].reshape(n_all, 6, B, H, CHUNK)
    st_b = _time_order(stats[:, :, B * H:], n_ctx, True).reshape(n_all, 6, B, H, CHUNK)
    a_i, w_i, mm_i, iw_i, en_i, dec_i = range(6)
    rowv = jnp.stack([st_f[:, a_i], st_b[:, a_i], st_f[:, dec_i], st_b[:, dec_i]], axis=-2)
    rowv = jnp.pad(jnp.transpose(rowv, (1, 2, 0, 3, 4)), ((0, 0),) * 3 + ((0, 4), (0, 0)))
    colv = jnp.stack([st_f[:, w_i], st_b[:, w_i], st_f[:, mm_i], st_b[:, mm_i],
                      st_f[:, iw_i], st_b[:, iw_i], st_f[:, en_i], st_b[:, en_i]], axis=-1)
    colv = jnp.transpose(colv, (1, 2, 0, 3, 4)).reshape(B, H, n_all * CHUNK, 8)

    ym = _mlstm(pm, pmc, rowv, colv, mlstm_norm_w[0].reshape(H, 1, HEAD_DIM))
    dl = jnp.broadcast_to(jnp.swapaxes(ret_decay_logit[0], 0, 1)[:, :, None], (H, 2, LANES))
    yr = _ret(pr, prc, dl, ret_norm_w[0].reshape(H, 1, HEAD_DIM))

    wrt = jnp.pad(w_router[0], ((0, 0), (0, LANES - n_experts)))
    xl, h2, aff = _outproj(ym, yr, x, w_out[0].astype(BF16), g1, norm2_w[0].reshape(1, D), sh2, sc2,
                           wrt, n_experts)

    aff_t = jnp.swapaxes(aff[:, :, :n_experts], 1, 2)
    pos = _route(aff_t.reshape(B, n_experts, T // LANES, LANES), cap).reshape(B, n_experts, T)
    xg, gate = _gather(pos.reshape(B, n_experts, 1, T), aff_t.reshape(B, n_experts, 1, T), h2, cap)
    y = _ffn(xg.reshape(n_experts, B * cap, D), gate.reshape(n_experts, B * cap, 1),
             w_gate[0], w_up[0], w_down[0])
    return _combine(y.reshape(n_experts, B, cap, D), jnp.swapaxes(pos, 1, 2), xl, g2,
                    final_norm_w.reshape(1, D), cap)
```

```python
import functools

import jax
import jax.numpy as jnp
from jax import lax
from jax.experimental import pallas as pl
from jax.experimental.pallas import tpu as pltpu

F32 = jnp.float32
BF16 = jnp.bfloat16

CHUNK = 128
GRID_W = 64
N_HEADS = 4
HEAD_DIM = 128
ROPE_BASE = 10000.0
EPS = 1e-6
EC_CAPACITY_FACTOR = 2
LANES = 128
VMEM_LIMIT_BYTES = 56 * 1024 * 1024
MIXER_UNROLL = 8


def _params(*sem):
    return pltpu.CompilerParams(dimension_semantics=sem, vmem_limit_bytes=VMEM_LIMIT_BYTES)


def _dot(a, b):
    return jnp.dot(a, b, preferred_element_type=F32)


def _dot_nt(a, b):
    return lax.dot_general(a, b, (((1,), (1,)), ((), ())), preferred_element_type=F32)


def _dot_tn(a, b):
    return lax.dot_general(a, b, (((0,), (0,)), ((), ())), preferred_element_type=F32)


def _sigmoid(x):
    return 1.0 / (1.0 + jnp.exp(-x))


def _log_sigmoid(x):
    return jnp.minimum(x, 0.0) - jnp.log1p(jnp.exp(-jnp.abs(x)))


def _tile(n, pref):
    t = min(n, pref)
    assert n % t == 0, (n, t)
    return t


def _ada_kernel(c_ref, w_ref, b_ref, o_ref):
    cv = c_ref[...]
    s = (cv * _sigmoid(cv)).astype(BF16)
    o_ref[...] = _dot(s, w_ref[...].astype(BF16)) + b_ref[...]


def _ada(cc, w, b):
    rows, d = cc.shape
    n = w.shape[1]
    tn = _tile(n, 1024)
    return pl.pallas_call(
        _ada_kernel,
        grid=(n // tn,),
        in_specs=[pl.BlockSpec((rows, d), lambda j: (0, 0)),
                  pl.BlockSpec((d, tn), lambda j: (0, j)),
                  pl.BlockSpec((1, tn), lambda j: (0, j))],
        out_specs=pl.BlockSpec((rows, tn), lambda j: (0, j)),
        out_shape=jax.ShapeDtypeStruct((rows, n), F32),
        compiler_params=_params("arbitrary"),
    )(cc, w, b.reshape(1, n))


def _rms_mod(x, nw, sh, sc):
    ms = jnp.mean(x * x, axis=-1, keepdims=True)
    return (x * lax.rsqrt(ms + EPS) * nw) * (1.0 + sc) + sh


def _inproj_kernel(x_ref, nw_ref, sh_ref, sc_ref, wm_ref, wg_ref, wr_ref, gb_ref, cos_ref, sin_ref,
                   pm_ref, g_ref, pr_ref, *, width, k_scale):
    hb = _rms_mod(x_ref[0], nw_ref[...], sh_ref[0], sc_ref[0]).astype(BF16)
    for j in range(4):
        cols = slice(j * width, (j + 1) * width)
        acc = _dot(hb, wm_ref[:, cols])
        if j == 1:
            acc = acc * k_scale
        pm_ref[0, :, cols] = acc.astype(BF16)
    g_ref[0] = _dot(hb, wg_ref[...]) + gb_ref[...]
    cos = cos_ref[...]
    sin = sin_ref[...]
    lane = lax.broadcasted_iota(jnp.int32, cos.shape, 1)
    even = ((lane // 32) % 2) == 0
    for j in range(4):
        cols = slice(j * width, (j + 1) * width)
        acc = _dot(hb, wr_ref[:, cols])
        if j == 1:
            acc = acc * k_scale
        if j < 2:
            swapped = jnp.where(even, pltpu.roll(acc, width - 32, 1), pltpu.roll(acc, 32, 1))
            acc = acc * cos + swapped * sin
        pr_ref[0, :, cols] = acc.astype(BF16)


def _inproj(x, nw, sh, sc, wm, wg, wr, gb, cos, sin):
    b, t, d = x.shape
    width = wm.shape[1] // 4
    tm = _tile(t, 512)
    kern = functools.partial(_inproj_kernel, width=width, k_scale=HEAD_DIM ** -0.5)
    const = lambda i, j: (0, 0)
    return pl.pallas_call(
        kern,
        grid=(b, t // tm),
        in_specs=[pl.BlockSpec((1, tm, d), lambda i, j: (i, j, 0)),
                  pl.BlockSpec((1, d), const),
                  pl.BlockSpec((1, 1, d), lambda i, j: (i, 0, 0)),
                  pl.BlockSpec((1, 1, d), lambda i, j: (i, 0, 0)),
                  pl.BlockSpec(wm.shape, const),
                  pl.BlockSpec(wg.shape, const),
                  pl.BlockSpec(wr.shape, const),
                  pl.BlockSpec((1, LANES), const),
                  pl.BlockSpec((tm, width), lambda i, j: (j, 0)),
                  pl.BlockSpec((tm, width), lambda i, j: (j, 0))],
        out_specs=[pl.BlockSpec((1, tm, 4 * width), lambda i, j: (i, j, 0)),
                   pl.BlockSpec((1, tm, LANES), lambda i, j: (i, j, 0)),
                   pl.BlockSpec((1, tm, 4 * width), lambda i, j: (i, j, 0))],
        out_shape=[jax.ShapeDtypeStruct((b, t, 4 * width), BF16),
                   jax.ShapeDtypeStruct((b, t, LANES), F32),
                   jax.ShapeDtypeStruct((b, t, 4 * width), BF16)],
        compiler_params=_params("arbitrary", "arbitrary"),
    )(x, nw, sh, sc, wm, wg, wr, gb, cos, sin)


def _scan_lanes(x, op, fill, reverse):
    lane = lax.broadcasted_iota(jnp.int32, x.shape, 1)
    sh = 1
    while sh < LANES:
        if reverse:
            x = op(x, jnp.where(lane < LANES - sh, pltpu.roll(x, LANES - sh, 1), fill))
        else:
            x = op(x, jnp.where(lane >= sh, pltpu.roll(x, sh, 1), fill))
        sh *= 2
    return x


def _gateprep_kernel(ig_ref, fg_ref, o_ref, *, n_chunks):
    rows = ig_ref.shape[1]
    half = rows // 2
    row = lax.broadcasted_iota(jnp.int32, (rows, LANES), 0)
    lane = lax.broadcasted_iota(jnp.int32, (rows, LANES), 1)
    is_fwd = row < half
    last = lane == jnp.where(is_fwd, LANES - 1, 0)

    def body(c, m):
        lf = _log_sigmoid(fg_ref[c])
        ig = ig_ref[c]
        b = jnp.where(is_fwd, _scan_lanes(lf, jnp.add, 0.0, False), _scan_lanes(lf, jnp.add, 0.0, True))
        b_tot = jnp.sum(jnp.where(last, b, 0.0), axis=1, keepdims=True)
        a = ig - b
        m_new = jnp.maximum(b_tot + m, jnp.max(b_tot + a, axis=1, keepdims=True))
        w = jnp.exp(b_tot + a - m_new)
        decay = jnp.exp(b_tot + m - m_new)
        run = jnp.where(is_fwd, _scan_lanes(a, jnp.maximum, -jnp.inf, False),
                        _scan_lanes(a, jnp.maximum, -jnp.inf, True))
        mm = jnp.maximum(m, run)
        o_ref[c, 0] = a
        o_ref[c, 1] = w
        o_ref[c, 2] = mm
        o_ref[c, 3] = jnp.exp(-(b + mm))
        o_ref[c, 4] = jnp.broadcast_to(decay, (rows, LANES))
        o_ref[c, 5] = jnp.broadcast_to(m, (rows, LANES))
        return m_new

    lax.fori_loop(0, n_chunks, body, jnp.zeros((rows, 1), F32))


def _gateprep(ig, fg):
    n_chunks, rows, _ = ig.shape
    return pl.pallas_call(
        functools.partial(_gateprep_kernel, n_chunks=n_chunks),
        out_shape=jax.ShapeDtypeStruct((n_chunks, 6, rows, LANES), F32),
        compiler_params=pltpu.CompilerParams(vmem_limit_bytes=VMEM_LIMIT_BYTES),
    )(ig, fg)


def _head_norm_gate(h, gate, nw):
    return gate * (h * lax.rsqrt(jnp.mean(h * h, axis=-1, keepdims=True) + EPS)) * nw


def _mlstm_kernel(q_ref, k_ref, v_ref, o_ref, kc_ref, vc_ref, rowv_ref, colv_ref, nw_ref, y_ref,
                  sf_ref, sb_ref, cf_ref, cb_ref, *, n_lat, n_ctx, unroll):
    L = CHUNK
    ones = jnp.ones((L, HEAD_DIM), BF16)

    def update(state_ref, k, v, w_row, decay_row):
        wkt = (k.astype(F32).T * w_row).astype(BF16)
        u = _dot(wkt, jnp.concatenate([v, ones], axis=1))
        dec = jnp.concatenate([decay_row, decay_row], axis=1)
        state_ref[...] = dec * state_ref[...] + u

    cf_ref[...] = jnp.zeros_like(cf_ref)
    cb_ref[...] = jnp.zeros_like(cb_ref)
    for c in range(n_ctx):
        rows = slice(c * L, (c + 1) * L)
        update(cf_ref, kc_ref[0, rows, :], vc_ref[0, rows, :],
               rowv_ref[0, 0, c, 4:5, :], rowv_ref[0, 0, c, 2:3, :])
    for c in reversed(range(n_ctx)):
        rows = slice(c * L, (c + 1) * L)
        update(cb_ref, kc_ref[0, rows, :], vc_ref[0, rows, :],
               rowv_ref[0, 0, c, 5:6, :], rowv_ref[0, 0, c, 3:4, :])

    def state_body(i, carry):
        jf = i
        jb = n_lat - 1 - i
        rf = pl.ds(pl.multiple_of(jf * L, L), L)
        rb = pl.ds(pl.multiple_of(jb * L, L), L)
        sf_ref[jf] = cf_ref[...].astype(BF16)
        sb_ref[jb] = cb_ref[...].astype(BF16)
        update(cf_ref, k_ref[0, rf, :], v_ref[0, rf, :],
               rowv_ref[0, 0, n_ctx + jf, 4:5, :], rowv_ref[0, 0, n_ctx + jf, 2:3, :])
        update(cb_ref, k_ref[0, rb, :], v_ref[0, rb, :],
               rowv_ref[0, 0, n_ctx + jb, 5:6, :], rowv_ref[0, 0, n_ctx + jb, 3:4, :])
        return carry

    lax.fori_loop(0, n_lat, state_body, 0, unroll=unroll)

    r_i = lax.broadcasted_iota(jnp.int32, (L, L), 0)
    c_i = lax.broadcasted_iota(jnp.int32, (L, L), 1)
    tril = c_i <= r_i
    triu = c_i >= r_i
    nw = nw_ref[0]

    def out_body(j, carry):
        rows = pl.ds(pl.multiple_of(j * L, L), L)
        crow = pl.ds(pl.multiple_of((n_ctx + j) * L, L), L)
        q = q_ref[0, rows, :]
        k = k_ref[0, rows, :]
        v = v_ref[0, rows, :]
        rv = rowv_ref[0, 0, n_ctx + j]
        cv = colv_ref[0, 0, crow, :]
        mm_f = jnp.broadcast_to(cv[:, 0:1], (L, L))
        mm_b = jnp.broadcast_to(cv[:, 1:2], (L, L))
        s = _dot_nt(q, k)
        d_f = jnp.where(tril, jnp.exp(rv[0:1, :] - mm_f), 0.0)
        d_b = jnp.where(triu, jnp.exp(rv[1:2, :] - mm_b), 0.0)
        p = jnp.concatenate([s * d_f, s * d_b], axis=0).astype(BF16)
        tot = _dot(p, jnp.concatenate([v, ones], axis=1))
        inter = _dot(q, jnp.concatenate([sf_ref[j], sb_ref[j]], axis=1))
        iw_f = jnp.exp(rv[6:7, :] - mm_f)
        iw_b = jnp.exp(rv[7:8, :] - mm_b)
        D, W = HEAD_DIM, HEAD_DIM + LANES
        num_f = tot[:L, :D] + iw_f * inter[:, :D]
        den_f = tot[:L, D:] + iw_f * inter[:, D:W]
        num_b = tot[L:, :D] + iw_b * inter[:, W:W + D]
        den_b = tot[L:, D:] + iw_b * inter[:, W + D:]
        h = (num_f / jnp.maximum(jnp.abs(den_f), cv[:, 2:3])
             + num_b / jnp.maximum(jnp.abs(den_b), cv[:, 3:4]))
        gate = _sigmoid(o_ref[0, rows, :].astype(F32))
        y_ref[0, rows, :] = _head_norm_gate(h, gate, nw).astype(BF16)
        return carry

    lax.fori_loop(0, n_lat, out_body, 0, unroll=unroll)


def _mlstm(pm, pmc, rowv, colv, nw):
    b, t, _ = pm.shape
    tc = pmc.shape[1]
    n_lat, n_ctx = t // CHUNK, tc // CHUNK
    H = N_HEADS
    blk = lambda off: pl.BlockSpec((1, t, HEAD_DIM), lambda i, h: (i, 0, off + h))
    cblk = lambda off: pl.BlockSpec((1, tc, HEAD_DIM), lambda i, h: (i, 0, off + h))
    return pl.pallas_call(
        functools.partial(_mlstm_kernel, n_lat=n_lat, n_ctx=n_ctx, unroll=MIXER_UNROLL),
        grid=(b, H),
        in_specs=[blk(0), blk(H), blk(2 * H), blk(3 * H), cblk(H), cblk(2 * H),
                  pl.BlockSpec((1, 1, n_ctx + n_lat, 8, LANES), lambda i, h: (i, h, 0, 0, 0)),
                  pl.BlockSpec((1, 1, tc + t, 4), lambda i, h: (i, h, 0, 0)),
                  pl.BlockSpec((1, 1, HEAD_DIM), lambda i, h: (h, 0, 0))],
        out_specs=pl.BlockSpec((1, t, HEAD_DIM), lambda i, h: (i, 0, h)),
        out_shape=jax.ShapeDtypeStruct((b, t, H * HEAD_DIM), BF16),
        scratch_shapes=[pltpu.VMEM((n_lat, HEAD_DIM, HEAD_DIM + LANES), BF16),
                        pltpu.VMEM((n_lat, HEAD_DIM, HEAD_DIM + LANES), BF16),
                        pltpu.VMEM((HEAD_DIM, HEAD_DIM + LANES), F32),
                        pltpu.VMEM((HEAD_DIM, HEAD_DIM + LANES), F32)],
        compiler_params=_params("arbitrary", "arbitrary"),
    )(pm, pm, pm, pm, pmc, pmc, rowv, colv, nw)


def _ret_kernel(q_ref, k_ref, v_ref, o_ref, kc_ref, vc_ref, dl_ref, nw_ref, y_ref,
                sf_ref, sb_ref, rf_ref, rb_ref, *, n_lat, n_ctx, unroll):
    L = CHUNK
    lg = _log_sigmoid(dl_ref[0])
    lg_f, lg_b = lg[0:1, :], lg[1:2, :]
    r_i = lax.broadcasted_iota(jnp.int32, (L, L), 0)
    c_i = lax.broadcasted_iota(jnp.int32, (L, L), 1)
    r_f = r_i.astype(F32)
    rel = (r_i - c_i).astype(F32)
    kw_f = jnp.exp((L - 1.0 - r_f) * lg_f)
    kw_b = jnp.exp(r_f * lg_b)
    in_f = jnp.exp((r_f + 1.0) * lg_f)
    in_b = jnp.exp((L - r_f) * lg_b)
    dch_f = jnp.exp(L * lg_f)
    dch_b = jnp.exp(L * lg_b)
    d_mat = (jnp.where(rel >= 0, jnp.exp(jnp.maximum(rel, 0.0) * lg_f), 0.0)
             + jnp.where(rel <= 0, jnp.exp(jnp.maximum(-rel, 0.0) * lg_b), 0.0))

    def update(state_ref, k, v, kw, dch):
        wk = (k.astype(F32) * kw).astype(BF16)
        state_ref[...] = dch * state_ref[...] + _dot_tn(wk, v)

    rf_ref[...] = jnp.zeros_like(rf_ref)
    rb_ref[...] = jnp.zeros_like(rb_ref)
    for c in range(n_ctx):
        rows = slice(c * L, (c + 1) * L)
        update(rf_ref, kc_ref[0, rows, :], vc_ref[0, rows, :], kw_f, dch_f)
    for c in reversed(range(n_ctx)):
        rows = slice(c * L, (c + 1) * L)
        update(rb_ref, kc_ref[0, rows, :], vc_ref[0, rows, :], kw_b, dch_b)

    def state_body(i, carry):
        jb = n_lat - 1 - i
        rowf = pl.ds(pl.multiple_of(i * L, L), L)
        rowb = pl.ds(pl.multiple_of(jb * L, L), L)
        sf_ref[i] = rf_ref[...].astype(BF16)
        sb_ref[jb] = rb_ref[...].astype(BF16)
        update(rf_ref, k_ref[0, rowf, :], v_ref[0, rowf, :], kw_f, dch_f)
        update(rb_ref, k_ref[0, rowb, :], v_ref[0, rowb, :], kw_b, dch_b)
        return carry

    lax.fori_loop(0, n_lat, state_body, 0, unroll=unroll)
    nw = nw_ref[0]

    def out_body(j, carry):
        rows = pl.ds(pl.multiple_of(j * L, L), L)
        q = q_ref[0, rows, :]
        s = _dot_nt(q, k_ref[0, rows, :])
        intra = _dot((s * d_mat).astype(BF16), v_ref[0, rows, :])
        inter = _dot(q, jnp.concatenate([sf_ref[j], sb_ref[j]], axis=1))
        out = intra + in_f * inter[:, :HEAD_DIM] + in_b * inter[:, HEAD_DIM:]
        og = o_ref[0, rows, :].astype(F32)
        y_ref[0, rows, :] = _head_norm_gate(out, og * _sigmoid(og), nw).astype(BF16)
        return carry

    lax.fori_loop(0, n_lat, out_body, 0, unroll=unroll)


def _ret(pr, prc, dl, nw):
    b, t, _ = pr.shape
    tc = prc.shape[1]
    n_lat, n_ctx = t // CHUNK, tc // CHUNK
    H = N_HEADS
    blk = lambda off: pl.BlockSpec((1, t, HEAD_DIM), lambda i, h: (i, 0, off + h))
    cblk = lambda off: pl.BlockSpec((1, tc, HEAD_DIM), lambda i, h: (i, 0, off + h))
    return pl.pallas_call(
        functools.partial(_ret_kernel, n_lat=n_lat, n_ctx=n_ctx, unroll=MIXER_UNROLL),
        grid=(b, H),
        in_specs=[blk(0), blk(H), blk(2 * H), blk(3 * H), cblk(H), cblk(2 * H),
                  pl.BlockSpec((1, 2, LANES), lambda i, h: (h, 0, 0)),
                  pl.BlockSpec((1, 1, HEAD_DIM), lambda i, h: (h, 0, 0))],
        out_specs=pl.BlockSpec((1, t, HEAD_DIM), lambda i, h: (i, 0, h)),
        out_shape=jax.ShapeDtypeStruct((b, t, H * HEAD_DIM), BF16),
        scratch_shapes=[pltpu.VMEM((n_lat, HEAD_DIM, HEAD_DIM), BF16),
                        pltpu.VMEM((n_lat, HEAD_DIM, HEAD_DIM), BF16),
                        pltpu.VMEM((HEAD_DIM, HEAD_DIM), F32),
                        pltpu.VMEM((HEAD_DIM, HEAD_DIM), F32)],
        compiler_params=_params("arbitrary", "arbitrary"),
    )(pr, pr, pr, pr, prc, prc, dl, nw)


def _outproj_kernel(ym_ref, yr_ref, x_ref, wo_ref, g1_ref, nw_ref, sh_ref, sc_ref, wrt_ref,
                    xl_ref, h2_ref, aff_ref, *, n_experts, width):
    y = _dot(ym_ref[0], wo_ref[:width, :]) + _dot(yr_ref[0], wo_ref[width:, :])
    xl = x_ref[0] + g1_ref[0] * y
    xl_ref[0] = xl
    h2 = _rms_mod(xl, nw_ref[...], sh_ref[0], sc_ref[0])
    h2_ref[0] = h2.astype(BF16)
    h_hi = h2.astype(BF16)
    h_lo = (h2 - h_hi.astype(F32)).astype(BF16)
    p_hi = _dot(h_hi, wrt_ref[...])
    p_lo = _dot(h_lo, wrt_ref[...])
    logits = p_hi + pltpu.roll(p_hi, LANES - n_experts, 1) + p_lo
    lane = lax.broadcasted_iota(jnp.int32, logits.shape, 1)
    logits = jnp.where(lane < n_experts, logits, -jnp.inf)
    e = jnp.exp(logits - jnp.max(logits, axis=-1, keepdims=True))
    aff_ref[0] = e / jnp.sum(e, axis=-1, keepdims=True)


def _outproj(ym, yr, x, wo, g1, nw, sh, sc, wrt, n_experts):
    b, t, d = x.shape
    width = ym.shape[2]
    tm = _tile(t, 512)
    const = lambda i, j: (0, 0)
    per_b = lambda i, j: (i, 0, 0)
    tok = lambda i, j: (i, j, 0)
    return pl.pallas_call(
        functools.partial(_outproj_kernel, n_experts=n_experts, width=width),
        grid=(b, t // tm),
        in_specs=[pl.BlockSpec((1, tm, width), tok), pl.BlockSpec((1, tm, width), tok),
                  pl.BlockSpec((1, tm, d), tok), pl.BlockSpec(wo.shape, const),
                  pl.BlockSpec((1, 1, d), per_b), pl.BlockSpec((1, d), const),
                  pl.BlockSpec((1, 1, d), per_b), pl.BlockSpec((1, 1, d), per_b),
                  pl.BlockSpec(wrt.shape, const)],
        out_specs=[pl.BlockSpec((1, tm, d), tok), pl.BlockSpec((1, tm, d), tok),
                   pl.BlockSpec((1, tm, LANES), tok)],
        out_shape=[jax.ShapeDtypeStruct((b, t, d), F32), jax.ShapeDtypeStruct((b, t, d), BF16),
                   jax.ShapeDtypeStruct((b, t, LANES), F32)],
        compiler_params=_params("arbitrary", "arbitrary"),
    )(ym, yr, x, wo, g1, nw, sh, sc, wrt)


def _route_kernel(aff_ref, pos_ref, *, cap):
    a = aff_ref[0]
    n_e, n_b, _ = a.shape

    def count(mask):
        return jnp.sum(jnp.sum(mask.astype(jnp.int32), axis=2, keepdims=True), axis=1, keepdims=True)

    def search(i, thr):
        cand = thr | jnp.left_shift(jnp.int32(1), 30 - i)
        return jnp.where(count(a >= pltpu.bitcast(cand, F32)) >= cap, cand, thr)

    thr = lax.fori_loop(0, 31, search, jnp.zeros((n_e, 1, 1), jnp.int32))
    above = a >= pltpu.bitcast(thr + 1, F32)
    band = (a >= pltpu.bitcast(thr, F32)) & jnp.logical_not(above)
    need = cap - count(above)
    tok = (lax.broadcasted_iota(jnp.int32, a.shape, 1) * LANES
           + lax.broadcasted_iota(jnp.int32, a.shape, 2))

    def take_one(_, carry):
        sel, band, need = carry
        in_band = band > 0
        best = jnp.max(jnp.max(jnp.where(in_band, a, -1.0), axis=2, keepdims=True), axis=1, keepdims=True)
        cand = jnp.where(in_band & (a == best), tok, n_b * LANES)
        first = jnp.min(jnp.min(cand, axis=2, keepdims=True), axis=1, keepdims=True)
        take = ((tok == first) & (need > 0)).astype(jnp.int32)
        return sel | take, band - take, need - 1

    sel, _, _ = lax.fori_loop(0, jnp.max(need), take_one,
                              (above.astype(jnp.int32), band.astype(jnp.int32), need))
    sel = sel > 0

    rows = n_e * n_b
    u_i = lax.broadcasted_iota(jnp.int32, (LANES, LANES), 0)
    s_i = lax.broadcasted_iota(jnp.int32, (LANES, LANES), 1)
    incl = (u_i <= s_i).astype(BF16)
    ones = jnp.ones((LANES, LANES), BF16)
    r_i = lax.broadcasted_iota(jnp.int32, (rows, rows), 0)
    c_i = lax.broadcasted_iota(jnp.int32, (rows, rows), 1)
    before = ((r_i // n_b == c_i // n_b) & (c_i < r_i)).astype(BF16)

    x = sel.astype(F32).reshape(rows, LANES)
    xb = x.astype(BF16)
    block_tot = _dot(xb, ones).astype(BF16)
    pre = _dot(before, block_tot) + _dot(xb, incl) - x
    pos_ref[0] = jnp.where(sel, pre.astype(jnp.int32).reshape(n_e, n_b, LANES), -1)


def _route(aff_t, cap):
    b, n_e, n_b, _ = aff_t.shape
    spec = pl.BlockSpec((1, n_e, n_b, LANES), lambda i: (i, 0, 0, 0))
    return pl.pallas_call(
        functools.partial(_route_kernel, cap=cap),
        grid=(b,), in_specs=[spec], out_specs=spec,
        out_shape=jax.ShapeDtypeStruct(aff_t.shape, jnp.int32),
        compiler_params=_params("arbitrary"),
    )(aff_t)


def _gather_kernel(pos_ref, aff_ref, h_ref, xg_ref, gate_ref, *, cap, tk):
    t = h_ref.shape[1]
    slot = lax.broadcasted_iota(jnp.int32, (cap, tk), 0)
    acc = jnp.zeros((cap, h_ref.shape[2]), F32)
    gate = jnp.zeros((cap, 1), F32)
    for c in range(t // tk):
        cols = slice(c * tk, (c + 1) * tk)
        hit = pos_ref[0, 0, :, cols] == slot
        acc = acc + _dot(hit.astype(BF16), h_ref[0, cols, :])
        gate = gate + jnp.sum(jnp.where(hit, aff_ref[0, 0, :, cols], 0.0), axis=1, keepdims=True)
    xg_ref[0, 0] = acc.astype(BF16)
    gate_ref[0, 0] = gate


def _gather(pos_row, aff_row, h2, cap):
    b, n_e, _, t = pos_row.shape
    d = h2.shape[2]
    tk = _tile(t, 512)
    row = pl.BlockSpec((1, 1, 1, t), lambda i, e: (i, e, 0, 0))
    return pl.pallas_call(
        functools.partial(_gather_kernel, cap=cap, tk=tk),
        grid=(b, n_e),
        in_specs=[row, row, pl.BlockSpec((1, t, d), lambda i, e: (i, 0, 0))],
        out_specs=[pl.BlockSpec((1, 1, cap, d), lambda i, e: (e, i, 0, 0)),
                   pl.BlockSpec((1, 1, cap, 1), lambda i, e: (e, i, 0, 0))],
        out_shape=[jax.ShapeDtypeStruct((n_e, b, cap, d), BF16),
                   jax.ShapeDtypeStruct((n_e, b, cap, 1), F32)],
        compiler_params=_params("arbitrary", "arbitrary"),
    )(pos_row, aff_row, h2)


def _ffn_kernel(x_ref, gate_ref, wg_ref, wu_ref, wd_ref, y_ref, acc_ref):
    f = pl.program_id(1)

    @pl.when(f == 0)
    def _():
        acc_ref[...] = jnp.zeros_like(acc_ref)

    x = x_ref[0]
    a = _dot(x, wg_ref[0].astype(BF16))
    u = _dot(x, wu_ref[0].astype(BF16))
    mid = (a * _sigmoid(a) * u).astype(BF16)
    acc_ref[...] += _dot(mid, wd_ref[0].astype(BF16))

    @pl.when(f == pl.num_programs(1) - 1)
    def _():
        y_ref[0] = (acc_ref[...] * gate_ref[0]).astype(BF16)


def _ffn(xg, gate, wg, wu, wd):
    n_e, m, d = xg.shape
    ff = wg.shape[2]
    tf = _tile(ff, 256)
    return pl.pallas_call(
        _ffn_kernel,
        grid=(n_e, ff // tf),
        in_specs=[pl.BlockSpec((1, m, d), lambda e, f: (e, 0, 0)),
                  pl.BlockSpec((1, m, 1), lambda e, f: (e, 0, 0)),
                  pl.BlockSpec((1, d, tf), lambda e, f: (e, 0, f)),
                  pl.BlockSpec((1, d, tf), lambda e, f: (e, 0, f)),
                  pl.BlockSpec((1, tf, d), lambda e, f: (e, f, 0))],
        out_specs=pl.BlockSpec((1, m, d), lambda e, f: (e, 0, 0)),
        out_shape=jax.ShapeDtypeStruct((n_e, m, d), BF16),
        scratch_shapes=[pltpu.VMEM((m, d), F32)],
        compiler_params=_params("arbitrary", "arbitrary"),
    )(xg, gate, wg, wu, wd)


def _combine_kernel(y_ref, pos_ref, xl_ref, g2_ref, fw_ref, o_ref, *, n_experts, cap):
    tt = xl_ref.shape[1]
    slot = lax.broadcasted_iota(jnp.int32, (tt, cap), 1)
    pos = pos_ref[0]
    acc = jnp.zeros((tt, xl_ref.shape[2]), F32)
    for e in range(n_experts):
        hit = (pos[:, e:e + 1] == slot).astype(BF16)
        acc = acc + _dot(hit, y_ref[e, 0])
    xl = xl_ref[0] + g2_ref[0] * acc
    ms = jnp.mean(xl * xl, axis=-1, keepdims=True)
    o_ref[0] = xl * lax.rsqrt(ms + EPS) * fw_ref[...]


def _combine(y, pos_col, xl, g2, fw, cap):
    b, t, d = xl.shape
    n_e = y.shape[0]
    tt = _tile(t, 512)
    tok = lambda i, j: (i, j, 0)
    return pl.pallas_call(
        functools.partial(_combine_kernel, n_experts=n_e, cap=cap),
        grid=(b, t // tt),
        in_specs=[pl.BlockSpec((n_e, 1, cap, d), lambda i, j: (0, i, 0, 0)),
                  pl.BlockSpec((1, tt, n_e), tok),
                  pl.BlockSpec((1, tt, d), tok),
                  pl.BlockSpec((1, 1, d), lambda i, j: (i, 0, 0)),
                  pl.BlockSpec((1, d), lambda i, j: (0, 0))],
        out_specs=pl.BlockSpec((1, tt, d), tok),
        out_shape=jax.ShapeDtypeStruct((b, t, d), F32),
        compiler_params=_params("arbitrary", "arbitrary"),
    )(y, pos_col, xl, g2, fw)


def _rope_tables(t):
    n_freq = HEAD_DIM // 4
    rows = jnp.repeat(jnp.arange(t // GRID_W, dtype=F32), GRID_W)
    cols = jnp.tile(jnp.arange(GRID_W, dtype=F32), t // GRID_W)
    inv_freq = ROPE_BASE ** (-jnp.arange(n_freq, dtype=F32) / n_freq)
    ang_r, ang_c = rows[:, None] * inv_freq, cols[:, None] * inv_freq
    cos = jnp.concatenate([jnp.cos(ang_r)] * 2 + [jnp.cos(ang_c)] * 2, axis=-1)
    sin = jnp.concatenate([-jnp.sin(ang_r), jnp.sin(ang_r), -jnp.sin(ang_c), jnp.sin(ang_c)], axis=-1)
    return jnp.tile(cos, (1, N_HEADS)), jnp.tile(sin, (1, N_HEADS))


def _scan_order(lat, ctx, reverse):
    def chunks(a):
        b, h, t = a.shape[:3]
        a = jnp.moveaxis(a.reshape((b, h, t // CHUNK, CHUNK) + a.shape[3:]), 2, 0)
        return a[::-1] if reverse else a
    return jnp.concatenate([chunks(ctx), chunks(lat)], axis=0)


def _time_order(a, n_ctx, reverse):
    if not reverse:
        return a
    return jnp.concatenate([a[:n_ctx][::-1], a[n_ctx:][::-1]], axis=0)


def kernel(x, c, ctx, c_ctx, w_ada, b_ada, norm1_w, norm2_w, w_in, mlstm_gate_bias, ret_decay_logit,
           mlstm_norm_w, ret_norm_w, w_out, w_router, w_gate, w_up, w_down, final_norm_w):
    B, T, D = x.shape
    Tc = ctx.shape[1]
    H = N_HEADS
    width = H * HEAD_DIM
    assert D == 2 * width and w_ada.shape[0] == 1
    assert T % CHUNK == 0 and Tc % CHUNK == 0 and T % GRID_W == 0
    n_lat, n_ctx = T // CHUNK, Tc // CHUNK
    n_experts = w_router.shape[2]
    cap = EC_CAPACITY_FACTOR * T // n_experts
    n_gate = 4 * H

    pad = (-(B + 1)) % 8
    cc = jnp.concatenate([c, c_ctx[None], jnp.zeros((pad, D), F32)], axis=0)
    mod = _ada(cc, w_ada[0], b_ada[0])
    sh1, sc1, g1, sh2, sc2, g2 = [mod[:B, None, i * D:(i + 1) * D] for i in range(6)]
    csh1, csc1 = [jnp.broadcast_to(mod[B, i * D:(i + 1) * D], (B, 1, D)) for i in range(2)]

    w_in0 = w_in[0]
    wm = w_in0[:, :4 * width].astype(BF16)
    wg = jnp.pad(w_in0[:, 4 * width:4 * width + n_gate], ((0, 0), (0, LANES - n_gate))).astype(BF16)
    wr = w_in0[:, 4 * width + n_gate:].astype(BF16)
    gb = jnp.pad(mlstm_gate_bias[0], (0, LANES - n_gate)).reshape(1, LANES)
    nw1 = norm1_w[0].reshape(1, D)
    cos, sin = _rope_tables(T)
    pm, gl, pr = _inproj(x, nw1, sh1, sc1, wm, wg, wr, gb, cos, sin)
    pmc, gc, prc = _inproj(ctx, nw1, csh1, csc1, wm, wg, wr, gb,
                           jnp.ones((Tc, width), F32), jnp.zeros((Tc, width), F32))

    def gate_cols(g, lo):
        return jnp.swapaxes(g[:, :, lo:lo + H], 1, 2)
    def stream_rows(lo_f, lo_b):
        f = _scan_order(gate_cols(gl, lo_f), gate_cols(gc, lo_f), False)
        r = _scan_order(gate_cols(gl, lo_b), gate_cols(gc, lo_b), True)
        n = f.shape[0]
        return jnp.concatenate([f.reshape(n, B * H, CHUNK), r.reshape(n, B * H, CHUNK)], axis=1)
    stats = _gateprep(stream_rows(0, H), stream_rows(2 * H, 3 * H))
    n_all = n_ctx + n_lat
    st_f = stats[:, :, :B * H].reshape(n_all, 6, B, H, CHUNK)
    st_b = _time_order(stats[:, :, B * H:], n_ctx, True).reshape(n_all, 6, B, H, CHUNK)
    a_i, w_i, mm_i, en_i, dec_i, m_i = range(6)
    rowv = jnp.stack([st[:, i] for i in (a_i, dec_i, w_i, m_i) for st in (st_f, st_b)], axis=-2)
    rowv = jnp.transpose(rowv, (1, 2, 0, 3, 4))
    colv = jnp.stack([st[:, i] for i in (mm_i, en_i) for st in (st_f, st_b)], axis=-1)
    colv = jnp.transpose(colv, (1, 2, 0, 3, 4)).reshape(B, H, n_all * CHUNK, 4)

    ym = _mlstm(pm, pmc, rowv, colv, mlstm_norm_w[0].reshape(H, 1, HEAD_DIM))
    dl = jnp.broadcast_to(jnp.swapaxes(ret_decay_logit[0], 0, 1)[:, :, None], (H, 2, LANES))
    yr = _ret(pr, prc, dl, ret_norm_w[0].reshape(H, 1, HEAD_DIM))

    wr_hi = w_router[0].astype(BF16)
    wr_lo = (w_router[0] - wr_hi.astype(F32)).astype(BF16)
    wrt = jnp.pad(jnp.concatenate([wr_hi, wr_lo], axis=1), ((0, 0), (0, LANES - 2 * n_experts)))
    xl, h2, aff = _outproj(ym, yr, x, w_out[0].astype(BF16), g1, norm2_w[0].reshape(1, D), sh2, sc2,
                           wrt, n_experts)

    aff_t = jnp.swapaxes(aff[:, :, :n_experts], 1, 2)
    pos = _route(aff_t.reshape(B, n_experts, T // LANES, LANES), cap).reshape(B, n_experts, T)
    xg, gate = _gather(pos.reshape(B, n_experts, 1, T), aff_t.reshape(B, n_experts, 1, T), h2, cap)
    y = _ffn(xg.reshape(n_experts, B * cap, D), gate.reshape(n_experts, B * cap, 1),
             w_gate[0], w_up[0], w_down[0])
    return _combine(y.reshape(n_experts, B, cap, D), jnp.swapaxes(pos, 1, 2), xl, g2,
                    final_norm_w.reshape(1, D), cap)
```

```python
import functools

import jax
import jax.numpy as jnp
from jax import lax
from jax.experimental import pallas as pl
from jax.experimental.pallas import tpu as pltpu

F32 = jnp.float32
BF16 = jnp.bfloat16

CHUNK = 128
GRID_W = 64
N_HEADS = 4
HEAD_DIM = 128
ROPE_BASE = 10000.0
EPS = 1e-6
EC_CAPACITY_FACTOR = 2
LANES = 128
VMEM_LIMIT_BYTES = 56 * 1024 * 1024
MIXER_UNROLL = 8


def _params(*sem):
    return pltpu.CompilerParams(dimension_semantics=sem, vmem_limit_bytes=VMEM_LIMIT_BYTES)


def _dot(a, b):
    return jnp.dot(a, b, preferred_element_type=F32)


def _dot_nt(a, b):
    return lax.dot_general(a, b, (((1,), (1,)), ((), ())), preferred_element_type=F32)


def _dot_tn(a, b):
    return lax.dot_general(a, b, (((0,), (0,)), ((), ())), preferred_element_type=F32)


def _sigmoid(x):
    return 1.0 / (1.0 + jnp.exp(-x))


def _log_sigmoid(x):
    return jnp.minimum(x, 0.0) - jnp.log1p(jnp.exp(-jnp.abs(x)))


def _tile(n, pref):
    t = min(n, pref)
    assert n % t == 0, (n, t)
    return t


def _ada_kernel(c_ref, w_ref, b_ref, o_ref):
    cv = c_ref[...]
    s = (cv * _sigmoid(cv)).astype(BF16)
    o_ref[...] = _dot(s, w_ref[...].astype(BF16)) + b_ref[...]


def _ada(cc, w, b):
    rows, d = cc.shape
    n = w.shape[1]
    tn = _tile(n, 1024)
    return pl.pallas_call(
        _ada_kernel,
        grid=(n // tn,),
        in_specs=[pl.BlockSpec((rows, d), lambda j: (0, 0)),
                  pl.BlockSpec((d, tn), lambda j: (0, j)),
                  pl.BlockSpec((1, tn), lambda j: (0, j))],
        out_specs=pl.BlockSpec((rows, tn), lambda j: (0, j)),
        out_shape=jax.ShapeDtypeStruct((rows, n), F32),
        compiler_params=_params("arbitrary"),
    )(cc, w, b.reshape(1, n))


def _rms_mod(x, nw, sh, sc):
    ms = jnp.mean(x * x, axis=-1, keepdims=True)
    return (x * lax.rsqrt(ms + EPS) * nw) * (1.0 + sc) + sh


def _inproj_kernel(x_ref, nw_ref, sh_ref, sc_ref, wm_ref, wg_ref, wr_ref, gb_ref, cos_ref, sin_ref,
                   pm_ref, g_ref, pr_ref, *, width, k_scale):
    hb = _rms_mod(x_ref[0], nw_ref[...], sh_ref[0], sc_ref[0]).astype(BF16)
    for j in range(4):
        cols = slice(j * width, (j + 1) * width)
        acc = _dot(hb, wm_ref[:, cols])
        if j == 1:
            acc = acc * k_scale
        pm_ref[0, :, cols] = acc.astype(BF16)
    g_ref[0] = _dot(hb, wg_ref[...]) + gb_ref[...]
    cos = cos_ref[...]
    sin = sin_ref[...]
    lane = lax.broadcasted_iota(jnp.int32, cos.shape, 1)
    even = ((lane // 32) % 2) == 0
    for j in range(4):
        cols = slice(j * width, (j + 1) * width)
        acc = _dot(hb, wr_ref[:, cols])
        if j == 1:
            acc = acc * k_scale
        if j < 2:
            swapped = jnp.where(even, pltpu.roll(acc, width - 32, 1), pltpu.roll(acc, 32, 1))
            acc = acc * cos + swapped * sin
        pr_ref[0, :, cols] = acc.astype(BF16)


def _inproj(x, nw, sh, sc, wm, wg, wr, gb, cos, sin):
    b, t, d = x.shape
    width = wm.shape[1] // 4
    tm = _tile(t, 512)
    kern = functools.partial(_inproj_kernel, width=width, k_scale=HEAD_DIM ** -0.5)
    const = lambda i, j: (0, 0)
    return pl.pallas_call(
        kern,
        grid=(b, t // tm),
        in_specs=[pl.BlockSpec((1, tm, d), lambda i, j: (i, j, 0)),
                  pl.BlockSpec((1, d), const),
                  pl.BlockSpec((1, 1, d), lambda i, j: (i, 0, 0)),
                  pl.BlockSpec((1, 1, d), lambda i, j: (i, 0, 0)),
                  pl.BlockSpec(wm.shape, const),
                  pl.BlockSpec(wg.shape, const),
                  pl.BlockSpec(wr.shape, const),
                  pl.BlockSpec((1, LANES), const),
                  pl.BlockSpec((tm, width), lambda i, j: (j, 0)),
                  pl.BlockSpec((tm, width), lambda i, j: (j, 0))],
        out_specs=[pl.BlockSpec((1, tm, 4 * width), lambda i, j: (i, j, 0)),
                   pl.BlockSpec((1, tm, LANES), lambda i, j: (i, j, 0)),
                   pl.BlockSpec((1, tm, 4 * width), lambda i, j: (i, j, 0))],
        out_shape=[jax.ShapeDtypeStruct((b, t, 4 * width), BF16),
                   jax.ShapeDtypeStruct((b, t, LANES), F32),
                   jax.ShapeDtypeStruct((b, t, 4 * width), BF16)],
        compiler_params=_params("arbitrary", "arbitrary"),
    )(x, nw, sh, sc, wm, wg, wr, gb, cos, sin)


def _scan_lanes(x, op, fill, reverse):
    lane = lax.broadcasted_iota(jnp.int32, x.shape, 1)
    sh = 1
    while sh < LANES:
        if reverse:
            x = op(x, jnp.where(lane < LANES - sh, pltpu.roll(x, LANES - sh, 1), fill))
        else:
            x = op(x, jnp.where(lane >= sh, pltpu.roll(x, sh, 1), fill))
        sh *= 2
    return x


def _gateprep_kernel(ig_ref, fg_ref, o_ref, *, n_chunks):
    rows = ig_ref.shape[1]
    half = rows // 2
    row = lax.broadcasted_iota(jnp.int32, (rows, LANES), 0)
    lane = lax.broadcasted_iota(jnp.int32, (rows, LANES), 1)
    is_fwd = row < half
    last = lane == jnp.where(is_fwd, LANES - 1, 0)

    def body(c, m):
        lf = _log_sigmoid(fg_ref[c])
        ig = ig_ref[c]
        b = jnp.where(is_fwd, _scan_lanes(lf, jnp.add, 0.0, False), _scan_lanes(lf, jnp.add, 0.0, True))
        b_tot = jnp.sum(jnp.where(last, b, 0.0), axis=1, keepdims=True)
        a = ig - b
        m_new = jnp.maximum(b_tot + m, jnp.max(b_tot + a, axis=1, keepdims=True))
        w = jnp.exp(b_tot + a - m_new)
        decay = jnp.exp(b_tot + m - m_new)
        run = jnp.where(is_fwd, _scan_lanes(a, jnp.maximum, -jnp.inf, False),
                        _scan_lanes(a, jnp.maximum, -jnp.inf, True))
        mm = jnp.maximum(m, run)
        o_ref[c, 0] = a
        o_ref[c, 1] = w
        o_ref[c, 2] = mm
        o_ref[c, 3] = jnp.exp(-(b + mm))
        o_ref[c, 4] = jnp.broadcast_to(decay, (rows, LANES))
        o_ref[c, 5] = jnp.broadcast_to(m, (rows, LANES))
        return m_new

    lax.fori_loop(0, n_chunks, body, jnp.zeros((rows, 1), F32))


def _gateprep(ig, fg):
    n_chunks, rows, _ = ig.shape
    return pl.pallas_call(
        functools.partial(_gateprep_kernel, n_chunks=n_chunks),
        out_shape=jax.ShapeDtypeStruct((n_chunks, 6, rows, LANES), F32),
        compiler_params=pltpu.CompilerParams(vmem_limit_bytes=VMEM_LIMIT_BYTES),
    )(ig, fg)


def _head_norm_gate(h, gate, nw):
    return gate * (h * lax.rsqrt(jnp.mean(h * h, axis=-1, keepdims=True) + EPS)) * nw


def _mlstm_kernel(q_ref, k_ref, v_ref, o_ref, kc_ref, vc_ref, rowv_ref, colv_ref, nw_ref, y_ref,
                  sf_ref, sb_ref, cf_ref, cb_ref, *, n_lat, n_ctx, unroll):
    L = CHUNK
    ones = jnp.ones((L, HEAD_DIM), BF16)

    def update(state_ref, k, v, w_row, decay_row):
        wkt = (k.astype(F32).T * w_row).astype(BF16)
        u = _dot(wkt, jnp.concatenate([v, ones], axis=1))
        dec = jnp.concatenate([decay_row, decay_row], axis=1)
        state_ref[...] = dec * state_ref[...] + u

    cf_ref[...] = jnp.zeros_like(cf_ref)
    cb_ref[...] = jnp.zeros_like(cb_ref)
    for c in range(n_ctx):
        rows = slice(c * L, (c + 1) * L)
        update(cf_ref, kc_ref[0, rows, :], vc_ref[0, rows, :],
               rowv_ref[0, 0, c, 4:5, :], rowv_ref[0, 0, c, 2:3, :])
    for c in reversed(range(n_ctx)):
        rows = slice(c * L, (c + 1) * L)
        update(cb_ref, kc_ref[0, rows, :], vc_ref[0, rows, :],
               rowv_ref[0, 0, c, 5:6, :], rowv_ref[0, 0, c, 3:4, :])

    def state_body(i, carry):
        jf = i
        jb = n_lat - 1 - i
        rf = pl.ds(pl.multiple_of(jf * L, L), L)
        rb = pl.ds(pl.multiple_of(jb * L, L), L)
        sf_ref[jf] = cf_ref[...].astype(BF16)
        sb_ref[jb] = cb_ref[...].astype(BF16)
        update(cf_ref, k_ref[0, rf, :], v_ref[0, rf, :],
               rowv_ref[0, 0, n_ctx + jf, 4:5, :], rowv_ref[0, 0, n_ctx + jf, 2:3, :])
        update(cb_ref, k_ref[0, rb, :], v_ref[0, rb, :],
               rowv_ref[0, 0, n_ctx + jb, 5:6, :], rowv_ref[0, 0, n_ctx + jb, 3:4, :])
        return carry

    lax.fori_loop(0, n_lat, state_body, 0, unroll=unroll)

    r_i = lax.broadcasted_iota(jnp.int32, (L, L), 0)
    c_i = lax.broadcasted_iota(jnp.int32, (L, L), 1)
    tril = c_i <= r_i
    triu = c_i >= r_i
    nw = nw_ref[0]

    def out_body(j, carry):
        rows = pl.ds(pl.multiple_of(j * L, L), L)
        crow = pl.ds(pl.multiple_of((n_ctx + j) * L, L), L)
        q = q_ref[0, rows, :]
        k = k_ref[0, rows, :]
        v = v_ref[0, rows, :]
        rv = rowv_ref[0, 0, n_ctx + j]
        cv = colv_ref[0, 0, crow, :]
        mm_f = jnp.broadcast_to(cv[:, 0:1], (L, L))
        mm_b = jnp.broadcast_to(cv[:, 1:2], (L, L))
        s = _dot_nt(q, k)
        d_f = jnp.where(tril, jnp.exp(rv[0:1, :] - mm_f), 0.0)
        d_b = jnp.where(triu, jnp.exp(rv[1:2, :] - mm_b), 0.0)
        p = jnp.concatenate([s * d_f, s * d_b], axis=0).astype(BF16)
        tot = _dot(p, jnp.concatenate([v, ones], axis=1))
        inter = _dot(q, jnp.concatenate([sf_ref[j], sb_ref[j]], axis=1))
        iw_f = jnp.exp(rv[6:7, :] - mm_f)
        iw_b = jnp.exp(rv[7:8, :] - mm_b)
        D, W = HEAD_DIM, HEAD_DIM + LANES
        num_f = tot[:L, :D] + iw_f * inter[:, :D]
        den_f = tot[:L, D:] + iw_f * inter[:, D:W]
        num_b = tot[L:, :D] + iw_b * inter[:, W:W + D]
        den_b = tot[L:, D:] + iw_b * inter[:, W + D:]
        h = (num_f / jnp.maximum(jnp.abs(den_f), cv[:, 2:3])
             + num_b / jnp.maximum(jnp.abs(den_b), cv[:, 3:4]))
        gate = _sigmoid(o_ref[0, rows, :].astype(F32))
        y_ref[0, rows, :] = _head_norm_gate(h, gate, nw).astype(BF16)
        return carry

    lax.fori_loop(0, n_lat, out_body, 0, unroll=unroll)


def _mlstm(pm, pmc, rowv, colv, nw):
    b, t, _ = pm.shape
    tc = pmc.shape[1]
    n_lat, n_ctx = t // CHUNK, tc // CHUNK
    H = N_HEADS
    blk = lambda off: pl.BlockSpec((1, t, HEAD_DIM), lambda i, h: (i, 0, off + h))
    cblk = lambda off: pl.BlockSpec((1, tc, HEAD_DIM), lambda i, h: (i, 0, off + h))
    return pl.pallas_call(
        functools.partial(_mlstm_kernel, n_lat=n_lat, n_ctx=n_ctx, unroll=MIXER_UNROLL),
        grid=(b, H),
        in_specs=[blk(0), blk(H), blk(2 * H), blk(3 * H), cblk(H), cblk(2 * H),
                  pl.BlockSpec((1, 1, n_ctx + n_lat, 8, LANES), lambda i, h: (i, h, 0, 0, 0)),
                  pl.BlockSpec((1, 1, tc + t, 4), lambda i, h: (i, h, 0, 0)),
                  pl.BlockSpec((1, 1, HEAD_DIM), lambda i, h: (h, 0, 0))],
        out_specs=pl.BlockSpec((1, t, HEAD_DIM), lambda i, h: (i, 0, h)),
        out_shape=jax.ShapeDtypeStruct((b, t, H * HEAD_DIM), BF16),
        scratch_shapes=[pltpu.VMEM((n_lat, HEAD_DIM, HEAD_DIM + LANES), BF16),
                        pltpu.VMEM((n_lat, HEAD_DIM, HEAD_DIM + LANES), BF16),
                        pltpu.VMEM((HEAD_DIM, HEAD_DIM + LANES), F32),
                        pltpu.VMEM((HEAD_DIM, HEAD_DIM + LANES), F32)],
        compiler_params=_params("arbitrary", "arbitrary"),
    )(pm, pm, pm, pm, pmc, pmc, rowv, colv, nw)


def _ret_kernel(q_ref, k_ref, v_ref, o_ref, kc_ref, vc_ref, dl_ref, nw_ref, y_ref,
                sf_ref, sb_ref, rf_ref, rb_ref, *, n_lat, n_ctx, unroll):
    L = CHUNK
    lg = _log_sigmoid(dl_ref[0])
    lg_f, lg_b = lg[0:1, :], lg[1:2, :]
    r_i = lax.broadcasted_iota(jnp.int32, (L, L), 0)
    c_i = lax.broadcasted_iota(jnp.int32, (L, L), 1)
    r_f = r_i.astype(F32)
    rel = (r_i - c_i).astype(F32)
    kw_f = jnp.exp((L - 1.0 - r_f) * lg_f)
    kw_b = jnp.exp(r_f * lg_b)
    in_f = jnp.exp((r_f + 1.0) * lg_f)
    in_b = jnp.exp((L - r_f) * lg_b)
    dch_f = jnp.exp(L * lg_f)
    dch_b = jnp.exp(L * lg_b)
    d_mat = (jnp.where(rel >= 0, jnp.exp(jnp.maximum(rel, 0.0) * lg_f), 0.0)
             + jnp.where(rel <= 0, jnp.exp(jnp.maximum(-rel, 0.0) * lg_b), 0.0))

    def update(state_ref, k, v, kw, dch):
        wk = (k.astype(F32) * kw).astype(BF16)
        state_ref[...] = dch * state_ref[...] + _dot_tn(wk, v)

    rf_ref[...] = jnp.zeros_like(rf_ref)
    rb_ref[...] = jnp.zeros_like(rb_ref)
    for c in range(n_ctx):
        rows = slice(c * L, (c + 1) * L)
        update(rf_ref, kc_ref[0, rows, :], vc_ref[0, rows, :], kw_f, dch_f)
    for c in reversed(range(n_ctx)):
        rows = slice(c * L, (c + 1) * L)
        update(rb_ref, kc_ref[0, rows, :], vc_ref[0, rows, :], kw_b, dch_b)

    def state_body(i, carry):
        jb = n_lat - 1 - i
        rowf = pl.ds(pl.multiple_of(i * L, L), L)
        rowb = pl.ds(pl.multiple_of(jb * L, L), L)
        sf_ref[i] = rf_ref[...].astype(BF16)
        sb_ref[jb] = rb_ref[...].astype(BF16)
        update(rf_ref, k_ref[0, rowf, :], v_ref[0, rowf, :], kw_f, dch_f)
        update(rb_ref, k_ref[0, rowb, :], v_ref[0, rowb, :], kw_b, dch_b)
        return carry

    lax.fori_loop(0, n_lat, state_body, 0, unroll=unroll)
    nw = nw_ref[0]

    def out_body(j, carry):
        rows = pl.ds(pl.multiple_of(j * L, L), L)
        q = q_ref[0, rows, :]
        s = _dot_nt(q, k_ref[0, rows, :])
        intra = _dot((s * d_mat).astype(BF16), v_ref[0, rows, :])
        inter = _dot(q, jnp.concatenate([sf_ref[j], sb_ref[j]], axis=1))
        out = intra + in_f * inter[:, :HEAD_DIM] + in_b * inter[:, HEAD_DIM:]
        og = o_ref[0, rows, :].astype(F32)
        y_ref[0, rows, :] = _head_norm_gate(out, og * _sigmoid(og), nw).astype(BF16)
        return carry

    lax.fori_loop(0, n_lat, out_body, 0, unroll=unroll)


def _ret(pr, prc, dl, nw):
    b, t, _ = pr.shape
    tc = prc.shape[1]
    n_lat, n_ctx = t // CHUNK, tc // CHUNK
    H = N_HEADS
    blk = lambda off: pl.BlockSpec((1, t, HEAD_DIM), lambda i, h: (i, 0, off + h))
    cblk = lambda off: pl.BlockSpec((1, tc, HEAD_DIM), lambda i, h: (i, 0, off + h))
    return pl.pallas_call(
        functools.partial(_ret_kernel, n_lat=n_lat, n_ctx=n_ctx, unroll=MIXER_UNROLL),
        grid=(b, H),
        in_specs=[blk(0), blk(H), blk(2 * H), blk(3 * H), cblk(H), cblk(2 * H),
                  pl.BlockSpec((1, 2, LANES), lambda i, h: (h, 0, 0)),
                  pl.BlockSpec((1, 1, HEAD_DIM), lambda i, h: (h, 0, 0))],
        out_specs=pl.BlockSpec((1, t, HEAD_DIM), lambda i, h: (i, 0, h)),
        out_shape=jax.ShapeDtypeStruct((b, t, H * HEAD_DIM), BF16),
        scratch_shapes=[pltpu.VMEM((n_lat, HEAD_DIM, HEAD_DIM), BF16),
                        pltpu.VMEM((n_lat, HEAD_DIM, HEAD_DIM), BF16),
                        pltpu.VMEM((HEAD_DIM, HEAD_DIM), F32),
                        pltpu.VMEM((HEAD_DIM, HEAD_DIM), F32)],
        compiler_params=_params("arbitrary", "arbitrary"),
    )(pr, pr, pr, pr, prc, prc, dl, nw)


def _outproj_kernel(ym_ref, yr_ref, x_ref, wo_ref, g1_ref, nw_ref, sh_ref, sc_ref, wrt_ref,
                    xl_ref, h2_ref, aff_ref, *, n_experts, width):
    y = _dot(ym_ref[0], wo_ref[:width, :]) + _dot(yr_ref[0], wo_ref[width:, :])
    xl = x_ref[0] + g1_ref[0] * y
    xl_ref[0] = xl
    h2 = _rms_mod(xl, nw_ref[...], sh_ref[0], sc_ref[0])
    h2_ref[0] = h2
    h_hi = h2.astype(BF16)
    h_lo = (h2 - h_hi.astype(F32)).astype(BF16)
    p_hi = _dot(h_hi, wrt_ref[...])
    p_lo = _dot(h_lo, wrt_ref[...])
    logits = p_hi + pltpu.roll(p_hi, LANES - n_experts, 1) + p_lo
    lane = lax.broadcasted_iota(jnp.int32, logits.shape, 1)
    logits = jnp.where(lane < n_experts, logits, -jnp.inf)
    e = jnp.exp(logits - jnp.max(logits, axis=-1, keepdims=True))
    aff_ref[0] = e / jnp.sum(e, axis=-1, keepdims=True)


def _outproj(ym, yr, x, wo, g1, nw, sh, sc, wrt, n_experts):
    b, t, d = x.shape
    width = ym.shape[2]
    tm = _tile(t, 512)
    const = lambda i, j: (0, 0)
    per_b = lambda i, j: (i, 0, 0)
    tok = lambda i, j: (i, j, 0)
    return pl.pallas_call(
        functools.partial(_outproj_kernel, n_experts=n_experts, width=width),
        grid=(b, t // tm),
        in_specs=[pl.BlockSpec((1, tm, width), tok), pl.BlockSpec((1, tm, width), tok),
                  pl.BlockSpec((1, tm, d), tok), pl.BlockSpec(wo.shape, const),
                  pl.BlockSpec((1, 1, d), per_b), pl.BlockSpec((1, d), const),
                  pl.BlockSpec((1, 1, d), per_b), pl.BlockSpec((1, 1, d), per_b),
                  pl.BlockSpec(wrt.shape, const)],
        out_specs=[pl.BlockSpec((1, tm, d), tok), pl.BlockSpec((1, tm, d), tok),
                   pl.BlockSpec((1, tm, LANES), tok)],
        out_shape=[jax.ShapeDtypeStruct((b, t, d), F32), jax.ShapeDtypeStruct((b, t, d), F32),
                   jax.ShapeDtypeStruct((b, t, LANES), F32)],
        compiler_params=_params("arbitrary", "arbitrary"),
    )(ym, yr, x, wo, g1, nw, sh, sc, wrt)


def _route_kernel(aff_ref, pos_ref, idx_ref, *, cap):
    a = aff_ref[0]
    n_e, n_b, _ = a.shape

    def count(mask):
        return jnp.sum(jnp.sum(mask.astype(jnp.int32), axis=2, keepdims=True), axis=1, keepdims=True)

    def search(i, thr):
        cand = thr | jnp.left_shift(jnp.int32(1), 30 - i)
        return jnp.where(count(a >= pltpu.bitcast(cand, F32)) >= cap, cand, thr)

    thr = lax.fori_loop(0, 31, search, jnp.zeros((n_e, 1, 1), jnp.int32))
    above = a >= pltpu.bitcast(thr + 1, F32)
    band = (a >= pltpu.bitcast(thr, F32)) & jnp.logical_not(above)
    need = cap - count(above)
    tok = (lax.broadcasted_iota(jnp.int32, a.shape, 1) * LANES
           + lax.broadcasted_iota(jnp.int32, a.shape, 2))

    def take_one(_, carry):
        sel, band, need = carry
        in_band = band > 0
        best = jnp.max(jnp.max(jnp.where(in_band, a, -1.0), axis=2, keepdims=True), axis=1, keepdims=True)
        cand = jnp.where(in_band & (a == best), tok, n_b * LANES)
        first = jnp.min(jnp.min(cand, axis=2, keepdims=True), axis=1, keepdims=True)
        take = ((tok == first) & (need > 0)).astype(jnp.int32)
        return sel | take, band - take, need - 1

    sel, _, _ = lax.fori_loop(0, jnp.max(need), take_one,
                              (above.astype(jnp.int32), band.astype(jnp.int32), need))
    sel = sel > 0

    rows = n_e * n_b
    u_i = lax.broadcasted_iota(jnp.int32, (LANES, LANES), 0)
    s_i = lax.broadcasted_iota(jnp.int32, (LANES, LANES), 1)
    incl = (u_i <= s_i).astype(BF16)
    ones = jnp.ones((LANES, LANES), BF16)
    r_i = lax.broadcasted_iota(jnp.int32, (rows, rows), 0)
    c_i = lax.broadcasted_iota(jnp.int32, (rows, rows), 1)
    before = ((r_i // n_b == c_i // n_b) & (c_i < r_i)).astype(BF16)

    x = sel.astype(F32).reshape(rows, LANES)
    xb = x.astype(BF16)
    block_tot = _dot(xb, ones).astype(BF16)
    start = _dot(before, block_tot)
    pos = jnp.where(x > 0, start + _dot(xb, incl) - x, -1.0).astype(jnp.int32)
    pos_ref[0] = pos.reshape(n_e, n_b, LANES)

    pad = (-rows) % LANES
    def pad_rows(m):
        return m if pad == 0 else jnp.concatenate([m, jnp.zeros((pad, LANES), m.dtype)], axis=0)
    start_p = pad_rows(start)
    end_p = pad_rows(start + block_tot.astype(F32))
    pos_p = pad_rows(pos)
    pos_p = jnp.where(pos_p < 0, 1023, pos_p)
    per_group = LANES // n_b
    j_col = lax.broadcasted_iota(jnp.int32, (cap, LANES), 0).astype(F32)
    lane = lax.broadcasted_iota(jnp.int32, (cap, LANES), 1)
    lane_f = lane.astype(F32)
    block_f = (lane % n_b).astype(F32)
    row0 = pl.program_id(0) * (n_b * LANES)
    for g in range((rows + pad) // LANES):
        rs = slice(g * LANES, (g + 1) * LANES)
        s_row = start_p[rs].T[0:1, :]
        e_row = end_p[rs].T[0:1, :]
        hi = (pos_p[rs] >> 4).astype(F32).astype(BF16)
        lo = (pos_p[rs] & 15).astype(F32).astype(BF16)
        for el in range(per_group):
            e = g * per_group + el
            if e >= n_e:
                break
            mine = (j_col >= s_row) & (j_col < e_row) & ((lane // n_b) == el)
            mine_b = mine.astype(F32).astype(BF16)
            slots = 16.0 * _dot(mine_b, hi) + _dot(mine_b, lo)
            in_block = jnp.sum(jnp.where(slots == j_col, lane_f, 0.0), axis=1, keepdims=True)
            block = jnp.sum(jnp.where(mine, block_f, 0.0), axis=1, keepdims=True)
            idx_ref[0, e] = (block * LANES + in_block).astype(jnp.int32) + row0


def _route(aff_t, cap):
    b, n_e, n_b, _ = aff_t.shape
    spec = pl.BlockSpec((1, n_e, n_b, LANES), lambda i: (i, 0, 0, 0))
    return pl.pallas_call(
        functools.partial(_route_kernel, cap=cap),
        grid=(b,), in_specs=[spec],
        out_specs=[spec, pl.BlockSpec((1, n_e, cap, 1), lambda i: (i, 0, 0, 0))],
        out_shape=[jax.ShapeDtypeStruct(aff_t.shape, jnp.int32),
                   jax.ShapeDtypeStruct((b, n_e, cap, 1), jnp.int32)],
        compiler_params=_params("arbitrary"),
    )(aff_t)


def _ffn_kernel(idx_ref, h_hbm, wg_ref, wu_ref, wd_ref, y_ref, stage_ref, x_ref, acc_ref, sem,
                *, m, per_step):
    e, f = pl.program_id(0), pl.program_id(1)
    n_e, n_f = pl.num_programs(0), pl.num_programs(1)
    rows_per_expert = stage_ref.shape[0]

    def row_copy(base, j):
        return pltpu.make_async_copy(h_hbm.at[pl.ds(idx_ref[base + j], 1), :],
                                     stage_ref.at[pl.ds(j, 1), :], sem.at[0])

    def wait_rows():
        pltpu.make_async_copy(h_hbm.at[pl.ds(0, rows_per_expert), :], stage_ref, sem.at[0]).wait()

    @pl.when((e == 0) & (f == 0))
    def _():
        def start_row(j, carry):
            row_copy(0, j).start()
            return carry
        lax.fori_loop(0, rows_per_expert, start_row, 0)

    @pl.when(f == 0)
    def _():
        wait_rows()
        x_ref[...] = stage_ref[:m, :].astype(BF16)
        acc_ref[...] = jnp.zeros_like(acc_ref)

    nxt = jnp.minimum(e + 1, n_e - 1) * rows_per_expert
    for i in range(per_step):
        row_copy(nxt, f * per_step + i).start()

    x = x_ref[...]
    a = _dot(x, wg_ref[0].astype(BF16))
    u = _dot(x, wu_ref[0].astype(BF16))
    mid = (a * _sigmoid(a) * u).astype(BF16)
    acc_ref[...] += _dot(mid, wd_ref[0].astype(BF16))

    @pl.when(f == n_f - 1)
    def _():
        y_ref[0] = acc_ref[...].astype(BF16)

    @pl.when((e == n_e - 1) & (f == n_f - 1))
    def _():
        wait_rows()


def _ffn(idx, h_rows, wg, wu, wd, m):
    n_e = idx.shape[0]
    d = h_rows.shape[1]
    ff = wg.shape[2]
    tf = _tile(ff, 256)
    n_f = ff // tf
    per_step = -(-m // (8 * n_f)) * 8
    idx = jnp.pad(idx, ((0, 0), (0, n_f * per_step - m))).reshape(-1)
    return pl.pallas_call(
        functools.partial(_ffn_kernel, m=m, per_step=per_step),
        grid_spec=pltpu.PrefetchScalarGridSpec(
            num_scalar_prefetch=1,
            grid=(n_e, n_f),
            in_specs=[pl.BlockSpec(memory_space=pl.ANY),
                      pl.BlockSpec((1, d, tf), lambda e, f, idx: (e, 0, f)),
                      pl.BlockSpec((1, d, tf), lambda e, f, idx: (e, 0, f)),
                      pl.BlockSpec((1, tf, d), lambda e, f, idx: (e, f, 0))],
            out_specs=pl.BlockSpec((1, m, d), lambda e, f, idx: (e, 0, 0)),
            scratch_shapes=[pltpu.VMEM((n_f * per_step, d), F32),
                            pltpu.VMEM((m, d), BF16),
                            pltpu.VMEM((m, d), F32),
                            pltpu.SemaphoreType.DMA((1,))]),
        out_shape=jax.ShapeDtypeStruct((n_e, m, d), BF16),
        compiler_params=_params("arbitrary", "arbitrary"),
    )(idx, h_rows, wg, wu, wd)


def _combine_kernel(y_ref, pos_ref, aff_ref, xl_ref, g2_ref, fw_ref, o_ref, *, n_experts, cap):
    tt = xl_ref.shape[1]
    slot = lax.broadcasted_iota(jnp.int32, (tt, cap), 1)
    pos = pos_ref[0]
    aff = aff_ref[0]
    acc = jnp.zeros((tt, xl_ref.shape[2]), F32)
    for e in range(n_experts):
        hit = (pos[:, e:e + 1] == slot).astype(BF16)
        acc = acc + aff[:, e:e + 1] * _dot(hit, y_ref[e, 0])
    xl = xl_ref[0] + g2_ref[0] * acc
    ms = jnp.mean(xl * xl, axis=-1, keepdims=True)
    o_ref[0] = xl * lax.rsqrt(ms + EPS) * fw_ref[...]


def _combine(y, pos_col, aff, xl, g2, fw, cap):
    b, t, d = xl.shape
    n_e = y.shape[0]
    tt = _tile(t, 512)
    tok = lambda i, j: (i, j, 0)
    return pl.pallas_call(
        functools.partial(_combine_kernel, n_experts=n_e, cap=cap),
        grid=(b, t // tt),
        in_specs=[pl.BlockSpec((n_e, 1, cap, d), lambda i, j: (0, i, 0, 0)),
                  pl.BlockSpec((1, tt, n_e), tok),
                  pl.BlockSpec((1, tt, LANES), tok),
                  pl.BlockSpec((1, tt, d), tok),
                  pl.BlockSpec((1, 1, d), lambda i, j: (i, 0, 0)),
                  pl.BlockSpec((1, d), lambda i, j: (0, 0))],
        out_specs=pl.BlockSpec((1, tt, d), tok),
        out_shape=jax.ShapeDtypeStruct((b, t, d), F32),
        compiler_params=_params("arbitrary", "arbitrary"),
    )(y, pos_col, aff, xl, g2, fw)


def _rope_tables(t):
    n_freq = HEAD_DIM // 4
    rows = jnp.repeat(jnp.arange(t // GRID_W, dtype=F32), GRID_W)
    cols = jnp.tile(jnp.arange(GRID_W, dtype=F32), t // GRID_W)
    inv_freq = ROPE_BASE ** (-jnp.arange(n_freq, dtype=F32) / n_freq)
    ang_r, ang_c = rows[:, None] * inv_freq, cols[:, None] * inv_freq
    cos = jnp.concatenate([jnp.cos(ang_r)] * 2 + [jnp.cos(ang_c)] * 2, axis=-1)
    sin = jnp.concatenate([-jnp.sin(ang_r), jnp.sin(ang_r), -jnp.sin(ang_c), jnp.sin(ang_c)], axis=-1)
    return jnp.tile(cos, (1, N_HEADS)), jnp.tile(sin, (1, N_HEADS))


def _scan_order(lat, ctx, reverse):
    def chunks(a):
        b, h, t = a.shape[:3]
        a = jnp.moveaxis(a.reshape((b, h, t // CHUNK, CHUNK) + a.shape[3:]), 2, 0)
        return a[::-1] if reverse else a
    return jnp.concatenate([chunks(ctx), chunks(lat)], axis=0)


def _time_order(a, n_ctx, reverse):
    if not reverse:
        return a
    return jnp.concatenate([a[:n_ctx][::-1], a[n_ctx:][::-1]], axis=0)


def kernel(x, c, ctx, c_ctx, w_ada, b_ada, norm1_w, norm2_w, w_in, mlstm_gate_bias, ret_decay_logit,
           mlstm_norm_w, ret_norm_w, w_out, w_router, w_gate, w_up, w_down, final_norm_w):
    B, T, D = x.shape
    Tc = ctx.shape[1]
    H = N_HEADS
    width = H * HEAD_DIM
    assert D == 2 * width and w_ada.shape[0] == 1
    assert T % CHUNK == 0 and Tc % CHUNK == 0 and T % GRID_W == 0
    n_lat, n_ctx = T // CHUNK, Tc // CHUNK
    n_experts = w_router.shape[2]
    cap = EC_CAPACITY_FACTOR * T // n_experts
    n_gate = 4 * H

    pad = (-(B + 1)) % 8
    cc = jnp.concatenate([c, c_ctx[None], jnp.zeros((pad, D), F32)], axis=0)
    mod = _ada(cc, w_ada[0], b_ada[0])
    sh1, sc1, g1, sh2, sc2, g2 = [mod[:B, None, i * D:(i + 1) * D] for i in range(6)]
    csh1, csc1 = [jnp.broadcast_to(mod[B, i * D:(i + 1) * D], (B, 1, D)) for i in range(2)]

    w_in0 = w_in[0]
    wm = w_in0[:, :4 * width].astype(BF16)
    wg = jnp.pad(w_in0[:, 4 * width:4 * width + n_gate], ((0, 0), (0, LANES - n_gate))).astype(BF16)
    wr = w_in0[:, 4 * width + n_gate:].astype(BF16)
    gb = jnp.pad(mlstm_gate_bias[0], (0, LANES - n_gate)).reshape(1, LANES)
    nw1 = norm1_w[0].reshape(1, D)
    cos, sin = _rope_tables(T)
    pm, gl, pr = _inproj(x, nw1, sh1, sc1, wm, wg, wr, gb, cos, sin)
    pmc, gc, prc = _inproj(ctx, nw1, csh1, csc1, wm, wg, wr, gb,
                           jnp.ones((Tc, width), F32), jnp.zeros((Tc, width), F32))

    def gate_cols(g, lo):
        return jnp.swapaxes(g[:, :, lo:lo + H], 1, 2)
    def stream_rows(lo_f, lo_b):
        f = _scan_order(gate_cols(gl, lo_f), gate_cols(gc, lo_f), False)
        r = _scan_order(gate_cols(gl, lo_b), gate_cols(gc, lo_b), True)
        n = f.shape[0]
        return jnp.concatenate([f.reshape(n, B * H, CHUNK), r.reshape(n, B * H, CHUNK)], axis=1)
    stats = _gateprep(stream_rows(0, H), stream_rows(2 * H, 3 * H))
    n_all = n_ctx + n_lat
    st_f = stats[:, :, :B * H].reshape(n_all, 6, B, H, CHUNK)
    st_b = _time_order(stats[:, :, B * H:], n_ctx, True).reshape(n_all, 6, B, H, CHUNK)
    a_i, w_i, mm_i, en_i, dec_i, m_i = range(6)
    rowv = jnp.stack([st[:, i] for i in (a_i, dec_i, w_i, m_i) for st in (st_f, st_b)], axis=-2)
    rowv = jnp.transpose(rowv, (1, 2, 0, 3, 4))
    colv = jnp.stack([st[:, i] for i in (mm_i, en_i) for st in (st_f, st_b)], axis=-1)
    colv = jnp.transpose(colv, (1, 2, 0, 3, 4)).reshape(B, H, n_all * CHUNK, 4)

    ym = _mlstm(pm, pmc, rowv, colv, mlstm_norm_w[0].reshape(H, 1, HEAD_DIM))
    dl = jnp.broadcast_to(jnp.swapaxes(ret_decay_logit[0], 0, 1)[:, :, None], (H, 2, LANES))
    yr = _ret(pr, prc, dl, ret_norm_w[0].reshape(H, 1, HEAD_DIM))

    wr_hi = w_router[0].astype(BF16)
    wr_lo = (w_router[0] - wr_hi.astype(F32)).astype(BF16)
    wrt = jnp.pad(jnp.concatenate([wr_hi, wr_lo], axis=1), ((0, 0), (0, LANES - 2 * n_experts)))
    xl, h2, aff = _outproj(ym, yr, x, w_out[0].astype(BF16), g1, norm2_w[0].reshape(1, D), sh2, sc2,
                           wrt, n_experts)

    aff_t = jnp.swapaxes(aff[:, :, :n_experts], 1, 2)
    pos, idx = _route(aff_t.reshape(B, n_experts, T // LANES, LANES), cap)
    idx = jnp.swapaxes(idx.reshape(B, n_experts, cap), 0, 1).reshape(n_experts, B * cap)
    y = _ffn(idx, h2.reshape(B * T, D), w_gate[0], w_up[0], w_down[0], B * cap)
    return _combine(y.reshape(n_experts, B, cap, D), jnp.swapaxes(pos.reshape(B, n_experts, T), 1, 2),
                    aff, xl, g2, final_norm_w.reshape(1, D), cap)
```

```python
import functools

import jax
import jax.numpy as jnp
from jax import lax
from jax.experimental import pallas as pl
from jax.experimental.pallas import tpu as pltpu

F32 = jnp.float32
BF16 = jnp.bfloat16

CHUNK = 128
GRID_W = 64
N_HEADS = 4
HEAD_DIM = 128
ROPE_BASE = 10000.0
EPS = 1e-6
EC_CAPACITY_FACTOR = 2
LANES = 128
VMEM_LIMIT_BYTES = 56 * 1024 * 1024
COMBINE_WINDOW = 128
MIXER_UNROLL = 8


def _params(*sem):
    return pltpu.CompilerParams(dimension_semantics=sem, vmem_limit_bytes=VMEM_LIMIT_BYTES)


def _dot(a, b):
    return jnp.dot(a, b, preferred_element_type=F32)


def _dot_nt(a, b):
    return lax.dot_general(a, b, (((1,), (1,)), ((), ())), preferred_element_type=F32)


def _dot_tn(a, b):
    return lax.dot_general(a, b, (((0,), (0,)), ((), ())), preferred_element_type=F32)


def _sigmoid(x):
    return 1.0 / (1.0 + jnp.exp(-x))


def _log_sigmoid(x):
    return jnp.minimum(x, 0.0) - jnp.log1p(jnp.exp(-jnp.abs(x)))


def _tile(n, pref):
    t = min(n, pref)
    assert n % t == 0, (n, t)
    return t


def _ada_kernel(c_ref, w_ref, b_ref, o_ref):
    cv = c_ref[...]
    s = (cv * _sigmoid(cv)).astype(BF16)
    o_ref[...] = _dot(s, w_ref[...].astype(BF16)) + b_ref[...]


def _ada(cc, w, b):
    rows, d = cc.shape
    n = w.shape[1]
    tn = _tile(n, 1024)
    return pl.pallas_call(
        _ada_kernel,
        grid=(n // tn,),
        in_specs=[pl.BlockSpec((rows, d), lambda j: (0, 0)),
                  pl.BlockSpec((d, tn), lambda j: (0, j)),
                  pl.BlockSpec((1, tn), lambda j: (0, j))],
        out_specs=pl.BlockSpec((rows, tn), lambda j: (0, j)),
        out_shape=jax.ShapeDtypeStruct((rows, n), F32),
        compiler_params=_params("arbitrary"),
    )(cc, w, b.reshape(1, n))


def _rms_mod(x, nw, sh, sc):
    ms = jnp.mean(x * x, axis=-1, keepdims=True)
    return (x * lax.rsqrt(ms + EPS) * nw) * (1.0 + sc) + sh


def _inproj_kernel(x_ref, nw_ref, sh_ref, sc_ref, wm_ref, wg_ref, wr_ref, gb_ref, cos_ref, sin_ref,
                   pm_ref, g_ref, pr_ref, *, width, k_scale):
    hb = _rms_mod(x_ref[0], nw_ref[...], sh_ref[0], sc_ref[0]).astype(BF16)
    for j in range(4):
        cols = slice(j * width, (j + 1) * width)
        acc = _dot(hb, wm_ref[:, cols])
        if j == 1:
            acc = acc * k_scale
        pm_ref[0, :, cols] = acc.astype(BF16)
    g_ref[0] = _dot(hb, wg_ref[...]) + gb_ref[...]
    cos = cos_ref[...]
    sin = sin_ref[...]
    lane = lax.broadcasted_iota(jnp.int32, cos.shape, 1)
    even = ((lane // 32) % 2) == 0
    for j in range(4):
        cols = slice(j * width, (j + 1) * width)
        acc = _dot(hb, wr_ref[:, cols])
        if j == 1:
            acc = acc * k_scale
        if j < 2:
            swapped = jnp.where(even, pltpu.roll(acc, width - 32, 1), pltpu.roll(acc, 32, 1))
            acc = acc * cos + swapped * sin
        pr_ref[0, :, cols] = acc.astype(BF16)


def _inproj(x, nw, sh, sc, wm, wg, wr, gb, cos, sin):
    b, t, d = x.shape
    width = wm.shape[1] // 4
    tm = _tile(t, 512)
    kern = functools.partial(_inproj_kernel, width=width, k_scale=HEAD_DIM ** -0.5)
    const = lambda i, j: (0, 0)
    return pl.pallas_call(
        kern,
        grid=(b, t // tm),
        in_specs=[pl.BlockSpec((1, tm, d), lambda i, j: (i, j, 0)),
                  pl.BlockSpec((1, d), const),
                  pl.BlockSpec((1, 1, d), lambda i, j: (i, 0, 0)),
                  pl.BlockSpec((1, 1, d), lambda i, j: (i, 0, 0)),
                  pl.BlockSpec(wm.shape, const),
                  pl.BlockSpec(wg.shape, const),
                  pl.BlockSpec(wr.shape, const),
                  pl.BlockSpec((1, LANES), const),
                  pl.BlockSpec((tm, width), lambda i, j: (j, 0)),
                  pl.BlockSpec((tm, width), lambda i, j: (j, 0))],
        out_specs=[pl.BlockSpec((1, tm, 4 * width), lambda i, j: (i, j, 0)),
                   pl.BlockSpec((1, tm, LANES), lambda i, j: (i, j, 0)),
                   pl.BlockSpec((1, tm, 4 * width), lambda i, j: (i, j, 0))],
        out_shape=[jax.ShapeDtypeStruct((b, t, 4 * width), BF16),
                   jax.ShapeDtypeStruct((b, t, LANES), F32),
                   jax.ShapeDtypeStruct((b, t, 4 * width), BF16)],
        compiler_params=_params("arbitrary", "arbitrary"),
    )(x, nw, sh, sc, wm, wg, wr, gb, cos, sin)


def _scan_lanes(x, op, fill, reverse):
    lane = lax.broadcasted_iota(jnp.int32, x.shape, 1)
    sh = 1
    while sh < LANES:
        if reverse:
            x = op(x, jnp.where(lane < LANES - sh, pltpu.roll(x, LANES - sh, 1), fill))
        else:
            x = op(x, jnp.where(lane >= sh, pltpu.roll(x, sh, 1), fill))
        sh *= 2
    return x


def _gateprep_kernel(ig_ref, fg_ref, o_ref, *, n_chunks):
    rows = ig_ref.shape[1]
    half = rows // 2
    row = lax.broadcasted_iota(jnp.int32, (rows, LANES), 0)
    lane = lax.broadcasted_iota(jnp.int32, (rows, LANES), 1)
    is_fwd = row < half
    last = lane == jnp.where(is_fwd, LANES - 1, 0)

    def body(c, m):
        lf = _log_sigmoid(fg_ref[c])
        ig = ig_ref[c]
        b = jnp.where(is_fwd, _scan_lanes(lf, jnp.add, 0.0, False), _scan_lanes(lf, jnp.add, 0.0, True))
        b_tot = jnp.sum(jnp.where(last, b, 0.0), axis=1, keepdims=True)
        a = ig - b
        m_new = jnp.maximum(b_tot + m, jnp.max(b_tot + a, axis=1, keepdims=True))
        w = jnp.exp(b_tot + a - m_new)
        decay = jnp.exp(b_tot + m - m_new)
        run = jnp.where(is_fwd, _scan_lanes(a, jnp.maximum, -jnp.inf, False),
                        _scan_lanes(a, jnp.maximum, -jnp.inf, True))
        mm = jnp.maximum(m, run)
        o_ref[c, 0] = a
        o_ref[c, 1] = w
        o_ref[c, 2] = mm
        o_ref[c, 3] = jnp.exp(-(b + mm))
        o_ref[c, 4] = jnp.broadcast_to(decay, (rows, LANES))
        o_ref[c, 5] = jnp.broadcast_to(m, (rows, LANES))
        return m_new

    lax.fori_loop(0, n_chunks, body, jnp.zeros((rows, 1), F32))


def _gateprep(ig, fg):
    n_chunks, rows, _ = ig.shape
    return pl.pallas_call(
        functools.partial(_gateprep_kernel, n_chunks=n_chunks),
        out_shape=jax.ShapeDtypeStruct((n_chunks, 6, rows, LANES), F32),
        compiler_params=pltpu.CompilerParams(vmem_limit_bytes=VMEM_LIMIT_BYTES),
    )(ig, fg)


def _head_norm_gate(h, gate, nw):
    return gate * (h * lax.rsqrt(jnp.mean(h * h, axis=-1, keepdims=True) + EPS)) * nw


def _mlstm_kernel(q_ref, k_ref, v_ref, o_ref, kc_ref, vc_ref, rowv_ref, colv_ref, nw_ref, y_ref,
                  sf_ref, sb_ref, cf_ref, cb_ref, *, n_lat, n_ctx, unroll):
    L = CHUNK
    ones = jnp.ones((L, HEAD_DIM), BF16)

    def update(state_ref, k, v, w_row, decay_row):
        wkt = (k.astype(F32).T * w_row).astype(BF16)
        u = _dot(wkt, jnp.concatenate([v, ones], axis=1))
        dec = jnp.concatenate([decay_row, decay_row], axis=1)
        state_ref[...] = dec * state_ref[...] + u

    cf_ref[...] = jnp.zeros_like(cf_ref)
    cb_ref[...] = jnp.zeros_like(cb_ref)
    for c in range(n_ctx):
        rows = slice(c * L, (c + 1) * L)
        update(cf_ref, kc_ref[0, rows, :], vc_ref[0, rows, :],
               rowv_ref[0, 0, c, 4:5, :], rowv_ref[0, 0, c, 2:3, :])
    for c in reversed(range(n_ctx)):
        rows = slice(c * L, (c + 1) * L)
        update(cb_ref, kc_ref[0, rows, :], vc_ref[0, rows, :],
               rowv_ref[0, 0, c, 5:6, :], rowv_ref[0, 0, c, 3:4, :])

    def state_body(i, carry):
        jf = i
        jb = n_lat - 1 - i
        rf = pl.ds(pl.multiple_of(jf * L, L), L)
        rb = pl.ds(pl.multiple_of(jb * L, L), L)
        sf_ref[jf] = cf_ref[...].astype(BF16)
        sb_ref[jb] = cb_ref[...].astype(BF16)
        update(cf_ref, k_ref[0, rf, :], v_ref[0, rf, :],
               rowv_ref[0, 0, n_ctx + jf, 4:5, :], rowv_ref[0, 0, n_ctx + jf, 2:3, :])
        update(cb_ref, k_ref[0, rb, :], v_ref[0, rb, :],
               rowv_ref[0, 0, n_ctx + jb, 5:6, :], rowv_ref[0, 0, n_ctx + jb, 3:4, :])
        return carry

    lax.fori_loop(0, n_lat, state_body, 0, unroll=unroll)

    r_i = lax.broadcasted_iota(jnp.int32, (L, L), 0)
    c_i = lax.broadcasted_iota(jnp.int32, (L, L), 1)
    tril = c_i <= r_i
    triu = c_i >= r_i
    nw = nw_ref[0]

    def out_body(j, carry):
        rows = pl.ds(pl.multiple_of(j * L, L), L)
        crow = pl.ds(pl.multiple_of((n_ctx + j) * L, L), L)
        q = q_ref[0, rows, :]
        k = k_ref[0, rows, :]
        v = v_ref[0, rows, :]
        rv = rowv_ref[0, 0, n_ctx + j]
        cv = colv_ref[0, 0, crow, :]
        mm_f = jnp.broadcast_to(cv[:, 0:1], (L, L))
        mm_b = jnp.broadcast_to(cv[:, 1:2], (L, L))
        s = _dot_nt(q, k)
        d_f = jnp.where(tril, jnp.exp(rv[0:1, :] - mm_f), 0.0)
        d_b = jnp.where(triu, jnp.exp(rv[1:2, :] - mm_b), 0.0)
        p = jnp.concatenate([s * d_f, s * d_b], axis=0).astype(BF16)
        tot = _dot(p, jnp.concatenate([v, ones], axis=1))
        inter = _dot(q, jnp.concatenate([sf_ref[j], sb_ref[j]], axis=1))
        iw_f = jnp.exp(rv[6:7, :] - mm_f)
        iw_b = jnp.exp(rv[7:8, :] - mm_b)
        D, W = HEAD_DIM, HEAD_DIM + LANES
        num_f = tot[:L, :D] + iw_f * inter[:, :D]
        den_f = tot[:L, D:] + iw_f * inter[:, D:W]
        num_b = tot[L:, :D] + iw_b * inter[:, W:W + D]
        den_b = tot[L:, D:] + iw_b * inter[:, W + D:]
        h = (num_f / jnp.maximum(jnp.abs(den_f), cv[:, 2:3])
             + num_b / jnp.maximum(jnp.abs(den_b), cv[:, 3:4]))
        gate = _sigmoid(o_ref[0, rows, :].astype(F32))
        y_ref[0, rows, :] = _head_norm_gate(h, gate, nw).astype(BF16)
        return carry

    lax.fori_loop(0, n_lat, out_body, 0, unroll=unroll)


def _mlstm(pm, pmc, rowv, colv, nw):
    b, t, _ = pm.shape
    tc = pmc.shape[1]
    n_lat, n_ctx = t // CHUNK, tc // CHUNK
    H = N_HEADS
    blk = lambda off: pl.BlockSpec((1, t, HEAD_DIM), lambda i, h: (i, 0, off + h))
    cblk = lambda off: pl.BlockSpec((1, tc, HEAD_DIM), lambda i, h: (i, 0, off + h))
    return pl.pallas_call(
        functools.partial(_mlstm_kernel, n_lat=n_lat, n_ctx=n_ctx, unroll=MIXER_UNROLL),
        grid=(b, H),
        in_specs=[blk(0), blk(H), blk(2 * H), blk(3 * H), cblk(H), cblk(2 * H),
                  pl.BlockSpec((1, 1, n_ctx + n_lat, 8, LANES), lambda i, h: (i, h, 0, 0, 0)),
                  pl.BlockSpec((1, 1, tc + t, 4), lambda i, h: (i, h, 0, 0)),
                  pl.BlockSpec((1, 1, HEAD_DIM), lambda i, h: (h, 0, 0))],
        out_specs=pl.BlockSpec((1, t, HEAD_DIM), lambda i, h: (i, 0, h)),
        out_shape=jax.ShapeDtypeStruct((b, t, H * HEAD_DIM), BF16),
        scratch_shapes=[pltpu.VMEM((n_lat, HEAD_DIM, HEAD_DIM + LANES), BF16),
                        pltpu.VMEM((n_lat, HEAD_DIM, HEAD_DIM + LANES), BF16),
                        pltpu.VMEM((HEAD_DIM, HEAD_DIM + LANES), F32),
                        pltpu.VMEM((HEAD_DIM, HEAD_DIM + LANES), F32)],
        compiler_params=_params("arbitrary", "arbitrary"),
    )(pm, pm, pm, pm, pmc, pmc, rowv, colv, nw)


def _ret_kernel(q_ref, k_ref, v_ref, o_ref, kc_ref, vc_ref, dl_ref, nw_ref, y_ref,
                sf_ref, sb_ref, rf_ref, rb_ref, *, n_lat, n_ctx, unroll):
    L = CHUNK
    lg = _log_sigmoid(dl_ref[0])
    lg_f, lg_b = lg[0:1, :], lg[1:2, :]
    r_i = lax.broadcasted_iota(jnp.int32, (L, L), 0)
    c_i = lax.broadcasted_iota(jnp.int32, (L, L), 1)
    r_f = r_i.astype(F32)
    rel = (r_i - c_i).astype(F32)
    kw_f = jnp.exp((L - 1.0 - r_f) * lg_f)
    kw_b = jnp.exp(r_f * lg_b)
    in_f = jnp.exp((r_f + 1.0) * lg_f)
    in_b = jnp.exp((L - r_f) * lg_b)
    dch_f = jnp.exp(L * lg_f)
    dch_b = jnp.exp(L * lg_b)
    d_mat = (jnp.where(rel >= 0, jnp.exp(jnp.maximum(rel, 0.0) * lg_f), 0.0)
             + jnp.where(rel <= 0, jnp.exp(jnp.maximum(-rel, 0.0) * lg_b), 0.0))

    def update(state_ref, k, v, kw, dch):
        wk = (k.astype(F32) * kw).astype(BF16)
        state_ref[...] = dch * state_ref[...] + _dot_tn(wk, v)

    rf_ref[...] = jnp.zeros_like(rf_ref)
    rb_ref[...] = jnp.zeros_like(rb_ref)
    for c in range(n_ctx):
        rows = slice(c * L, (c + 1) * L)
        update(rf_ref, kc_ref[0, rows, :], vc_ref[0, rows, :], kw_f, dch_f)
    for c in reversed(range(n_ctx)):
        rows = slice(c * L, (c + 1) * L)
        update(rb_ref, kc_ref[0, rows, :], vc_ref[0, rows, :], kw_b, dch_b)

    def state_body(i, carry):
        jb = n_lat - 1 - i
        rowf = pl.ds(pl.multiple_of(i * L, L), L)
        rowb = pl.ds(pl.multiple_of(jb * L, L), L)
        sf_ref[i] = rf_ref[...].astype(BF16)
        sb_ref[jb] = rb_ref[...].astype(BF16)
        update(rf_ref, k_ref[0, rowf, :], v_ref[0, rowf, :], kw_f, dch_f)
        update(rb_ref, k_ref[0, rowb, :], v_ref[0, rowb, :], kw_b, dch_b)
        return carry

    lax.fori_loop(0, n_lat, state_body, 0, unroll=unroll)
    nw = nw_ref[0]

    def out_body(j, carry):
        rows = pl.ds(pl.multiple_of(j * L, L), L)
        q = q_ref[0, rows, :]
        s = _dot_nt(q, k_ref[0, rows, :])
        intra = _dot((s * d_mat).astype(BF16), v_ref[0, rows, :])
        inter = _dot(q, jnp.concatenate([sf_ref[j], sb_ref[j]], axis=1))
        out = intra + in_f * inter[:, :HEAD_DIM] + in_b * inter[:, HEAD_DIM:]
        og = o_ref[0, rows, :].astype(F32)
        y_ref[0, rows, :] = _head_norm_gate(out, og * _sigmoid(og), nw).astype(BF16)
        return carry

    lax.fori_loop(0, n_lat, out_body, 0, unroll=unroll)


def _ret(pr, prc, dl, nw):
    b, t, _ = pr.shape
    tc = prc.shape[1]
    n_lat, n_ctx = t // CHUNK, tc // CHUNK
    H = N_HEADS
    blk = lambda off: pl.BlockSpec((1, t, HEAD_DIM), lambda i, h: (i, 0, off + h))
    cblk = lambda off: pl.BlockSpec((1, tc, HEAD_DIM), lambda i, h: (i, 0, off + h))
    return pl.pallas_call(
        functools.partial(_ret_kernel, n_lat=n_lat, n_ctx=n_ctx, unroll=MIXER_UNROLL),
        grid=(b, H),
        in_specs=[blk(0), blk(H), blk(2 * H), blk(3 * H), cblk(H), cblk(2 * H),
                  pl.BlockSpec((1, 2, LANES), lambda i, h: (h, 0, 0)),
                  pl.BlockSpec((1, 1, HEAD_DIM), lambda i, h: (h, 0, 0))],
        out_specs=pl.BlockSpec((1, t, HEAD_DIM), lambda i, h: (i, 0, h)),
        out_shape=jax.ShapeDtypeStruct((b, t, H * HEAD_DIM), BF16),
        scratch_shapes=[pltpu.VMEM((n_lat, HEAD_DIM, HEAD_DIM), BF16),
                        pltpu.VMEM((n_lat, HEAD_DIM, HEAD_DIM), BF16),
                        pltpu.VMEM((HEAD_DIM, HEAD_DIM), F32),
                        pltpu.VMEM((HEAD_DIM, HEAD_DIM), F32)],
        compiler_params=_params("arbitrary", "arbitrary"),
    )(pr, pr, pr, pr, prc, prc, dl, nw)


def _outproj_kernel(ym_ref, yr_ref, x_ref, wo_ref, g1_ref, nw_ref, sh_ref, sc_ref, wrt_ref,
                    xl_ref, h2_ref, aff_ref, *, n_experts, width):
    y = _dot(ym_ref[0], wo_ref[:width, :]) + _dot(yr_ref[0], wo_ref[width:, :])
    xl = x_ref[0] + g1_ref[0] * y
    xl_ref[0] = xl
    h2 = _rms_mod(xl, nw_ref[...], sh_ref[0], sc_ref[0])
    h2_ref[0] = h2
    h_hi = h2.astype(BF16)
    h_lo = (h2 - h_hi.astype(F32)).astype(BF16)
    p_hi = _dot(h_hi, wrt_ref[...])
    p_lo = _dot(h_lo, wrt_ref[...])
    logits = p_hi + pltpu.roll(p_hi, LANES - n_experts, 1) + p_lo
    lane = lax.broadcasted_iota(jnp.int32, logits.shape, 1)
    logits = jnp.where(lane < n_experts, logits, -jnp.inf)
    e = jnp.exp(logits - jnp.max(logits, axis=-1, keepdims=True))
    aff_ref[0] = e / jnp.sum(e, axis=-1, keepdims=True)


def _outproj(ym, yr, x, wo, g1, nw, sh, sc, wrt, n_experts):
    b, t, d = x.shape
    width = ym.shape[2]
    tm = _tile(t, 512)
    const = lambda i, j: (0, 0)
    per_b = lambda i, j: (i, 0, 0)
    tok = lambda i, j: (i, j, 0)
    return pl.pallas_call(
        functools.partial(_outproj_kernel, n_experts=n_experts, width=width),
        grid=(b, t // tm),
        in_specs=[pl.BlockSpec((1, tm, width), tok), pl.BlockSpec((1, tm, width), tok),
                  pl.BlockSpec((1, tm, d), tok), pl.BlockSpec(wo.shape, const),
                  pl.BlockSpec((1, 1, d), per_b), pl.BlockSpec((1, d), const),
                  pl.BlockSpec((1, 1, d), per_b), pl.BlockSpec((1, 1, d), per_b),
                  pl.BlockSpec(wrt.shape, const)],
        out_specs=[pl.BlockSpec((1, tm, d), tok), pl.BlockSpec((1, tm, d), tok),
                   pl.BlockSpec((1, tm, LANES), tok)],
        out_shape=[jax.ShapeDtypeStruct((b, t, d), F32), jax.ShapeDtypeStruct((b, t, d), F32),
                   jax.ShapeDtypeStruct((b, t, LANES), F32)],
        compiler_params=_params("arbitrary", "arbitrary"),
    )(ym, yr, x, wo, g1, nw, sh, sc, wrt)


def _route_kernel(aff_ref, pos_ref, idx_ref, gate_ref, start_ref, *, cap):
    a = aff_ref[0]
    n_e, n_b, _ = a.shape

    def count(mask):
        return jnp.sum(jnp.sum(mask.astype(jnp.int32), axis=2, keepdims=True), axis=1, keepdims=True)

    def search(i, thr):
        cand = thr | jnp.left_shift(jnp.int32(1), 30 - i)
        return jnp.where(count(a >= pltpu.bitcast(cand, F32)) >= cap, cand, thr)

    thr = lax.fori_loop(0, 31, search, jnp.zeros((n_e, 1, 1), jnp.int32))
    above = a >= pltpu.bitcast(thr + 1, F32)
    band = (a >= pltpu.bitcast(thr, F32)) & jnp.logical_not(above)
    need = cap - count(above)
    tok = (lax.broadcasted_iota(jnp.int32, a.shape, 1) * LANES
           + lax.broadcasted_iota(jnp.int32, a.shape, 2))

    def take_one(_, carry):
        sel, band, need = carry
        in_band = band > 0
        best = jnp.max(jnp.max(jnp.where(in_band, a, -1.0), axis=2, keepdims=True), axis=1, keepdims=True)
        cand = jnp.where(in_band & (a == best), tok, n_b * LANES)
        first = jnp.min(jnp.min(cand, axis=2, keepdims=True), axis=1, keepdims=True)
        take = ((tok == first) & (need > 0)).astype(jnp.int32)
        return sel | take, band - take, need - 1

    sel, _, _ = lax.fori_loop(0, jnp.max(need), take_one,
                              (above.astype(jnp.int32), band.astype(jnp.int32), need))
    sel = sel > 0

    rows = n_e * n_b
    u_i = lax.broadcasted_iota(jnp.int32, (LANES, LANES), 0)
    s_i = lax.broadcasted_iota(jnp.int32, (LANES, LANES), 1)
    incl = (u_i <= s_i).astype(BF16)
    ones = jnp.ones((LANES, LANES), BF16)
    r_i = lax.broadcasted_iota(jnp.int32, (rows, rows), 0)
    c_i = lax.broadcasted_iota(jnp.int32, (rows, rows), 1)
    before = ((r_i // n_b == c_i // n_b) & (c_i < r_i)).astype(BF16)

    x = sel.astype(F32).reshape(rows, LANES)
    xb = x.astype(BF16)
    block_tot = _dot(xb, ones).astype(BF16)
    start = _dot(before, block_tot)
    pos = jnp.where(x > 0, start + _dot(xb, incl) - x, -1.0).astype(jnp.int32)
    pos_ref[0] = pos.reshape(n_e, n_b, LANES)
    start_ref[0] = start.astype(jnp.int32)

    pad = (-rows) % LANES
    def pad_rows(m):
        return m if pad == 0 else jnp.concatenate([m, jnp.zeros((pad, LANES), m.dtype)], axis=0)
    start_p = pad_rows(start)
    end_p = pad_rows(start + block_tot.astype(F32))
    pos_p = pad_rows(pos)
    pos_p = jnp.where(pos_p < 0, 1023, pos_p)
    a_p = pad_rows(a.reshape(rows, LANES))
    a_hi = a_p.astype(BF16)
    a_mid = (a_p - a_hi.astype(F32)).astype(BF16)
    a_lo = (a_p - a_hi.astype(F32) - a_mid.astype(F32)).astype(BF16)
    per_group = LANES // n_b
    j_col = lax.broadcasted_iota(jnp.int32, (cap, LANES), 0).astype(F32)
    lane = lax.broadcasted_iota(jnp.int32, (cap, LANES), 1)
    lane_f = lane.astype(F32)
    block_f = (lane % n_b).astype(F32)
    row0 = pl.program_id(0) * (n_b * LANES)
    for g in range((rows + pad) // LANES):
        rs = slice(g * LANES, (g + 1) * LANES)
        s_row = start_p[rs].T[0:1, :]
        e_row = end_p[rs].T[0:1, :]
        hi = (pos_p[rs] >> 4).astype(F32).astype(BF16)
        lo = (pos_p[rs] & 15).astype(F32).astype(BF16)
        for el in range(per_group):
            e = g * per_group + el
            if e >= n_e:
                break
            mine = (j_col >= s_row) & (j_col < e_row) & ((lane // n_b) == el)
            mine_b = mine.astype(F32).astype(BF16)
            slots = 16.0 * _dot(mine_b, hi) + _dot(mine_b, lo)
            hit = slots == j_col
            in_block = jnp.sum(jnp.where(hit, lane_f, 0.0), axis=1, keepdims=True)
            block = jnp.sum(jnp.where(mine, block_f, 0.0), axis=1, keepdims=True)
            idx_ref[0, e] = (block * LANES + in_block).astype(jnp.int32) + row0
            a_blk = _dot(mine_b, a_hi[rs]) + _dot(mine_b, a_mid[rs]) + _dot(mine_b, a_lo[rs])
            gate_ref[0, e] = jnp.sum(jnp.where(hit, a_blk, 0.0), axis=1, keepdims=True)


def _route(aff_t, cap):
    b, n_e, n_b, _ = aff_t.shape
    spec = pl.BlockSpec((1, n_e, n_b, LANES), lambda i: (i, 0, 0, 0))
    slot_spec = pl.BlockSpec((1, n_e, cap, 1), lambda i: (i, 0, 0, 0))
    return pl.pallas_call(
        functools.partial(_route_kernel, cap=cap),
        grid=(b,), in_specs=[spec],
        out_specs=[spec, slot_spec, slot_spec, pl.BlockSpec((1, n_e * n_b, LANES), lambda i: (i, 0, 0))],
        out_shape=[jax.ShapeDtypeStruct(aff_t.shape, jnp.int32),
                   jax.ShapeDtypeStruct((b, n_e, cap, 1), jnp.int32),
                   jax.ShapeDtypeStruct((b, n_e, cap, 1), F32),
                   jax.ShapeDtypeStruct((b, n_e * n_b, LANES), jnp.int32)],
        compiler_params=_params("arbitrary"),
    )(aff_t)


GATHER_DMA_PRIORITY = 1


def _ffn_kernel(idx_ref, h_hbm, gate_ref, wg_ref, wu_ref, wd_ref, y_ref, stage_ref, x_ref, acc_ref, sem,
                *, m, per_step):
    e, f = pl.program_id(0), pl.program_id(1)
    n_e, n_f = pl.num_programs(0), pl.num_programs(1)
    rows_per_expert = stage_ref.shape[0]

    def row_copy(base, j):
        return pltpu.make_async_copy(h_hbm.at[pl.ds(idx_ref[base + j], 1), :],
                                     stage_ref.at[pl.ds(j, 1), :], sem.at[0])

    def wait_rows():
        pltpu.make_async_copy(h_hbm.at[pl.ds(0, rows_per_expert), :], stage_ref, sem.at[0]).wait()

    @pl.when((e == 0) & (f == 0))
    def _():
        def start_row(j, carry):
            row_copy(0, j).start(priority=GATHER_DMA_PRIORITY)
            return carry
        lax.fori_loop(0, rows_per_expert, start_row, 0)

    @pl.when(f == 0)
    def _():
        wait_rows()
        x_ref[...] = stage_ref[:m, :].astype(BF16)
        acc_ref[...] = jnp.zeros_like(acc_ref)

    nxt = jnp.minimum(e + 1, n_e - 1) * rows_per_expert
    for i in range(per_step):
        row_copy(nxt, f * per_step + i).start(priority=GATHER_DMA_PRIORITY)

    x = x_ref[...]
    a = _dot(x, wg_ref[0].astype(BF16))
    u = _dot(x, wu_ref[0].astype(BF16))
    mid = (a * _sigmoid(a) * u).astype(BF16)
    acc_ref[...] += _dot(mid, wd_ref[0].astype(BF16))

    @pl.when(f == n_f - 1)
    def _():
        y_ref[0] = (acc_ref[...] * gate_ref[0]).astype(BF16)

    @pl.when((e == n_e - 1) & (f == n_f - 1))
    def _():
        wait_rows()


def _ffn(idx, gate, h_rows, wg, wu, wd):
    n_e, m = idx.shape
    d = h_rows.shape[1]
    ff = wg.shape[2]
    tf = _tile(ff, 256)
    n_f = ff // tf
    per_step = -(-m // (8 * n_f)) * 8
    idx = jnp.pad(idx, ((0, 0), (0, n_f * per_step - m))).reshape(-1)
    return pl.pallas_call(
        functools.partial(_ffn_kernel, m=m, per_step=per_step),
        grid_spec=pltpu.PrefetchScalarGridSpec(
            num_scalar_prefetch=1,
            grid=(n_e, n_f),
            in_specs=[pl.BlockSpec(memory_space=pl.ANY),
                      pl.BlockSpec((1, m, 1), lambda e, f, idx: (e, 0, 0)),
                      pl.BlockSpec((1, d, tf), lambda e, f, idx: (e, 0, f)),
                      pl.BlockSpec((1, d, tf), lambda e, f, idx: (e, 0, f)),
                      pl.BlockSpec((1, tf, d), lambda e, f, idx: (e, f, 0))],
            out_specs=pl.BlockSpec((1, m, d), lambda e, f, idx: (e, 0, 0)),
            scratch_shapes=[pltpu.VMEM((n_f * per_step, d), F32),
                            pltpu.VMEM((m, d), BF16),
                            pltpu.VMEM((m, d), F32),
                            pltpu.SemaphoreType.DMA((1,))]),
        out_shape=jax.ShapeDtypeStruct((n_e, m, d), BF16),
        compiler_params=_params("arbitrary", "arbitrary"),
    )(idx, h_rows, gate, wg, wu, wd)


def _combine_kernel(start_ref, y_ref, pos_ref, xl_ref, g2_ref, fw_ref, o_ref, yc_ref, acc_ref,
                    *, n_experts, cap, n_b, win):
    i, j = pl.program_id(0), pl.program_id(1)
    tt = xl_ref.shape[1]
    blocks_per_tile = tt // LANES
    lane = lax.broadcasted_iota(jnp.int32, (tt, win), 1)
    pos = pos_ref[0]
    last_tile = j == pl.num_programs(1) - 1
    first, n_pass = [], 0
    for e in range(n_experts):
        row = (i * n_experts + e) * n_b
        lo = start_ref[row + j * blocks_per_tile]
        hi = jnp.where(last_tile, cap, start_ref[row + jnp.minimum((j + 1) * blocks_per_tile, n_b - 1)])
        w0 = jnp.minimum((lo // 16) * 16, cap - win)
        first.append(w0)
        n_pass = jnp.maximum(n_pass, jnp.where(hi > lo, (hi - w0 + win - 1) // win, 0))
    acc_ref[...] = jnp.zeros_like(acc_ref)

    def one_pass(p, carry):
        hits = []
        for e in range(n_experts):
            done = first[e] + p * win
            off = pl.multiple_of(jnp.minimum(done, cap - win), 16)
            yc_ref[e * win:(e + 1) * win, :] = y_ref[e, 0, pl.ds(off, win), :]
            pe = pos[:, e:e + 1]
            hits.append(((pe - off) == lane) & (pe >= done))
        onehot = jnp.concatenate(hits, axis=1).astype(F32).astype(BF16)
        acc_ref[...] += _dot(onehot, yc_ref[...])
        return carry

    lax.fori_loop(0, n_pass, one_pass, 0)
    xl = xl_ref[0] + g2_ref[0] * acc_ref[...]
    ms = jnp.mean(xl * xl, axis=-1, keepdims=True)
    o_ref[0] = xl * lax.rsqrt(ms + EPS) * fw_ref[...]


def _combine(starts, y, pos_col, xl, g2, fw, cap):
    b, t, d = xl.shape
    n_e = y.shape[0]
    tt = _tile(t, 512)
    win = min(cap, COMBINE_WINDOW)
    assert cap % 16 == 0 and win % 16 == 0 and tt % LANES == 0
    tok = lambda i, j, st: (i, j, 0)
    return pl.pallas_call(
        functools.partial(_combine_kernel, n_experts=n_e, cap=cap, n_b=t // LANES, win=win),
        grid_spec=pltpu.PrefetchScalarGridSpec(
            num_scalar_prefetch=1,
            grid=(b, t // tt),
            in_specs=[pl.BlockSpec((n_e, 1, cap, d), lambda i, j, st: (0, i, 0, 0)),
                      pl.BlockSpec((1, tt, n_e), tok),
                      pl.BlockSpec((1, tt, d), tok),
                      pl.BlockSpec((1, 1, d), lambda i, j, st: (i, 0, 0)),
                      pl.BlockSpec((1, d), lambda i, j, st: (0, 0))],
            out_specs=pl.BlockSpec((1, tt, d), tok),
            scratch_shapes=[pltpu.VMEM((n_e * win, d), BF16), pltpu.VMEM((tt, d), F32)]),
        out_shape=jax.ShapeDtypeStruct((b, t, d), F32),
        compiler_params=_params("arbitrary", "arbitrary"),
    )(starts, y, pos_col, xl, g2, fw)


def _rope_tables(t):
    n_freq = HEAD_DIM // 4
    rows = jnp.repeat(jnp.arange(t // GRID_W, dtype=F32), GRID_W)
    cols = jnp.tile(jnp.arange(GRID_W, dtype=F32), t // GRID_W)
    inv_freq = ROPE_BASE ** (-jnp.arange(n_freq, dtype=F32) / n_freq)
    ang_r, ang_c = rows[:, None] * inv_freq, cols[:, None] * inv_freq
    cos = jnp.concatenate([jnp.cos(ang_r)] * 2 + [jnp.cos(ang_c)] * 2, axis=-1)
    sin = jnp.concatenate([-jnp.sin(ang_r), jnp.sin(ang_r), -jnp.sin(ang_c), jnp.sin(ang_c)], axis=-1)
    return jnp.tile(cos, (1, N_HEADS)), jnp.tile(sin, (1, N_HEADS))


def _scan_order(lat, ctx, reverse):
    def chunks(a):
        b, h, t = a.shape[:3]
        a = jnp.moveaxis(a.reshape((b, h, t // CHUNK, CHUNK) + a.shape[3:]), 2, 0)
        return a[::-1] if reverse else a
    return jnp.concatenate([chunks(ctx), chunks(lat)], axis=0)


def _time_order(a, n_ctx, reverse):
    if not reverse:
        return a
    return jnp.concatenate([a[:n_ctx][::-1], a[n_ctx:][::-1]], axis=0)


def kernel(x, c, ctx, c_ctx, w_ada, b_ada, norm1_w, norm2_w, w_in, mlstm_gate_bias, ret_decay_logit,
           mlstm_norm_w, ret_norm_w, w_out, w_router, w_gate, w_up, w_down, final_norm_w):
    B, T, D = x.shape
    Tc = ctx.shape[1]
    H = N_HEADS
    width = H * HEAD_DIM
    assert D == 2 * width and w_ada.shape[0] == 1
    assert T % CHUNK == 0 and Tc % CHUNK == 0 and T % GRID_W == 0
    n_lat, n_ctx = T // CHUNK, Tc // CHUNK
    n_experts = w_router.shape[2]
    cap = EC_CAPACITY_FACTOR * T // n_experts
    n_gate = 4 * H

    pad = (-(B + 1)) % 8
    cc = jnp.concatenate([c, c_ctx[None], jnp.zeros((pad, D), F32)], axis=0)
    mod = _ada(cc, w_ada[0], b_ada[0])
    sh1, sc1, g1, sh2, sc2, g2 = [mod[:B, None, i * D:(i + 1) * D] for i in range(6)]
    csh1, csc1 = [jnp.broadcast_to(mod[B, i * D:(i + 1) * D], (B, 1, D)) for i in range(2)]

    w_in0 = w_in[0]
    wm = w_in0[:, :4 * width].astype(BF16)
    wg = jnp.pad(w_in0[:, 4 * width:4 * width + n_gate], ((0, 0), (0, LANES - n_gate))).astype(BF16)
    wr = w_in0[:, 4 * width + n_gate:].astype(BF16)
    gb = jnp.pad(mlstm_gate_bias[0], (0, LANES - n_gate)).reshape(1, LANES)
    nw1 = norm1_w[0].reshape(1, D)
    cos, sin = _rope_tables(T)
    pm, gl, pr = _inproj(x, nw1, sh1, sc1, wm, wg, wr, gb, cos, sin)
    pmc, gc, prc = _inproj(ctx, nw1, csh1, csc1, wm, wg, wr, gb,
                           jnp.ones((Tc, width), F32), jnp.zeros((Tc, width), F32))

    def gate_cols(g, lo):
        return jnp.swapaxes(g[:, :, lo:lo + H], 1, 2)
    def stream_rows(lo_f, lo_b):
        f = _scan_order(gate_cols(gl, lo_f), gate_cols(gc, lo_f), False)
        r = _scan_order(gate_cols(gl, lo_b), gate_cols(gc, lo_b), True)
        n = f.shape[0]
        return jnp.concatenate([f.reshape(n, B * H, CHUNK), r.reshape(n, B * H, CHUNK)], axis=1)
    stats = _gateprep(stream_rows(0, H), stream_rows(2 * H, 3 * H))
    n_all = n_ctx + n_lat
    st_f = stats[:, :, :B * H].reshape(n_all, 6, B, H, CHUNK)
    st_b = _time_order(stats[:, :, B * H:], n_ctx, True).reshape(n_all, 6, B, H, CHUNK)
    a_i, w_i, mm_i, en_i, dec_i, m_i = range(6)
    rowv = jnp.stack([st[:, i] for i in (a_i, dec_i, w_i, m_i) for st in (st_f, st_b)], axis=-2)
    rowv = jnp.transpose(rowv, (1, 2, 0, 3, 4))
    colv = jnp.stack([st[:, i] for i in (mm_i, en_i) for st in (st_f, st_b)], axis=-1)
    colv = jnp.transpose(colv, (1, 2, 0, 3, 4)).reshape(B, H, n_all * CHUNK, 4)

    ym = _mlstm(pm, pmc, rowv, colv, mlstm_norm_w[0].reshape(H, 1, HEAD_DIM))
    dl = jnp.broadcast_to(jnp.swapaxes(ret_decay_logit[0], 0, 1)[:, :, None], (H, 2, LANES))
    yr = _ret(pr, prc, dl, ret_norm_w[0].reshape(H, 1, HEAD_DIM))

    wr_hi = w_router[0].astype(BF16)
    wr_lo = (w_router[0] - wr_hi.astype(F32)).astype(BF16)
    wrt = jnp.pad(jnp.concatenate([wr_hi, wr_lo], axis=1), ((0, 0), (0, LANES - 2 * n_experts)))
    xl, h2, aff = _outproj(ym, yr, x, w_out[0].astype(BF16), g1, norm2_w[0].reshape(1, D), sh2, sc2,
                           wrt, n_experts)

    aff_t = jnp.swapaxes(aff[:, :, :n_experts], 1, 2)
    pos, idx, gate, starts = _route(aff_t.reshape(B, n_experts, T // LANES, LANES), cap)
    by_expert = lambda a: jnp.swapaxes(a.reshape(B, n_experts, cap), 0, 1).reshape(n_experts, B * cap)
    y = _ffn(by_expert(idx), by_expert(gate)[:, :, None], h2.reshape(B * T, D),
             w_gate[0], w_up[0], w_down[0])
    return _combine(starts[:, :, 0].reshape(-1), y.reshape(n_experts, B, cap, D),
                    jnp.swapaxes(pos.reshape(B, n_experts, T), 1, 2), xl, g2,
                    final_norm_w.reshape(1, D), cap)
```

```python
import functools

import jax
import jax.numpy as jnp
from jax import lax
from jax.experimental import pallas as pl
from jax.experimental.pallas import tpu as pltpu

F32 = jnp.float32
BF16 = jnp.bfloat16

CHUNK = 128
GRID_W = 64
N_HEADS = 4
HEAD_DIM = 128
ROPE_BASE = 10000.0
EPS = 1e-6
EC_CAPACITY_FACTOR = 2
LANES = 128
VMEM_LIMIT_BYTES = 56 * 1024 * 1024
COMBINE_WINDOW = 128
MIXER_UNROLL = 8


def _params(*sem):
    return pltpu.CompilerParams(dimension_semantics=sem, vmem_limit_bytes=VMEM_LIMIT_BYTES)


def _dot(a, b):
    return jnp.dot(a, b, preferred_element_type=F32)


def _dot_nt(a, b):
    return lax.dot_general(a, b, (((1,), (1,)), ((), ())), preferred_element_type=F32)


def _dot_tn(a, b):
    return lax.dot_general(a, b, (((0,), (0,)), ((), ())), preferred_element_type=F32)


def _sigmoid(x):
    return 1.0 / (1.0 + jnp.exp(-x))


def _log_sigmoid(x):
    return jnp.minimum(x, 0.0) - jnp.log1p(jnp.exp(-jnp.abs(x)))


def _tile(n, pref):
    t = min(n, pref)
    assert n % t == 0, (n, t)
    return t


def _ada_kernel(c_ref, w_ref, b_ref, o_ref):
    cv = c_ref[...]
    s = (cv * _sigmoid(cv)).astype(BF16)
    o_ref[...] = _dot(s, w_ref[...].astype(BF16)) + b_ref[...]


def _ada(cc, w, b):
    rows, d = cc.shape
    n = w.shape[1]
    tn = _tile(n, 1024)
    return pl.pallas_call(
        _ada_kernel,
        grid=(n // tn,),
        in_specs=[pl.BlockSpec((rows, d), lambda j: (0, 0)),
                  pl.BlockSpec((d, tn), lambda j: (0, j)),
                  pl.BlockSpec((1, tn), lambda j: (0, j))],
        out_specs=pl.BlockSpec((rows, tn), lambda j: (0, j)),
        out_shape=jax.ShapeDtypeStruct((rows, n), F32),
        compiler_params=_params("arbitrary"),
    )(cc, w, b.reshape(1, n))


def _rms_mod(x, nw, sh, sc):
    ms = jnp.mean(x * x, axis=-1, keepdims=True)
    return (x * lax.rsqrt(ms + EPS) * nw) * (1.0 + sc) + sh


def _inproj_kernel(x_ref, nw_ref, sh_ref, sc_ref, wm_ref, wg_ref, wr_ref, gb_ref, cos_ref, sin_ref,
                   pm_ref, g_ref, pr_ref, *, width, k_scale):
    hb = _rms_mod(x_ref[0], nw_ref[...], sh_ref[0], sc_ref[0]).astype(BF16)
    for j in range(4):
        cols = slice(j * width, (j + 1) * width)
        acc = _dot(hb, wm_ref[:, cols])
        if j == 1:
            acc = acc * k_scale
        pm_ref[0, :, cols] = acc.astype(BF16)
    g_ref[0] = _dot(hb, wg_ref[...]) + gb_ref[...]
    cos = cos_ref[...]
    sin = sin_ref[...]
    lane = lax.broadcasted_iota(jnp.int32, cos.shape, 1)
    even = ((lane // 32) % 2) == 0
    for j in range(4):
        cols = slice(j * width, (j + 1) * width)
        acc = _dot(hb, wr_ref[:, cols])
        if j == 1:
            acc = acc * k_scale
        if j < 2:
            swapped = jnp.where(even, pltpu.roll(acc, width - 32, 1), pltpu.roll(acc, 32, 1))
            acc = acc * cos + swapped * sin
        pr_ref[0, :, cols] = acc.astype(BF16)


def _inproj(x, nw, sh, sc, wm, wg, wr, gb, cos, sin):
    b, t, d = x.shape
    width = wm.shape[1] // 4
    tm = _tile(t, 512)
    kern = functools.partial(_inproj_kernel, width=width, k_scale=HEAD_DIM ** -0.5)
    const = lambda i, j: (0, 0)
    return pl.pallas_call(
        kern,
        grid=(b, t // tm),
        in_specs=[pl.BlockSpec((1, tm, d), lambda i, j: (i, j, 0)),
                  pl.BlockSpec((1, d), const),
                  pl.BlockSpec((1, 1, d), lambda i, j: (i, 0, 0)),
                  pl.BlockSpec((1, 1, d), lambda i, j: (i, 0, 0)),
                  pl.BlockSpec(wm.shape, const),
                  pl.BlockSpec(wg.shape, const),
                  pl.BlockSpec(wr.shape, const),
                  pl.BlockSpec((1, LANES), const),
                  pl.BlockSpec((tm, width), lambda i, j: (j, 0)),
                  pl.BlockSpec((tm, width), lambda i, j: (j, 0))],
        out_specs=[pl.BlockSpec((1, tm, 4 * width), lambda i, j: (i, j, 0)),
                   pl.BlockSpec((1, tm, LANES), lambda i, j: (i, j, 0)),
                   pl.BlockSpec((1, tm, 4 * width), lambda i, j: (i, j, 0))],
        out_shape=[jax.ShapeDtypeStruct((b, t, 4 * width), BF16),
                   jax.ShapeDtypeStruct((b, t, LANES), F32),
                   jax.ShapeDtypeStruct((b, t, 4 * width), BF16)],
        compiler_params=_params("arbitrary", "arbitrary"),
    )(x, nw, sh, sc, wm, wg, wr, gb, cos, sin)


def _scan_lanes(x, op, fill, reverse):
    lane = lax.broadcasted_iota(jnp.int32, x.shape, 1)
    sh = 1
    while sh < LANES:
        if reverse:
            x = op(x, jnp.where(lane < LANES - sh, pltpu.roll(x, LANES - sh, 1), fill))
        else:
            x = op(x, jnp.where(lane >= sh, pltpu.roll(x, sh, 1), fill))
        sh *= 2
    return x


def _gateprep_kernel(ig_ref, fg_ref, o_ref, *, n_chunks):
    rows = ig_ref.shape[1]
    half = rows // 2
    row = lax.broadcasted_iota(jnp.int32, (rows, LANES), 0)
    lane = lax.broadcasted_iota(jnp.int32, (rows, LANES), 1)
    is_fwd = row < half
    last = lane == jnp.where(is_fwd, LANES - 1, 0)

    def body(c, m):
        lf = _log_sigmoid(fg_ref[c])
        ig = ig_ref[c]
        b = jnp.where(is_fwd, _scan_lanes(lf, jnp.add, 0.0, False), _scan_lanes(lf, jnp.add, 0.0, True))
        b_tot = jnp.sum(jnp.where(last, b, 0.0), axis=1, keepdims=True)
        a = ig - b
        m_new = jnp.maximum(b_tot + m, jnp.max(b_tot + a, axis=1, keepdims=True))
        w = jnp.exp(b_tot + a - m_new)
        decay = jnp.exp(b_tot + m - m_new)
        run = jnp.where(is_fwd, _scan_lanes(a, jnp.maximum, -jnp.inf, False),
                        _scan_lanes(a, jnp.maximum, -jnp.inf, True))
        mm = jnp.maximum(m, run)
        o_ref[c, 0] = a
        o_ref[c, 1] = w
        o_ref[c, 2] = mm
        o_ref[c, 3] = jnp.exp(-(b + mm))
        o_ref[c, 4] = jnp.broadcast_to(decay, (rows, LANES))
        o_ref[c, 5] = jnp.broadcast_to(m, (rows, LANES))
        return m_new

    lax.fori_loop(0, n_chunks, body, jnp.zeros((rows, 1), F32))


def _gateprep(ig, fg):
    n_chunks, rows, _ = ig.shape
    return pl.pallas_call(
        functools.partial(_gateprep_kernel, n_chunks=n_chunks),
        out_shape=jax.ShapeDtypeStruct((n_chunks, 6, rows, LANES), F32),
        compiler_params=pltpu.CompilerParams(vmem_limit_bytes=VMEM_LIMIT_BYTES),
    )(ig, fg)


def _head_norm_gate(h, gate, nw):
    return gate * (h * lax.rsqrt(jnp.mean(h * h, axis=-1, keepdims=True) + EPS)) * nw


def _mlstm_kernel(q_ref, k_ref, v_ref, o_ref, kc_ref, vc_ref, rowv_ref, colv_ref, nw_ref, y_ref,
                  sf_ref, sb_ref, cf_ref, cb_ref, *, n_lat, n_ctx, unroll):
    L = CHUNK
    ones = jnp.ones((L, HEAD_DIM), BF16)

    def update(state_ref, k, v, w_row, decay_row):
        wkt = (k.astype(F32).T * w_row).astype(BF16)
        u = _dot(wkt, jnp.concatenate([v, ones], axis=1))
        dec = jnp.concatenate([decay_row, decay_row], axis=1)
        state_ref[...] = dec * state_ref[...] + u

    cf_ref[...] = jnp.zeros_like(cf_ref)
    cb_ref[...] = jnp.zeros_like(cb_ref)
    for c in range(n_ctx):
        rows = slice(c * L, (c + 1) * L)
        update(cf_ref, kc_ref[0, rows, :], vc_ref[0, rows, :],
               rowv_ref[0, 0, c, 4:5, :], rowv_ref[0, 0, c, 2:3, :])
    for c in reversed(range(n_ctx)):
        rows = slice(c * L, (c + 1) * L)
        update(cb_ref, kc_ref[0, rows, :], vc_ref[0, rows, :],
               rowv_ref[0, 0, c, 5:6, :], rowv_ref[0, 0, c, 3:4, :])

    def state_body(i, carry):
        jf = i
        jb = n_lat - 1 - i
        rf = pl.ds(pl.multiple_of(jf * L, L), L)
        rb = pl.ds(pl.multiple_of(jb * L, L), L)
        sf_ref[jf] = cf_ref[...].astype(BF16)
        sb_ref[jb] = cb_ref[...].astype(BF16)
        update(cf_ref, k_ref[0, rf, :], v_ref[0, rf, :],
               rowv_ref[0, 0, n_ctx + jf, 4:5, :], rowv_ref[0, 0, n_ctx + jf, 2:3, :])
        update(cb_ref, k_ref[0, rb, :], v_ref[0, rb, :],
               rowv_ref[0, 0, n_ctx + jb, 5:6, :], rowv_ref[0, 0, n_ctx + jb, 3:4, :])
        return carry

    lax.fori_loop(0, n_lat, state_body, 0, unroll=unroll)

    r_i = lax.broadcasted_iota(jnp.int32, (L, L), 0)
    c_i = lax.broadcasted_iota(jnp.int32, (L, L), 1)
    tril = c_i <= r_i
    triu = c_i >= r_i
    nw = nw_ref[0]

    def out_body(j, carry):
        rows = pl.ds(pl.multiple_of(j * L, L), L)
        crow = pl.ds(pl.multiple_of((n_ctx + j) * L, L), L)
        q = q_ref[0, rows, :]
        k = k_ref[0, rows, :]
        v = v_ref[0, rows, :]
        rv = rowv_ref[0, 0, n_ctx + j]
        cv = colv_ref[0, 0, crow, :]
        mm_f = jnp.broadcast_to(cv[:, 0:1], (L, L))
        mm_b = jnp.broadcast_to(cv[:, 1:2], (L, L))
        s = _dot_nt(q, k)
        d_f = jnp.where(tril, jnp.exp(rv[0:1, :] - mm_f), 0.0)
        d_b = jnp.where(triu, jnp.exp(rv[1:2, :] - mm_b), 0.0)
        p = jnp.concatenate([s * d_f, s * d_b], axis=0).astype(BF16)
        tot = _dot(p, jnp.concatenate([v, ones], axis=1))
        inter = _dot(q, jnp.concatenate([sf_ref[j], sb_ref[j]], axis=1))
        iw_f = jnp.exp(rv[6:7, :] - mm_f)
        iw_b = jnp.exp(rv[7:8, :] - mm_b)
        D, W = HEAD_DIM, HEAD_DIM + LANES
        num_f = tot[:L, :D] + iw_f * inter[:, :D]
        den_f = tot[:L, D:] + iw_f * inter[:, D:W]
        num_b = tot[L:, :D] + iw_b * inter[:, W:W + D]
        den_b = tot[L:, D:] + iw_b * inter[:, W + D:]
        h = (num_f / jnp.maximum(jnp.abs(den_f), cv[:, 2:3])
             + num_b / jnp.maximum(jnp.abs(den_b), cv[:, 3:4]))
        gate = _sigmoid(o_ref[0, rows, :].astype(F32))
        y_ref[0, rows, :] = _head_norm_gate(h, gate, nw).astype(BF16)
        return carry

    lax.fori_loop(0, n_lat, out_body, 0, unroll=unroll)


def _mlstm(pm, pmc, rowv, colv, nw):
    b, t, _ = pm.shape
    tc = pmc.shape[1]
    n_lat, n_ctx = t // CHUNK, tc // CHUNK
    H = N_HEADS
    blk = lambda off: pl.BlockSpec((1, t, HEAD_DIM), lambda i, h: (i, 0, off + h))
    cblk = lambda off: pl.BlockSpec((1, tc, HEAD_DIM), lambda i, h: (i, 0, off + h))
    return pl.pallas_call(
        functools.partial(_mlstm_kernel, n_lat=n_lat, n_ctx=n_ctx, unroll=MIXER_UNROLL),
        grid=(b, H),
        in_specs=[blk(0), blk(H), blk(2 * H), blk(3 * H), cblk(H), cblk(2 * H),
                  pl.BlockSpec((1, 1, n_ctx + n_lat, 8, LANES), lambda i, h: (i, h, 0, 0, 0)),
                  pl.BlockSpec((1, 1, tc + t, 4), lambda i, h: (i, h, 0, 0)),
                  pl.BlockSpec((1, 1, HEAD_DIM), lambda i, h: (h, 0, 0))],
        out_specs=pl.BlockSpec((1, t, HEAD_DIM), lambda i, h: (i, 0, h)),
        out_shape=jax.ShapeDtypeStruct((b, t, H * HEAD_DIM), BF16),
        scratch_shapes=[pltpu.VMEM((n_lat, HEAD_DIM, HEAD_DIM + LANES), BF16),
                        pltpu.VMEM((n_lat, HEAD_DIM, HEAD_DIM + LANES), BF16),
                        pltpu.VMEM((HEAD_DIM, HEAD_DIM + LANES), F32),
                        pltpu.VMEM((HEAD_DIM, HEAD_DIM + LANES), F32)],
        compiler_params=_params("arbitrary", "arbitrary"),
    )(pm, pm, pm, pm, pmc, pmc, rowv, colv, nw)


def _ret_kernel(q_ref, k_ref, v_ref, o_ref, kc_ref, vc_ref, dl_ref, nw_ref, y_ref,
                sf_ref, sb_ref, rf_ref, rb_ref, *, n_lat, n_ctx, unroll):
    L = CHUNK
    lg = _log_sigmoid(dl_ref[0])
    lg_f, lg_b = lg[0:1, :], lg[1:2, :]
    r_i = lax.broadcasted_iota(jnp.int32, (L, L), 0)
    c_i = lax.broadcasted_iota(jnp.int32, (L, L), 1)
    r_f = r_i.astype(F32)
    rel = (r_i - c_i).astype(F32)
    kw_f = jnp.exp((L - 1.0 - r_f) * lg_f)
    kw_b = jnp.exp(r_f * lg_b)
    in_f = jnp.exp((r_f + 1.0) * lg_f)
    in_b = jnp.exp((L - r_f) * lg_b)
    dch_f = jnp.exp(L * lg_f)
    dch_b = jnp.exp(L * lg_b)
    d_mat = (jnp.where(rel >= 0, jnp.exp(jnp.maximum(rel, 0.0) * lg_f), 0.0)
             + jnp.where(rel <= 0, jnp.exp(jnp.maximum(-rel, 0.0) * lg_b), 0.0))

    def update(state_ref, k, v, kw, dch):
        wk = (k.astype(F32) * kw).astype(BF16)
        state_ref[...] = dch * state_ref[...] + _dot_tn(wk, v)

    rf_ref[...] = jnp.zeros_like(rf_ref)
    rb_ref[...] = jnp.zeros_like(rb_ref)
    for c in range(n_ctx):
        rows = slice(c * L, (c + 1) * L)
        update(rf_ref, kc_ref[0, rows, :], vc_ref[0, rows, :], kw_f, dch_f)
    for c in reversed(range(n_ctx)):
        rows = slice(c * L, (c + 1) * L)
        update(rb_ref, kc_ref[0, rows, :], vc_ref[0, rows, :], kw_b, dch_b)

    def state_body(i, carry):
        jb = n_lat - 1 - i
        rowf = pl.ds(pl.multiple_of(i * L, L), L)
        rowb = pl.ds(pl.multiple_of(jb * L, L), L)
        sf_ref[i] = rf_ref[...].astype(BF16)
        sb_ref[jb] = rb_ref[...].astype(BF16)
        update(rf_ref, k_ref[0, rowf, :], v_ref[0, rowf, :], kw_f, dch_f)
        update(rb_ref, k_ref[0, rowb, :], v_ref[0, rowb, :], kw_b, dch_b)
        return carry

    lax.fori_loop(0, n_lat, state_body, 0, unroll=unroll)
    nw = nw_ref[0]

    def out_body(j, carry):
        rows = pl.ds(pl.multiple_of(j * L, L), L)
        q = q_ref[0, rows, :]
        s = _dot_nt(q, k_ref[0, rows, :])
        intra = _dot((s * d_mat).astype(BF16), v_ref[0, rows, :])
        inter = _dot(q, jnp.concatenate([sf_ref[j], sb_ref[j]], axis=1))
        out = intra + in_f * inter[:, :HEAD_DIM] + in_b * inter[:, HEAD_DIM:]
        og = o_ref[0, rows, :].astype(F32)
        y_ref[0, rows, :] = _head_norm_gate(out, og * _sigmoid(og), nw).astype(BF16)
        return carry

    lax.fori_loop(0, n_lat, out_body, 0, unroll=unroll)


def _ret(pr, prc, dl, nw):
    b, t, _ = pr.shape
    tc = prc.shape[1]
    n_lat, n_ctx = t // CHUNK, tc // CHUNK
    H = N_HEADS
    blk = lambda off: pl.BlockSpec((1, t, HEAD_DIM), lambda i, h: (i, 0, off + h))
    cblk = lambda off: pl.BlockSpec((1, tc, HEAD_DIM), lambda i, h: (i, 0, off + h))
    return pl.pallas_call(
        functools.partial(_ret_kernel, n_lat=n_lat, n_ctx=n_ctx, unroll=MIXER_UNROLL),
        grid=(b, H),
        in_specs=[blk(0), blk(H), blk(2 * H), blk(3 * H), cblk(H), cblk(2 * H),
                  pl.BlockSpec((1, 2, LANES), lambda i, h: (h, 0, 0)),
                  pl.BlockSpec((1, 1, HEAD_DIM), lambda i, h: (h, 0, 0))],
        out_specs=pl.BlockSpec((1, t, HEAD_DIM), lambda i, h: (i, 0, h)),
        out_shape=jax.ShapeDtypeStruct((b, t, H * HEAD_DIM), BF16),
        scratch_shapes=[pltpu.VMEM((n_lat, HEAD_DIM, HEAD_DIM), BF16),
                        pltpu.VMEM((n_lat, HEAD_DIM, HEAD_DIM), BF16),
                        pltpu.VMEM((HEAD_DIM, HEAD_DIM), F32),
                        pltpu.VMEM((HEAD_DIM, HEAD_DIM), F32)],
        compiler_params=_params("arbitrary", "arbitrary"),
    )(pr, pr, pr, pr, prc, prc, dl, nw)


def _outproj_kernel(ym_ref, yr_ref, x_ref, wo_ref, g1_ref, nw_ref, sh_ref, sc_ref, wrt_ref,
                    xl_ref, h2_ref, aff_ref, *, n_experts, width):
    y = _dot(ym_ref[0], wo_ref[:width, :]) + _dot(yr_ref[0], wo_ref[width:, :])
    xl = x_ref[0] + g1_ref[0] * y
    xl_ref[0] = xl
    h2 = _rms_mod(xl, nw_ref[...], sh_ref[0], sc_ref[0])
    h2_ref[0] = pltpu.einshape("m(cl)->mcl", h2, l=LANES)
    h_hi = h2.astype(BF16)
    h_lo = (h2 - h_hi.astype(F32)).astype(BF16)
    p_hi = _dot(h_hi, wrt_ref[...])
    p_lo = _dot(h_lo, wrt_ref[...])
    logits = p_hi + pltpu.roll(p_hi, LANES - n_experts, 1) + p_lo
    lane = lax.broadcasted_iota(jnp.int32, logits.shape, 1)
    logits = jnp.where(lane < n_experts, logits, -jnp.inf)
    e = jnp.exp(logits - jnp.max(logits, axis=-1, keepdims=True))
    aff_ref[0] = e / jnp.sum(e, axis=-1, keepdims=True)


def _outproj(ym, yr, x, wo, g1, nw, sh, sc, wrt, n_experts):
    b, t, d = x.shape
    width = ym.shape[2]
    tm = _tile(t, 512)
    const = lambda i, j: (0, 0)
    per_b = lambda i, j: (i, 0, 0)
    tok = lambda i, j: (i, j, 0)
    return pl.pallas_call(
        functools.partial(_outproj_kernel, n_experts=n_experts, width=width),
        grid=(b, t // tm),
        in_specs=[pl.BlockSpec((1, tm, width), tok), pl.BlockSpec((1, tm, width), tok),
                  pl.BlockSpec((1, tm, d), tok), pl.BlockSpec(wo.shape, const),
                  pl.BlockSpec((1, 1, d), per_b), pl.BlockSpec((1, d), const),
                  pl.BlockSpec((1, 1, d), per_b), pl.BlockSpec((1, 1, d), per_b),
                  pl.BlockSpec(wrt.shape, const)],
        out_specs=[pl.BlockSpec((1, tm, d), tok),
                   pl.BlockSpec((1, tm, d // LANES, LANES), lambda i, j: (i, j, 0, 0)),
                   pl.BlockSpec((1, tm, LANES), tok)],
        out_shape=[jax.ShapeDtypeStruct((b, t, d), F32),
                   jax.ShapeDtypeStruct((b, t, d // LANES, LANES), F32),
                   jax.ShapeDtypeStruct((b, t, LANES), F32)],
        compiler_params=_params("arbitrary", "arbitrary"),
    )(ym, yr, x, wo, g1, nw, sh, sc, wrt)


def _route_kernel(aff_ref, pos_ref, idx_ref, gate_ref, start_ref, *, cap):
    a = aff_ref[0]
    n_e, n_b, _ = a.shape

    def count(mask):
        return jnp.sum(jnp.sum(mask.astype(jnp.int32), axis=2, keepdims=True), axis=1, keepdims=True)

    def search(i, thr):
        cand = thr | jnp.left_shift(jnp.int32(1), 30 - i)
        return jnp.where(count(a >= pltpu.bitcast(cand, F32)) >= cap, cand, thr)

    thr = lax.fori_loop(0, 31, search, jnp.zeros((n_e, 1, 1), jnp.int32))
    above = a >= pltpu.bitcast(thr + 1, F32)
    band = (a >= pltpu.bitcast(thr, F32)) & jnp.logical_not(above)
    need = cap - count(above)
    tok = (lax.broadcasted_iota(jnp.int32, a.shape, 1) * LANES
           + lax.broadcasted_iota(jnp.int32, a.shape, 2))

    def take_one(_, carry):
        sel, band, need = carry
        in_band = band > 0
        best = jnp.max(jnp.max(jnp.where(in_band, a, -1.0), axis=2, keepdims=True), axis=1, keepdims=True)
        cand = jnp.where(in_band & (a == best), tok, n_b * LANES)
        first = jnp.min(jnp.min(cand, axis=2, keepdims=True), axis=1, keepdims=True)
        take = ((tok == first) & (need > 0)).astype(jnp.int32)
        return sel | take, band - take, need - 1

    sel, _, _ = lax.fori_loop(0, jnp.max(need), take_one,
                              (above.astype(jnp.int32), band.astype(jnp.int32), need))
    sel = sel > 0

    rows = n_e * n_b
    u_i = lax.broadcasted_iota(jnp.int32, (LANES, LANES), 0)
    s_i = lax.broadcasted_iota(jnp.int32, (LANES, LANES), 1)
    incl = (u_i <= s_i).astype(BF16)
    ones = jnp.ones((LANES, LANES), BF16)
    r_i = lax.broadcasted_iota(jnp.int32, (rows, rows), 0)
    c_i = lax.broadcasted_iota(jnp.int32, (rows, rows), 1)
    before = ((r_i // n_b == c_i // n_b) & (c_i < r_i)).astype(BF16)

    x = sel.astype(F32).reshape(rows, LANES)
    xb = x.astype(BF16)
    block_tot = _dot(xb, ones).astype(BF16)
    start = _dot(before, block_tot)
    pos = jnp.where(x > 0, start + _dot(xb, incl) - x, -1.0).astype(jnp.int32)
    pos_ref[0] = pos.reshape(n_e, n_b, LANES)
    start_ref[0] = start.astype(jnp.int32)

    pad = (-rows) % LANES
    def pad_rows(m):
        return m if pad == 0 else jnp.concatenate([m, jnp.zeros((pad, LANES), m.dtype)], axis=0)
    start_p = pad_rows(start)
    end_p = pad_rows(start + block_tot.astype(F32))
    pos_p = pad_rows(pos)
    pos_p = jnp.where(pos_p < 0, 1023, pos_p)
    a_p = pad_rows(a.reshape(rows, LANES))
    a_hi = a_p.astype(BF16)
    a_mid = (a_p - a_hi.astype(F32)).astype(BF16)
    a_lo = (a_p - a_hi.astype(F32) - a_mid.astype(F32)).astype(BF16)
    per_group = LANES // n_b
    j_col = lax.broadcasted_iota(jnp.int32, (cap, LANES), 0).astype(F32)
    lane = lax.broadcasted_iota(jnp.int32, (cap, LANES), 1)
    lane_f = lane.astype(F32)
    block_f = (lane % n_b).astype(F32)
    row0 = pl.program_id(0) * (n_b * LANES)
    for g in range((rows + pad) // LANES):
        rs = slice(g * LANES, (g + 1) * LANES)
        s_row = start_p[rs].T[0:1, :]
        e_row = end_p[rs].T[0:1, :]
        hi = (pos_p[rs] >> 4).astype(F32).astype(BF16)
        lo = (pos_p[rs] & 15).astype(F32).astype(BF16)
        for el in range(per_group):
            e = g * per_group + el
            if e >= n_e:
                break
            mine = (j_col >= s_row) & (j_col < e_row) & ((lane // n_b) == el)
            mine_b = mine.astype(F32).astype(BF16)
            slots = 16.0 * _dot(mine_b, hi) + _dot(mine_b, lo)
            hit = slots == j_col
            in_block = jnp.sum(jnp.where(hit, lane_f, 0.0), axis=1, keepdims=True)
            block = jnp.sum(jnp.where(mine, block_f, 0.0), axis=1, keepdims=True)
            idx_ref[0, e] = (block * LANES + in_block).astype(jnp.int32) + row0
            a_blk = _dot(mine_b, a_hi[rs]) + _dot(mine_b, a_mid[rs]) + _dot(mine_b, a_lo[rs])
            gate_ref[0, e] = jnp.sum(jnp.where(hit, a_blk, 0.0), axis=1, keepdims=True)


def _route(aff_t, cap):
    b, n_e, n_b, _ = aff_t.shape
    spec = pl.BlockSpec((1, n_e, n_b, LANES), lambda i: (i, 0, 0, 0))
    slot_spec = pl.BlockSpec((1, n_e, cap, 1), lambda i: (i, 0, 0, 0))
    return pl.pallas_call(
        functools.partial(_route_kernel, cap=cap),
        grid=(b,), in_specs=[spec],
        out_specs=[spec, slot_spec, slot_spec, pl.BlockSpec((1, n_e * n_b, LANES), lambda i: (i, 0, 0))],
        out_shape=[jax.ShapeDtypeStruct(aff_t.shape, jnp.int32),
                   jax.ShapeDtypeStruct((b, n_e, cap, 1), jnp.int32),
                   jax.ShapeDtypeStruct((b, n_e, cap, 1), F32),
                   jax.ShapeDtypeStruct((b, n_e * n_b, LANES), jnp.int32)],
        compiler_params=_params("arbitrary"),
    )(aff_t)


GATHER_DMA_PRIORITY = 1


def _ffn_kernel(idx_ref, h_hbm, gate_ref, wg_ref, wu_ref, wd_ref, y_ref, stage_ref, x_ref, acc_ref, sem,
                *, m, per_step):
    e, f = pl.program_id(0), pl.program_id(1)
    n_e, n_f = pl.num_programs(0), pl.num_programs(1)
    rows_per_expert = stage_ref.shape[0]

    def row_copy(base, j):
        return pltpu.make_async_copy(h_hbm.at[pl.ds(idx_ref[base + j], 1)],
                                     stage_ref.at[pl.ds(j, 1)], sem.at[0])

    def wait_rows():
        pltpu.make_async_copy(h_hbm.at[pl.ds(0, rows_per_expert)], stage_ref, sem.at[0]).wait()

    @pl.when((e == 0) & (f == 0))
    def _():
        def start_row(j, carry):
            row_copy(0, j).start(priority=GATHER_DMA_PRIORITY)
            return carry
        lax.fori_loop(0, rows_per_expert, start_row, 0)

    @pl.when(f == 0)
    def _():
        wait_rows()
        x_ref[...] = pltpu.einshape("mcl->m(cl)", stage_ref[:m]).astype(BF16)
        acc_ref[...] = jnp.zeros_like(acc_ref)

    nxt = jnp.minimum(e + 1, n_e - 1) * rows_per_expert
    for i in range(per_step):
        row_copy(nxt, f * per_step + i).start(priority=GATHER_DMA_PRIORITY)

    x = x_ref[...]
    a = _dot(x, wg_ref[0].astype(BF16))
    u = _dot(x, wu_ref[0].astype(BF16))
    mid = (a * _sigmoid(a) * u).astype(BF16)
    acc_ref[...] += _dot(mid, wd_ref[0].astype(BF16))

    @pl.when(f == n_f - 1)
    def _():
        y_ref[0] = (acc_ref[...] * gate_ref[0]).astype(BF16)

    @pl.when((e == n_e - 1) & (f == n_f - 1))
    def _():
        wait_rows()


def _ffn(idx, gate, h_rows, wg, wu, wd):
    n_e, m = idx.shape
    d = h_rows.shape[1] * LANES
    ff = wg.shape[2]
    tf = _tile(ff, 256)
    n_f = ff // tf
    per_step = -(-m // (8 * n_f)) * 8
    idx = jnp.pad(idx, ((0, 0), (0, n_f * per_step - m))).reshape(-1)
    return pl.pallas_call(
        functools.partial(_ffn_kernel, m=m, per_step=per_step),
        grid_spec=pltpu.PrefetchScalarGridSpec(
            num_scalar_prefetch=1,
            grid=(n_e, n_f),
            in_specs=[pl.BlockSpec(memory_space=pl.ANY),
                      pl.BlockSpec((1, m, 1), lambda e, f, idx: (e, 0, 0)),
                      pl.BlockSpec((1, d, tf), lambda e, f, idx: (e, 0, f)),
                      pl.BlockSpec((1, d, tf), lambda e, f, idx: (e, 0, f)),
                      pl.BlockSpec((1, tf, d), lambda e, f, idx: (e, f, 0))],
            out_specs=pl.BlockSpec((1, m, d), lambda e, f, idx: (e, 0, 0)),
            scratch_shapes=[pltpu.VMEM((n_f * per_step, d // LANES, LANES), F32),
                            pltpu.VMEM((m, d), BF16),
                            pltpu.VMEM((m, d), F32),
                            pltpu.SemaphoreType.DMA((1,))]),
        out_shape=jax.ShapeDtypeStruct((n_e, m, d), BF16),
        compiler_params=_params("arbitrary", "arbitrary"),
    )(idx, h_rows, gate, wg, wu, wd)


def _combine_kernel(start_ref, y_ref, pos_ref, xl_ref, g2_ref, fw_ref, o_ref, yc_ref, acc_ref,
                    *, n_experts, cap, n_b, win):
    i, j = pl.program_id(0), pl.program_id(1)
    tt = xl_ref.shape[1]
    blocks_per_tile = tt // LANES
    lane = lax.broadcasted_iota(jnp.int32, (tt, win), 1)
    pos = pos_ref[0]
    last_tile = j == pl.num_programs(1) - 1
    first, n_pass = [], 0
    for e in range(n_experts):
        row = (i * n_experts + e) * n_b
        lo = start_ref[row + j * blocks_per_tile]
        hi = jnp.where(last_tile, cap, start_ref[row + jnp.minimum((j + 1) * blocks_per_tile, n_b - 1)])
        w0 = jnp.minimum((lo // 16) * 16, cap - win)
        first.append(w0)
        n_pass = jnp.maximum(n_pass, jnp.where(hi > lo, (hi - w0 + win - 1) // win, 0))
    acc_ref[...] = jnp.zeros_like(acc_ref)

    def one_pass(p, carry):
        hits = []
        for e in range(n_experts):
            done = first[e] + p * win
            off = pl.multiple_of(jnp.minimum(done, cap - win), 16)
            yc_ref[e * win:(e + 1) * win, :] = y_ref[e, 0, pl.ds(off, win), :]
            pe = pos[:, e:e + 1]
            hits.append(((pe - off) == lane) & (pe >= done))
        onehot = jnp.concatenate(hits, axis=1).astype(F32).astype(BF16)
        acc_ref[...] += _dot(onehot, yc_ref[...])
        return carry

    lax.fori_loop(0, n_pass, one_pass, 0)
    xl = xl_ref[0] + g2_ref[0] * acc_ref[...]
    ms = jnp.mean(xl * xl, axis=-1, keepdims=True)
    o_ref[0] = xl * lax.rsqrt(ms + EPS) * fw_ref[...]


def _combine(starts, y, pos_col, xl, g2, fw, cap):
    b, t, d = xl.shape
    n_e = y.shape[0]
    tt = _tile(t, 512)
    win = min(cap, COMBINE_WINDOW)
    assert cap % 16 == 0 and win % 16 == 0 and tt % LANES == 0
    tok = lambda i, j, st: (i, j, 0)
    return pl.pallas_call(
        functools.partial(_combine_kernel, n_experts=n_e, cap=cap, n_b=t // LANES, win=win),
        grid_spec=pltpu.PrefetchScalarGridSpec(
            num_scalar_prefetch=1,
            grid=(b, t // tt),
            in_specs=[pl.BlockSpec((n_e, 1, cap, d), lambda i, j, st: (0, i, 0, 0)),
                      pl.BlockSpec((1, tt, n_e), tok),
                      pl.BlockSpec((1, tt, d), tok),
                      pl.BlockSpec((1, 1, d), lambda i, j, st: (i, 0, 0)),
                      pl.BlockSpec((1, d), lambda i, j, st: (0, 0))],
            out_specs=pl.BlockSpec((1, tt, d), tok),
            scratch_shapes=[pltpu.VMEM((n_e * win, d), BF16), pltpu.VMEM((tt, d), F32)]),
        out_shape=jax.ShapeDtypeStruct((b, t, d), F32),
        compiler_params=_params("arbitrary", "arbitrary"),
    )(starts, y, pos_col, xl, g2, fw)


def _rope_tables(t):
    n_freq = HEAD_DIM // 4
    rows = jnp.repeat(jnp.arange(t // GRID_W, dtype=F32), GRID_W)
    cols = jnp.tile(jnp.arange(GRID_W, dtype=F32), t // GRID_W)
    inv_freq = ROPE_BASE ** (-jnp.arange(n_freq, dtype=F32) / n_freq)
    ang_r, ang_c = rows[:, None] * inv_freq, cols[:, None] * inv_freq
    cos = jnp.concatenate([jnp.cos(ang_r)] * 2 + [jnp.cos(ang_c)] * 2, axis=-1)
    sin = jnp.concatenate([-jnp.sin(ang_r), jnp.sin(ang_r), -jnp.sin(ang_c), jnp.sin(ang_c)], axis=-1)
    return jnp.tile(cos, (1, N_HEADS)), jnp.tile(sin, (1, N_HEADS))


def _scan_order(lat, ctx, reverse):
    def chunks(a):
        b, h, t = a.shape[:3]
        a = jnp.moveaxis(a.reshape((b, h, t // CHUNK, CHUNK) + a.shape[3:]), 2, 0)
        return a[::-1] if reverse else a
    return jnp.concatenate([chunks(ctx), chunks(lat)], axis=0)


def _time_order(a, n_ctx, reverse):
    if not reverse:
        return a
    return jnp.concatenate([a[:n_ctx][::-1], a[n_ctx:][::-1]], axis=0)


def kernel(x, c, ctx, c_ctx, w_ada, b_ada, norm1_w, norm2_w, w_in, mlstm_gate_bias, ret_decay_logit,
           mlstm_norm_w, ret_norm_w, w_out, w_router, w_gate, w_up, w_down, final_norm_w):
    B, T, D = x.shape
    Tc = ctx.shape[1]
    H = N_HEADS
    width = H * HEAD_DIM
    assert D == 2 * width and w_ada.shape[0] == 1
    assert T % CHUNK == 0 and Tc % CHUNK == 0 and T % GRID_W == 0
    n_lat, n_ctx = T // CHUNK, Tc // CHUNK
    n_experts = w_router.shape[2]
    cap = EC_CAPACITY_FACTOR * T // n_experts
    n_gate = 4 * H

    pad = (-(B + 1)) % 8
    cc = jnp.concatenate([c, c_ctx[None], jnp.zeros((pad, D), F32)], axis=0)
    mod = _ada(cc, w_ada[0], b_ada[0])
    sh1, sc1, g1, sh2, sc2, g2 = [mod[:B, None, i * D:(i + 1) * D] for i in range(6)]
    csh1, csc1 = [jnp.broadcast_to(mod[B, i * D:(i + 1) * D], (B, 1, D)) for i in range(2)]

    w_in0 = w_in[0]
    wm = w_in0[:, :4 * width].astype(BF16)
    wg = jnp.pad(w_in0[:, 4 * width:4 * width + n_gate], ((0, 0), (0, LANES - n_gate))).astype(BF16)
    wr = w_in0[:, 4 * width + n_gate:].astype(BF16)
    gb = jnp.pad(mlstm_gate_bias[0], (0, LANES - n_gate)).reshape(1, LANES)
    nw1 = norm1_w[0].reshape(1, D)
    cos, sin = _rope_tables(T)
    pm, gl, pr = _inproj(x, nw1, sh1, sc1, wm, wg, wr, gb, cos, sin)
    pmc, gc, prc = _inproj(ctx, nw1, csh1, csc1, wm, wg, wr, gb,
                           jnp.ones((Tc, width), F32), jnp.zeros((Tc, width), F32))

    def gate_cols(g, lo):
        return jnp.swapaxes(g[:, :, lo:lo + H], 1, 2)
    def stream_rows(lo_f, lo_b):
        f = _scan_order(gate_cols(gl, lo_f), gate_cols(gc, lo_f), False)
        r = _scan_order(gate_cols(gl, lo_b), gate_cols(gc, lo_b), True)
        n = f.shape[0]
        return jnp.concatenate([f.reshape(n, B * H, CHUNK), r.reshape(n, B * H, CHUNK)], axis=1)
    stats = _gateprep(stream_rows(0, H), stream_rows(2 * H, 3 * H))
    n_all = n_ctx + n_lat
    st_f = stats[:, :, :B * H].reshape(n_all, 6, B, H, CHUNK)
    st_b = _time_order(stats[:, :, B * H:], n_ctx, True).reshape(n_all, 6, B, H, CHUNK)
    a_i, w_i, mm_i, en_i, dec_i, m_i = range(6)
    rowv = jnp.stack([st[:, i] for i in (a_i, dec_i, w_i, m_i) for st in (st_f, st_b)], axis=-2)
    rowv = jnp.transpose(rowv, (1, 2, 0, 3, 4))
    colv = jnp.stack([st[:, i] for i in (mm_i, en_i) for st in (st_f, st_b)], axis=-1)
    colv = jnp.transpose(colv, (1, 2, 0, 3, 4)).reshape(B, H, n_all * CHUNK, 4)

    ym = _mlstm(pm, pmc, rowv, colv, mlstm_norm_w[0].reshape(H, 1, HEAD_DIM))
    dl = jnp.broadcast_to(jnp.swapaxes(ret_decay_logit[0], 0, 1)[:, :, None], (H, 2, LANES))
    yr = _ret(pr, prc, dl, ret_norm_w[0].reshape(H, 1, HEAD_DIM))

    wr_hi = w_router[0].astype(BF16)
    wr_lo = (w_router[0] - wr_hi.astype(F32)).astype(BF16)
    wrt = jnp.pad(jnp.concatenate([wr_hi, wr_lo], axis=1), ((0, 0), (0, LANES - 2 * n_experts)))
    xl, h2, aff = _outproj(ym, yr, x, w_out[0].astype(BF16), g1, norm2_w[0].reshape(1, D), sh2, sc2,
                           wrt, n_experts)

    aff_t = jnp.swapaxes(aff[:, :, :n_experts], 1, 2)
    pos, idx, gate, starts = _route(aff_t.reshape(B, n_experts, T // LANES, LANES), cap)
    by_expert = lambda a: jnp.swapaxes(a.reshape(B, n_experts, cap), 0, 1).reshape(n_experts, B * cap)
    y = _ffn(by_expert(idx), by_expert(gate)[:, :, None], h2.reshape(B * T, D // LANES, LANES),
             w_gate[0], w_up[0], w_down[0])
    return _combine(starts[:, :, 0].reshape(-1), y.reshape(n_experts, B, cap, D),
                    jnp.swapaxes(pos.reshape(B, n_experts, T), 1, 2), xl, g2,
                    final_norm_w.reshape(1, D), cap)
```

```python
import functools

import jax
import jax.numpy as jnp
from jax import lax
from jax.experimental import pallas as pl
from jax.experimental.pallas import tpu as pltpu

F32 = jnp.float32
BF16 = jnp.bfloat16

CHUNK = 128
GRID_W = 64
N_HEADS = 4
HEAD_DIM = 128
N_GATE = 4 * N_HEADS
ROPE_BASE = 10000.0
EPS = 1e-6
EC_CAPACITY_FACTOR = 2
LANES = 128
VMEM_LIMIT_BYTES = 56 * 1024 * 1024
COMBINE_WINDOW = 128
MIXER_UNROLL = 8


def _params(*sem):
    return pltpu.CompilerParams(dimension_semantics=sem, vmem_limit_bytes=VMEM_LIMIT_BYTES)


def _dot(a, b):
    return jnp.dot(a, b, preferred_element_type=F32)


def _dot_nt(a, b):
    return lax.dot_general(a, b, (((1,), (1,)), ((), ())), preferred_element_type=F32)


def _dot_tn(a, b):
    return lax.dot_general(a, b, (((0,), (0,)), ((), ())), preferred_element_type=F32)


def _sigmoid(x):
    return 1.0 / (1.0 + jnp.exp(-x))


def _log_sigmoid(x):
    return jnp.minimum(x, 0.0) - jnp.log1p(jnp.exp(-jnp.abs(x)))


def _col_to_row(col):
    return jnp.broadcast_to(col, (col.shape[0], LANES)).T[0:1, :]


def _row_to_cols(row):
    pieces = [jnp.broadcast_to(row[:, c:c + LANES], (LANES, LANES)).T for c in range(0, row.shape[1], LANES)]
    return pieces[0] if len(pieces) == 1 else jnp.concatenate(pieces, axis=0)


def _tile(n, pref):
    t = min(n, pref)
    assert n % t == 0, (n, t)
    return t


def _ada_kernel(c_ref, w_ref, b_ref, o_ref):
    cv = c_ref[...]
    s = (cv * _sigmoid(cv)).astype(BF16)
    o_ref[...] = _dot(s, w_ref[...].astype(BF16)) + b_ref[...]


def _ada(cc, w, b):
    rows, d = cc.shape
    n = w.shape[1]
    tn = _tile(n, 1024)
    return pl.pallas_call(
        _ada_kernel,
        grid=(n // tn,),
        in_specs=[pl.BlockSpec((rows, d), lambda j: (0, 0)),
                  pl.BlockSpec((d, tn), lambda j: (0, j)),
                  pl.BlockSpec((1, tn), lambda j: (0, j))],
        out_specs=pl.BlockSpec((rows, tn), lambda j: (0, j)),
        out_shape=jax.ShapeDtypeStruct((rows, n), F32),
        compiler_params=_params("arbitrary"),
    )(cc, w, b.reshape(1, n))


def _rms_mod(x, nw, sh, sc):
    ms = jnp.mean(x * x, axis=-1, keepdims=True)
    return (x * lax.rsqrt(ms + EPS) * nw) * (1.0 + sc) + sh


def _inproj_kernel(x_ref, nw_ref, sh_ref, sc_ref, wm_ref, wg_ref, wr_ref, gb_ref, cos_ref, sin_ref,
                   pm_ref, g_ref, pr_ref, *, width, k_scale):
    hb = _rms_mod(x_ref[0], nw_ref[...], sh_ref[0], sc_ref[0]).astype(BF16)
    for j in range(4):
        cols = slice(j * width, (j + 1) * width)
        acc = _dot(hb, wm_ref[:, cols])
        if j == 1:
            acc = acc * k_scale
        pm_ref[0, :, cols] = acc.astype(BF16)
    g_ref[0] = (_dot(hb, wg_ref[...]) + gb_ref[...]).T[:g_ref.shape[1]]
    cos = cos_ref[...]
    sin = sin_ref[...]
    lane = lax.broadcasted_iota(jnp.int32, cos.shape, 1)
    even = ((lane // 32) % 2) == 0
    for j in range(4):
        cols = slice(j * width, (j + 1) * width)
        acc = _dot(hb, wr_ref[:, cols])
        if j == 1:
            acc = acc * k_scale
        if j < 2:
            swapped = jnp.where(even, pltpu.roll(acc, width - 32, 1), pltpu.roll(acc, 32, 1))
            acc = acc * cos + swapped * sin
        pr_ref[0, :, cols] = acc.astype(BF16)


def _inproj(x, nw, sh, sc, wm, wg, wr, gb, cos, sin):
    b, t, d = x.shape
    width = wm.shape[1] // 4
    tm = _tile(t, 512)
    kern = functools.partial(_inproj_kernel, width=width, k_scale=HEAD_DIM ** -0.5)
    const = lambda i, j: (0, 0)
    return pl.pallas_call(
        kern,
        grid=(b, t // tm),
        in_specs=[pl.BlockSpec((1, tm, d), lambda i, j: (i, j, 0)),
                  pl.BlockSpec((1, d), const),
                  pl.BlockSpec((1, 1, d), lambda i, j: (i, 0, 0)),
                  pl.BlockSpec((1, 1, d), lambda i, j: (i, 0, 0)),
                  pl.BlockSpec(wm.shape, const),
                  pl.BlockSpec(wg.shape, const),
                  pl.BlockSpec(wr.shape, const),
                  pl.BlockSpec((1, LANES), const),
                  pl.BlockSpec((tm, width), lambda i, j: (j, 0)),
                  pl.BlockSpec((tm, width), lambda i, j: (j, 0))],
        out_specs=[pl.BlockSpec((1, tm, 4 * width), lambda i, j: (i, j, 0)),
                   pl.BlockSpec((1, N_GATE, tm), lambda i, j: (i, 0, j)),
                   pl.BlockSpec((1, tm, 4 * width), lambda i, j: (i, j, 0))],
        out_shape=[jax.ShapeDtypeStruct((b, t, 4 * width), BF16),
                   jax.ShapeDtypeStruct((b, N_GATE, t), F32),
                   jax.ShapeDtypeStruct((b, t, 4 * width), BF16)],
        compiler_params=_params("arbitrary", "arbitrary"),
    )(x, nw, sh, sc, wm, wg, wr, gb, cos, sin)


def _scan_lanes(x, op, fill, reverse):
    lane = lax.broadcasted_iota(jnp.int32, x.shape, 1)
    sh = 1
    while sh < LANES:
        if reverse:
            x = op(x, jnp.where(lane < LANES - sh, pltpu.roll(x, LANES - sh, 1), fill))
        else:
            x = op(x, jnp.where(lane >= sh, pltpu.roll(x, sh, 1), fill))
        sh *= 2
    return x


def _gateprep_kernel(gl_ref, gc_ref, o_ref, *, n_lat, n_ctx):
    n_b = gl_ref.shape[0]
    half = N_GATE // 2
    rows = n_b * half
    row = lax.broadcasted_iota(jnp.int32, (rows, LANES), 0)
    lane = lax.broadcasted_iota(jnp.int32, (rows, LANES), 1)
    is_fwd = (row % half) < N_HEADS
    last = lane == jnp.where(is_fwd, LANES - 1, 0)

    def chunk_rows(ref, lo, c_fwd, c_bwd):
        def at(c):
            cols = (slice(c * LANES, (c + 1) * LANES) if isinstance(c, int)
                    else pl.ds(pl.multiple_of(c * LANES, LANES), LANES))
            return jnp.concatenate([ref[s, lo:lo + half, cols] for s in range(n_b)], axis=0)
        return jnp.where(is_fwd, at(c_fwd), at(c_bwd))

    def step(c, ig, fg, m):
        lf = _log_sigmoid(fg)
        b = jnp.where(is_fwd, _scan_lanes(lf, jnp.add, 0.0, False), _scan_lanes(lf, jnp.add, 0.0, True))
        b_tot = jnp.sum(jnp.where(last, b, 0.0), axis=1, keepdims=True)
        a = ig - b
        m_new = jnp.maximum(b_tot + m, jnp.max(b_tot + a, axis=1, keepdims=True))
        w = jnp.exp(b_tot + a - m_new)
        decay = jnp.exp(b_tot + m - m_new)
        run = jnp.where(is_fwd, _scan_lanes(a, jnp.maximum, -jnp.inf, False),
                        _scan_lanes(a, jnp.maximum, -jnp.inf, True))
        mm = jnp.maximum(m, run)
        o_ref[c, 0] = a
        o_ref[c, 1] = w
        o_ref[c, 2] = mm
        o_ref[c, 3] = jnp.exp(-(b + mm))
        o_ref[c, 4] = jnp.broadcast_to(decay, (rows, LANES))
        o_ref[c, 5] = jnp.broadcast_to(m, (rows, LANES))
        return m_new

    m = jnp.zeros((rows, 1), F32)
    for c in range(n_ctx):
        m = step(c, chunk_rows(gc_ref, 0, c, n_ctx - 1 - c), chunk_rows(gc_ref, half, c, n_ctx - 1 - c), m)

    def latent(i, m):
        return step(n_ctx + i, chunk_rows(gl_ref, 0, i, n_lat - 1 - i),
                    chunk_rows(gl_ref, half, i, n_lat - 1 - i), m)

    lax.fori_loop(0, n_lat, latent, m, unroll=min(n_lat, MIXER_UNROLL))


def _gateprep(gl, gc):
    n_lat, n_ctx = gl.shape[2] // CHUNK, gc.shape[2] // CHUNK
    return pl.pallas_call(
        functools.partial(_gateprep_kernel, n_lat=n_lat, n_ctx=n_ctx),
        out_shape=jax.ShapeDtypeStruct((n_ctx + n_lat, 6, gl.shape[0] * N_GATE // 2, LANES), F32),
        compiler_params=pltpu.CompilerParams(vmem_limit_bytes=VMEM_LIMIT_BYTES),
    )(gl, gc)


def _head_norm_gate(h, gate, nw):
    return gate * (h * lax.rsqrt(jnp.mean(h * h, axis=-1, keepdims=True) + EPS)) * nw


def _mlstm_kernel(q_ref, k_ref, v_ref, o_ref, kc_ref, vc_ref, st_ref, nw_ref, y_ref,
                  sf_ref, sb_ref, cf_ref, cb_ref, *, n_lat, n_ctx, unroll):
    L = CHUNK
    A, WT, MM, EN, DECAY, M0 = range(6)
    ones = jnp.ones((L, HEAD_DIM), BF16)
    stream_f = pl.program_id(0) * (N_GATE // 2) + pl.program_id(1)
    stream_b = stream_f + N_HEADS

    def stat(step, plane, stream):
        return st_ref[step, plane, pl.ds(stream, 1), :]

    def to_rows(row):
        return jnp.broadcast_to(row, (L, L)).T

    def update(state_ref, k, v, step, stream):
        wkt = (k.astype(F32).T * stat(step, WT, stream)).astype(BF16)
        u = _dot(wkt, jnp.concatenate([v, ones], axis=1))
        decay = stat(step, DECAY, stream)
        state_ref[...] = jnp.concatenate([decay, decay], axis=1) * state_ref[...] + u

    cf_ref[...] = jnp.zeros_like(cf_ref)
    cb_ref[...] = jnp.zeros_like(cb_ref)
    for c in range(n_ctx):
        rf = slice(c * L, (c + 1) * L)
        rb = slice((n_ctx - 1 - c) * L, (n_ctx - c) * L)
        update(cf_ref, kc_ref[0, rf, :], vc_ref[0, rf, :], c, stream_f)
        update(cb_ref, kc_ref[0, rb, :], vc_ref[0, rb, :], c, stream_b)

    def state_body(i, carry):
        jb = n_lat - 1 - i
        rf = pl.ds(pl.multiple_of(i * L, L), L)
        rb = pl.ds(pl.multiple_of(jb * L, L), L)
        sf_ref[i] = cf_ref[...].astype(BF16)
        sb_ref[jb] = cb_ref[...].astype(BF16)
        update(cf_ref, k_ref[0, rf, :], v_ref[0, rf, :], n_ctx + i, stream_f)
        update(cb_ref, k_ref[0, rb, :], v_ref[0, rb, :], n_ctx + i, stream_b)
        return carry

    lax.fori_loop(0, n_lat, state_body, 0, unroll=unroll)

    r_i = lax.broadcasted_iota(jnp.int32, (L, L), 0)
    c_i = lax.broadcasted_iota(jnp.int32, (L, L), 1)
    tril = c_i <= r_i
    triu = c_i >= r_i
    nw = nw_ref[0]

    def out_body(j, carry):
        rows = pl.ds(pl.multiple_of(j * L, L), L)
        step_f = n_ctx + j
        step_b = n_ctx + n_lat - 1 - j
        q = q_ref[0, rows, :]
        k = k_ref[0, rows, :]
        v = v_ref[0, rows, :]
        mm_f = to_rows(stat(step_f, MM, stream_f))
        mm_b = to_rows(stat(step_b, MM, stream_b))
        s = _dot_nt(q, k)
        d_f = jnp.where(tril, jnp.exp(stat(step_f, A, stream_f) - mm_f), 0.0)
        d_b = jnp.where(triu, jnp.exp(stat(step_b, A, stream_b) - mm_b), 0.0)
        p = jnp.concatenate([s * d_f, s * d_b], axis=0).astype(BF16)
        tot = _dot(p, jnp.concatenate([v, ones], axis=1))
        inter = _dot(q, jnp.concatenate([sf_ref[j], sb_ref[j]], axis=1))
        iw_f = jnp.exp(stat(step_f, M0, stream_f) - mm_f)
        iw_b = jnp.exp(stat(step_b, M0, stream_b) - mm_b)
        D, W = HEAD_DIM, HEAD_DIM + LANES
        num_f = tot[:L, :D] + iw_f * inter[:, :D]
        den_f = tot[:L, D:] + iw_f * inter[:, D:W]
        num_b = tot[L:, :D] + iw_b * inter[:, W:W + D]
        den_b = tot[L:, D:] + iw_b * inter[:, W + D:]
        h = (num_f / jnp.maximum(jnp.abs(den_f), to_rows(stat(step_f, EN, stream_f)))
             + num_b / jnp.maximum(jnp.abs(den_b), to_rows(stat(step_b, EN, stream_b))))
        gate = _sigmoid(o_ref[0, rows, :].astype(F32))
        y_ref[0, rows, :] = _head_norm_gate(h, gate, nw).astype(BF16)
        return carry

    lax.fori_loop(0, n_lat, out_body, 0, unroll=unroll)


def _mlstm(pm, pmc, stats, nw):
    b, t, _ = pm.shape
    tc = pmc.shape[1]
    n_lat, n_ctx = t // CHUNK, tc // CHUNK
    H = N_HEADS
    blk = lambda off: pl.BlockSpec((1, t, HEAD_DIM), lambda i, h: (i, 0, off + h))
    cblk = lambda off: pl.BlockSpec((1, tc, HEAD_DIM), lambda i, h: (i, 0, off + h))
    return pl.pallas_call(
        functools.partial(_mlstm_kernel, n_lat=n_lat, n_ctx=n_ctx, unroll=MIXER_UNROLL),
        grid=(b, H),
        in_specs=[blk(0), blk(H), blk(2 * H), blk(3 * H), cblk(H), cblk(2 * H),
                  pl.BlockSpec(stats.shape, lambda i, h: (0, 0, 0, 0)),
                  pl.BlockSpec((1, 1, HEAD_DIM), lambda i, h: (h, 0, 0))],
        out_specs=pl.BlockSpec((1, t, HEAD_DIM), lambda i, h: (i, 0, h)),
        out_shape=jax.ShapeDtypeStruct((b, t, H * HEAD_DIM), BF16),
        scratch_shapes=[pltpu.VMEM((n_lat, HEAD_DIM, HEAD_DIM + LANES), BF16),
                        pltpu.VMEM((n_lat, HEAD_DIM, HEAD_DIM + LANES), BF16),
                        pltpu.VMEM((HEAD_DIM, HEAD_DIM + LANES), F32),
                        pltpu.VMEM((HEAD_DIM, HEAD_DIM + LANES), F32)],
        compiler_params=_params("arbitrary", "arbitrary"),
    )(pm, pm, pm, pm, pmc, pmc, stats, nw)


def _ret_kernel(q_ref, k_ref, v_ref, o_ref, kc_ref, vc_ref, dl_ref, nw_ref, y_ref,
                sf_ref, sb_ref, rf_ref, rb_ref, *, n_lat, n_ctx, unroll):
    L = CHUNK
    lg = _log_sigmoid(dl_ref[0])
    lg_f, lg_b = lg[0:1, :], lg[1:2, :]
    r_i = lax.broadcasted_iota(jnp.int32, (L, L), 0)
    c_i = lax.broadcasted_iota(jnp.int32, (L, L), 1)
    r_f = r_i.astype(F32)
    rel = (r_i - c_i).astype(F32)
    kw_f = jnp.exp((L - 1.0 - r_f) * lg_f)
    kw_b = jnp.exp(r_f * lg_b)
    in_f = jnp.exp((r_f + 1.0) * lg_f)
    in_b = jnp.exp((L - r_f) * lg_b)
    dch_f = jnp.exp(L * lg_f)
    dch_b = jnp.exp(L * lg_b)
    d_mat = (jnp.where(rel >= 0, jnp.exp(jnp.maximum(rel, 0.0) * lg_f), 0.0)
             + jnp.where(rel <= 0, jnp.exp(jnp.maximum(-rel, 0.0) * lg_b), 0.0))

    def update(state_ref, k, v, kw, dch):
        wk = (k.astype(F32) * kw).astype(BF16)
        state_ref[...] = dch * state_ref[...] + _dot_tn(wk, v)

    rf_ref[...] = jnp.zeros_like(rf_ref)
    rb_ref[...] = jnp.zeros_like(rb_ref)
    for c in range(n_ctx):
        rows = slice(c * L, (c + 1) * L)
        update(rf_ref, kc_ref[0, rows, :], vc_ref[0, rows, :], kw_f, dch_f)
    for c in reversed(range(n_ctx)):
        rows = slice(c * L, (c + 1) * L)
        update(rb_ref, kc_ref[0, rows, :], vc_ref[0, rows, :], kw_b, dch_b)

    def state_body(i, carry):
        jb = n_lat - 1 - i
        rowf = pl.ds(pl.multiple_of(i * L, L), L)
        rowb = pl.ds(pl.multiple_of(jb * L, L), L)
        sf_ref[i] = rf_ref[...].astype(BF16)
        sb_ref[jb] = rb_ref[...].astype(BF16)
        update(rf_ref, k_ref[0, rowf, :], v_ref[0, rowf, :], kw_f, dch_f)
        update(rb_ref, k_ref[0, rowb, :], v_ref[0, rowb, :], kw_b, dch_b)
        return carry

    lax.fori_loop(0, n_lat, state_body, 0, unroll=unroll)
    nw = nw_ref[0]

    def out_body(j, carry):
        rows = pl.ds(pl.multiple_of(j * L, L), L)
        q = q_ref[0, rows, :]
        s = _dot_nt(q, k_ref[0, rows, :])
        intra = _dot((s * d_mat).astype(BF16), v_ref[0, rows, :])
        inter = _dot(q, jnp.concatenate([sf_ref[j], sb_ref[j]], axis=1))
        out = intra + in_f * inter[:, :HEAD_DIM] + in_b * inter[:, HEAD_DIM:]
        og = o_ref[0, rows, :].astype(F32)
        y_ref[0, rows, :] = _head_norm_gate(out, og * _sigmoid(og), nw).astype(BF16)
        return carry

    lax.fori_loop(0, n_lat, out_body, 0, unroll=unroll)


def _ret(pr, prc, dl, nw):
    b, t, _ = pr.shape
    tc = prc.shape[1]
    n_lat, n_ctx = t // CHUNK, tc // CHUNK
    H = N_HEADS
    blk = lambda off: pl.BlockSpec((1, t, HEAD_DIM), lambda i, h: (i, 0, off + h))
    cblk = lambda off: pl.BlockSpec((1, tc, HEAD_DIM), lambda i, h: (i, 0, off + h))
    return pl.pallas_call(
        functools.partial(_ret_kernel, n_lat=n_lat, n_ctx=n_ctx, unroll=MIXER_UNROLL),
        grid=(b, H),
        in_specs=[blk(0), blk(H), blk(2 * H), blk(3 * H), cblk(H), cblk(2 * H),
                  pl.BlockSpec((1, 2, LANES), lambda i, h: (h, 0, 0)),
                  pl.BlockSpec((1, 1, HEAD_DIM), lambda i, h: (h, 0, 0))],
        out_specs=pl.BlockSpec((1, t, HEAD_DIM), lambda i, h: (i, 0, h)),
        out_shape=jax.ShapeDtypeStruct((b, t, H * HEAD_DIM), BF16),
        scratch_shapes=[pltpu.VMEM((n_lat, HEAD_DIM, HEAD_DIM), BF16),
                        pltpu.VMEM((n_lat, HEAD_DIM, HEAD_DIM), BF16),
                        pltpu.VMEM((HEAD_DIM, HEAD_DIM), F32),
                        pltpu.VMEM((HEAD_DIM, HEAD_DIM), F32)],
        compiler_params=_params("arbitrary", "arbitrary"),
    )(pr, pr, pr, pr, prc, prc, dl, nw)


def _outproj_kernel(ym_ref, yr_ref, x_ref, wo_ref, g1_ref, nw_ref, sh_ref, sc_ref, wrt_ref,
                    xl_ref, h2_ref, aff_ref, *, n_experts, width):
    y = _dot(ym_ref[0], wo_ref[:width, :]) + _dot(yr_ref[0], wo_ref[width:, :])
    xl = x_ref[0] + g1_ref[0] * y
    xl_ref[0] = xl
    h2 = _rms_mod(xl, nw_ref[...], sh_ref[0], sc_ref[0])
    h2_ref[0] = h2.reshape(h2.shape[0], h2.shape[1] // LANES, LANES)
    h_hi = h2.astype(BF16)
    h_lo = (h2 - h_hi.astype(F32)).astype(BF16)
    p_hi = _dot(h_hi, wrt_ref[...])
    p_lo = _dot(h_lo, wrt_ref[...])
    logits = p_hi + pltpu.roll(p_hi, LANES - n_experts, 1) + p_lo
    lane = lax.broadcasted_iota(jnp.int32, logits.shape, 1)
    logits = jnp.where(lane < n_experts, logits, -jnp.inf)
    e = jnp.exp(logits - jnp.max(logits, axis=-1, keepdims=True))
    aff_t = (e / jnp.sum(e, axis=-1, keepdims=True)).T[:n_experts]
    aff_ref[0] = aff_t.reshape(n_experts, aff_t.shape[1] // LANES, LANES)


def _outproj(ym, yr, x, wo, g1, nw, sh, sc, wrt, n_experts):
    b, t, d = x.shape
    width = ym.shape[2]
    tm = _tile(t, 8 * LANES)
    const = lambda i, j: (0, 0)
    per_b = lambda i, j: (i, 0, 0)
    tok = lambda i, j: (i, j, 0)
    return pl.pallas_call(
        functools.partial(_outproj_kernel, n_experts=n_experts, width=width),
        grid=(b, t // tm),
        in_specs=[pl.BlockSpec((1, tm, width), tok), pl.BlockSpec((1, tm, width), tok),
                  pl.BlockSpec((1, tm, d), tok), pl.BlockSpec(wo.shape, const),
                  pl.BlockSpec((1, 1, d), per_b), pl.BlockSpec((1, d), const),
                  pl.BlockSpec((1, 1, d), per_b), pl.BlockSpec((1, 1, d), per_b),
                  pl.BlockSpec(wrt.shape, const)],
        out_specs=[pl.BlockSpec((1, tm, d), tok),
                   pl.BlockSpec((1, tm, d // LANES, LANES), lambda i, j: (i, j, 0, 0)),
                   pl.BlockSpec((1, n_experts, tm // LANES, LANES), lambda i, j: (i, 0, j, 0))],
        out_shape=[jax.ShapeDtypeStruct((b, t, d), F32),
                   jax.ShapeDtypeStruct((b, t, d // LANES, LANES), F32),
                   jax.ShapeDtypeStruct((b, n_experts, t // LANES, LANES), F32)],
        compiler_params=_params("arbitrary", "arbitrary"),
    )(ym, yr, x, wo, g1, nw, sh, sc, wrt)


def _route_kernel(aff_ref, pos_ref, idx_ref, gate_ref, start_ref, *, cap):
    a = aff_ref[0]
    n_e, n_b, _ = a.shape

    def count(mask):
        return jnp.sum(jnp.sum(mask.astype(jnp.int32), axis=2, keepdims=True), axis=1, keepdims=True)

    def search(i, thr):
        cand = thr | jnp.left_shift(jnp.int32(1), 30 - i)
        return jnp.where(count(a >= pltpu.bitcast(cand, F32)) >= cap, cand, thr)

    thr = lax.fori_loop(0, 31, search, jnp.zeros((n_e, 1, 1), jnp.int32))
    above = a >= pltpu.bitcast(thr + 1, F32)
    band = (a >= pltpu.bitcast(thr, F32)) & jnp.logical_not(above)
    need = cap - count(above)
    tok = (lax.broadcasted_iota(jnp.int32, a.shape, 1) * LANES
           + lax.broadcasted_iota(jnp.int32, a.shape, 2))

    def take_one(_, carry):
        sel, band, need = carry
        in_band = band > 0
        best = jnp.max(jnp.max(jnp.where(in_band, a, -1.0), axis=2, keepdims=True), axis=1, keepdims=True)
        cand = jnp.where(in_band & (a == best), tok, n_b * LANES)
        first = jnp.min(jnp.min(cand, axis=2, keepdims=True), axis=1, keepdims=True)
        take = ((tok == first) & (need > 0)).astype(jnp.int32)
        return sel | take, band - take, need - 1

    sel, _, _ = lax.fori_loop(0, jnp.max(need), take_one,
                              (above.astype(jnp.int32), band.astype(jnp.int32), need))
    sel = sel > 0

    rows = n_e * n_b
    u_i = lax.broadcasted_iota(jnp.int32, (LANES, LANES), 0)
    s_i = lax.broadcasted_iota(jnp.int32, (LANES, LANES), 1)
    incl = (u_i <= s_i).astype(BF16)
    ones = jnp.ones((LANES, LANES), BF16)
    r_i = lax.broadcasted_iota(jnp.int32, (rows, rows), 0)
    c_i = lax.broadcasted_iota(jnp.int32, (rows, rows), 1)
    before = ((r_i // n_b == c_i // n_b) & (c_i < r_i)).astype(BF16)

    x = sel.astype(F32).reshape(rows, LANES)
    xb = x.astype(BF16)
    block_tot = _dot(xb, ones).astype(BF16)
    start = _dot(before, block_tot)
    pos = jnp.where(x > 0, start + _dot(xb, incl) - x, -1.0).astype(jnp.int32)
    pos_ref[0] = pos.reshape(n_e, n_b, LANES)
    start_ref[0] = start.astype(jnp.int32)

    pad = (-rows) % LANES
    def pad_rows(m):
        return m if pad == 0 else jnp.concatenate([m, jnp.zeros((pad, LANES), m.dtype)], axis=0)
    start_p = pad_rows(start)
    end_p = pad_rows(start + block_tot.astype(F32))
    pos_p = pad_rows(pos)
    pos_p = jnp.where(pos_p < 0, 1023, pos_p)
    a_p = pad_rows(a.reshape(rows, LANES))
    a_hi = a_p.astype(BF16)
    a_mid = (a_p - a_hi.astype(F32)).astype(BF16)
    a_lo = (a_p - a_hi.astype(F32) - a_mid.astype(F32)).astype(BF16)
    per_group = LANES // n_b
    j_col = lax.broadcasted_iota(jnp.int32, (cap, LANES), 0).astype(F32)
    lane = lax.broadcasted_iota(jnp.int32, (cap, LANES), 1)
    lane_f = lane.astype(F32)
    block_f = (lane % n_b).astype(F32)
    row0 = pl.program_id(0) * (n_b * LANES)
    for g in range((rows + pad) // LANES):
        rs = slice(g * LANES, (g + 1) * LANES)
        s_row = start_p[rs].T[0:1, :]
        e_row = end_p[rs].T[0:1, :]
        hi = (pos_p[rs] >> 4).astype(F32).astype(BF16)
        lo = (pos_p[rs] & 15).astype(F32).astype(BF16)
        for el in range(per_group):
            e = g * per_group + el
            if e >= n_e:
                break
            mine = (j_col >= s_row) & (j_col < e_row) & ((lane // n_b) == el)
            mine_b = mine.astype(F32).astype(BF16)
            slots = 16.0 * _dot(mine_b, hi) + _dot(mine_b, lo)
            hit = slots == j_col
            in_block = jnp.sum(jnp.where(hit, lane_f, 0.0), axis=1, keepdims=True)
            block = jnp.sum(jnp.where(mine, block_f, 0.0), axis=1, keepdims=True)
            a_blk = _dot(mine_b, a_hi[rs]) + _dot(mine_b, a_mid[rs]) + _dot(mine_b, a_lo[rs])
            gate = jnp.sum(jnp.where(hit, a_blk, 0.0), axis=1, keepdims=True)
            idx_ref[0, e] = _col_to_row(block * LANES + in_block).astype(jnp.int32) + row0
            gate_ref[0, e] = _col_to_row(gate)


def _route(aff_t, cap):
    b, n_e, n_b, _ = aff_t.shape
    spec = pl.BlockSpec((1, n_e, n_b, LANES), lambda i: (i, 0, 0, 0))
    slot_spec = pl.BlockSpec((1, n_e, 1, cap), lambda i: (i, 0, 0, 0))
    return pl.pallas_call(
        functools.partial(_route_kernel, cap=cap),
        grid=(b,), in_specs=[spec],
        out_specs=[spec, slot_spec, slot_spec, pl.BlockSpec((1, n_e * n_b, LANES), lambda i: (i, 0, 0))],
        out_shape=[jax.ShapeDtypeStruct(aff_t.shape, jnp.int32),
                   jax.ShapeDtypeStruct((b, n_e, 1, cap), jnp.int32),
                   jax.ShapeDtypeStruct((b, n_e, 1, cap), F32),
                   jax.ShapeDtypeStruct((b, n_e * n_b, LANES), jnp.int32)],
        compiler_params=_params("arbitrary"),
    )(aff_t)


GATHER_DMA_PRIORITY = 1


def _ffn_kernel(idx_ref, h_hbm, gate_ref, wg_ref, wu_ref, wd_ref, y_ref, stage_ref, x_ref, acc_ref, sem,
                *, m, per_step):
    e, f = pl.program_id(0), pl.program_id(1)
    n_e, n_f = pl.num_programs(0), pl.num_programs(1)
    rows_per_expert = stage_ref.shape[0]

    def row_copy(base, j):
        return pltpu.make_async_copy(h_hbm.at[pl.ds(idx_ref[base + j], 1)],
                                     stage_ref.at[pl.ds(j, 1)], sem.at[0])

    def wait_rows():
        pltpu.make_async_copy(h_hbm.at[pl.ds(0, rows_per_expert)], stage_ref, sem.at[0]).wait()

    @pl.when((e == 0) & (f == 0))
    def _():
        def start_row(j, carry):
            row_copy(0, j).start(priority=GATHER_DMA_PRIORITY)
            return carry
        lax.fori_loop(0, rows_per_expert, start_row, 0)

    @pl.when(f == 0)
    def _():
        wait_rows()
        x_ref[...] = stage_ref[:m].reshape(x_ref.shape).astype(BF16)
        acc_ref[...] = jnp.zeros_like(acc_ref)

    nxt = jnp.minimum(e + 1, n_e - 1) * rows_per_expert
    for i in range(per_step):
        row_copy(nxt, f * per_step + i).start(priority=GATHER_DMA_PRIORITY)

    x = x_ref[...]
    a = _dot(x, wg_ref[0].astype(BF16))
    u = _dot(x, wu_ref[0].astype(BF16))
    mid = (a * _sigmoid(a) * u).astype(BF16)
    acc_ref[...] += _dot(mid, wd_ref[0].astype(BF16))

    @pl.when(f == n_f - 1)
    def _():
        gate = _row_to_cols(gate_ref[0])
        for c in range(0, y_ref.shape[2], LANES):
            y_ref[0, :, c:c + LANES] = (acc_ref[:, c:c + LANES] * gate).astype(BF16)

    @pl.when((e == n_e - 1) & (f == n_f - 1))
    def _():
        wait_rows()


def _ffn(idx, gate, h_rows, wg, wu, wd):
    n_e, m = idx.shape
    d = h_rows.shape[1] * LANES
    ff = wg.shape[2]
    tf = _tile(ff, 256)
    n_f = ff // tf
    per_step = -(-m // (8 * n_f)) * 8
    idx = jnp.pad(idx, ((0, 0), (0, n_f * per_step - m))).reshape(-1)
    return pl.pallas_call(
        functools.partial(_ffn_kernel, m=m, per_step=per_step),
        grid_spec=pltpu.PrefetchScalarGridSpec(
            num_scalar_prefetch=1,
            grid=(n_e, n_f),
            in_specs=[pl.BlockSpec(memory_space=pl.ANY),
                      pl.BlockSpec((1, 1, m), lambda e, f, idx: (e, 0, 0)),
                      pl.BlockSpec((1, d, tf), lambda e, f, idx: (e, 0, f)),
                      pl.BlockSpec((1, d, tf), lambda e, f, idx: (e, 0, f)),
                      pl.BlockSpec((1, tf, d), lambda e, f, idx: (e, f, 0))],
            out_specs=pl.BlockSpec((1, m, d), lambda e, f, idx: (e, 0, 0)),
            scratch_shapes=[pltpu.VMEM((n_f * per_step, d // LANES, LANES), F32),
                            pltpu.VMEM((m, d), BF16),
                            pltpu.VMEM((m, d), F32),
                            pltpu.SemaphoreType.DMA((1,))]),
        out_shape=jax.ShapeDtypeStruct((n_e, m, d), BF16),
        compiler_params=_params("arbitrary", "arbitrary"),
    )(idx, h_rows, gate, wg, wu, wd)


def _combine_kernel(start_ref, y_ref, pos_ref, xl_ref, g2_ref, fw_ref, o_ref, yc_ref, acc_ref,
                    *, n_experts, cap, n_b, win):
    i, j = pl.program_id(0), pl.program_id(1)
    tt = xl_ref.shape[1]
    blocks_per_tile = tt // LANES
    lane = lax.broadcasted_iota(jnp.int32, (tt, win), 1)
    pos = pos_ref[0]
    last_tile = j == pl.num_programs(1) - 1
    first, n_pass = [], 0
    for e in range(n_experts):
        row = (i * n_experts + e) * n_b
        lo = start_ref[row + j * blocks_per_tile]
        hi = jnp.where(last_tile, cap, start_ref[row + jnp.minimum((j + 1) * blocks_per_tile, n_b - 1)])
        w0 = jnp.minimum((lo // 16) * 16, cap - win)
        first.append(w0)
        n_pass = jnp.maximum(n_pass, jnp.where(hi > lo, (hi - w0 + win - 1) // win, 0))
    acc_ref[...] = jnp.zeros_like(acc_ref)

    def one_pass(p, carry):
        hits = []
        for e in range(n_experts):
            done = first[e] + p * win
            off = pl.multiple_of(jnp.minimum(done, cap - win), 16)
            yc_ref[e * win:(e + 1) * win, :] = y_ref[e, 0, pl.ds(off, win), :]
            pe = pos[:, e:e + 1]
            hits.append(((pe - off) == lane) & (pe >= done))
        onehot = jnp.concatenate(hits, axis=1).astype(F32).astype(BF16)
        acc_ref[...] += _dot(onehot, yc_ref[...])
        return carry

    lax.fori_loop(0, n_pass, one_pass, 0)
    xl = xl_ref[0] + g2_ref[0] * acc_ref[...]
    ms = jnp.mean(xl * xl, axis=-1, keepdims=True)
    o_ref[0] = xl * lax.rsqrt(ms + EPS) * fw_ref[...]


def _combine(starts, y, pos_col, xl, g2, fw, cap):
    b, t, d = xl.shape
    n_e = y.shape[0]
    tt = _tile(t, 512)
    win = min(cap, COMBINE_WINDOW)
    assert cap % 16 == 0 and win % 16 == 0 and tt % LANES == 0
    tok = lambda i, j, st: (i, j, 0)
    return pl.pallas_call(
        functools.partial(_combine_kernel, n_experts=n_e, cap=cap, n_b=t // LANES, win=win),
        grid_spec=pltpu.PrefetchScalarGridSpec(
            num_scalar_prefetch=1,
            grid=(b, t // tt),
            in_specs=[pl.BlockSpec((n_e, 1, cap, d), lambda i, j, st: (0, i, 0, 0)),
                      pl.BlockSpec((1, tt, n_e), tok),
                      pl.BlockSpec((1, tt, d), tok),
                      pl.BlockSpec((1, 1, d), lambda i, j, st: (i, 0, 0)),
                      pl.BlockSpec((1, d), lambda i, j, st: (0, 0))],
            out_specs=pl.BlockSpec((1, tt, d), tok),
            scratch_shapes=[pltpu.VMEM((n_e * win, d), BF16), pltpu.VMEM((tt, d), F32)]),
        out_shape=jax.ShapeDtypeStruct((b, t, d), F32),
        compiler_params=_params("arbitrary", "arbitrary"),
    )(starts, y, pos_col, xl, g2, fw)


def _rope_tables(t):
    n_freq = HEAD_DIM // 4
    rows = jnp.repeat(jnp.arange(t // GRID_W, dtype=F32), GRID_W)
    cols = jnp.tile(jnp.arange(GRID_W, dtype=F32), t // GRID_W)
    inv_freq = ROPE_BASE ** (-jnp.arange(n_freq, dtype=F32) / n_freq)
    ang_r, ang_c = rows[:, None] * inv_freq, cols[:, None] * inv_freq
    cos = jnp.concatenate([jnp.cos(ang_r)] * 2 + [jnp.cos(ang_c)] * 2, axis=-1)
    sin = jnp.concatenate([-jnp.sin(ang_r), jnp.sin(ang_r), -jnp.sin(ang_c), jnp.sin(ang_c)], axis=-1)
    return jnp.tile(cos, (1, N_HEADS)), jnp.tile(sin, (1, N_HEADS))


def kernel(x, c, ctx, c_ctx, w_ada, b_ada, norm1_w, norm2_w, w_in, mlstm_gate_bias, ret_decay_logit,
           mlstm_norm_w, ret_norm_w, w_out, w_router, w_gate, w_up, w_down, final_norm_w):
    B, T, D = x.shape
    Tc = ctx.shape[1]
    H = N_HEADS
    width = H * HEAD_DIM
    assert D == 2 * width and w_ada.shape[0] == 1
    assert T % CHUNK == 0 and Tc % CHUNK == 0 and T % GRID_W == 0
    n_lat, n_ctx = T // CHUNK, Tc // CHUNK
    n_experts = w_router.shape[2]
    cap = EC_CAPACITY_FACTOR * T // n_experts
    n_gate = 4 * H

    pad = (-(B + 1)) % 8
    cc = jnp.concatenate([c, c_ctx[None], jnp.zeros((pad, D), F32)], axis=0)
    mod = _ada(cc, w_ada[0], b_ada[0])
    sh1, sc1, g1, sh2, sc2, g2 = [mod[:B, None, i * D:(i + 1) * D] for i in range(6)]
    csh1, csc1 = [jnp.broadcast_to(mod[B, i * D:(i + 1) * D], (B, 1, D)) for i in range(2)]

    w_in0 = w_in[0]
    wm = w_in0[:, :4 * width].astype(BF16)
    wg = jnp.pad(w_in0[:, 4 * width:4 * width + n_gate], ((0, 0), (0, LANES - n_gate))).astype(BF16)
    wr = w_in0[:, 4 * width + n_gate:].astype(BF16)
    gb = jnp.pad(mlstm_gate_bias[0], (0, LANES - n_gate)).reshape(1, LANES)
    nw1 = norm1_w[0].reshape(1, D)
    cos, sin = _rope_tables(T)
    pm, gl, pr = _inproj(x, nw1, sh1, sc1, wm, wg, wr, gb, cos, sin)
    pmc, gc, prc = _inproj(ctx, nw1, csh1, csc1, wm, wg, wr, gb,
                           jnp.ones((Tc, width), F32), jnp.zeros((Tc, width), F32))

    ym = _mlstm(pm, pmc, _gateprep(gl, gc), mlstm_norm_w[0].reshape(H, 1, HEAD_DIM))
    dl = jnp.broadcast_to(jnp.swapaxes(ret_decay_logit[0], 0, 1)[:, :, None], (H, 2, LANES))
    yr = _ret(pr, prc, dl, ret_norm_w[0].reshape(H, 1, HEAD_DIM))

    wr_hi = w_router[0].astype(BF16)
    wr_lo = (w_router[0] - wr_hi.astype(F32)).astype(BF16)
    wrt = jnp.pad(jnp.concatenate([wr_hi, wr_lo], axis=1), ((0, 0), (0, LANES - 2 * n_experts)))
    xl, h2, aff_t = _outproj(ym, yr, x, w_out[0].astype(BF16), g1, norm2_w[0].reshape(1, D), sh2, sc2,
                             wrt, n_experts)

    pos, idx, gate, starts = _route(aff_t, cap)
    by_expert = lambda a: jnp.swapaxes(a.reshape(B, n_experts, cap), 0, 1).reshape(n_experts, B * cap)
    y = _ffn(by_expert(idx), by_expert(gate)[:, None, :], h2.reshape(B * T, D // LANES, LANES),
             w_gate[0], w_up[0], w_down[0])
    return _combine(starts[:, :, 0].reshape(-1), y.reshape(n_experts, B, cap, D),
                    jnp.swapaxes(pos.reshape(B, n_experts, T), 1, 2), xl, g2,
                    final_norm_w.reshape(1, D), cap)
```

```python
import functools

import jax
import jax.numpy as jnp
from jax import lax
from jax.experimental import pallas as pl
from jax.experimental.pallas import tpu as pltpu

F32 = jnp.float32
BF16 = jnp.bfloat16

CHUNK = 128
GRID_W = 64
N_HEADS = 4
HEAD_DIM = 128
N_GATE = 4 * N_HEADS
ROPE_BASE = 10000.0
EPS = 1e-6
LOG2E = 1.4426950408889634
EC_CAPACITY_FACTOR = 2
LANES = 128
VMEM_LIMIT_BYTES = 56 * 1024 * 1024
COMBINE_WINDOW = 128
COMBINE_GROUP = 4
MIXER_UNROLL = 8


def _params(*sem):
    return pltpu.CompilerParams(dimension_semantics=sem, vmem_limit_bytes=VMEM_LIMIT_BYTES)


def _dot(a, b):
    return jnp.dot(a, b, preferred_element_type=F32)


def _dot_nt(a, b):
    return lax.dot_general(a, b, (((1,), (1,)), ((), ())), preferred_element_type=F32)


def _dot_tn(a, b):
    return lax.dot_general(a, b, (((0,), (0,)), ((), ())), preferred_element_type=F32)


def _sigmoid(x):
    return 1.0 / (1.0 + jnp.exp(-x))


def _log_sigmoid(x):
    return jnp.minimum(x, 0.0) - jnp.log1p(jnp.exp(-jnp.abs(x)))


def _col_to_row(col):
    return jnp.broadcast_to(col, (col.shape[0], LANES)).T[0:1, :]


def _row_to_cols(row):
    pieces = [jnp.broadcast_to(row[:, c:c + LANES], (LANES, LANES)).T for c in range(0, row.shape[1], LANES)]
    return pieces[0] if len(pieces) == 1 else jnp.concatenate(pieces, axis=0)


def _tile(n, pref):
    t = min(n, pref)
    assert n % t == 0, (n, t)
    return t


def _ada_kernel(c_ref, w_ref, b_ref, o_ref):
    cv = c_ref[...]
    s = (cv * _sigmoid(cv)).astype(BF16)
    o_ref[...] = _dot(s, w_ref[...].astype(BF16)) + b_ref[...]


def _ada(cc, w, b):
    rows, d = cc.shape
    n = w.shape[1]
    tn = _tile(n, 1024)
    return pl.pallas_call(
        _ada_kernel,
        grid=(n // tn,),
        in_specs=[pl.BlockSpec((rows, d), lambda j: (0, 0)),
                  pl.BlockSpec((d, tn), lambda j: (0, j)),
                  pl.BlockSpec((1, tn), lambda j: (0, j))],
        out_specs=pl.BlockSpec((rows, tn), lambda j: (0, j)),
        out_shape=jax.ShapeDtypeStruct((rows, n), F32),
        compiler_params=_params("arbitrary"),
    )(cc, w, b.reshape(1, n))


def _rms_mod(x, nw, sh, sc):
    ms = jnp.mean(x * x, axis=-1, keepdims=True)
    return (x * lax.rsqrt(ms + EPS) * nw) * (1.0 + sc) + sh


def _inproj_kernel(x_ref, nw_ref, sh_ref, sc_ref, wm_ref, wg_ref, wr_ref, gb_ref, cos_ref, sin_ref,
                   pm_ref, g_ref, pr_ref, *, width, k_scale):
    hb = _rms_mod(x_ref[0], nw_ref[...], sh_ref[0], sc_ref[0]).astype(BF16)
    for j in range(4):
        cols = slice(j * width, (j + 1) * width)
        acc = _dot(hb, wm_ref[:, cols])
        if j == 1:
            acc = acc * k_scale
        pm_ref[0, :, cols] = acc.astype(BF16)
    g_ref[0] = (_dot(hb, wg_ref[...]) + gb_ref[...]).T[:g_ref.shape[1]]
    cos = cos_ref[...]
    sin = sin_ref[...]
    lane = lax.broadcasted_iota(jnp.int32, cos.shape, 1)
    even = ((lane // 32) % 2) == 0
    for j in range(4):
        cols = slice(j * width, (j + 1) * width)
        acc = _dot(hb, wr_ref[:, cols])
        if j == 1:
            acc = acc * k_scale
        if j < 2:
            swapped = jnp.where(even, pltpu.roll(acc, width - 32, 1), pltpu.roll(acc, 32, 1))
            acc = acc * cos + swapped * sin
        pr_ref[0, :, cols] = acc.astype(BF16)


def _inproj(x, nw, sh, sc, wm, wg, wr, gb, cos, sin):
    b, t, d = x.shape
    width = wm.shape[1] // 4
    tm = _tile(t, 512)
    kern = functools.partial(_inproj_kernel, width=width, k_scale=HEAD_DIM ** -0.5)
    const = lambda i, j: (0, 0)
    return pl.pallas_call(
        kern,
        grid=(b, t // tm),
        in_specs=[pl.BlockSpec((1, tm, d), lambda i, j: (i, j, 0)),
                  pl.BlockSpec((1, d), const),
                  pl.BlockSpec((1, 1, d), lambda i, j: (i, 0, 0)),
                  pl.BlockSpec((1, 1, d), lambda i, j: (i, 0, 0)),
                  pl.BlockSpec(wm.shape, const),
                  pl.BlockSpec(wg.shape, const),
                  pl.BlockSpec(wr.shape, const),
                  pl.BlockSpec((1, LANES), const),
                  pl.BlockSpec((tm, width), lambda i, j: (j, 0)),
                  pl.BlockSpec((tm, width), lambda i, j: (j, 0))],
        out_specs=[pl.BlockSpec((1, tm, 4 * width), lambda i, j: (i, j, 0)),
                   pl.BlockSpec((1, N_GATE, tm), lambda i, j: (i, 0, j)),
                   pl.BlockSpec((1, tm, 4 * width), lambda i, j: (i, j, 0))],
        out_shape=[jax.ShapeDtypeStruct((b, t, 4 * width), BF16),
                   jax.ShapeDtypeStruct((b, N_GATE, t), F32),
                   jax.ShapeDtypeStruct((b, t, 4 * width), BF16)],
        compiler_params=_params("arbitrary", "arbitrary"),
    )(x, nw, sh, sc, wm, wg, wr, gb, cos, sin)


def _scan_lanes(x, op, fill, reverse):
    lane = lax.broadcasted_iota(jnp.int32, x.shape, 1)
    sh = 1
    while sh < LANES:
        if reverse:
            x = op(x, jnp.where(lane < LANES - sh, pltpu.roll(x, LANES - sh, 1), fill))
        else:
            x = op(x, jnp.where(lane >= sh, pltpu.roll(x, sh, 1), fill))
        sh *= 2
    return x


def _gateprep_kernel(gl_ref, gc_ref, o_ref, *, n_lat, n_ctx):
    n_b = gl_ref.shape[0]
    half = N_GATE // 2
    rows = n_b * half
    row = lax.broadcasted_iota(jnp.int32, (rows, LANES), 0)
    lane = lax.broadcasted_iota(jnp.int32, (rows, LANES), 1)
    is_fwd = (row % half) < N_HEADS
    last = lane == jnp.where(is_fwd, LANES - 1, 0)

    def chunk_rows(ref, lo, c_fwd, c_bwd):
        def at(c):
            cols = (slice(c * LANES, (c + 1) * LANES) if isinstance(c, int)
                    else pl.ds(pl.multiple_of(c * LANES, LANES), LANES))
            return jnp.concatenate([ref[s, lo:lo + half, cols] for s in range(n_b)], axis=0)
        return jnp.where(is_fwd, at(c_fwd), at(c_bwd))

    def step(c, ig, fg, m):
        lf = _log_sigmoid(fg)
        b = jnp.where(is_fwd, _scan_lanes(lf, jnp.add, 0.0, False), _scan_lanes(lf, jnp.add, 0.0, True))
        b_tot = jnp.sum(jnp.where(last, b, 0.0), axis=1, keepdims=True)
        a = ig - b
        m_new = jnp.maximum(b_tot + m, jnp.max(b_tot + a, axis=1, keepdims=True))
        w = jnp.exp(b_tot + a - m_new)
        decay = jnp.exp(b_tot + m - m_new)
        run = jnp.where(is_fwd, _scan_lanes(a, jnp.maximum, -jnp.inf, False),
                        _scan_lanes(a, jnp.maximum, -jnp.inf, True))
        mm = jnp.maximum(m, run)
        o_ref[c, 0] = a * LOG2E
        o_ref[c, 1] = w
        o_ref[c, 2] = mm * LOG2E
        o_ref[c, 3] = jnp.exp(-(b + mm))
        o_ref[c, 4] = jnp.broadcast_to(decay, (rows, LANES))
        o_ref[c, 5] = jnp.broadcast_to(m * LOG2E, (rows, LANES))
        return m_new

    m = jnp.zeros((rows, 1), F32)
    for c in range(n_ctx):
        m = step(c, chunk_rows(gc_ref, 0, c, n_ctx - 1 - c), chunk_rows(gc_ref, half, c, n_ctx - 1 - c), m)

    def latent(i, m):
        return step(n_ctx + i, chunk_rows(gl_ref, 0, i, n_lat - 1 - i),
                    chunk_rows(gl_ref, half, i, n_lat - 1 - i), m)

    lax.fori_loop(0, n_lat, latent, m, unroll=min(n_lat, MIXER_UNROLL))


def _gateprep(gl, gc):
    n_lat, n_ctx = gl.shape[2] // CHUNK, gc.shape[2] // CHUNK
    return pl.pallas_call(
        functools.partial(_gateprep_kernel, n_lat=n_lat, n_ctx=n_ctx),
        out_shape=jax.ShapeDtypeStruct((n_ctx + n_lat, 6, gl.shape[0] * N_GATE // 2, LANES), F32),
        compiler_params=pltpu.CompilerParams(vmem_limit_bytes=VMEM_LIMIT_BYTES),
    )(gl, gc)


def _head_norm_gate(h, gate, nw):
    return gate * (h * lax.rsqrt(jnp.mean(h * h, axis=-1, keepdims=True) + EPS)) * nw


def _mlstm_kernel(q_ref, k_ref, v_ref, o_ref, kc_ref, vc_ref, st_ref, nw_ref, y_ref,
                  sf_ref, sb_ref, cf_ref, cb_ref, *, n_lat, n_ctx, unroll):
    L = CHUNK
    A, WT, MM, EN, DECAY, M0 = range(6)
    ones = jnp.ones((L, HEAD_DIM), BF16)
    stream_f = pl.program_id(0) * (N_GATE // 2) + pl.program_id(1)
    stream_b = stream_f + N_HEADS

    def stat(step, plane, stream):
        return st_ref[step, plane, pl.ds(stream, 1), :]

    def to_rows(row):
        return jnp.broadcast_to(row, (L, L)).T

    def update(state_ref, k, v, step, stream):
        wkt = (k.astype(F32).T * stat(step, WT, stream)).astype(BF16)
        u = _dot(wkt, jnp.concatenate([v, ones], axis=1))
        decay = stat(step, DECAY, stream)
        state_ref[...] = jnp.concatenate([decay, decay], axis=1) * state_ref[...] + u

    cf_ref[...] = jnp.zeros_like(cf_ref)
    cb_ref[...] = jnp.zeros_like(cb_ref)
    for c in range(n_ctx):
        rf = slice(c * L, (c + 1) * L)
        rb = slice((n_ctx - 1 - c) * L, (n_ctx - c) * L)
        update(cf_ref, kc_ref[0, rf, :], vc_ref[0, rf, :], c, stream_f)
        update(cb_ref, kc_ref[0, rb, :], vc_ref[0, rb, :], c, stream_b)

    def state_body(i, carry):
        jb = n_lat - 1 - i
        rf = pl.ds(pl.multiple_of(i * L, L), L)
        rb = pl.ds(pl.multiple_of(jb * L, L), L)
        sf_ref[i] = cf_ref[...].astype(BF16)
        sb_ref[jb] = cb_ref[...].astype(BF16)
        update(cf_ref, k_ref[0, rf, :], v_ref[0, rf, :], n_ctx + i, stream_f)
        update(cb_ref, k_ref[0, rb, :], v_ref[0, rb, :], n_ctx + i, stream_b)
        return carry

    lax.fori_loop(0, n_lat, state_body, 0, unroll=unroll)

    r_i = lax.broadcasted_iota(jnp.int32, (L, L), 0)
    c_i = lax.broadcasted_iota(jnp.int32, (L, L), 1)
    tril = c_i <= r_i
    triu = c_i >= r_i
    nw = nw_ref[0]

    def out_body(j, carry):
        rows = pl.ds(pl.multiple_of(j * L, L), L)
        step_f = n_ctx + j
        step_b = n_ctx + n_lat - 1 - j
        q = q_ref[0, rows, :]
        k = k_ref[0, rows, :]
        v = v_ref[0, rows, :]
        mm_f = to_rows(stat(step_f, MM, stream_f))
        mm_b = to_rows(stat(step_b, MM, stream_b))
        s = _dot_nt(q, k)
        d_f = jnp.where(tril, jnp.exp2(stat(step_f, A, stream_f) - mm_f), 0.0)
        d_b = jnp.where(triu, jnp.exp2(stat(step_b, A, stream_b) - mm_b), 0.0)
        p = jnp.concatenate([s * d_f, s * d_b], axis=0).astype(BF16)
        tot = _dot(p, jnp.concatenate([v, ones], axis=1))
        inter = _dot(q, jnp.concatenate([sf_ref[j], sb_ref[j]], axis=1))
        iw_f = jnp.exp2(stat(step_f, M0, stream_f) - mm_f)
        iw_b = jnp.exp2(stat(step_b, M0, stream_b) - mm_b)
        D, W = HEAD_DIM, HEAD_DIM + LANES
        num_f = tot[:L, :D] + iw_f * inter[:, :D]
        den_f = tot[:L, D:] + iw_f * inter[:, D:W]
        num_b = tot[L:, :D] + iw_b * inter[:, W:W + D]
        den_b = tot[L:, D:] + iw_b * inter[:, W + D:]
        h = (num_f / jnp.maximum(jnp.abs(den_f), to_rows(stat(step_f, EN, stream_f)))
             + num_b / jnp.maximum(jnp.abs(den_b), to_rows(stat(step_b, EN, stream_b))))
        gate = _sigmoid(o_ref[0, rows, :].astype(F32))
        y_ref[0, rows, :] = _head_norm_gate(h, gate, nw).astype(BF16)
        return carry

    lax.fori_loop(0, n_lat, out_body, 0, unroll=unroll)


def _mlstm(pm, pmc, stats, nw):
    b, t, _ = pm.shape
    tc = pmc.shape[1]
    n_lat, n_ctx = t // CHUNK, tc // CHUNK
    H = N_HEADS
    blk = lambda off: pl.BlockSpec((1, t, HEAD_DIM), lambda i, h: (i, 0, off + h))
    cblk = lambda off: pl.BlockSpec((1, tc, HEAD_DIM), lambda i, h: (i, 0, off + h))
    return pl.pallas_call(
        functools.partial(_mlstm_kernel, n_lat=n_lat, n_ctx=n_ctx, unroll=MIXER_UNROLL),
        grid=(b, H),
        in_specs=[blk(0), blk(H), blk(2 * H), blk(3 * H), cblk(H), cblk(2 * H),
                  pl.BlockSpec(stats.shape, lambda i, h: (0, 0, 0, 0)),
                  pl.BlockSpec((1, 1, HEAD_DIM), lambda i, h: (h, 0, 0))],
        out_specs=pl.BlockSpec((1, t, HEAD_DIM), lambda i, h: (i, 0, h)),
        out_shape=jax.ShapeDtypeStruct((b, t, H * HEAD_DIM), BF16),
        scratch_shapes=[pltpu.VMEM((n_lat, HEAD_DIM, HEAD_DIM + LANES), BF16),
                        pltpu.VMEM((n_lat, HEAD_DIM, HEAD_DIM + LANES), BF16),
                        pltpu.VMEM((HEAD_DIM, HEAD_DIM + LANES), F32),
                        pltpu.VMEM((HEAD_DIM, HEAD_DIM + LANES), F32)],
        compiler_params=_params("arbitrary", "arbitrary"),
    )(pm, pm, pm, pm, pmc, pmc, stats, nw)


def _ret_kernel(q_ref, k_ref, v_ref, o_ref, kc_ref, vc_ref, dl_ref, nw_ref, y_ref,
                sf_ref, sb_ref, rf_ref, rb_ref, *, n_lat, n_ctx, unroll):
    L = CHUNK
    lg = _log_sigmoid(dl_ref[0])
    lg_f, lg_b = lg[0:1, :], lg[1:2, :]
    r_i = lax.broadcasted_iota(jnp.int32, (L, L), 0)
    c_i = lax.broadcasted_iota(jnp.int32, (L, L), 1)
    r_f = r_i.astype(F32)
    rel = (r_i - c_i).astype(F32)
    kw_f = jnp.exp((L - 1.0 - r_f) * lg_f)
    kw_b = jnp.exp(r_f * lg_b)
    in_f = jnp.exp((r_f + 1.0) * lg_f)
    in_b = jnp.exp((L - r_f) * lg_b)
    dch_f = jnp.exp(L * lg_f)
    dch_b = jnp.exp(L * lg_b)
    d_mat = (jnp.where(rel >= 0, jnp.exp(jnp.maximum(rel, 0.0) * lg_f), 0.0)
             + jnp.where(rel <= 0, jnp.exp(jnp.maximum(-rel, 0.0) * lg_b), 0.0))

    def update(state_ref, k, v, kw, dch):
        wk = (k.astype(F32) * kw).astype(BF16)
        state_ref[...] = dch * state_ref[...] + _dot_tn(wk, v)

    rf_ref[...] = jnp.zeros_like(rf_ref)
    rb_ref[...] = jnp.zeros_like(rb_ref)
    for c in range(n_ctx):
        rows = slice(c * L, (c + 1) * L)
        update(rf_ref, kc_ref[0, rows, :], vc_ref[0, rows, :], kw_f, dch_f)
    for c in reversed(range(n_ctx)):
        rows = slice(c * L, (c + 1) * L)
        update(rb_ref, kc_ref[0, rows, :], vc_ref[0, rows, :], kw_b, dch_b)

    def state_body(i, carry):
        jb = n_lat - 1 - i
        rowf = pl.ds(pl.multiple_of(i * L, L), L)
        rowb = pl.ds(pl.multiple_of(jb * L, L), L)
        sf_ref[i] = rf_ref[...].astype(BF16)
        sb_ref[jb] = rb_ref[...].astype(BF16)
        update(rf_ref, k_ref[0, rowf, :], v_ref[0, rowf, :], kw_f, dch_f)
        update(rb_ref, k_ref[0, rowb, :], v_ref[0, rowb, :], kw_b, dch_b)
        return carry

    lax.fori_loop(0, n_lat, state_body, 0, unroll=unroll)
    nw = nw_ref[0]

    def out_body(j, carry):
        rows = pl.ds(pl.multiple_of(j * L, L), L)
        q = q_ref[0, rows, :]
        s = _dot_nt(q, k_ref[0, rows, :])
        intra = _dot((s * d_mat).astype(BF16), v_ref[0, rows, :])
        inter = _dot(q, jnp.concatenate([sf_ref[j], sb_ref[j]], axis=1))
        out = intra + in_f * inter[:, :HEAD_DIM] + in_b * inter[:, HEAD_DIM:]
        og = o_ref[0, rows, :].astype(F32)
        y_ref[0, rows, :] = _head_norm_gate(out, og * _sigmoid(og), nw).astype(BF16)
        return carry

    lax.fori_loop(0, n_lat, out_body, 0, unroll=unroll)


def _ret(pr, prc, dl, nw):
    b, t, _ = pr.shape
    tc = prc.shape[1]
    n_lat, n_ctx = t // CHUNK, tc // CHUNK
    H = N_HEADS
    blk = lambda off: pl.BlockSpec((1, t, HEAD_DIM), lambda i, h: (i, 0, off + h))
    cblk = lambda off: pl.BlockSpec((1, tc, HEAD_DIM), lambda i, h: (i, 0, off + h))
    return pl.pallas_call(
        functools.partial(_ret_kernel, n_lat=n_lat, n_ctx=n_ctx, unroll=MIXER_UNROLL),
        grid=(b, H),
        in_specs=[blk(0), blk(H), blk(2 * H), blk(3 * H), cblk(H), cblk(2 * H),
                  pl.BlockSpec((1, 2, LANES), lambda i, h: (h, 0, 0)),
                  pl.BlockSpec((1, 1, HEAD_DIM), lambda i, h: (h, 0, 0))],
        out_specs=pl.BlockSpec((1, t, HEAD_DIM), lambda i, h: (i, 0, h)),
        out_shape=jax.ShapeDtypeStruct((b, t, H * HEAD_DIM), BF16),
        scratch_shapes=[pltpu.VMEM((n_lat, HEAD_DIM, HEAD_DIM), BF16),
                        pltpu.VMEM((n_lat, HEAD_DIM, HEAD_DIM), BF16),
                        pltpu.VMEM((HEAD_DIM, HEAD_DIM), F32),
                        pltpu.VMEM((HEAD_DIM, HEAD_DIM), F32)],
        compiler_params=_params("arbitrary", "arbitrary"),
    )(pr, pr, pr, pr, prc, prc, dl, nw)


def _outproj_kernel(ym_ref, yr_ref, x_ref, wo_ref, g1_ref, nw_ref, sh_ref, sc_ref, wrt_ref,
                    xl_ref, h2_ref, aff_ref, *, n_experts, width):
    y = _dot(ym_ref[0], wo_ref[:width, :]) + _dot(yr_ref[0], wo_ref[width:, :])
    xl = x_ref[0] + g1_ref[0] * y
    xl_ref[0] = xl
    h2 = _rms_mod(xl, nw_ref[...], sh_ref[0], sc_ref[0])
    h2_ref[0] = h2.reshape(h2.shape[0], h2.shape[1] // LANES, LANES)
    h_hi = h2.astype(BF16)
    h_lo = (h2 - h_hi.astype(F32)).astype(BF16)
    p_hi = _dot(h_hi, wrt_ref[...])
    p_lo = _dot(h_lo, wrt_ref[...])
    logits = p_hi + pltpu.roll(p_hi, LANES - n_experts, 1) + p_lo
    lane = lax.broadcasted_iota(jnp.int32, logits.shape, 1)
    logits = jnp.where(lane < n_experts, logits, -jnp.inf)
    e = jnp.exp(logits - jnp.max(logits, axis=-1, keepdims=True))
    aff_t = (e / jnp.sum(e, axis=-1, keepdims=True)).T[:n_experts]
    aff_ref[0] = aff_t.reshape(n_experts, aff_t.shape[1] // LANES, LANES)


def _outproj(ym, yr, x, wo, g1, nw, sh, sc, wrt, n_experts):
    b, t, d = x.shape
    width = ym.shape[2]
    tm = _tile(t, 8 * LANES)
    const = lambda i, j: (0, 0)
    per_b = lambda i, j: (i, 0, 0)
    tok = lambda i, j: (i, j, 0)
    return pl.pallas_call(
        functools.partial(_outproj_kernel, n_experts=n_experts, width=width),
        grid=(b, t // tm),
        in_specs=[pl.BlockSpec((1, tm, width), tok), pl.BlockSpec((1, tm, width), tok),
                  pl.BlockSpec((1, tm, d), tok), pl.BlockSpec(wo.shape, const),
                  pl.BlockSpec((1, 1, d), per_b), pl.BlockSpec((1, d), const),
                  pl.BlockSpec((1, 1, d), per_b), pl.BlockSpec((1, 1, d), per_b),
                  pl.BlockSpec(wrt.shape, const)],
        out_specs=[pl.BlockSpec((1, tm, d), tok),
                   pl.BlockSpec((1, tm, d // LANES, LANES), lambda i, j: (i, j, 0, 0)),
                   pl.BlockSpec((1, n_experts, tm // LANES, LANES), lambda i, j: (i, 0, j, 0))],
        out_shape=[jax.ShapeDtypeStruct((b, t, d), F32),
                   jax.ShapeDtypeStruct((b, t, d // LANES, LANES), F32),
                   jax.ShapeDtypeStruct((b, n_experts, t // LANES, LANES), F32)],
        compiler_params=_params("arbitrary", "arbitrary"),
    )(ym, yr, x, wo, g1, nw, sh, sc, wrt)


def _route_kernel(aff_ref, pos_ref, idx_ref, gate_ref, start_ref, *, cap):
    a = aff_ref[0]
    n_e, n_b, _ = a.shape

    def count(mask):
        return jnp.sum(jnp.sum(mask.astype(jnp.int32), axis=2, keepdims=True), axis=1, keepdims=True)

    def search(i, thr):
        cand = thr | jnp.left_shift(jnp.int32(1), 30 - i)
        return jnp.where(count(a >= pltpu.bitcast(cand, F32)) >= cap, cand, thr)

    thr = lax.fori_loop(0, 31, search, jnp.zeros((n_e, 1, 1), jnp.int32))
    above = a >= pltpu.bitcast(thr + 1, F32)
    band = (a >= pltpu.bitcast(thr, F32)) & jnp.logical_not(above)
    need = cap - count(above)
    tok = (lax.broadcasted_iota(jnp.int32, a.shape, 1) * LANES
           + lax.broadcasted_iota(jnp.int32, a.shape, 2))

    def take_one(_, carry):
        sel, band, need = carry
        in_band = band > 0
        best = jnp.max(jnp.max(jnp.where(in_band, a, -1.0), axis=2, keepdims=True), axis=1, keepdims=True)
        cand = jnp.where(in_band & (a == best), tok, n_b * LANES)
        first = jnp.min(jnp.min(cand, axis=2, keepdims=True), axis=1, keepdims=True)
        take = ((tok == first) & (need > 0)).astype(jnp.int32)
        return sel | take, band - take, need - 1

    sel, _, _ = lax.fori_loop(0, jnp.max(need), take_one,
                              (above.astype(jnp.int32), band.astype(jnp.int32), need))
    sel = sel > 0

    rows = n_e * n_b
    u_i = lax.broadcasted_iota(jnp.int32, (LANES, LANES), 0)
    s_i = lax.broadcasted_iota(jnp.int32, (LANES, LANES), 1)
    incl = (u_i <= s_i).astype(BF16)
    ones = jnp.ones((LANES, LANES), BF16)
    r_i = lax.broadcasted_iota(jnp.int32, (rows, rows), 0)
    c_i = lax.broadcasted_iota(jnp.int32, (rows, rows), 1)
    before = ((r_i // n_b == c_i // n_b) & (c_i < r_i)).astype(BF16)

    x = sel.astype(F32).reshape(rows, LANES)
    xb = x.astype(BF16)
    block_tot = _dot(xb, ones).astype(BF16)
    start = _dot(before, block_tot)
    pos = jnp.where(x > 0, start + _dot(xb, incl) - x, -1.0).astype(jnp.int32)
    pos_ref[0] = pos.reshape(n_e, n_b, LANES)
    start_ref[0] = start.astype(jnp.int32)

    pad = (-rows) % LANES
    def pad_rows(m):
        return m if pad == 0 else jnp.concatenate([m, jnp.zeros((pad, LANES), m.dtype)], axis=0)
    start_p = pad_rows(start)
    end_p = pad_rows(start + block_tot.astype(F32))
    pos_p = pad_rows(pos)
    pos_p = jnp.where(pos_p < 0, 1023, pos_p)
    a_p = pad_rows(a.reshape(rows, LANES))
    a_hi = a_p.astype(BF16)
    a_mid = (a_p - a_hi.astype(F32)).astype(BF16)
    a_lo = (a_p - a_hi.astype(F32) - a_mid.astype(F32)).astype(BF16)
    per_group = LANES // n_b
    j_col = lax.broadcasted_iota(jnp.int32, (cap, LANES), 0).astype(F32)
    lane = lax.broadcasted_iota(jnp.int32, (cap, LANES), 1)
    lane_f = lane.astype(F32)
    block_f = (lane % n_b).astype(F32)
    row0 = pl.program_id(0) * (n_b * LANES)
    for g in range((rows + pad) // LANES):
        rs = slice(g * LANES, (g + 1) * LANES)
        s_row = start_p[rs].T[0:1, :]
        e_row = end_p[rs].T[0:1, :]
        hi = (pos_p[rs] >> 4).astype(F32).astype(BF16)
        lo = (pos_p[rs] & 15).astype(F32).astype(BF16)
        for el in range(per_group):
            e = g * per_group + el
            if e >= n_e:
                break
            mine = (j_col >= s_row) & (j_col < e_row) & ((lane // n_b) == el)
            mine_b = mine.astype(F32).astype(BF16)
            slots = 16.0 * _dot(mine_b, hi) + _dot(mine_b, lo)
            hit = slots == j_col
            in_block = jnp.sum(jnp.where(hit, lane_f, 0.0), axis=1, keepdims=True)
            block = jnp.sum(jnp.where(mine, block_f, 0.0), axis=1, keepdims=True)
            a_blk = _dot(mine_b, a_hi[rs]) + _dot(mine_b, a_mid[rs]) + _dot(mine_b, a_lo[rs])
            gate = jnp.sum(jnp.where(hit, a_blk, 0.0), axis=1, keepdims=True)
            idx_ref[0, e] = _col_to_row(block * LANES + in_block).astype(jnp.int32) + row0
            gate_ref[0, e] = _col_to_row(gate)


def _route(aff_t, cap):
    b, n_e, n_b, _ = aff_t.shape
    spec = pl.BlockSpec((1, n_e, n_b, LANES), lambda i: (i, 0, 0, 0))
    slot_spec = pl.BlockSpec((1, n_e, 1, cap), lambda i: (i, 0, 0, 0))
    return pl.pallas_call(
        functools.partial(_route_kernel, cap=cap),
        grid=(b,), in_specs=[spec],
        out_specs=[spec, slot_spec, slot_spec, pl.BlockSpec((1, n_e * n_b, LANES), lambda i: (i, 0, 0))],
        out_shape=[jax.ShapeDtypeStruct(aff_t.shape, jnp.int32),
                   jax.ShapeDtypeStruct((b, n_e, 1, cap), jnp.int32),
                   jax.ShapeDtypeStruct((b, n_e, 1, cap), F32),
                   jax.ShapeDtypeStruct((b, n_e * n_b, LANES), jnp.int32)],
        compiler_params=_params("arbitrary"),
    )(aff_t)


GATHER_DMA_PRIORITY = 1


def _ffn_kernel(idx_ref, h_hbm, gate_ref, wg_ref, wu_ref, wd_ref, y_ref, stage_ref, x_ref, acc_ref, sem,
                *, m, per_step):
    e, f = pl.program_id(0), pl.program_id(1)
    n_e, n_f = pl.num_programs(0), pl.num_programs(1)
    rows_per_expert = stage_ref.shape[0]

    def row_copy(base, j):
        return pltpu.make_async_copy(h_hbm.at[pl.ds(idx_ref[base + j], 1)],
                                     stage_ref.at[pl.ds(j, 1)], sem.at[0])

    def wait_rows():
        pltpu.make_async_copy(h_hbm.at[pl.ds(0, rows_per_expert)], stage_ref, sem.at[0]).wait()

    @pl.when((e == 0) & (f == 0))
    def _():
        def start_row(j, carry):
            row_copy(0, j).start(priority=GATHER_DMA_PRIORITY)
            return carry
        lax.fori_loop(0, rows_per_expert, start_row, 0)

    @pl.when(f == 0)
    def _():
        wait_rows()
        x_ref[...] = stage_ref[:m].reshape(x_ref.shape).astype(BF16)
        acc_ref[...] = jnp.zeros_like(acc_ref)

    nxt = jnp.minimum(e + 1, n_e - 1) * rows_per_expert
    for i in range(per_step):
        row_copy(nxt, f * per_step + i).start(priority=GATHER_DMA_PRIORITY)

    x = x_ref[...]
    a = _dot(x, wg_ref[0].astype(BF16))
    u = _dot(x, wu_ref[0].astype(BF16))
    mid = (a * _sigmoid(a) * u).astype(BF16)
    acc_ref[...] += _dot(mid, wd_ref[0].astype(BF16))

    @pl.when(f == n_f - 1)
    def _():
        gate = _row_to_cols(gate_ref[0])
        for c in range(0, y_ref.shape[2], LANES):
            y_ref[0, :, c:c + LANES] = (acc_ref[:, c:c + LANES] * gate).astype(BF16)

    @pl.when((e == n_e - 1) & (f == n_f - 1))
    def _():
        wait_rows()


def _ffn(idx, gate, h_rows, wg, wu, wd):
    n_e, m = idx.shape
    d = h_rows.shape[1] * LANES
    ff = wg.shape[2]
    tf = _tile(ff, 256)
    n_f = ff // tf
    per_step = -(-m // (8 * n_f)) * 8
    idx = jnp.pad(idx, ((0, 0), (0, n_f * per_step - m))).reshape(-1)
    return pl.pallas_call(
        functools.partial(_ffn_kernel, m=m, per_step=per_step),
        grid_spec=pltpu.PrefetchScalarGridSpec(
            num_scalar_prefetch=1,
            grid=(n_e, n_f),
            in_specs=[pl.BlockSpec(memory_space=pl.ANY),
                      pl.BlockSpec((1, 1, m), lambda e, f, idx: (e, 0, 0)),
                      pl.BlockSpec((1, d, tf), lambda e, f, idx: (e, 0, f)),
                      pl.BlockSpec((1, d, tf), lambda e, f, idx: (e, 0, f)),
                      pl.BlockSpec((1, tf, d), lambda e, f, idx: (e, f, 0))],
            out_specs=pl.BlockSpec((1, m, d), lambda e, f, idx: (e, 0, 0)),
            scratch_shapes=[pltpu.VMEM((n_f * per_step, d // LANES, LANES), F32),
                            pltpu.VMEM((m, d), BF16),
                            pltpu.VMEM((m, d), F32),
                            pltpu.SemaphoreType.DMA((1,))]),
        out_shape=jax.ShapeDtypeStruct((n_e, m, d), BF16),
        compiler_params=_params("arbitrary", "arbitrary"),
    )(idx, h_rows, gate, wg, wu, wd)


def _combine_kernel(start_ref, y_ref, pos_ref, xl_ref, g2_ref, fw_ref, o_ref, yc_ref, acc_ref,
                    *, n_experts, cap, n_b, win):
    i, j = pl.program_id(0), pl.program_id(1)
    tt = xl_ref.shape[1]
    blocks_per_tile = tt // LANES
    lane = lax.broadcasted_iota(jnp.int32, (tt, win), 1)
    pos = pos_ref[0]
    last_tile = j == pl.num_programs(1) - 1
    first, n_pass = [], 0
    for e in range(n_experts):
        row = (i * n_experts + e) * n_b
        lo = start_ref[row + j * blocks_per_tile]
        hi = jnp.where(last_tile, cap, start_ref[row + jnp.minimum((j + 1) * blocks_per_tile, n_b - 1)])
        w0 = jnp.minimum((lo // 16) * 16, cap - win)
        first.append(w0)
        n_pass = jnp.maximum(n_pass, jnp.where(hi > lo, (hi - w0 + win - 1) // win, 0))
    group = min(n_experts, COMBINE_GROUP)

    def add_pass(p, acc):
        for g0 in range(0, n_experts, group):
            hits = []
            for e in range(g0, g0 + group):
                done = first[e] + p * win
                off = pl.multiple_of(jnp.minimum(done, cap - win), 16)
                yc_ref[e * win:(e + 1) * win, :] = y_ref[e, 0, pl.ds(off, win), :]
                pe = pos[:, e:e + 1]
                hits.append(((pe - off) == lane) & (pe >= done))
            onehot = jnp.concatenate(hits, axis=1).astype(F32).astype(BF16)
            acc = acc + _dot(onehot, yc_ref[g0 * win:(g0 + group) * win, :])
        return acc

    acc_ref[...] = add_pass(0, jnp.zeros(acc_ref.shape, F32))

    def extra_pass(p, carry):
        acc_ref[...] = add_pass(p, acc_ref[...])
        return carry

    lax.fori_loop(1, n_pass, extra_pass, 0)
    xl = xl_ref[0] + g2_ref[0] * acc_ref[...]
    ms = jnp.mean(xl * xl, axis=-1, keepdims=True)
    o_ref[0] = xl * lax.rsqrt(ms + EPS) * fw_ref[...]


def _combine(starts, y, pos_col, xl, g2, fw, cap):
    b, t, d = xl.shape
    n_e = y.shape[0]
    tt = _tile(t, 512)
    win = min(cap, COMBINE_WINDOW)
    assert cap % 16 == 0 and win % 16 == 0 and tt % LANES == 0
    tok = lambda i, j, st: (i, j, 0)
    return pl.pallas_call(
        functools.partial(_combine_kernel, n_experts=n_e, cap=cap, n_b=t // LANES, win=win),
        grid_spec=pltpu.PrefetchScalarGridSpec(
            num_scalar_prefetch=1,
            grid=(b, t // tt),
            in_specs=[pl.BlockSpec((n_e, 1, cap, d), lambda i, j, st: (0, i, 0, 0)),
                      pl.BlockSpec((1, tt, n_e), tok),
                      pl.BlockSpec((1, tt, d), tok),
                      pl.BlockSpec((1, 1, d), lambda i, j, st: (i, 0, 0)),
                      pl.BlockSpec((1, d), lambda i, j, st: (0, 0))],
            out_specs=pl.BlockSpec((1, tt, d), tok),
            scratch_shapes=[pltpu.VMEM((n_e * win, d), BF16), pltpu.VMEM((tt, d), F32)]),
        out_shape=jax.ShapeDtypeStruct((b, t, d), F32),
        compiler_params=_params("arbitrary", "arbitrary"),
    )(starts, y, pos_col, xl, g2, fw)


def _rope_tables(t):
    n_freq = HEAD_DIM // 4
    rows = jnp.repeat(jnp.arange(t // GRID_W, dtype=F32), GRID_W)
    cols = jnp.tile(jnp.arange(GRID_W, dtype=F32), t // GRID_W)
    inv_freq = ROPE_BASE ** (-jnp.arange(n_freq, dtype=F32) / n_freq)
    ang_r, ang_c = rows[:, None] * inv_freq, cols[:, None] * inv_freq
    cos = jnp.concatenate([jnp.cos(ang_r)] * 2 + [jnp.cos(ang_c)] * 2, axis=-1)
    sin = jnp.concatenate([-jnp.sin(ang_r), jnp.sin(ang_r), -jnp.sin(ang_c), jnp.sin(ang_c)], axis=-1)
    return jnp.tile(cos, (1, N_HEADS)), jnp.tile(sin, (1, N_HEADS))


def kernel(x, c, ctx, c_ctx, w_ada, b_ada, norm1_w, norm2_w, w_in, mlstm_gate_bias, ret_decay_logit,
           mlstm_norm_w, ret_norm_w, w_out, w_router, w_gate, w_up, w_down, final_norm_w):
    B, T, D = x.shape
    Tc = ctx.shape[1]
    H = N_HEADS
    width = H * HEAD_DIM
    assert D == 2 * width and w_ada.shape[0] == 1
    assert T % CHUNK == 0 and Tc % CHUNK == 0 and T % GRID_W == 0
    n_lat, n_ctx = T // CHUNK, Tc // CHUNK
    n_experts = w_router.shape[2]
    cap = EC_CAPACITY_FACTOR * T // n_experts
    n_gate = 4 * H

    pad = (-(B + 1)) % 8
    cc = jnp.concatenate([c, c_ctx[None], jnp.zeros((pad, D), F32)], axis=0)
    mod = _ada(cc, w_ada[0], b_ada[0])
    sh1, sc1, g1, sh2, sc2, g2 = [mod[:B, None, i * D:(i + 1) * D] for i in range(6)]
    csh1, csc1 = [jnp.broadcast_to(mod[B, i * D:(i + 1) * D], (B, 1, D)) for i in range(2)]

    w_in0 = w_in[0]
    wm = w_in0[:, :4 * width].astype(BF16)
    wg = jnp.pad(w_in0[:, 4 * width:4 * width + n_gate], ((0, 0), (0, LANES - n_gate))).astype(BF16)
    wr = w_in0[:, 4 * width + n_gate:].astype(BF16)
    gb = jnp.pad(mlstm_gate_bias[0], (0, LANES - n_gate)).reshape(1, LANES)
    nw1 = norm1_w[0].reshape(1, D)
    cos, sin = _rope_tables(T)
    pm, gl, pr = _inproj(x, nw1, sh1, sc1, wm, wg, wr, gb, cos, sin)
    pmc, gc, prc = _inproj(ctx, nw1, csh1, csc1, wm, wg, wr, gb,
                           jnp.ones((Tc, width), F32), jnp.zeros((Tc, width), F32))

    ym = _mlstm(pm, pmc, _gateprep(gl, gc), mlstm_norm_w[0].reshape(H, 1, HEAD_DIM))
    dl = jnp.broadcast_to(jnp.swapaxes(ret_decay_logit[0], 0, 1)[:, :, None], (H, 2, LANES))
    yr = _ret(pr, prc, dl, ret_norm_w[0].reshape(H, 1, HEAD_DIM))

    wr_hi = w_router[0].astype(BF16)
    wr_lo = (w_router[0] - wr_hi.astype(F32)).astype(BF16)
    wrt = jnp.pad(jnp.concatenate([wr_hi, wr_lo], axis=1), ((0, 0), (0, LANES - 2 * n_experts)))
    xl, h2, aff_t = _outproj(ym, yr, x, w_out[0].astype(BF16), g1, norm2_w[0].reshape(1, D), sh2, sc2,
                             wrt, n_experts)

    pos, idx, gate, starts = _route(aff_t, cap)
    by_expert = lambda a: jnp.swapaxes(a.reshape(B, n_experts, cap), 0, 1).reshape(n_experts, B * cap)
    y = _ffn(by_expert(idx), by_expert(gate)[:, None, :], h2.reshape(B * T, D // LANES, LANES),
             w_gate[0], w_up[0], w_down[0])
    return _combine(starts[:, :, 0].reshape(-1), y.reshape(n_experts, B, cap, D),
                    jnp.swapaxes(pos.reshape(B, n_experts, T), 1, 2), xl, g2,
                    final_norm_w.reshape(1, D), cap)
```

```python
import functools

import jax
import jax.numpy as jnp
from jax import lax
from jax.experimental import pallas as pl
from jax.experimental.pallas import tpu as pltpu

F32 = jnp.float32
BF16 = jnp.bfloat16

CHUNK = 128
GRID_W = 64
N_HEADS = 4
HEAD_DIM = 128
N_GATE = 4 * N_HEADS
ROPE_BASE = 10000.0
EPS = 1e-6
LOG2E = 1.4426950408889634
EC_CAPACITY_FACTOR = 2
LANES = 128
VMEM_LIMIT_BYTES = 56 * 1024 * 1024
COMBINE_WINDOW = 128
UNSELECTED_SLOT = 1023
COMBINE_GROUP = 4
MIXER_UNROLL = 16
GATE_UNROLL = 8


def _params(*sem):
    return pltpu.CompilerParams(dimension_semantics=sem, vmem_limit_bytes=VMEM_LIMIT_BYTES)


def _dot(a, b):
    return jnp.dot(a, b, preferred_element_type=F32)


def _dot_nt(a, b):
    return lax.dot_general(a, b, (((1,), (1,)), ((), ())), preferred_element_type=F32)


def _dot_tn(a, b):
    return lax.dot_general(a, b, (((0,), (0,)), ((), ())), preferred_element_type=F32)


def _sigmoid(x):
    return 1.0 / (1.0 + jnp.exp(-x))


def _log_sigmoid(x):
    return jnp.minimum(x, 0.0) - jnp.log1p(jnp.exp(-jnp.abs(x)))


def _col_to_row(col):
    return jnp.broadcast_to(col, (col.shape[0], LANES)).T[0:1, :]


def _row_to_cols(row):
    pieces = [jnp.broadcast_to(row[:, c:c + LANES], (LANES, LANES)).T for c in range(0, row.shape[1], LANES)]
    return pieces[0] if len(pieces) == 1 else jnp.concatenate(pieces, axis=0)


def _tile(n, pref):
    t = min(n, pref)
    assert n % t == 0, (n, t)
    return t


def _ada_kernel(c_ref, w_ref, b_ref, o_ref):
    cv = c_ref[...]
    s = (cv * _sigmoid(cv)).astype(BF16)
    o_ref[...] = _dot(s, w_ref[...].astype(BF16)) + b_ref[...]


def _ada(cc, w, b):
    rows, d = cc.shape
    n = w.shape[1]
    tn = _tile(n, 1024)
    return pl.pallas_call(
        _ada_kernel,
        grid=(n // tn,),
        in_specs=[pl.BlockSpec((rows, d), lambda j: (0, 0)),
                  pl.BlockSpec((d, tn), lambda j: (0, j)),
                  pl.BlockSpec((1, tn), lambda j: (0, j))],
        out_specs=pl.BlockSpec((rows, tn), lambda j: (0, j)),
        out_shape=jax.ShapeDtypeStruct((rows, n), F32),
        compiler_params=_params("arbitrary"),
    )(cc, w, b.reshape(1, n))


def _rms_mod(x, nw, sh, sc):
    ms = jnp.mean(x * x, axis=-1, keepdims=True)
    return (x * lax.rsqrt(ms + EPS) * nw) * (1.0 + sc) + sh


def _inproj_kernel(x_ref, nw_ref, sh_ref, sc_ref, wm_ref, wg_ref, wr_ref, gb_ref, cos_ref, sin_ref,
                   pm_ref, g_ref, pr_ref, *, width, k_scale):
    hb = _rms_mod(x_ref[0], nw_ref[...], sh_ref[0], sc_ref[0]).astype(BF16)
    for j in range(4):
        cols = slice(j * width, (j + 1) * width)
        acc = _dot(hb, wm_ref[:, cols])
        if j == 1:
            acc = acc * k_scale
        pm_ref[0, :, cols] = acc.astype(BF16)
    g_ref[0] = (_dot(hb, wg_ref[...]) + gb_ref[...]).T[:g_ref.shape[1]]
    cos = cos_ref[...]
    sin = sin_ref[...]
    lane = lax.broadcasted_iota(jnp.int32, cos.shape, 1)
    even = ((lane // 32) % 2) == 0
    for j in range(4):
        cols = slice(j * width, (j + 1) * width)
        acc = _dot(hb, wr_ref[:, cols])
        if j == 1:
            acc = acc * k_scale
        if j < 2:
            swapped = jnp.where(even, pltpu.roll(acc, width - 32, 1), pltpu.roll(acc, 32, 1))
            acc = acc * cos + swapped * sin
        pr_ref[0, :, cols] = acc.astype(BF16)


def _inproj(x, nw, sh, sc, wm, wg, wr, gb, cos, sin):
    b, t, d = x.shape
    width = wm.shape[1] // 4
    tm = _tile(t, 512)
    kern = functools.partial(_inproj_kernel, width=width, k_scale=HEAD_DIM ** -0.5)
    const = lambda i, j: (0, 0)
    return pl.pallas_call(
        kern,
        grid=(b, t // tm),
        in_specs=[pl.BlockSpec((1, tm, d), lambda i, j: (i, j, 0)),
                  pl.BlockSpec((1, d), const),
                  pl.BlockSpec((1, 1, d), lambda i, j: (i, 0, 0)),
                  pl.BlockSpec((1, 1, d), lambda i, j: (i, 0, 0)),
                  pl.BlockSpec(wm.shape, const),
                  pl.BlockSpec(wg.shape, const),
                  pl.BlockSpec(wr.shape, const),
                  pl.BlockSpec((1, LANES), const),
                  pl.BlockSpec((tm, width), lambda i, j: (j, 0)),
                  pl.BlockSpec((tm, width), lambda i, j: (j, 0))],
        out_specs=[pl.BlockSpec((1, tm, 4 * width), lambda i, j: (i, j, 0)),
                   pl.BlockSpec((1, N_GATE, tm), lambda i, j: (i, 0, j)),
                   pl.BlockSpec((1, tm, 4 * width), lambda i, j: (i, j, 0))],
        out_shape=[jax.ShapeDtypeStruct((b, t, 4 * width), BF16),
                   jax.ShapeDtypeStruct((b, N_GATE, t), F32),
                   jax.ShapeDtypeStruct((b, t, 4 * width), BF16)],
        compiler_params=_params("arbitrary", "arbitrary"),
    )(x, nw, sh, sc, wm, wg, wr, gb, cos, sin)


def _scan_lanes(x, op, fill, reverse):
    lane = lax.broadcasted_iota(jnp.int32, x.shape, 1)
    sh = 1
    while sh < LANES:
        if reverse:
            x = op(x, jnp.where(lane < LANES - sh, pltpu.roll(x, LANES - sh, 1), fill))
        else:
            x = op(x, jnp.where(lane >= sh, pltpu.roll(x, sh, 1), fill))
        sh *= 2
    return x


def _gateprep_kernel(gl_ref, gc_ref, o_ref, *, n_lat, n_ctx):
    n_b = gl_ref.shape[0]
    half = N_GATE // 2
    rows = n_b * half
    row = lax.broadcasted_iota(jnp.int32, (rows, LANES), 0)
    lane = lax.broadcasted_iota(jnp.int32, (rows, LANES), 1)
    is_fwd = (row % half) < N_HEADS
    last = lane == jnp.where(is_fwd, LANES - 1, 0)

    def chunk_rows(ref, lo, c_fwd, c_bwd):
        def at(c):
            cols = (slice(c * LANES, (c + 1) * LANES) if isinstance(c, int)
                    else pl.ds(pl.multiple_of(c * LANES, LANES), LANES))
            return jnp.concatenate([ref[s, lo:lo + half, cols] for s in range(n_b)], axis=0)
        return jnp.where(is_fwd, at(c_fwd), at(c_bwd))

    def step(c, ig, fg, m):
        lf = _log_sigmoid(fg)
        b = jnp.where(is_fwd, _scan_lanes(lf, jnp.add, 0.0, False), _scan_lanes(lf, jnp.add, 0.0, True))
        b_tot = jnp.sum(jnp.where(last, b, 0.0), axis=1, keepdims=True)
        a = ig - b
        m_new = jnp.maximum(b_tot + m, jnp.max(b_tot + a, axis=1, keepdims=True))
        w = jnp.exp(b_tot + a - m_new)
        decay = jnp.exp(b_tot + m - m_new)
        run = jnp.where(is_fwd, _scan_lanes(a, jnp.maximum, -jnp.inf, False),
                        _scan_lanes(a, jnp.maximum, -jnp.inf, True))
        mm = jnp.maximum(m, run)
        o_ref[c, 0] = a * LOG2E
        o_ref[c, 1] = w
        o_ref[c, 2] = mm * LOG2E
        o_ref[c, 3] = jnp.exp(-(b + mm))
        o_ref[c, 4] = jnp.broadcast_to(decay, (rows, LANES))
        o_ref[c, 5] = jnp.broadcast_to(m * LOG2E, (rows, LANES))
        return m_new

    m = jnp.zeros((rows, 1), F32)
    for c in range(n_ctx):
        m = step(c, chunk_rows(gc_ref, 0, c, n_ctx - 1 - c), chunk_rows(gc_ref, half, c, n_ctx - 1 - c), m)

    def latent(i, m):
        return step(n_ctx + i, chunk_rows(gl_ref, 0, i, n_lat - 1 - i),
                    chunk_rows(gl_ref, half, i, n_lat - 1 - i), m)

    lax.fori_loop(0, n_lat, latent, m, unroll=min(n_lat, GATE_UNROLL))


def _gateprep(gl, gc):
    n_lat, n_ctx = gl.shape[2] // CHUNK, gc.shape[2] // CHUNK
    return pl.pallas_call(
        functools.partial(_gateprep_kernel, n_lat=n_lat, n_ctx=n_ctx),
        out_shape=jax.ShapeDtypeStruct((n_ctx + n_lat, 6, gl.shape[0] * N_GATE // 2, LANES), F32),
        compiler_params=pltpu.CompilerParams(vmem_limit_bytes=VMEM_LIMIT_BYTES),
    )(gl, gc)


def _head_norm_gate(h, gate, nw):
    return gate * (h * lax.rsqrt(jnp.mean(h * h, axis=-1, keepdims=True) + EPS)) * nw


def _mlstm_kernel(q_ref, k_ref, v_ref, o_ref, kc_ref, vc_ref, st_ref, nw_ref, y_ref,
                  sf_ref, sb_ref, cf_ref, cb_ref, *, n_lat, n_ctx, unroll):
    L = CHUNK
    A, WT, MM, EN, DECAY, M0 = range(6)
    ones = jnp.ones((L, HEAD_DIM), BF16)
    stream_f = pl.program_id(0) * (N_GATE // 2) + pl.program_id(1)
    stream_b = stream_f + N_HEADS

    def stat(step, plane, stream):
        return st_ref[step, plane, pl.ds(stream, 1), :]

    def to_rows(row):
        return jnp.broadcast_to(row, (L, L)).T

    def update(state_ref, k, v, step, stream):
        wkt = (k.astype(F32).T * stat(step, WT, stream)).astype(BF16)
        u = _dot(wkt, jnp.concatenate([v, ones], axis=1))
        decay = stat(step, DECAY, stream)
        state_ref[...] = jnp.concatenate([decay, decay], axis=1) * state_ref[...] + u

    cf_ref[...] = jnp.zeros_like(cf_ref)
    cb_ref[...] = jnp.zeros_like(cb_ref)
    for c in range(n_ctx):
        rf = slice(c * L, (c + 1) * L)
        rb = slice((n_ctx - 1 - c) * L, (n_ctx - c) * L)
        update(cf_ref, kc_ref[0, rf, :], vc_ref[0, rf, :], c, stream_f)
        update(cb_ref, kc_ref[0, rb, :], vc_ref[0, rb, :], c, stream_b)

    def state_body(i, carry):
        jb = n_lat - 1 - i
        rf = pl.ds(pl.multiple_of(i * L, L), L)
        rb = pl.ds(pl.multiple_of(jb * L, L), L)
        sf_ref[i] = cf_ref[...].astype(BF16)
        sb_ref[jb] = cb_ref[...].astype(BF16)
        update(cf_ref, k_ref[0, rf, :], v_ref[0, rf, :], n_ctx + i, stream_f)
        update(cb_ref, k_ref[0, rb, :], v_ref[0, rb, :], n_ctx + i, stream_b)
        return carry

    lax.fori_loop(0, n_lat, state_body, 0, unroll=unroll)

    r_i = lax.broadcasted_iota(jnp.int32, (L, L), 0)
    c_i = lax.broadcasted_iota(jnp.int32, (L, L), 1)
    tril = c_i <= r_i
    triu = c_i >= r_i
    nw = nw_ref[0]

    def out_body(j, carry):
        rows = pl.ds(pl.multiple_of(j * L, L), L)
        step_f = n_ctx + j
        step_b = n_ctx + n_lat - 1 - j
        q = q_ref[0, rows, :]
        k = k_ref[0, rows, :]
        v = v_ref[0, rows, :]
        mm_f = to_rows(stat(step_f, MM, stream_f))
        mm_b = to_rows(stat(step_b, MM, stream_b))
        s = _dot_nt(q, k)
        d_f = jnp.where(tril, jnp.exp2(stat(step_f, A, stream_f) - mm_f), 0.0)
        d_b = jnp.where(triu, jnp.exp2(stat(step_b, A, stream_b) - mm_b), 0.0)
        iw_f = jnp.exp2(stat(step_f, M0, stream_f) - mm_f)
        iw_b = jnp.exp2(stat(step_b, M0, stream_b) - mm_b)
        qf = q.astype(F32)
        vext = jnp.concatenate([v, ones], axis=1)
        lhs_f = jnp.concatenate([(s * d_f).astype(BF16), (qf * iw_f).astype(BF16)], axis=1)
        lhs_b = jnp.concatenate([(s * d_b).astype(BF16), (qf * iw_b).astype(BF16)], axis=1)
        tot_f = _dot(lhs_f, jnp.concatenate([vext, sf_ref[j]], axis=0))
        tot_b = _dot(lhs_b, jnp.concatenate([vext, sb_ref[j]], axis=0))
        D = HEAD_DIM
        h = (tot_f[:, :D] / jnp.maximum(jnp.abs(tot_f[:, D:]), to_rows(stat(step_f, EN, stream_f)))
             + tot_b[:, :D] / jnp.maximum(jnp.abs(tot_b[:, D:]), to_rows(stat(step_b, EN, stream_b))))
        gate = _sigmoid(o_ref[0, rows, :].astype(F32))
        y_ref[0, rows, :] = _head_norm_gate(h, gate, nw).astype(BF16)
        return carry

    lax.fori_loop(0, n_lat, out_body, 0, unroll=unroll)


def _mlstm(pm, pmc, stats, nw):
    b, t, _ = pm.shape
    tc = pmc.shape[1]
    n_lat, n_ctx = t // CHUNK, tc // CHUNK
    H = N_HEADS
    blk = lambda off: pl.BlockSpec((1, t, HEAD_DIM), lambda i, h: (i, 0, off + h))
    cblk = lambda off: pl.BlockSpec((1, tc, HEAD_DIM), lambda i, h: (i, 0, off + h))
    return pl.pallas_call(
        functools.partial(_mlstm_kernel, n_lat=n_lat, n_ctx=n_ctx, unroll=min(n_lat, MIXER_UNROLL)),
        grid=(b, H),
        in_specs=[blk(0), blk(H), blk(2 * H), blk(3 * H), cblk(H), cblk(2 * H),
                  pl.BlockSpec(stats.shape, lambda i, h: (0, 0, 0, 0)),
                  pl.BlockSpec((1, 1, HEAD_DIM), lambda i, h: (h, 0, 0))],
        out_specs=pl.BlockSpec((1, t, HEAD_DIM), lambda i, h: (i, 0, h)),
        out_shape=jax.ShapeDtypeStruct((b, t, H * HEAD_DIM), BF16),
        scratch_shapes=[pltpu.VMEM((n_lat, HEAD_DIM, HEAD_DIM + LANES), BF16),
                        pltpu.VMEM((n_lat, HEAD_DIM, HEAD_DIM + LANES), BF16),
                        pltpu.VMEM((HEAD_DIM, HEAD_DIM + LANES), F32),
                        pltpu.VMEM((HEAD_DIM, HEAD_DIM + LANES), F32)],
        compiler_params=_params("arbitrary", "arbitrary"),
    )(pm, pm, pm, pm, pmc, pmc, stats, nw)


def _ret_kernel(q_ref, k_ref, v_ref, o_ref, kc_ref, vc_ref, dl_ref, nw_ref, y_ref,
                sf_ref, sb_ref, rf_ref, rb_ref, *, n_lat, n_ctx, unroll):
    L = CHUNK
    lg = _log_sigmoid(dl_ref[0])
    lg_f, lg_b = lg[0:1, :], lg[1:2, :]
    r_i = lax.broadcasted_iota(jnp.int32, (L, L), 0)
    c_i = lax.broadcasted_iota(jnp.int32, (L, L), 1)
    r_f = r_i.astype(F32)
    rel = (r_i - c_i).astype(F32)
    kw_f = jnp.exp((L - 1.0 - r_f) * lg_f)
    kw_b = jnp.exp(r_f * lg_b)
    in_f = jnp.exp((r_f + 1.0) * lg_f)
    in_b = jnp.exp((L - r_f) * lg_b)
    dch_f = jnp.exp(L * lg_f)
    dch_b = jnp.exp(L * lg_b)
    d_mat = (jnp.where(rel >= 0, jnp.exp(jnp.maximum(rel, 0.0) * lg_f), 0.0)
             + jnp.where(rel <= 0, jnp.exp(jnp.maximum(-rel, 0.0) * lg_b), 0.0))

    def update(state_ref, k, v, kw, dch):
        wk = (k.astype(F32) * kw).astype(BF16)
        state_ref[...] = dch * state_ref[...] + _dot_tn(wk, v)

    rf_ref[...] = jnp.zeros_like(rf_ref)
    rb_ref[...] = jnp.zeros_like(rb_ref)
    for c in range(n_ctx):
        rows = slice(c * L, (c + 1) * L)
        update(rf_ref, kc_ref[0, rows, :], vc_ref[0, rows, :], kw_f, dch_f)
    for c in reversed(range(n_ctx)):
        rows = slice(c * L, (c + 1) * L)
        update(rb_ref, kc_ref[0, rows, :], vc_ref[0, rows, :], kw_b, dch_b)

    def state_body(i, carry):
        jb = n_lat - 1 - i
        rowf = pl.ds(pl.multiple_of(i * L, L), L)
        rowb = pl.ds(pl.multiple_of(jb * L, L), L)
        sf_ref[i] = rf_ref[...].astype(BF16)
        sb_ref[jb] = rb_ref[...].astype(BF16)
        update(rf_ref, k_ref[0, rowf, :], v_ref[0, rowf, :], kw_f, dch_f)
        update(rb_ref, k_ref[0, rowb, :], v_ref[0, rowb, :], kw_b, dch_b)
        return carry

    lax.fori_loop(0, n_lat, state_body, 0, unroll=unroll)
    nw = nw_ref[0]

    def out_body(j, carry):
        rows = pl.ds(pl.multiple_of(j * L, L), L)
        q = q_ref[0, rows, :]
        s = _dot_nt(q, k_ref[0, rows, :])
        intra = _dot((s * d_mat).astype(BF16), v_ref[0, rows, :])
        inter = _dot(q, jnp.concatenate([sf_ref[j], sb_ref[j]], axis=1))
        out = intra + in_f * inter[:, :HEAD_DIM] + in_b * inter[:, HEAD_DIM:]
        og = o_ref[0, rows, :].astype(F32)
        y_ref[0, rows, :] = _head_norm_gate(out, og * _sigmoid(og), nw).astype(BF16)
        return carry

    lax.fori_loop(0, n_lat, out_body, 0, unroll=unroll)


def _ret(pr, prc, dl, nw):
    b, t, _ = pr.shape
    tc = prc.shape[1]
    n_lat, n_ctx = t // CHUNK, tc // CHUNK
    H = N_HEADS
    blk = lambda off: pl.BlockSpec((1, t, HEAD_DIM), lambda i, h: (i, 0, off + h))
    cblk = lambda off: pl.BlockSpec((1, tc, HEAD_DIM), lambda i, h: (i, 0, off + h))
    return pl.pallas_call(
        functools.partial(_ret_kernel, n_lat=n_lat, n_ctx=n_ctx, unroll=min(n_lat, MIXER_UNROLL)),
        grid=(b, H),
        in_specs=[blk(0), blk(H), blk(2 * H), blk(3 * H), cblk(H), cblk(2 * H),
                  pl.BlockSpec((1, 2, LANES), lambda i, h: (h, 0, 0)),
                  pl.BlockSpec((1, 1, HEAD_DIM), lambda i, h: (h, 0, 0))],
        out_specs=pl.BlockSpec((1, t, HEAD_DIM), lambda i, h: (i, 0, h)),
        out_shape=jax.ShapeDtypeStruct((b, t, H * HEAD_DIM), BF16),
        scratch_shapes=[pltpu.VMEM((n_lat, HEAD_DIM, HEAD_DIM), BF16),
                        pltpu.VMEM((n_lat, HEAD_DIM, HEAD_DIM), BF16),
                        pltpu.VMEM((HEAD_DIM, HEAD_DIM), F32),
                        pltpu.VMEM((HEAD_DIM, HEAD_DIM), F32)],
        compiler_params=_params("arbitrary", "arbitrary"),
    )(pr, pr, pr, pr, prc, prc, dl, nw)


def _outproj_kernel(ym_ref, yr_ref, x_ref, wo_ref, g1_ref, nw_ref, sh_ref, sc_ref, wrt_ref,
                    xl_ref, h2_ref, aff_ref, *, n_experts, width):
    y = _dot(ym_ref[0], wo_ref[:width, :]) + _dot(yr_ref[0], wo_ref[width:, :])
    xl = x_ref[0] + g1_ref[0] * y
    xl_ref[0] = xl
    h2 = _rms_mod(xl, nw_ref[...], sh_ref[0], sc_ref[0])
    h2_ref[0] = h2.reshape(h2.shape[0], h2.shape[1] // LANES, LANES)
    h_hi = h2.astype(BF16)
    h_lo = (h2 - h_hi.astype(F32)).astype(BF16)
    p_hi = _dot(h_hi, wrt_ref[...])
    p_lo = _dot(h_lo, wrt_ref[...])
    logits = p_hi + pltpu.roll(p_hi, LANES - n_experts, 1) + p_lo
    lane = lax.broadcasted_iota(jnp.int32, logits.shape, 1)
    logits = jnp.where(lane < n_experts, logits, -jnp.inf)
    e = jnp.exp(logits - jnp.max(logits, axis=-1, keepdims=True))
    aff_t = (e / jnp.sum(e, axis=-1, keepdims=True)).T[:n_experts]
    aff_ref[0] = aff_t.reshape(n_experts, aff_t.shape[1] // LANES, LANES)


def _outproj(ym, yr, x, wo, g1, nw, sh, sc, wrt, n_experts):
    b, t, d = x.shape
    width = ym.shape[2]
    tm = _tile(t, 8 * LANES)
    const = lambda i, j: (0, 0)
    per_b = lambda i, j: (i, 0, 0)
    tok = lambda i, j: (i, j, 0)
    return pl.pallas_call(
        functools.partial(_outproj_kernel, n_experts=n_experts, width=width),
        grid=(b, t // tm),
        in_specs=[pl.BlockSpec((1, tm, width), tok), pl.BlockSpec((1, tm, width), tok),
                  pl.BlockSpec((1, tm, d), tok), pl.BlockSpec(wo.shape, const),
                  pl.BlockSpec((1, 1, d), per_b), pl.BlockSpec((1, d), const),
                  pl.BlockSpec((1, 1, d), per_b), pl.BlockSpec((1, 1, d), per_b),
                  pl.BlockSpec(wrt.shape, const)],
        out_specs=[pl.BlockSpec((1, tm, d), tok),
                   pl.BlockSpec((1, tm, d // LANES, LANES), lambda i, j: (i, j, 0, 0)),
                   pl.BlockSpec((1, n_experts, tm // LANES, LANES), lambda i, j: (i, 0, j, 0))],
        out_shape=[jax.ShapeDtypeStruct((b, t, d), F32),
                   jax.ShapeDtypeStruct((b, t, d // LANES, LANES), F32),
                   jax.ShapeDtypeStruct((b, n_experts, t // LANES, LANES), F32)],
        compiler_params=_params("arbitrary", "arbitrary"),
    )(ym, yr, x, wo, g1, nw, sh, sc, wrt)


def _route_kernel(aff_ref, pos_ref, idx_ref, gate_ref, start_ref, *, cap):
    a = aff_ref[0]
    n_e, n_b, _ = a.shape

    def count(mask):
        return jnp.sum(jnp.sum(mask.astype(jnp.int32), axis=2, keepdims=True), axis=1, keepdims=True)

    def search(i, thr):
        cand = thr | jnp.left_shift(jnp.int32(1), 30 - i)
        return jnp.where(count(a >= pltpu.bitcast(cand, F32)) >= cap, cand, thr)

    thr = lax.fori_loop(0, 31, search, jnp.zeros((n_e, 1, 1), jnp.int32))
    above = a >= pltpu.bitcast(thr + 1, F32)
    band = (a >= pltpu.bitcast(thr, F32)) & jnp.logical_not(above)
    need = cap - count(above)
    tok = (lax.broadcasted_iota(jnp.int32, a.shape, 1) * LANES
           + lax.broadcasted_iota(jnp.int32, a.shape, 2))

    def take_one(_, carry):
        sel, band, need = carry
        in_band = band > 0
        best = jnp.max(jnp.max(jnp.where(in_band, a, -1.0), axis=2, keepdims=True), axis=1, keepdims=True)
        cand = jnp.where(in_band & (a == best), tok, n_b * LANES)
        first = jnp.min(jnp.min(cand, axis=2, keepdims=True), axis=1, keepdims=True)
        take = ((tok == first) & (need > 0)).astype(jnp.int32)
        return sel | take, band - take, need - 1

    sel, _, _ = lax.fori_loop(0, jnp.max(need), take_one,
                              (above.astype(jnp.int32), band.astype(jnp.int32), need))
    sel = sel > 0

    rows = n_e * n_b
    u_i = lax.broadcasted_iota(jnp.int32, (LANES, LANES), 0)
    s_i = lax.broadcasted_iota(jnp.int32, (LANES, LANES), 1)
    incl = (u_i <= s_i).astype(BF16)
    ones = jnp.ones((LANES, LANES), BF16)
    r_i = lax.broadcasted_iota(jnp.int32, (rows, rows), 0)
    c_i = lax.broadcasted_iota(jnp.int32, (rows, rows), 1)
    before = ((r_i // n_b == c_i // n_b) & (c_i < r_i)).astype(BF16)

    x = sel.astype(F32).reshape(rows, LANES)
    xb = x.astype(BF16)
    block_tot = _dot(xb, ones).astype(BF16)
    start = _dot(before, block_tot)
    pos = jnp.where(x > 0, start + _dot(xb, incl) - x, -1.0).astype(jnp.int32)
    pos_ref[0] = pos.reshape(n_e, n_b, LANES)
    start_ref[0] = start.astype(jnp.int32)

    pad = (-rows) % LANES
    def pad_rows(m):
        return m if pad == 0 else jnp.concatenate([m, jnp.zeros((pad, LANES), m.dtype)], axis=0)
    start_p = pad_rows(start)
    end_p = pad_rows(start + block_tot.astype(F32))
    pos_p = pad_rows(pos)
    pos_p = jnp.where(pos_p < 0, 1023, pos_p)
    a_p = pad_rows(a.reshape(rows, LANES))
    a_hi = a_p.astype(BF16)
    a_mid = (a_p - a_hi.astype(F32)).astype(BF16)
    a_lo = (a_p - a_hi.astype(F32) - a_mid.astype(F32)).astype(BF16)
    per_group = LANES // n_b
    j_col = lax.broadcasted_iota(jnp.int32, (cap, LANES), 0).astype(F32)
    lane = lax.broadcasted_iota(jnp.int32, (cap, LANES), 1)
    lane_f = lane.astype(F32)
    block_f = (lane % n_b).astype(F32)
    row0 = pl.program_id(0) * (n_b * LANES)
    for g in range((rows + pad) // LANES):
        rs = slice(g * LANES, (g + 1) * LANES)
        s_row = start_p[rs].T[0:1, :]
        e_row = end_p[rs].T[0:1, :]
        hi = (pos_p[rs] >> 4).astype(F32).astype(BF16)
        lo = (pos_p[rs] & 15).astype(F32).astype(BF16)
        for el in range(per_group):
            e = g * per_group + el
            if e >= n_e:
                break
            mine = (j_col >= s_row) & (j_col < e_row) & ((lane // n_b) == el)
            mine_b = mine.astype(F32).astype(BF16)
            slots = 16.0 * _dot(mine_b, hi) + _dot(mine_b, lo)
            hit = slots == j_col
            in_block = jnp.sum(jnp.where(hit, lane_f, 0.0), axis=1, keepdims=True)
            block = jnp.sum(jnp.where(mine, block_f, 0.0), axis=1, keepdims=True)
            a_blk = _dot(mine_b, a_hi[rs]) + _dot(mine_b, a_mid[rs]) + _dot(mine_b, a_lo[rs])
            gate = jnp.sum(jnp.where(hit, a_blk, 0.0), axis=1, keepdims=True)
            idx_ref[0, e] = _col_to_row(block * LANES + in_block).astype(jnp.int32) + row0
            gate_ref[0, e] = _col_to_row(gate)


def _route(aff_t, cap):
    b, n_e, n_b, _ = aff_t.shape
    spec = pl.BlockSpec((1, n_e, n_b, LANES), lambda i: (i, 0, 0, 0))
    slot_spec = pl.BlockSpec((1, n_e, 1, cap), lambda i: (i, 0, 0, 0))
    return pl.pallas_call(
        functools.partial(_route_kernel, cap=cap),
        grid=(b,), in_specs=[spec],
        out_specs=[spec, slot_spec, slot_spec, pl.BlockSpec((1, n_e * n_b, LANES), lambda i: (i, 0, 0))],
        out_shape=[jax.ShapeDtypeStruct(aff_t.shape, jnp.int32),
                   jax.ShapeDtypeStruct((b, n_e, 1, cap), jnp.int32),
                   jax.ShapeDtypeStruct((b, n_e, 1, cap), F32),
                   jax.ShapeDtypeStruct((b, n_e * n_b, LANES), jnp.int32)],
        compiler_params=_params("arbitrary"),
    )(aff_t)


GATHER_DMA_PRIORITY = 1


def _ffn_kernel(idx_ref, h_hbm, gate_ref, wg_ref, wu_ref, wd_ref, y_ref, stage_ref, x_ref, acc_ref, sem,
                *, m, per_step):
    e, f = pl.program_id(0), pl.program_id(1)
    n_e, n_f = pl.num_programs(0), pl.num_programs(1)
    rows_per_expert = stage_ref.shape[0]

    def row_copy(base, j):
        return pltpu.make_async_copy(h_hbm.at[pl.ds(idx_ref[base + j], 1)],
                                     stage_ref.at[pl.ds(j, 1)], sem.at[0])

    def wait_rows():
        pltpu.make_async_copy(h_hbm.at[pl.ds(0, rows_per_expert)], stage_ref, sem.at[0]).wait()

    @pl.when((e == 0) & (f == 0))
    def _():
        def start_row(j, carry):
            row_copy(0, j).start(priority=GATHER_DMA_PRIORITY)
            return carry
        lax.fori_loop(0, rows_per_expert, start_row, 0)

    @pl.when(f == 0)
    def _():
        wait_rows()
        x_ref[...] = stage_ref[:m].reshape(x_ref.shape).astype(BF16)
        acc_ref[...] = jnp.zeros_like(acc_ref)

    nxt = jnp.minimum(e + 1, n_e - 1) * rows_per_expert
    for i in range(per_step):
        row_copy(nxt, f * per_step + i).start(priority=GATHER_DMA_PRIORITY)

    x = x_ref[...]
    a = _dot(x, wg_ref[0].astype(BF16))
    u = _dot(x, wu_ref[0].astype(BF16))
    mid = (a * _sigmoid(a) * u).astype(BF16)
    acc_ref[...] += _dot(mid, wd_ref[0].astype(BF16))

    @pl.when(f == n_f - 1)
    def _():
        gate = _row_to_cols(gate_ref[0])
        for c in range(0, y_ref.shape[2], LANES):
            y_ref[0, :, c:c + LANES] = (acc_ref[:, c:c + LANES] * gate).astype(BF16)

    @pl.when((e == n_e - 1) & (f == n_f - 1))
    def _():
        wait_rows()


def _ffn(idx, gate, h_rows, wg, wu, wd):
    n_e, m = idx.shape
    d = h_rows.shape[1] * LANES
    ff = wg.shape[2]
    tf = _tile(ff, 256)
    n_f = ff // tf
    per_step = -(-m // (8 * n_f)) * 8
    idx = jnp.pad(idx, ((0, 0), (0, n_f * per_step - m))).reshape(-1)
    return pl.pallas_call(
        functools.partial(_ffn_kernel, m=m, per_step=per_step),
        grid_spec=pltpu.PrefetchScalarGridSpec(
            num_scalar_prefetch=1,
            grid=(n_e, n_f),
            in_specs=[pl.BlockSpec(memory_space=pl.ANY),
                      pl.BlockSpec((1, 1, m), lambda e, f, idx: (e, 0, 0)),
                      pl.BlockSpec((1, d, tf), lambda e, f, idx: (e, 0, f)),
                      pl.BlockSpec((1, d, tf), lambda e, f, idx: (e, 0, f)),
                      pl.BlockSpec((1, tf, d), lambda e, f, idx: (e, f, 0))],
            out_specs=pl.BlockSpec((1, m, d), lambda e, f, idx: (e, 0, 0)),
            scratch_shapes=[pltpu.VMEM((n_f * per_step, d // LANES, LANES), F32),
                            pltpu.VMEM((m, d), BF16),
                            pltpu.VMEM((m, d), F32),
                            pltpu.SemaphoreType.DMA((1,))]),
        out_shape=jax.ShapeDtypeStruct((n_e, m, d), BF16),
        compiler_params=_params("arbitrary", "arbitrary"),
    )(idx, h_rows, gate, wg, wu, wd)


def _combine_kernel(start_ref, y_ref, pos_ref, xl_ref, g2_ref, fw_ref, o_ref, yc_ref, acc_ref,
                    *, n_experts, cap, n_b, win):
    i, j = pl.program_id(0), pl.program_id(1)
    tt = xl_ref.shape[1]
    blocks_per_tile = tt // LANES
    lane = lax.broadcasted_iota(jnp.int32, (tt, win), 1).astype(F32)
    r_i = lax.broadcasted_iota(jnp.int32, (2 * n_experts, n_experts * win), 0)
    c_e = lax.broadcasted_iota(jnp.int32, (2 * n_experts, n_experts * win), 1) // win
    spread = jnp.where(r_i == c_e, 16.0, jnp.where(r_i == c_e + n_experts, 1.0, 0.0)).astype(BF16)
    slots = _dot(pos_ref[0], spread)
    last_tile = j == pl.num_programs(1) - 1
    first, n_pass = [], 0
    for e in range(n_experts):
        row = (i * n_experts + e) * n_b
        lo = start_ref[row + j * blocks_per_tile]
        hi = jnp.where(last_tile, cap, start_ref[row + jnp.minimum((j + 1) * blocks_per_tile, n_b - 1)])
        w0 = jnp.minimum((lo // 16) * 16, cap - win)
        first.append(w0)
        n_pass = jnp.maximum(n_pass, jnp.where(hi > lo, (hi - w0 + win - 1) // win, 0))
    group = min(n_experts, COMBINE_GROUP)

    def add_pass(p, acc):
        for g0 in range(0, n_experts, group):
            hits = []
            for e in range(g0, g0 + group):
                done = first[e] + p * win
                off = pl.multiple_of(jnp.minimum(done, cap - win), 16)
                yc_ref[e * win:(e + 1) * win, :] = y_ref[e, 0, pl.ds(off, win), :]
                pe = slots[:, e * win:(e + 1) * win]
                hits.append(((pe - off.astype(F32)) == lane) & (pe >= done.astype(F32)))
            onehot = jnp.concatenate(hits, axis=1).astype(F32).astype(BF16)
            acc = acc + _dot(onehot, yc_ref[g0 * win:(g0 + group) * win, :])
        return acc

    acc_ref[...] = add_pass(0, jnp.zeros(acc_ref.shape, F32))

    def extra_pass(p, carry):
        acc_ref[...] = add_pass(p, acc_ref[...])
        return carry

    lax.fori_loop(1, n_pass, extra_pass, 0)
    xl = xl_ref[0] + g2_ref[0] * acc_ref[...]
    ms = jnp.mean(xl * xl, axis=-1, keepdims=True)
    o_ref[0] = xl * lax.rsqrt(ms + EPS) * fw_ref[...]


def _combine(starts, y, pos_col, xl, g2, fw, cap):
    b, t, d = xl.shape
    n_e = y.shape[0]
    tt = _tile(t, 512)
    win = min(cap, COMBINE_WINDOW)
    assert cap % 16 == 0 and win % 16 == 0 and tt % LANES == 0 and cap + win <= UNSELECTED_SLOT
    tok = lambda i, j, st: (i, j, 0)
    return pl.pallas_call(
        functools.partial(_combine_kernel, n_experts=n_e, cap=cap, n_b=t // LANES, win=win),
        grid_spec=pltpu.PrefetchScalarGridSpec(
            num_scalar_prefetch=1,
            grid=(b, t // tt),
            in_specs=[pl.BlockSpec((n_e, 1, cap, d), lambda i, j, st: (0, i, 0, 0)),
                      pl.BlockSpec((1, tt, 2 * n_e), tok),
                      pl.BlockSpec((1, tt, d), tok),
                      pl.BlockSpec((1, 1, d), lambda i, j, st: (i, 0, 0)),
                      pl.BlockSpec((1, d), lambda i, j, st: (0, 0))],
            out_specs=pl.BlockSpec((1, tt, d), tok),
            scratch_shapes=[pltpu.VMEM((n_e * win, d), BF16), pltpu.VMEM((tt, d), F32)]),
        out_shape=jax.ShapeDtypeStruct((b, t, d), F32),
        compiler_params=_params("arbitrary", "arbitrary"),
    )(starts, y, pos_col, xl, g2, fw)


def _rope_tables(t):
    n_freq = HEAD_DIM // 4
    rows = jnp.repeat(jnp.arange(t // GRID_W, dtype=F32), GRID_W)
    cols = jnp.tile(jnp.arange(GRID_W, dtype=F32), t // GRID_W)
    inv_freq = ROPE_BASE ** (-jnp.arange(n_freq, dtype=F32) / n_freq)
    ang_r, ang_c = rows[:, None] * inv_freq, cols[:, None] * inv_freq
    cos = jnp.concatenate([jnp.cos(ang_r)] * 2 + [jnp.cos(ang_c)] * 2, axis=-1)
    sin = jnp.concatenate([-jnp.sin(ang_r), jnp.sin(ang_r), -jnp.sin(ang_c), jnp.sin(ang_c)], axis=-1)
    return jnp.tile(cos, (1, N_HEADS)), jnp.tile(sin, (1, N_HEADS))


def kernel(x, c, ctx, c_ctx, w_ada, b_ada, norm1_w, norm2_w, w_in, mlstm_gate_bias, ret_decay_logit,
           mlstm_norm_w, ret_norm_w, w_out, w_router, w_gate, w_up, w_down, final_norm_w):
    B, T, D = x.shape
    Tc = ctx.shape[1]
    H = N_HEADS
    width = H * HEAD_DIM
    assert D == 2 * width and w_ada.shape[0] == 1
    assert T % CHUNK == 0 and Tc % CHUNK == 0 and T % GRID_W == 0
    n_lat, n_ctx = T // CHUNK, Tc // CHUNK
    n_experts = w_router.shape[2]
    cap = EC_CAPACITY_FACTOR * T // n_experts
    n_gate = 4 * H

    pad = (-(B + 1)) % 8
    cc = jnp.concatenate([c, c_ctx[None], jnp.zeros((pad, D), F32)], axis=0)
    mod = _ada(cc, w_ada[0], b_ada[0])
    sh1, sc1, g1, sh2, sc2, g2 = [mod[:B, None, i * D:(i + 1) * D] for i in range(6)]
    csh1, csc1 = [jnp.broadcast_to(mod[B, i * D:(i + 1) * D], (B, 1, D)) for i in range(2)]

    w_in0 = w_in[0]
    wm = w_in0[:, :4 * width].astype(BF16)
    wg = jnp.pad(w_in0[:, 4 * width:4 * width + n_gate], ((0, 0), (0, LANES - n_gate))).astype(BF16)
    wr = w_in0[:, 4 * width + n_gate:].astype(BF16)
    gb = jnp.pad(mlstm_gate_bias[0], (0, LANES - n_gate)).reshape(1, LANES)
    nw1 = norm1_w[0].reshape(1, D)
    cos, sin = _rope_tables(T)
    pm, gl, pr = _inproj(x, nw1, sh1, sc1, wm, wg, wr, gb, cos, sin)
    pmc, gc, prc = _inproj(ctx, nw1, csh1, csc1, wm, wg, wr, gb,
                           jnp.ones((Tc, width), F32), jnp.zeros((Tc, width), F32))

    ym = _mlstm(pm, pmc, _gateprep(gl, gc), mlstm_norm_w[0].reshape(H, 1, HEAD_DIM))
    dl = jnp.broadcast_to(jnp.swapaxes(ret_decay_logit[0], 0, 1)[:, :, None], (H, 2, LANES))
    yr = _ret(pr, prc, dl, ret_norm_w[0].reshape(H, 1, HEAD_DIM))

    wr_hi = w_router[0].astype(BF16)
    wr_lo = (w_router[0] - wr_hi.astype(F32)).astype(BF16)
    wrt = jnp.pad(jnp.concatenate([wr_hi, wr_lo], axis=1), ((0, 0), (0, LANES - 2 * n_experts)))
    xl, h2, aff_t = _outproj(ym, yr, x, w_out[0].astype(BF16), g1, norm2_w[0].reshape(1, D), sh2, sc2,
                             wrt, n_experts)

    pos, idx, gate, starts = _route(aff_t, cap)
    by_expert = lambda a: jnp.swapaxes(a.reshape(B, n_experts, cap), 0, 1).reshape(n_experts, B * cap)
    y = _ffn(by_expert(idx), by_expert(gate)[:, None, :], h2.reshape(B * T, D // LANES, LANES),
             w_gate[0], w_up[0], w_down[0])
    pos_t = jnp.swapaxes(pos.reshape(B, n_experts, T), 1, 2)
    pos_t = jnp.where(pos_t < 0, UNSELECTED_SLOT, pos_t)
    pos_hl = jnp.concatenate([pos_t >> 4, pos_t & 15], axis=-1).astype(BF16)
    return _combine(starts[:, :, 0].reshape(-1), y.reshape(n_experts, B, cap, D), pos_hl, xl, g2,
                    final_norm_w.reshape(1, D), cap)
```

```python
import functools

import jax
import jax.numpy as jnp
from jax import lax
from jax.experimental import pallas as pl
from jax.experimental.pallas import tpu as pltpu

F32 = jnp.float32
BF16 = jnp.bfloat16

CHUNK = 128
GRID_W = 64
N_HEADS = 4
HEAD_DIM = 128
N_GATE = 4 * N_HEADS
ROPE_BASE = 10000.0
EPS = 1e-6
LOG2E = 1.4426950408889634
EC_CAPACITY_FACTOR = 2
LANES = 128
VMEM_LIMIT_BYTES = 56 * 1024 * 1024
COMBINE_WINDOW = 128
UNSELECTED_SLOT = 1023
COMBINE_GROUP = 4
MIXER_UNROLL = 16
GATE_UNROLL = 8


def _params(*sem):
    return pltpu.CompilerParams(dimension_semantics=sem, vmem_limit_bytes=VMEM_LIMIT_BYTES)


def _dot(a, b):
    return jnp.dot(a, b, preferred_element_type=F32)


def _dot_nt(a, b):
    return lax.dot_general(a, b, (((1,), (1,)), ((), ())), preferred_element_type=F32)


def _dot_tn(a, b):
    return lax.dot_general(a, b, (((0,), (0,)), ((), ())), preferred_element_type=F32)


def _sigmoid(x):
    return 1.0 / (1.0 + jnp.exp(-x))


def _log_sigmoid(x):
    return jnp.minimum(x, 0.0) - jnp.log1p(jnp.exp(-jnp.abs(x)))


def _col_to_row(col):
    return jnp.broadcast_to(col, (col.shape[0], LANES)).T[0:1, :]


def _row_to_cols(row):
    pieces = [jnp.broadcast_to(row[:, c:c + LANES], (LANES, LANES)).T for c in range(0, row.shape[1], LANES)]
    return pieces[0] if len(pieces) == 1 else jnp.concatenate(pieces, axis=0)


def _tile(n, pref):
    t = min(n, pref)
    assert n % t == 0, (n, t)
    return t


def _ada_kernel(c_ref, w_ref, b_ref, o_ref):
    cv = c_ref[...]
    s = (cv * _sigmoid(cv)).astype(BF16)
    o_ref[...] = _dot(s, w_ref[...].astype(BF16)) + b_ref[...]


def _ada(cc, w, b):
    rows, d = cc.shape
    n = w.shape[1]
    tn = _tile(n, 1024)
    return pl.pallas_call(
        _ada_kernel,
        grid=(n // tn,),
        in_specs=[pl.BlockSpec((rows, d), lambda j: (0, 0)),
                  pl.BlockSpec((d, tn), lambda j: (0, j)),
                  pl.BlockSpec((1, tn), lambda j: (0, j))],
        out_specs=pl.BlockSpec((rows, tn), lambda j: (0, j)),
        out_shape=jax.ShapeDtypeStruct((rows, n), F32),
        compiler_params=_params("arbitrary"),
    )(cc, w, b.reshape(1, n))


def _rms_mod(x, nw, sh, sc):
    ms = jnp.mean(x * x, axis=-1, keepdims=True)
    return (x * lax.rsqrt(ms + EPS) * nw) * (1.0 + sc) + sh


def _inproj_kernel(x_ref, nw_ref, sh_ref, sc_ref, wm_ref, wg_ref, wr_ref, gb_ref, cos_ref, sin_ref,
                   pm_ref, g_ref, pr_ref, *, width, k_scale):
    hb = _rms_mod(x_ref[0], nw_ref[...], sh_ref[0], sc_ref[0]).astype(BF16)
    for j in range(4):
        cols = slice(j * width, (j + 1) * width)
        acc = _dot(hb, wm_ref[:, cols])
        if j == 1:
            acc = acc * k_scale
        for h in range(N_HEADS):
            pm_ref[0, j * N_HEADS + h] = acc[:, h * HEAD_DIM:(h + 1) * HEAD_DIM].astype(BF16)
    g_ref[0] = (_dot(hb, wg_ref[...]) + gb_ref[...]).T[:g_ref.shape[1]]
    cos = cos_ref[...]
    sin = sin_ref[...]
    lane = lax.broadcasted_iota(jnp.int32, cos.shape, 1)
    even = ((lane // 32) % 2) == 0
    for j in range(4):
        cols = slice(j * width, (j + 1) * width)
        acc = _dot(hb, wr_ref[:, cols])
        if j == 1:
            acc = acc * k_scale
        if j < 2:
            swapped = jnp.where(even, pltpu.roll(acc, width - 32, 1), pltpu.roll(acc, 32, 1))
            acc = acc * cos + swapped * sin
        for h in range(N_HEADS):
            pr_ref[0, j * N_HEADS + h] = acc[:, h * HEAD_DIM:(h + 1) * HEAD_DIM].astype(BF16)


def _inproj(x, nw, sh, sc, wm, wg, wr, gb, cos, sin):
    b, t, d = x.shape
    width = wm.shape[1] // 4
    tm = _tile(t, 512)
    kern = functools.partial(_inproj_kernel, width=width, k_scale=HEAD_DIM ** -0.5)
    const = lambda i, j: (0, 0)
    return pl.pallas_call(
        kern,
        grid=(b, t // tm),
        in_specs=[pl.BlockSpec((1, tm, d), lambda i, j: (i, j, 0)),
                  pl.BlockSpec((1, d), const),
                  pl.BlockSpec((1, 1, d), lambda i, j: (i, 0, 0)),
                  pl.BlockSpec((1, 1, d), lambda i, j: (i, 0, 0)),
                  pl.BlockSpec(wm.shape, const),
                  pl.BlockSpec(wg.shape, const),
                  pl.BlockSpec(wr.shape, const),
                  pl.BlockSpec((1, LANES), const),
                  pl.BlockSpec((tm, width), lambda i, j: (j, 0)),
                  pl.BlockSpec((tm, width), lambda i, j: (j, 0))],
        out_specs=[pl.BlockSpec((1, 4 * N_HEADS, tm, HEAD_DIM), lambda i, j: (i, 0, j, 0)),
                   pl.BlockSpec((1, N_GATE, tm), lambda i, j: (i, 0, j)),
                   pl.BlockSpec((1, 4 * N_HEADS, tm, HEAD_DIM), lambda i, j: (i, 0, j, 0))],
        out_shape=[jax.ShapeDtypeStruct((b, 4 * N_HEADS, t, HEAD_DIM), BF16),
                   jax.ShapeDtypeStruct((b, N_GATE, t), F32),
                   jax.ShapeDtypeStruct((b, 4 * N_HEADS, t, HEAD_DIM), BF16)],
        compiler_params=_params("arbitrary", "arbitrary"),
    )(x, nw, sh, sc, wm, wg, wr, gb, cos, sin)


def _scan_lanes(x, op, fill, reverse):
    lane = lax.broadcasted_iota(jnp.int32, x.shape, 1)
    sh = 1
    while sh < LANES:
        if reverse:
            x = op(x, jnp.where(lane < LANES - sh, pltpu.roll(x, LANES - sh, 1), fill))
        else:
            x = op(x, jnp.where(lane >= sh, pltpu.roll(x, sh, 1), fill))
        sh *= 2
    return x


def _gateprep_kernel(gl_ref, gc_ref, o_ref, *, n_lat, n_ctx):
    n_b = gl_ref.shape[0]
    half = N_GATE // 2
    rows = n_b * half
    row = lax.broadcasted_iota(jnp.int32, (rows, LANES), 0)
    lane = lax.broadcasted_iota(jnp.int32, (rows, LANES), 1)
    is_fwd = (row % half) < N_HEADS
    last = lane == jnp.where(is_fwd, LANES - 1, 0)

    def chunk_rows(ref, lo, c_fwd, c_bwd):
        def at(c):
            cols = (slice(c * LANES, (c + 1) * LANES) if isinstance(c, int)
                    else pl.ds(pl.multiple_of(c * LANES, LANES), LANES))
            return jnp.concatenate([ref[s, lo:lo + half, cols] for s in range(n_b)], axis=0)
        return jnp.where(is_fwd, at(c_fwd), at(c_bwd))

    def step(c, ig, fg, m):
        lf = _log_sigmoid(fg)
        b = jnp.where(is_fwd, _scan_lanes(lf, jnp.add, 0.0, False), _scan_lanes(lf, jnp.add, 0.0, True))
        b_tot = jnp.sum(jnp.where(last, b, 0.0), axis=1, keepdims=True)
        a = ig - b
        m_new = jnp.maximum(b_tot + m, jnp.max(b_tot + a, axis=1, keepdims=True))
        w = jnp.exp(b_tot + a - m_new)
        decay = jnp.exp(b_tot + m - m_new)
        run = jnp.where(is_fwd, _scan_lanes(a, jnp.maximum, -jnp.inf, False),
                        _scan_lanes(a, jnp.maximum, -jnp.inf, True))
        mm = jnp.maximum(m, run)
        o_ref[c, 0] = a * LOG2E
        o_ref[c, 1] = w
        o_ref[c, 2] = mm * LOG2E
        o_ref[c, 3] = jnp.exp(-(b + mm))
        o_ref[c, 4] = jnp.broadcast_to(decay, (rows, LANES))
        o_ref[c, 5] = jnp.broadcast_to(m * LOG2E, (rows, LANES))
        return m_new

    m = jnp.zeros((rows, 1), F32)
    for c in range(n_ctx):
        m = step(c, chunk_rows(gc_ref, 0, c, n_ctx - 1 - c), chunk_rows(gc_ref, half, c, n_ctx - 1 - c), m)

    def latent(i, m):
        return step(n_ctx + i, chunk_rows(gl_ref, 0, i, n_lat - 1 - i),
                    chunk_rows(gl_ref, half, i, n_lat - 1 - i), m)

    lax.fori_loop(0, n_lat, latent, m, unroll=min(n_lat, GATE_UNROLL))


def _gateprep(gl, gc):
    n_lat, n_ctx = gl.shape[2] // CHUNK, gc.shape[2] // CHUNK
    return pl.pallas_call(
        functools.partial(_gateprep_kernel, n_lat=n_lat, n_ctx=n_ctx),
        out_shape=jax.ShapeDtypeStruct((n_ctx + n_lat, 6, gl.shape[0] * N_GATE // 2, LANES), F32),
        compiler_params=pltpu.CompilerParams(vmem_limit_bytes=VMEM_LIMIT_BYTES),
    )(gl, gc)


def _head_norm_gate(h, gate, nw):
    return gate * (h * lax.rsqrt(jnp.mean(h * h, axis=-1, keepdims=True) + EPS)) * nw


def _mlstm_kernel(q_ref, k_ref, v_ref, o_ref, kc_ref, vc_ref, st_ref, nw_ref, y_ref,
                  sf_ref, sb_ref, cf_ref, cb_ref, *, n_lat, n_ctx, unroll):
    L = CHUNK
    A, WT, MM, EN, DECAY, M0 = range(6)
    ones = jnp.ones((L, HEAD_DIM), BF16)
    stream_f = pl.program_id(0) * (N_GATE // 2) + pl.program_id(1)
    stream_b = stream_f + N_HEADS

    def stat(step, plane, stream):
        return st_ref[step, plane, pl.ds(stream, 1), :]

    def to_rows(row):
        return jnp.broadcast_to(row, (L, L)).T

    def update(state_ref, k, v, step, stream):
        wkt = (k.astype(F32).T * stat(step, WT, stream)).astype(BF16)
        u = _dot(wkt, jnp.concatenate([v, ones], axis=1))
        decay = stat(step, DECAY, stream)
        state_ref[...] = jnp.concatenate([decay, decay], axis=1) * state_ref[...] + u

    cf_ref[...] = jnp.zeros_like(cf_ref)
    cb_ref[...] = jnp.zeros_like(cb_ref)
    for c in range(n_ctx):
        rf = slice(c * L, (c + 1) * L)
        rb = slice((n_ctx - 1 - c) * L, (n_ctx - c) * L)
        update(cf_ref, kc_ref[0, rf, :], vc_ref[0, rf, :], c, stream_f)
        update(cb_ref, kc_ref[0, rb, :], vc_ref[0, rb, :], c, stream_b)

    def state_body(i, carry):
        jb = n_lat - 1 - i
        rf = pl.ds(pl.multiple_of(i * L, L), L)
        rb = pl.ds(pl.multiple_of(jb * L, L), L)
        sf_ref[i] = cf_ref[...].astype(BF16)
        sb_ref[jb] = cb_ref[...].astype(BF16)
        update(cf_ref, k_ref[0, rf, :], v_ref[0, rf, :], n_ctx + i, stream_f)
        update(cb_ref, k_ref[0, rb, :], v_ref[0, rb, :], n_ctx + i, stream_b)
        return carry

    lax.fori_loop(0, n_lat, state_body, 0, unroll=unroll)

    r_i = lax.broadcasted_iota(jnp.int32, (L, L), 0)
    c_i = lax.broadcasted_iota(jnp.int32, (L, L), 1)
    tril = c_i <= r_i
    triu = c_i >= r_i
    nw = nw_ref[0]

    def out_body(j, carry):
        rows = pl.ds(pl.multiple_of(j * L, L), L)
        step_f = n_ctx + j
        step_b = n_ctx + n_lat - 1 - j
        q = q_ref[0, rows, :]
        k = k_ref[0, rows, :]
        v = v_ref[0, rows, :]
        mm_f = to_rows(stat(step_f, MM, stream_f))
        mm_b = to_rows(stat(step_b, MM, stream_b))
        s = _dot_nt(q, k)
        d_f = jnp.where(tril, jnp.exp2(stat(step_f, A, stream_f) - mm_f), 0.0)
        d_b = jnp.where(triu, jnp.exp2(stat(step_b, A, stream_b) - mm_b), 0.0)
        iw_f = jnp.exp2(stat(step_f, M0, stream_f) - mm_f)
        iw_b = jnp.exp2(stat(step_b, M0, stream_b) - mm_b)
        qf = q.astype(F32)
        vext = jnp.concatenate([v, ones], axis=1)
        lhs_f = jnp.concatenate([(s * d_f).astype(BF16), (qf * iw_f).astype(BF16)], axis=1)
        lhs_b = jnp.concatenate([(s * d_b).astype(BF16), (qf * iw_b).astype(BF16)], axis=1)
        tot_f = _dot(lhs_f, jnp.concatenate([vext, sf_ref[j]], axis=0))
        tot_b = _dot(lhs_b, jnp.concatenate([vext, sb_ref[j]], axis=0))
        D = HEAD_DIM
        h = (tot_f[:, :D] / jnp.maximum(jnp.abs(tot_f[:, D:]), to_rows(stat(step_f, EN, stream_f)))
             + tot_b[:, :D] / jnp.maximum(jnp.abs(tot_b[:, D:]), to_rows(stat(step_b, EN, stream_b))))
        gate = _sigmoid(o_ref[0, rows, :].astype(F32))
        y_ref[0, rows, :] = _head_norm_gate(h, gate, nw).astype(BF16)
        return carry

    lax.fori_loop(0, n_lat, out_body, 0, unroll=unroll)


def _head_slab(t, group):
    return pl.BlockSpec((None, 1, t, HEAD_DIM), lambda i, h: (i, group * N_HEADS + h, 0, 0))


def _mlstm(pm, pmc, stats, nw):
    b, _, t, _ = pm.shape
    tc = pmc.shape[2]
    n_lat, n_ctx = t // CHUNK, tc // CHUNK
    H = N_HEADS
    blk = lambda off: _head_slab(t, off // H)
    cblk = lambda off: _head_slab(tc, off // H)
    return pl.pallas_call(
        functools.partial(_mlstm_kernel, n_lat=n_lat, n_ctx=n_ctx, unroll=min(n_lat, MIXER_UNROLL)),
        grid=(b, H),
        in_specs=[blk(0), blk(H), blk(2 * H), blk(3 * H), cblk(H), cblk(2 * H),
                  pl.BlockSpec(stats.shape, lambda i, h: (0, 0, 0, 0)),
                  pl.BlockSpec((1, 1, HEAD_DIM), lambda i, h: (h, 0, 0))],
        out_specs=_head_slab(t, 0),
        out_shape=jax.ShapeDtypeStruct((b, H, t, HEAD_DIM), BF16),
        scratch_shapes=[pltpu.VMEM((n_lat, HEAD_DIM, HEAD_DIM + LANES), BF16),
                        pltpu.VMEM((n_lat, HEAD_DIM, HEAD_DIM + LANES), BF16),
                        pltpu.VMEM((HEAD_DIM, HEAD_DIM + LANES), F32),
                        pltpu.VMEM((HEAD_DIM, HEAD_DIM + LANES), F32)],
        compiler_params=_params("arbitrary", "arbitrary"),
    )(pm, pm, pm, pm, pmc, pmc, stats, nw)


def _ret_kernel(q_ref, k_ref, v_ref, o_ref, kc_ref, vc_ref, dl_ref, nw_ref, y_ref,
                sf_ref, sb_ref, rf_ref, rb_ref, *, n_lat, n_ctx, unroll):
    L = CHUNK
    lg = _log_sigmoid(dl_ref[0])
    lg_f, lg_b = lg[0:1, :], lg[1:2, :]
    r_i = lax.broadcasted_iota(jnp.int32, (L, L), 0)
    c_i = lax.broadcasted_iota(jnp.int32, (L, L), 1)
    r_f = r_i.astype(F32)
    rel = (r_i - c_i).astype(F32)
    kw_f = jnp.exp((L - 1.0 - r_f) * lg_f)
    kw_b = jnp.exp(r_f * lg_b)
    in_f = jnp.exp((r_f + 1.0) * lg_f)
    in_b = jnp.exp((L - r_f) * lg_b)
    dch_f = jnp.exp(L * lg_f)
    dch_b = jnp.exp(L * lg_b)
    d_mat = (jnp.where(rel >= 0, jnp.exp(jnp.maximum(rel, 0.0) * lg_f), 0.0)
             + jnp.where(rel <= 0, jnp.exp(jnp.maximum(-rel, 0.0) * lg_b), 0.0))

    def update(state_ref, k, v, kw, dch):
        wk = (k.astype(F32) * kw).astype(BF16)
        state_ref[...] = dch * state_ref[...] + _dot_tn(wk, v)

    rf_ref[...] = jnp.zeros_like(rf_ref)
    rb_ref[...] = jnp.zeros_like(rb_ref)
    for c in range(n_ctx):
        rows = slice(c * L, (c + 1) * L)
        update(rf_ref, kc_ref[0, rows, :], vc_ref[0, rows, :], kw_f, dch_f)
    for c in reversed(range(n_ctx)):
        rows = slice(c * L, (c + 1) * L)
        update(rb_ref, kc_ref[0, rows, :], vc_ref[0, rows, :], kw_b, dch_b)

    def state_body(i, carry):
        jb = n_lat - 1 - i
        rowf = pl.ds(pl.multiple_of(i * L, L), L)
        rowb = pl.ds(pl.multiple_of(jb * L, L), L)
        sf_ref[i] = rf_ref[...].astype(BF16)
        sb_ref[jb] = rb_ref[...].astype(BF16)
        update(rf_ref, k_ref[0, rowf, :], v_ref[0, rowf, :], kw_f, dch_f)
        update(rb_ref, k_ref[0, rowb, :], v_ref[0, rowb, :], kw_b, dch_b)
        return carry

    lax.fori_loop(0, n_lat, state_body, 0, unroll=unroll)
    nw = nw_ref[0]

    def out_body(j, carry):
        rows = pl.ds(pl.multiple_of(j * L, L), L)
        q = q_ref[0, rows, :]
        s = _dot_nt(q, k_ref[0, rows, :])
        intra = _dot((s * d_mat).astype(BF16), v_ref[0, rows, :])
        inter = _dot(q, jnp.concatenate([sf_ref[j], sb_ref[j]], axis=1))
        out = intra + in_f * inter[:, :HEAD_DIM] + in_b * inter[:, HEAD_DIM:]
        og = o_ref[0, rows, :].astype(F32)
        y_ref[0, rows, :] = _head_norm_gate(out, og * _sigmoid(og), nw).astype(BF16)
        return carry

    lax.fori_loop(0, n_lat, out_body, 0, unroll=unroll)


def _ret(pr, prc, dl, nw):
    b, _, t, _ = pr.shape
    tc = prc.shape[2]
    n_lat, n_ctx = t // CHUNK, tc // CHUNK
    H = N_HEADS
    blk = lambda off: _head_slab(t, off // H)
    cblk = lambda off: _head_slab(tc, off // H)
    return pl.pallas_call(
        functools.partial(_ret_kernel, n_lat=n_lat, n_ctx=n_ctx, unroll=min(n_lat, MIXER_UNROLL)),
        grid=(b, H),
        in_specs=[blk(0), blk(H), blk(2 * H), blk(3 * H), cblk(H), cblk(2 * H),
                  pl.BlockSpec((1, 2, LANES), lambda i, h: (h, 0, 0)),
                  pl.BlockSpec((1, 1, HEAD_DIM), lambda i, h: (h, 0, 0))],
        out_specs=_head_slab(t, 0),
        out_shape=jax.ShapeDtypeStruct((b, H, t, HEAD_DIM), BF16),
        scratch_shapes=[pltpu.VMEM((n_lat, HEAD_DIM, HEAD_DIM), BF16),
                        pltpu.VMEM((n_lat, HEAD_DIM, HEAD_DIM), BF16),
                        pltpu.VMEM((HEAD_DIM, HEAD_DIM), F32),
                        pltpu.VMEM((HEAD_DIM, HEAD_DIM), F32)],
        compiler_params=_params("arbitrary", "arbitrary"),
    )(pr, pr, pr, pr, prc, prc, dl, nw)


def _outproj_kernel(ym_ref, yr_ref, x_ref, wo_ref, g1_ref, nw_ref, sh_ref, sc_ref, wrt_ref,
                    xl_ref, h2_ref, aff_ref, *, n_experts):
    heads = [ym_ref[0, h] for h in range(N_HEADS)] + [yr_ref[0, h] for h in range(N_HEADS)]
    y = _dot(jnp.concatenate(heads, axis=1), wo_ref[...])
    xl = x_ref[0] + g1_ref[0] * y
    xl_ref[0] = xl
    h2 = _rms_mod(xl, nw_ref[...], sh_ref[0], sc_ref[0])
    h2_ref[0] = h2.reshape(h2.shape[0], h2.shape[1] // LANES, LANES)
    h_hi = h2.astype(BF16)
    h_lo = (h2 - h_hi.astype(F32)).astype(BF16)
    p_hi = _dot(h_hi, wrt_ref[...])
    p_lo = _dot(h_lo, wrt_ref[...])
    logits = p_hi + pltpu.roll(p_hi, LANES - n_experts, 1) + p_lo
    lane = lax.broadcasted_iota(jnp.int32, logits.shape, 1)
    logits = jnp.where(lane < n_experts, logits, -jnp.inf)
    e = jnp.exp(logits - jnp.max(logits, axis=-1, keepdims=True))
    aff_t = (e / jnp.sum(e, axis=-1, keepdims=True)).T[:n_experts]
    aff_ref[0] = aff_t.reshape(n_experts, aff_t.shape[1] // LANES, LANES)


def _outproj(ym, yr, x, wo, g1, nw, sh, sc, wrt, n_experts):
    b, t, d = x.shape
    tm = _tile(t, 8 * LANES)
    slab = pl.BlockSpec((1, N_HEADS, tm, HEAD_DIM), lambda i, j: (i, 0, j, 0))
    const = lambda i, j: (0, 0)
    per_b = lambda i, j: (i, 0, 0)
    tok = lambda i, j: (i, j, 0)
    return pl.pallas_call(
        functools.partial(_outproj_kernel, n_experts=n_experts),
        grid=(b, t // tm),
        in_specs=[slab, slab,
                  pl.BlockSpec((1, tm, d), tok), pl.BlockSpec(wo.shape, const),
                  pl.BlockSpec((1, 1, d), per_b), pl.BlockSpec((1, d), const),
                  pl.BlockSpec((1, 1, d), per_b), pl.BlockSpec((1, 1, d), per_b),
                  pl.BlockSpec(wrt.shape, const)],
        out_specs=[pl.BlockSpec((1, tm, d), tok),
                   pl.BlockSpec((1, tm, d // LANES, LANES), lambda i, j: (i, j, 0, 0)),
                   pl.BlockSpec((1, n_experts, tm // LANES, LANES), lambda i, j: (i, 0, j, 0))],
        out_shape=[jax.ShapeDtypeStruct((b, t, d), F32),
                   jax.ShapeDtypeStruct((b, t, d // LANES, LANES), F32),
                   jax.ShapeDtypeStruct((b, n_experts, t // LANES, LANES), F32)],
        compiler_params=_params("arbitrary", "arbitrary"),
    )(ym, yr, x, wo, g1, nw, sh, sc, wrt)


def _select_top(a, cap):
    n_e, n_b, _ = a.shape

    ones = jnp.ones((LANES, LANES), BF16)

    def count(mask):
        per_block = _dot(mask.astype(F32).astype(BF16).reshape(n_e * n_b, LANES), ones)
        return jnp.sum(per_block.reshape(n_e, n_b, LANES), axis=1, keepdims=True).astype(jnp.int32)

    def search(i, thr):
        cand = thr | jnp.left_shift(jnp.int32(1), 30 - i)
        return jnp.where(count(a >= pltpu.bitcast(cand, F32)) >= cap, cand, thr)

    thr = lax.fori_loop(0, 31, search, jnp.zeros((n_e, 1, LANES), jnp.int32))
    above = a >= pltpu.bitcast(thr + 1, F32)
    band = (a >= pltpu.bitcast(thr, F32)) & jnp.logical_not(above)
    need = cap - count(above)[:, :, 0:1]
    tok = (lax.broadcasted_iota(jnp.int32, a.shape, 1) * LANES
           + lax.broadcasted_iota(jnp.int32, a.shape, 2))

    def take_one(_, carry):
        sel, band, need = carry
        in_band = band > 0
        best = jnp.max(jnp.max(jnp.where(in_band, a, -1.0), axis=2, keepdims=True), axis=1, keepdims=True)
        cand = jnp.where(in_band & (a == best), tok, n_b * LANES)
        first = jnp.min(jnp.min(cand, axis=2, keepdims=True), axis=1, keepdims=True)
        take = ((tok == first) & (need > 0)).astype(jnp.int32)
        return sel | take, band - take, need - 1

    sel, _, _ = lax.fori_loop(0, jnp.max(need), take_one,
                              (above.astype(jnp.int32), band.astype(jnp.int32), need))
    return sel


def _route_kernel(aff_ref, pos_ref, idx_ref, gate_ref, start_ref, sel_ref, *, cap):
    i = pl.program_id(0)
    n_s, n_e, n_b, _ = aff_ref.shape

    @pl.when(i == 0)
    def _():
        sel_all = _select_top(aff_ref[...].reshape(n_s * n_e, n_b, LANES), cap)
        sel_ref[...] = sel_all.reshape(n_s, n_e, n_b, LANES)

    a = aff_ref[i]
    sel = sel_ref[i] > 0

    rows = n_e * n_b
    u_i = lax.broadcasted_iota(jnp.int32, (LANES, LANES), 0)
    s_i = lax.broadcasted_iota(jnp.int32, (LANES, LANES), 1)
    incl = (u_i <= s_i).astype(BF16)
    ones = jnp.ones((LANES, LANES), BF16)
    r_i = lax.broadcasted_iota(jnp.int32, (rows, rows), 0)
    c_i = lax.broadcasted_iota(jnp.int32, (rows, rows), 1)
    before = ((r_i // n_b == c_i // n_b) & (c_i < r_i)).astype(BF16)

    x = sel.astype(F32).reshape(rows, LANES)
    xb = x.astype(BF16)
    block_tot = _dot(xb, ones).astype(BF16)
    start = _dot(before, block_tot)
    pos = jnp.where(x > 0, start + _dot(xb, incl) - x, -1.0).astype(jnp.int32)
    pos_ref[0] = pos.reshape(n_e, n_b, LANES)
    start_ref[0] = start.astype(jnp.int32)

    pad = (-rows) % LANES
    def pad_rows(m):
        return m if pad == 0 else jnp.concatenate([m, jnp.zeros((pad, LANES), m.dtype)], axis=0)
    start_p = pad_rows(start)
    end_p = pad_rows(start + block_tot.astype(F32))
    pos_p = pad_rows(pos)
    pos_p = jnp.where(pos_p < 0, 1023, pos_p)
    a_p = pad_rows(a.reshape(rows, LANES))
    a_hi = a_p.astype(BF16)
    a_mid = (a_p - a_hi.astype(F32)).astype(BF16)
    a_lo = (a_p - a_hi.astype(F32) - a_mid.astype(F32)).astype(BF16)
    per_group = LANES // n_b
    j_col = lax.broadcasted_iota(jnp.int32, (cap, LANES), 0).astype(F32)
    lane = lax.broadcasted_iota(jnp.int32, (cap, LANES), 1)
    lane_f = lane.astype(F32)
    block_f = (lane % n_b).astype(F32)
    row0 = pl.program_id(0) * (n_b * LANES)
    for g in range((rows + pad) // LANES):
        rs = slice(g * LANES, (g + 1) * LANES)
        s_row = start_p[rs].T[0:1, :]
        e_row = end_p[rs].T[0:1, :]
        hi = (pos_p[rs] >> 4).astype(F32).astype(BF16)
        lo = (pos_p[rs] & 15).astype(F32).astype(BF16)
        for el in range(per_group):
            e = g * per_group + el
            if e >= n_e:
                break
            mine = (j_col >= s_row) & (j_col < e_row) & ((lane // n_b) == el)
            mine_b = mine.astype(F32).astype(BF16)
            slots = 16.0 * _dot(mine_b, hi) + _dot(mine_b, lo)
            hit = slots == j_col
            in_block = jnp.sum(jnp.where(hit, lane_f, 0.0), axis=1, keepdims=True)
            block = jnp.sum(jnp.where(mine, block_f, 0.0), axis=1, keepdims=True)
            a_blk = _dot(mine_b, a_hi[rs]) + _dot(mine_b, a_mid[rs]) + _dot(mine_b, a_lo[rs])
            gate = jnp.sum(jnp.where(hit, a_blk, 0.0), axis=1, keepdims=True)
            idx_ref[0, e] = _col_to_row(block * LANES + in_block).astype(jnp.int32) + row0
            gate_ref[0, e] = _col_to_row(gate)


def _route(aff_t, cap):
    b, n_e, n_b, _ = aff_t.shape
    spec = pl.BlockSpec((1, n_e, n_b, LANES), lambda i: (i, 0, 0, 0))
    slot_spec = pl.BlockSpec((1, n_e, 1, cap), lambda i: (i, 0, 0, 0))
    return pl.pallas_call(
        functools.partial(_route_kernel, cap=cap),
        grid=(b,), in_specs=[pl.BlockSpec(aff_t.shape, lambda i: (0, 0, 0, 0))],
        out_specs=[spec, slot_spec, slot_spec, pl.BlockSpec((1, n_e * n_b, LANES), lambda i: (i, 0, 0))],
        scratch_shapes=[pltpu.VMEM(aff_t.shape, jnp.int32)],
        out_shape=[jax.ShapeDtypeStruct(aff_t.shape, jnp.int32),
                   jax.ShapeDtypeStruct((b, n_e, 1, cap), jnp.int32),
                   jax.ShapeDtypeStruct((b, n_e, 1, cap), F32),
                   jax.ShapeDtypeStruct((b, n_e * n_b, LANES), jnp.int32)],
        compiler_params=_params("arbitrary"),
    )(aff_t)


GATHER_DMA_PRIORITY = 1


def _ffn_kernel(idx_ref, h_hbm, gate_ref, wg_ref, wu_ref, wd_ref, y_ref, stage_ref, x_ref, acc_ref, sem,
                *, m, per_step):
    e, f = pl.program_id(0), pl.program_id(1)
    n_e, n_f = pl.num_programs(0), pl.num_programs(1)
    rows_per_expert = stage_ref.shape[0]

    def row_copy(base, j):
        return pltpu.make_async_copy(h_hbm.at[pl.ds(idx_ref[base + j], 1)],
                                     stage_ref.at[pl.ds(j, 1)], sem.at[0])

    def wait_rows():
        pltpu.make_async_copy(h_hbm.at[pl.ds(0, rows_per_expert)], stage_ref, sem.at[0]).wait()

    @pl.when((e == 0) & (f == 0))
    def _():
        def start_row(j, carry):
            row_copy(0, j).start(priority=GATHER_DMA_PRIORITY)
            return carry
        lax.fori_loop(0, rows_per_expert, start_row, 0)

    @pl.when(f == 0)
    def _():
        wait_rows()
        x_ref[...] = stage_ref[:m].reshape(x_ref.shape).astype(BF16)
        acc_ref[...] = jnp.zeros_like(acc_ref)

    nxt = jnp.minimum(e + 1, n_e - 1) * rows_per_expert
    for i in range(per_step):
        row_copy(nxt, f * per_step + i).start(priority=GATHER_DMA_PRIORITY)

    x = x_ref[...]
    a = _dot(x, wg_ref[0].astype(BF16))
    u = _dot(x, wu_ref[0].astype(BF16))
    mid = (a * _sigmoid(a) * u).astype(BF16)
    acc_ref[...] += _dot(mid, wd_ref[0].astype(BF16))

    @pl.when(f == n_f - 1)
    def _():
        gate = _row_to_cols(gate_ref[0])
        for c in range(0, y_ref.shape[2], LANES):
            y_ref[0, :, c:c + LANES] = (acc_ref[:, c:c + LANES] * gate).astype(BF16)

    @pl.when((e == n_e - 1) & (f == n_f - 1))
    def _():
        wait_rows()


def _ffn(idx, gate, h_rows, wg, wu, wd):
    n_e, m = idx.shape
    d = h_rows.shape[1] * LANES
    ff = wg.shape[2]
    tf = _tile(ff, 256)
    n_f = ff // tf
    per_step = -(-m // (8 * n_f)) * 8
    idx = jnp.pad(idx, ((0, 0), (0, n_f * per_step - m))).reshape(-1)
    return pl.pallas_call(
        functools.partial(_ffn_kernel, m=m, per_step=per_step),
        grid_spec=pltpu.PrefetchScalarGridSpec(
            num_scalar_prefetch=1,
            grid=(n_e, n_f),
            in_specs=[pl.BlockSpec(memory_space=pl.ANY),
                      pl.BlockSpec((1, 1, m), lambda e, f, idx: (e, 0, 0)),
                      pl.BlockSpec((1, d, tf), lambda e, f, idx: (e, 0, f)),
                      pl.BlockSpec((1, d, tf), lambda e, f, idx: (e, 0, f)),
                      pl.BlockSpec((1, tf, d), lambda e, f, idx: (e, f, 0))],
            out_specs=pl.BlockSpec((1, m, d), lambda e, f, idx: (e, 0, 0)),
            scratch_shapes=[pltpu.VMEM((n_f * per_step, d // LANES, LANES), F32),
                            pltpu.VMEM((m, d), BF16),
                            pltpu.VMEM((m, d), F32),
                            pltpu.SemaphoreType.DMA((1,))]),
        out_shape=jax.ShapeDtypeStruct((n_e, m, d), BF16),
        compiler_params=_params("arbitrary", "arbitrary"),
    )(idx, h_rows, gate, wg, wu, wd)


def _combine_kernel(start_ref, y_ref, pos_ref, xl_ref, g2_ref, fw_ref, o_ref, yc_ref, acc_ref,
                    *, n_experts, cap, n_b, win):
    i, j = pl.program_id(0), pl.program_id(1)
    tt = xl_ref.shape[1]
    blocks_per_tile = tt // LANES
    lane = lax.broadcasted_iota(jnp.int32, (tt, win), 1).astype(F32)
    r_i = lax.broadcasted_iota(jnp.int32, (2 * n_experts, n_experts * win), 0)
    c_e = lax.broadcasted_iota(jnp.int32, (2 * n_experts, n_experts * win), 1) // win
    spread = jnp.where(r_i == c_e, 16.0, jnp.where(r_i == c_e + n_experts, 1.0, 0.0)).astype(BF16)
    slots = _dot(pos_ref[0], spread)
    last_tile = j == pl.num_programs(1) - 1
    first, n_pass = [], 0
    for e in range(n_experts):
        row = (i * n_experts + e) * n_b
        lo = start_ref[row + j * blocks_per_tile]
        hi = jnp.where(last_tile, cap, start_ref[row + jnp.minimum((j + 1) * blocks_per_tile, n_b - 1)])
        w0 = jnp.minimum((lo // 16) * 16, cap - win)
        first.append(w0)
        n_pass = jnp.maximum(n_pass, jnp.where(hi > lo, (hi - w0 + win - 1) // win, 0))
    group = min(n_experts, COMBINE_GROUP)

    def add_pass(p, acc):
        for g0 in range(0, n_experts, group):
            hits = []
            for e in range(g0, g0 + group):
                done = first[e] + p * win
                off = pl.multiple_of(jnp.minimum(done, cap - win), 16)
                yc_ref[e * win:(e + 1) * win, :] = y_ref[e, 0, pl.ds(off, win), :]
                pe = slots[:, e * win:(e + 1) * win]
                hits.append(((pe - off.astype(F32)) == lane) & (pe >= done.astype(F32)))
            onehot = jnp.concatenate(hits, axis=1).astype(F32).astype(BF16)
            acc = acc + _dot(onehot, yc_ref[g0 * win:(g0 + group) * win, :])
        return acc

    acc_ref[...] = add_pass(0, jnp.zeros(acc_ref.shape, F32))

    def extra_pass(p, carry):
        acc_ref[...] = add_pass(p, acc_ref[...])
        return carry

    lax.fori_loop(1, n_pass, extra_pass, 0)
    xl = xl_ref[0] + g2_ref[0] * acc_ref[...]
    ms = jnp.mean(xl * xl, axis=-1, keepdims=True)
    o_ref[0] = xl * lax.rsqrt(ms + EPS) * fw_ref[...]


def _combine(starts, y, pos_col, xl, g2, fw, cap):
    b, t, d = xl.shape
    n_e = y.shape[0]
    tt = _tile(t, 512)
    win = min(cap, COMBINE_WINDOW)
    assert cap % 16 == 0 and win % 16 == 0 and tt % LANES == 0 and cap + win <= UNSELECTED_SLOT
    tok = lambda i, j, st: (i, j, 0)
    return pl.pallas_call(
        functools.partial(_combine_kernel, n_experts=n_e, cap=cap, n_b=t // LANES, win=win),
        grid_spec=pltpu.PrefetchScalarGridSpec(
            num_scalar_prefetch=1,
            grid=(b, t // tt),
            in_specs=[pl.BlockSpec((n_e, 1, cap, d), lambda i, j, st: (0, i, 0, 0)),
                      pl.BlockSpec((1, tt, 2 * n_e), tok),
                      pl.BlockSpec((1, tt, d), tok),
                      pl.BlockSpec((1, 1, d), lambda i, j, st: (i, 0, 0)),
                      pl.BlockSpec((1, d), lambda i, j, st: (0, 0))],
            out_specs=pl.BlockSpec((1, tt, d), tok),
            scratch_shapes=[pltpu.VMEM((n_e * win, d), BF16), pltpu.VMEM((tt, d), F32)]),
        out_shape=jax.ShapeDtypeStruct((b, t, d), F32),
        compiler_params=_params("arbitrary", "arbitrary"),
    )(starts, y, pos_col, xl, g2, fw)


def _rope_tables(t):
    n_freq = HEAD_DIM // 4
    rows = jnp.repeat(jnp.arange(t // GRID_W, dtype=F32), GRID_W)
    cols = jnp.tile(jnp.arange(GRID_W, dtype=F32), t // GRID_W)
    inv_freq = ROPE_BASE ** (-jnp.arange(n_freq, dtype=F32) / n_freq)
    ang_r, ang_c = rows[:, None] * inv_freq, cols[:, None] * inv_freq
    cos = jnp.concatenate([jnp.cos(ang_r)] * 2 + [jnp.cos(ang_c)] * 2, axis=-1)
    sin = jnp.concatenate([-jnp.sin(ang_r), jnp.sin(ang_r), -jnp.sin(ang_c), jnp.sin(ang_c)], axis=-1)
    return jnp.tile(cos, (1, N_HEADS)), jnp.tile(sin, (1, N_HEADS))


def kernel(x, c, ctx, c_ctx, w_ada, b_ada, norm1_w, norm2_w, w_in, mlstm_gate_bias, ret_decay_logit,
           mlstm_norm_w, ret_norm_w, w_out, w_router, w_gate, w_up, w_down, final_norm_w):
    B, T, D = x.shape
    Tc = ctx.shape[1]
    H = N_HEADS
    width = H * HEAD_DIM
    assert D == 2 * width and w_ada.shape[0] == 1
    assert T % CHUNK == 0 and Tc % CHUNK == 0 and T % GRID_W == 0
    n_lat, n_ctx = T // CHUNK, Tc // CHUNK
    n_experts = w_router.shape[2]
    cap = EC_CAPACITY_FACTOR * T // n_experts
    n_gate = 4 * H

    pad = (-(B + 1)) % 8
    cc = jnp.concatenate([c, c_ctx[None], jnp.zeros((pad, D), F32)], axis=0)
    mod = _ada(cc, w_ada[0], b_ada[0])
    sh1, sc1, g1, sh2, sc2, g2 = [mod[:B, None, i * D:(i + 1) * D] for i in range(6)]
    csh1, csc1 = [jnp.broadcast_to(mod[B, i * D:(i + 1) * D], (B, 1, D)) for i in range(2)]

    w_in0 = w_in[0]
    wm = w_in0[:, :4 * width].astype(BF16)
    wg = jnp.pad(w_in0[:, 4 * width:4 * width + n_gate], ((0, 0), (0, LANES - n_gate))).astype(BF16)
    wr = w_in0[:, 4 * width + n_gate:].astype(BF16)
    gb = jnp.pad(mlstm_gate_bias[0], (0, LANES - n_gate)).reshape(1, LANES)
    nw1 = norm1_w[0].reshape(1, D)
    cos, sin = _rope_tables(T)
    pm, gl, pr = _inproj(x, nw1, sh1, sc1, wm, wg, wr, gb, cos, sin)
    pmc, gc, prc = _inproj(ctx, nw1, csh1, csc1, wm, wg, wr, gb,
                           jnp.ones((Tc, width), F32), jnp.zeros((Tc, width), F32))

    ym = _mlstm(pm, pmc, _gateprep(gl, gc), mlstm_norm_w[0].reshape(H, 1, HEAD_DIM))
    dl = jnp.broadcast_to(jnp.swapaxes(ret_decay_logit[0], 0, 1)[:, :, None], (H, 2, LANES))
    yr = _ret(pr, prc, dl, ret_norm_w[0].reshape(H, 1, HEAD_DIM))

    wr_hi = w_router[0].astype(BF16)
    wr_lo = (w_router[0] - wr_hi.astype(F32)).astype(BF16)
    wrt = jnp.pad(jnp.concatenate([wr_hi, wr_lo], axis=1), ((0, 0), (0, LANES - 2 * n_experts)))
    xl, h2, aff_t = _outproj(ym, yr, x, w_out[0].astype(BF16), g1, norm2_w[0].reshape(1, D), sh2, sc2,
                             wrt, n_experts)

    pos, idx, gate, starts = _route(aff_t, cap)
    by_expert = lambda a: jnp.swapaxes(a.reshape(B, n_experts, cap), 0, 1).reshape(n_experts, B * cap)
    y = _ffn(by_expert(idx), by_expert(gate)[:, None, :], h2.reshape(B * T, D // LANES, LANES),
             w_gate[0], w_up[0], w_down[0])
    pos_t = jnp.swapaxes(pos.reshape(B, n_experts, T), 1, 2)
    pos_t = jnp.where(pos_t < 0, UNSELECTED_SLOT, pos_t)
    pos_hl = jnp.concatenate([pos_t >> 4, pos_t & 15], axis=-1).astype(BF16)
    return _combine(starts[:, :, 0].reshape(-1), y.reshape(n_experts, B, cap, D), pos_hl, xl, g2,
                    final_norm_w.reshape(1, D), cap)
```

```python
import functools

import jax
import jax.numpy as jnp
from jax import lax
from jax.experimental import pallas as pl
from jax.experimental.pallas import tpu as pltpu

F32 = jnp.float32
BF16 = jnp.bfloat16

CHUNK = 128
GRID_W = 64
N_HEADS = 4
HEAD_DIM = 128
N_GATE = 4 * N_HEADS
ROPE_BASE = 10000.0
EPS = 1e-6
LOG2E = 1.4426950408889634
EC_CAPACITY_FACTOR = 2
LANES = 128
VMEM_LIMIT_BYTES = 56 * 1024 * 1024
COMBINE_WINDOW = 128
UNSELECTED_SLOT = 1023
COMBINE_GROUP = 4
MIXER_UNROLL = 16
GATE_UNROLL = 8


def _params(*sem):
    return pltpu.CompilerParams(dimension_semantics=sem, vmem_limit_bytes=VMEM_LIMIT_BYTES)


def _dot(a, b):
    return jnp.dot(a, b, preferred_element_type=F32)


def _dot_nt(a, b):
    return lax.dot_general(a, b, (((1,), (1,)), ((), ())), preferred_element_type=F32)


def _dot_tn(a, b):
    return lax.dot_general(a, b, (((0,), (0,)), ((), ())), preferred_element_type=F32)


def _sigmoid(x):
    return 1.0 / (1.0 + jnp.exp(-x))


def _log_sigmoid(x):
    return jnp.minimum(x, 0.0) - jnp.log1p(jnp.exp(-jnp.abs(x)))


def _col_to_row(col):
    return jnp.broadcast_to(col, (col.shape[0], LANES)).T[0:1, :]


def _row_to_cols(row):
    pieces = [jnp.broadcast_to(row[:, c:c + LANES], (LANES, LANES)).T for c in range(0, row.shape[1], LANES)]
    return pieces[0] if len(pieces) == 1 else jnp.concatenate(pieces, axis=0)


def _tile(n, pref):
    t = min(n, pref)
    assert n % t == 0, (n, t)
    return t


def _ada_kernel(c_ref, w_ref, b_ref, o_ref):
    cv = c_ref[...]
    s = (cv * _sigmoid(cv)).astype(BF16)
    o_ref[...] = _dot(s, w_ref[...].astype(BF16)) + b_ref[...]


def _ada(cc, w, b):
    rows, d = cc.shape
    n = w.shape[1]
    tn = _tile(n, 1024)
    return pl.pallas_call(
        _ada_kernel,
        grid=(n // tn,),
        in_specs=[pl.BlockSpec((rows, d), lambda j: (0, 0)),
                  pl.BlockSpec((d, tn), lambda j: (0, j)),
                  pl.BlockSpec((1, tn), lambda j: (0, j))],
        out_specs=pl.BlockSpec((rows, tn), lambda j: (0, j)),
        out_shape=jax.ShapeDtypeStruct((rows, n), F32),
        compiler_params=_params("arbitrary"),
    )(cc, w, b.reshape(1, n))


def _rms_mod(x, nw, sh, sc):
    ms = jnp.mean(x * x, axis=-1, keepdims=True)
    return (x * lax.rsqrt(ms + EPS) * nw) * (1.0 + sc) + sh


def _inproj_kernel(x_ref, nw_ref, sh_ref, sc_ref, wm_ref, wg_ref, wr_ref, gb_ref, cos_ref, sin_ref,
                   pm_ref, g_ref, pr_ref, *, width, k_scale, groups):
    hb = _rms_mod(x_ref[0], nw_ref[...], sh_ref[0], sc_ref[0]).astype(BF16)
    for slot, j in enumerate(groups):
        acc = _dot(hb, wm_ref[:, j * width:(j + 1) * width].astype(BF16))
        if j == 1:
            acc = acc * k_scale
        for h in range(N_HEADS):
            pm_ref[0, slot * N_HEADS + h] = acc[:, h * HEAD_DIM:(h + 1) * HEAD_DIM].astype(BF16)
    g_ref[0] = (_dot(hb, wg_ref[...].astype(BF16)) + gb_ref[...]).T[:g_ref.shape[1]]
    cos = cos_ref[...]
    sin = sin_ref[...]
    lane = lax.broadcasted_iota(jnp.int32, cos.shape, 1)
    even = ((lane // 32) % 2) == 0
    for slot, j in enumerate(groups):
        acc = _dot(hb, wr_ref[:, j * width:(j + 1) * width].astype(BF16))
        if j == 1:
            acc = acc * k_scale
        if j < 2:
            swapped = jnp.where(even, pltpu.roll(acc, width - 32, 1), pltpu.roll(acc, 32, 1))
            acc = acc * cos + swapped * sin
        for h in range(N_HEADS):
            pr_ref[0, slot * N_HEADS + h] = acc[:, h * HEAD_DIM:(h + 1) * HEAD_DIM].astype(BF16)


def _inproj(x, nw, sh, sc, w_in, wr, gb, cos, sin, groups):
    b, t, d = x.shape
    width = wr.shape[1] // 4
    tm = _tile(t, 512)
    slabs = len(groups) * N_HEADS
    kern = functools.partial(_inproj_kernel, width=width, k_scale=HEAD_DIM ** -0.5, groups=groups)
    const = lambda i, j: (0, 0)
    return pl.pallas_call(
        kern,
        grid=(b, t // tm),
        in_specs=[pl.BlockSpec((1, tm, d), lambda i, j: (i, j, 0)),
                  pl.BlockSpec((1, d), const),
                  pl.BlockSpec((1, 1, d), lambda i, j: (i, 0, 0)),
                  pl.BlockSpec((1, 1, d), lambda i, j: (i, 0, 0)),
                  pl.BlockSpec((d, 4 * width), const),
                  pl.BlockSpec((d, LANES), lambda i, j: (0, 4 * width // LANES)),
                  pl.BlockSpec(wr.shape, const),
                  pl.BlockSpec((1, LANES), const),
                  pl.BlockSpec((tm, width), lambda i, j: (j, 0)),
                  pl.BlockSpec((tm, width), lambda i, j: (j, 0))],
        out_specs=[pl.BlockSpec((1, slabs, tm, HEAD_DIM), lambda i, j: (i, 0, j, 0)),
                   pl.BlockSpec((1, N_GATE, tm), lambda i, j: (i, 0, j)),
                   pl.BlockSpec((1, slabs, tm, HEAD_DIM), lambda i, j: (i, 0, j, 0))],
        out_shape=[jax.ShapeDtypeStruct((b, slabs, t, HEAD_DIM), BF16),
                   jax.ShapeDtypeStruct((b, N_GATE, t), F32),
                   jax.ShapeDtypeStruct((b, slabs, t, HEAD_DIM), BF16)],
        compiler_params=_params("arbitrary", "arbitrary"),
    )(x, nw, sh, sc, w_in, w_in, wr, gb, cos, sin)


def _scan_lanes(x, op, fill, reverse):
    lane = lax.broadcasted_iota(jnp.int32, x.shape, 1)
    sh = 1
    while sh < LANES:
        if reverse:
            x = op(x, jnp.where(lane < LANES - sh, pltpu.roll(x, LANES - sh, 1), fill))
        else:
            x = op(x, jnp.where(lane >= sh, pltpu.roll(x, sh, 1), fill))
        sh *= 2
    return x


def _gateprep_kernel(gl_ref, gc_ref, o_ref, *, n_lat, n_ctx):
    n_b = gl_ref.shape[0]
    half = N_GATE // 2
    rows = n_b * half
    row = lax.broadcasted_iota(jnp.int32, (rows, LANES), 0)
    lane = lax.broadcasted_iota(jnp.int32, (rows, LANES), 1)
    is_fwd = (row % half) < N_HEADS
    last = lane == jnp.where(is_fwd, LANES - 1, 0)

    def chunk_rows(ref, lo, c_fwd, c_bwd):
        def at(c):
            cols = (slice(c * LANES, (c + 1) * LANES) if isinstance(c, int)
                    else pl.ds(pl.multiple_of(c * LANES, LANES), LANES))
            return jnp.concatenate([ref[s, lo:lo + half, cols] for s in range(n_b)], axis=0)
        return jnp.where(is_fwd, at(c_fwd), at(c_bwd))

    def step(c, ig, fg, m):
        lf = _log_sigmoid(fg)
        b = jnp.where(is_fwd, _scan_lanes(lf, jnp.add, 0.0, False), _scan_lanes(lf, jnp.add, 0.0, True))
        b_tot = jnp.sum(jnp.where(last, b, 0.0), axis=1, keepdims=True)
        a = ig - b
        m_new = jnp.maximum(b_tot + m, jnp.max(b_tot + a, axis=1, keepdims=True))
        w = jnp.exp(b_tot + a - m_new)
        decay = jnp.exp(b_tot + m - m_new)
        run = jnp.where(is_fwd, _scan_lanes(a, jnp.maximum, -jnp.inf, False),
                        _scan_lanes(a, jnp.maximum, -jnp.inf, True))
        mm = jnp.maximum(m, run)
        o_ref[c, 0] = a * LOG2E
        o_ref[c, 1] = w
        o_ref[c, 2] = mm * LOG2E
        o_ref[c, 3] = jnp.exp(-(b + mm))
        o_ref[c, 4] = jnp.broadcast_to(decay, (rows, LANES))
        o_ref[c, 5] = jnp.broadcast_to(m * LOG2E, (rows, LANES))
        return m_new

    m = jnp.zeros((rows, 1), F32)
    for c in range(n_ctx):
        m = step(c, chunk_rows(gc_ref, 0, c, n_ctx - 1 - c), chunk_rows(gc_ref, half, c, n_ctx - 1 - c), m)

    def latent(i, m):
        return step(n_ctx + i, chunk_rows(gl_ref, 0, i, n_lat - 1 - i),
                    chunk_rows(gl_ref, half, i, n_lat - 1 - i), m)

    lax.fori_loop(0, n_lat, latent, m, unroll=min(n_lat, GATE_UNROLL))


def _gateprep(gl, gc):
    n_lat, n_ctx = gl.shape[2] // CHUNK, gc.shape[2] // CHUNK
    return pl.pallas_call(
        functools.partial(_gateprep_kernel, n_lat=n_lat, n_ctx=n_ctx),
        out_shape=jax.ShapeDtypeStruct((n_ctx + n_lat, 6, gl.shape[0] * N_GATE // 2, LANES), F32),
        compiler_params=pltpu.CompilerParams(vmem_limit_bytes=VMEM_LIMIT_BYTES),
    )(gl, gc)


def _head_norm_gate(h, gate, nw):
    return gate * (h * lax.rsqrt(jnp.mean(h * h, axis=-1, keepdims=True) + EPS)) * nw


def _mlstm_kernel(q_ref, k_ref, v_ref, o_ref, kc_ref, vc_ref, st_ref, nw_ref, y_ref,
                  sf_ref, sb_ref, cf_ref, cb_ref, *, n_lat, n_ctx, unroll):
    L = CHUNK
    A, WT, MM, EN, DECAY, M0 = range(6)
    ones = jnp.ones((L, HEAD_DIM), BF16)
    stream_f = pl.program_id(0) * (N_GATE // 2) + pl.program_id(1)
    stream_b = stream_f + N_HEADS

    def stat(step, plane, stream):
        return st_ref[step, plane, pl.ds(stream, 1), :]

    def to_rows(row):
        return jnp.broadcast_to(row, (L, L)).T

    def update(state_ref, k, v, step, stream):
        wkt = (k.astype(F32).T * stat(step, WT, stream)).astype(BF16)
        u = _dot(wkt, jnp.concatenate([v, ones], axis=1))
        decay = stat(step, DECAY, stream)
        state_ref[...] = jnp.concatenate([decay, decay], axis=1) * state_ref[...] + u

    cf_ref[...] = jnp.zeros_like(cf_ref)
    cb_ref[...] = jnp.zeros_like(cb_ref)
    for c in range(n_ctx):
        rf = slice(c * L, (c + 1) * L)
        rb = slice((n_ctx - 1 - c) * L, (n_ctx - c) * L)
        update(cf_ref, kc_ref[0, rf, :], vc_ref[0, rf, :], c, stream_f)
        update(cb_ref, kc_ref[0, rb, :], vc_ref[0, rb, :], c, stream_b)

    def state_body(i, carry):
        jb = n_lat - 1 - i
        rf = pl.ds(pl.multiple_of(i * L, L), L)
        rb = pl.ds(pl.multiple_of(jb * L, L), L)
        sf_ref[i] = cf_ref[...].astype(BF16)
        sb_ref[jb] = cb_ref[...].astype(BF16)
        update(cf_ref, k_ref[0, rf, :], v_ref[0, rf, :], n_ctx + i, stream_f)
        update(cb_ref, k_ref[0, rb, :], v_ref[0, rb, :], n_ctx + i, stream_b)
        return carry

    lax.fori_loop(0, n_lat, state_body, 0, unroll=unroll)

    r_i = lax.broadcasted_iota(jnp.int32, (L, L), 0)
    c_i = lax.broadcasted_iota(jnp.int32, (L, L), 1)
    tril = c_i <= r_i
    triu = c_i >= r_i
    nw = nw_ref[0]

    def out_body(j, carry):
        rows = pl.ds(pl.multiple_of(j * L, L), L)
        step_f = n_ctx + j
        step_b = n_ctx + n_lat - 1 - j
        q = q_ref[0, rows, :]
        k = k_ref[0, rows, :]
        v = v_ref[0, rows, :]
        mm_f = to_rows(stat(step_f, MM, stream_f))
        mm_b = to_rows(stat(step_b, MM, stream_b))
        s = _dot_nt(q, k)
        d_f = jnp.where(tril, jnp.exp2(stat(step_f, A, stream_f) - mm_f), 0.0)
        d_b = jnp.where(triu, jnp.exp2(stat(step_b, A, stream_b) - mm_b), 0.0)
        iw_f = jnp.exp2(stat(step_f, M0, stream_f) - mm_f)
        iw_b = jnp.exp2(stat(step_b, M0, stream_b) - mm_b)
        qf = q.astype(F32)
        vext = jnp.concatenate([v, ones], axis=1)
        lhs_f = jnp.concatenate([(s * d_f).astype(BF16), (qf * iw_f).astype(BF16)], axis=1)
        lhs_b = jnp.concatenate([(s * d_b).astype(BF16), (qf * iw_b).astype(BF16)], axis=1)
        tot_f = _dot(lhs_f, jnp.concatenate([vext, sf_ref[j]], axis=0))
        tot_b = _dot(lhs_b, jnp.concatenate([vext, sb_ref[j]], axis=0))
        D = HEAD_DIM
        h = (tot_f[:, :D] / jnp.maximum(jnp.abs(tot_f[:, D:]), to_rows(stat(step_f, EN, stream_f)))
             + tot_b[:, :D] / jnp.maximum(jnp.abs(tot_b[:, D:]), to_rows(stat(step_b, EN, stream_b))))
        gate = _sigmoid(o_ref[0, rows, :].astype(F32))
        y_ref[0, rows, :] = _head_norm_gate(h, gate, nw).astype(BF16)
        return carry

    lax.fori_loop(0, n_lat, out_body, 0, unroll=unroll)


def _head_slab(t, group):
    return pl.BlockSpec((None, 1, t, HEAD_DIM), lambda i, h: (i, group * N_HEADS + h, 0, 0))


def _mlstm(pm, pmc, stats, nw):
    b, _, t, _ = pm.shape
    tc = pmc.shape[2]
    n_lat, n_ctx = t // CHUNK, tc // CHUNK
    H = N_HEADS
    blk = lambda off: _head_slab(t, off // H)
    cblk = lambda off: _head_slab(tc, off // H - 1)
    return pl.pallas_call(
        functools.partial(_mlstm_kernel, n_lat=n_lat, n_ctx=n_ctx, unroll=min(n_lat, MIXER_UNROLL)),
        grid=(b, H),
        in_specs=[blk(0), blk(H), blk(2 * H), blk(3 * H), cblk(H), cblk(2 * H),
                  pl.BlockSpec(stats.shape, lambda i, h: (0, 0, 0, 0)),
                  pl.BlockSpec((1, 1, HEAD_DIM), lambda i, h: (h, 0, 0))],
        out_specs=_head_slab(t, 0),
        out_shape=jax.ShapeDtypeStruct((b, H, t, HEAD_DIM), BF16),
        scratch_shapes=[pltpu.VMEM((n_lat, HEAD_DIM, HEAD_DIM + LANES), BF16),
                        pltpu.VMEM((n_lat, HEAD_DIM, HEAD_DIM + LANES), BF16),
                        pltpu.VMEM((HEAD_DIM, HEAD_DIM + LANES), F32),
                        pltpu.VMEM((HEAD_DIM, HEAD_DIM + LANES), F32)],
        compiler_params=_params("arbitrary", "arbitrary"),
    )(pm, pm, pm, pm, pmc, pmc, stats, nw)


def _ret_kernel(q_ref, k_ref, v_ref, o_ref, kc_ref, vc_ref, dl_ref, nw_ref, y_ref,
                sf_ref, sb_ref, rf_ref, rb_ref, *, n_lat, n_ctx, unroll):
    L = CHUNK
    lg = _log_sigmoid(dl_ref[0])
    lg_f, lg_b = lg[0:1, :], lg[1:2, :]
    r_i = lax.broadcasted_iota(jnp.int32, (L, L), 0)
    c_i = lax.broadcasted_iota(jnp.int32, (L, L), 1)
    r_f = r_i.astype(F32)
    rel = (r_i - c_i).astype(F32)
    kw_f = jnp.exp((L - 1.0 - r_f) * lg_f)
    kw_b = jnp.exp(r_f * lg_b)
    in_f = jnp.exp((r_f + 1.0) * lg_f)
    in_b = jnp.exp((L - r_f) * lg_b)
    dch_f = jnp.exp(L * lg_f)
    dch_b = jnp.exp(L * lg_b)
    d_mat = (jnp.where(rel >= 0, jnp.exp(jnp.maximum(rel, 0.0) * lg_f), 0.0)
             + jnp.where(rel <= 0, jnp.exp(jnp.maximum(-rel, 0.0) * lg_b), 0.0))

    def update(state_ref, k, v, kw, dch):
        wk = (k.astype(F32) * kw).astype(BF16)
        state_ref[...] = dch * state_ref[...] + _dot_tn(wk, v)

    rf_ref[...] = jnp.zeros_like(rf_ref)
    rb_ref[...] = jnp.zeros_like(rb_ref)
    for c in range(n_ctx):
        rows = slice(c * L, (c + 1) * L)
        update(rf_ref, kc_ref[0, rows, :], vc_ref[0, rows, :], kw_f, dch_f)
    for c in reversed(range(n_ctx)):
        rows = slice(c * L, (c + 1) * L)
        update(rb_ref, kc_ref[0, rows, :], vc_ref[0, rows, :], kw_b, dch_b)

    def state_body(i, carry):
        jb = n_lat - 1 - i
        rowf = pl.ds(pl.multiple_of(i * L, L), L)
        rowb = pl.ds(pl.multiple_of(jb * L, L), L)
        sf_ref[i] = rf_ref[...].astype(BF16)
        sb_ref[jb] = rb_ref[...].astype(BF16)
        update(rf_ref, k_ref[0, rowf, :], v_ref[0, rowf, :], kw_f, dch_f)
        update(rb_ref, k_ref[0, rowb, :], v_ref[0, rowb, :], kw_b, dch_b)
        return carry

    lax.fori_loop(0, n_lat, state_body, 0, unroll=unroll)
    nw = nw_ref[0]

    def out_body(j, carry):
        rows = pl.ds(pl.multiple_of(j * L, L), L)
        q = q_ref[0, rows, :]
        s = _dot_nt(q, k_ref[0, rows, :])
        intra = _dot((s * d_mat).astype(BF16), v_ref[0, rows, :])
        inter = _dot(q, jnp.concatenate([sf_ref[j], sb_ref[j]], axis=1))
        out = intra + in_f * inter[:, :HEAD_DIM] + in_b * inter[:, HEAD_DIM:]
        og = o_ref[0, rows, :].astype(F32)
        y_ref[0, rows, :] = _head_norm_gate(out, og * _sigmoid(og), nw).astype(BF16)
        return carry

    lax.fori_loop(0, n_lat, out_body, 0, unroll=unroll)


def _ret(pr, prc, dl, nw):
    b, _, t, _ = pr.shape
    tc = prc.shape[2]
    n_lat, n_ctx = t // CHUNK, tc // CHUNK
    H = N_HEADS
    blk = lambda off: _head_slab(t, off // H)
    cblk = lambda off: _head_slab(tc, off // H - 1)
    return pl.pallas_call(
        functools.partial(_ret_kernel, n_lat=n_lat, n_ctx=n_ctx, unroll=min(n_lat, MIXER_UNROLL)),
        grid=(b, H),
        in_specs=[blk(0), blk(H), blk(2 * H), blk(3 * H), cblk(H), cblk(2 * H),
                  pl.BlockSpec((1, 2, LANES), lambda i, h: (h, 0, 0)),
                  pl.BlockSpec((1, 1, HEAD_DIM), lambda i, h: (h, 0, 0))],
        out_specs=_head_slab(t, 0),
        out_shape=jax.ShapeDtypeStruct((b, H, t, HEAD_DIM), BF16),
        scratch_shapes=[pltpu.VMEM((n_lat, HEAD_DIM, HEAD_DIM), BF16),
                        pltpu.VMEM((n_lat, HEAD_DIM, HEAD_DIM), BF16),
                        pltpu.VMEM((HEAD_DIM, HEAD_DIM), F32),
                        pltpu.VMEM((HEAD_DIM, HEAD_DIM), F32)],
        compiler_params=_params("arbitrary", "arbitrary"),
    )(pr, pr, pr, pr, prc, prc, dl, nw)


def _outproj_kernel(ym_ref, yr_ref, x_ref, wo_ref, g1_ref, nw_ref, sh_ref, sc_ref, wrt_ref,
                    xl_ref, h2_ref, aff_ref, *, n_experts):
    heads = [ym_ref[0, h] for h in range(N_HEADS)] + [yr_ref[0, h] for h in range(N_HEADS)]
    y = _dot(jnp.concatenate(heads, axis=1), wo_ref[...])
    xl = x_ref[0] + g1_ref[0] * y
    xl_ref[0] = xl
    h2 = _rms_mod(xl, nw_ref[...], sh_ref[0], sc_ref[0])
    h2_ref[0] = h2.reshape(h2.shape[0], h2.shape[1] // LANES, LANES)
    h_hi = h2.astype(BF16)
    h_lo = (h2 - h_hi.astype(F32)).astype(BF16)
    p_hi = _dot(h_hi, wrt_ref[...])
    p_lo = _dot(h_lo, wrt_ref[...])
    logits = p_hi + pltpu.roll(p_hi, LANES - n_experts, 1) + p_lo
    lane = lax.broadcasted_iota(jnp.int32, logits.shape, 1)
    logits = jnp.where(lane < n_experts, logits, -jnp.inf)
    e = jnp.exp(logits - jnp.max(logits, axis=-1, keepdims=True))
    aff_t = (e / jnp.sum(e, axis=-1, keepdims=True)).T[:n_experts]
    aff_ref[0] = aff_t.reshape(n_experts, aff_t.shape[1] // LANES, LANES)


def _outproj(ym, yr, x, wo, g1, nw, sh, sc, wrt, n_experts):
    b, t, d = x.shape
    tm = _tile(t, 8 * LANES)
    slab = pl.BlockSpec((1, N_HEADS, tm, HEAD_DIM), lambda i, j: (i, 0, j, 0))
    const = lambda i, j: (0, 0)
    per_b = lambda i, j: (i, 0, 0)
    tok = lambda i, j: (i, j, 0)
    return pl.pallas_call(
        functools.partial(_outproj_kernel, n_experts=n_experts),
        grid=(b, t // tm),
        in_specs=[slab, slab,
                  pl.BlockSpec((1, tm, d), tok), pl.BlockSpec(wo.shape, const),
                  pl.BlockSpec((1, 1, d), per_b), pl.BlockSpec((1, d), const),
                  pl.BlockSpec((1, 1, d), per_b), pl.BlockSpec((1, 1, d), per_b),
                  pl.BlockSpec(wrt.shape, const)],
        out_specs=[pl.BlockSpec((1, tm, d), tok),
                   pl.BlockSpec((1, tm, d // LANES, LANES), lambda i, j: (i, j, 0, 0)),
                   pl.BlockSpec((1, n_experts, tm // LANES, LANES), lambda i, j: (i, 0, j, 0))],
        out_shape=[jax.ShapeDtypeStruct((b, t, d), F32),
                   jax.ShapeDtypeStruct((b, t, d // LANES, LANES), F32),
                   jax.ShapeDtypeStruct((b, n_experts, t // LANES, LANES), F32)],
        compiler_params=_params("arbitrary", "arbitrary"),
    )(ym, yr, x, wo, g1, nw, sh, sc, wrt)


def _select_top(a, cap):
    n_e, n_b, _ = a.shape

    ones = jnp.ones((LANES, LANES), BF16)

    def count(mask):
        per_block = _dot(mask.astype(F32).astype(BF16).reshape(n_e * n_b, LANES), ones)
        return jnp.sum(per_block.reshape(n_e, n_b, LANES), axis=1, keepdims=True).astype(jnp.int32)

    def search(i, thr):
        cand = thr | jnp.left_shift(jnp.int32(1), 30 - i)
        return jnp.where(count(a >= pltpu.bitcast(cand, F32)) >= cap, cand, thr)

    thr = lax.fori_loop(0, 31, search, jnp.zeros((n_e, 1, LANES), jnp.int32))
    above = a >= pltpu.bitcast(thr + 1, F32)
    band = (a >= pltpu.bitcast(thr, F32)) & jnp.logical_not(above)
    need = cap - count(above)[:, :, 0:1]
    tok = (lax.broadcasted_iota(jnp.int32, a.shape, 1) * LANES
           + lax.broadcasted_iota(jnp.int32, a.shape, 2))

    def take_one(_, carry):
        sel, band, need = carry
        in_band = band > 0
        best = jnp.max(jnp.max(jnp.where(in_band, a, -1.0), axis=2, keepdims=True), axis=1, keepdims=True)
        cand = jnp.where(in_band & (a == best), tok, n_b * LANES)
        first = jnp.min(jnp.min(cand, axis=2, keepdims=True), axis=1, keepdims=True)
        take = ((tok == first) & (need > 0)).astype(jnp.int32)
        return sel | take, band - take, need - 1

    sel, _, _ = lax.fori_loop(0, jnp.max(need), take_one,
                              (above.astype(jnp.int32), band.astype(jnp.int32), need))
    return sel


def _route_kernel(aff_ref, pos_ref, idx_ref, gate_ref, start_ref, sel_ref, *, cap):
    i = pl.program_id(0)
    n_s, n_e, n_b, _ = aff_ref.shape

    @pl.when(i == 0)
    def _():
        sel_all = _select_top(aff_ref[...].reshape(n_s * n_e, n_b, LANES), cap)
        sel_ref[...] = sel_all.reshape(n_s, n_e, n_b, LANES)

    a = aff_ref[i]
    sel = sel_ref[i] > 0

    rows = n_e * n_b
    u_i = lax.broadcasted_iota(jnp.int32, (LANES, LANES), 0)
    s_i = lax.broadcasted_iota(jnp.int32, (LANES, LANES), 1)
    incl = (u_i <= s_i).astype(BF16)
    ones = jnp.ones((LANES, LANES), BF16)
    r_i = lax.broadcasted_iota(jnp.int32, (rows, rows), 0)
    c_i = lax.broadcasted_iota(jnp.int32, (rows, rows), 1)
    before = ((r_i // n_b == c_i // n_b) & (c_i < r_i)).astype(BF16)

    x = sel.astype(F32).reshape(rows, LANES)
    xb = x.astype(BF16)
    block_tot = _dot(xb, ones).astype(BF16)
    start = _dot(before, block_tot)
    pos = jnp.where(x > 0, start + _dot(xb, incl) - x, -1.0).astype(jnp.int32)
    pos_ref[0] = pos.reshape(n_e, n_b, LANES)
    start_ref[0] = start.astype(jnp.int32)

    pad = (-rows) % LANES
    def pad_rows(m):
        return m if pad == 0 else jnp.concatenate([m, jnp.zeros((pad, LANES), m.dtype)], axis=0)
    start_p = pad_rows(start)
    end_p = pad_rows(start + block_tot.astype(F32))
    pos_p = pad_rows(pos)
    pos_p = jnp.where(pos_p < 0, 1023, pos_p)
    a_p = pad_rows(a.reshape(rows, LANES))
    a_hi = a_p.astype(BF16)
    a_mid = (a_p - a_hi.astype(F32)).astype(BF16)
    a_lo = (a_p - a_hi.astype(F32) - a_mid.astype(F32)).astype(BF16)
    per_group = LANES // n_b
    j_col = lax.broadcasted_iota(jnp.int32, (cap, LANES), 0).astype(F32)
    lane = lax.broadcasted_iota(jnp.int32, (cap, LANES), 1)
    lane_f = lane.astype(F32)
    block_f = (lane % n_b).astype(F32)
    row0 = pl.program_id(0) * (n_b * LANES)
    for g in range((rows + pad) // LANES):
        rs = slice(g * LANES, (g + 1) * LANES)
        s_row = start_p[rs].T[0:1, :]
        e_row = end_p[rs].T[0:1, :]
        hi = (pos_p[rs] >> 4).astype(F32).astype(BF16)
        lo = (pos_p[rs] & 15).astype(F32).astype(BF16)
        for el in range(per_group):
            e = g * per_group + el
            if e >= n_e:
                break
            mine = (j_col >= s_row) & (j_col < e_row) & ((lane // n_b) == el)
            mine_b = mine.astype(F32).astype(BF16)
            slots = 16.0 * _dot(mine_b, hi) + _dot(mine_b, lo)
            hit = slots == j_col
            in_block = jnp.sum(jnp.where(hit, lane_f, 0.0), axis=1, keepdims=True)
            block = jnp.sum(jnp.where(mine, block_f, 0.0), axis=1, keepdims=True)
            a_blk = _dot(mine_b, a_hi[rs]) + _dot(mine_b, a_mid[rs]) + _dot(mine_b, a_lo[rs])
            gate = jnp.sum(jnp.where(hit, a_blk, 0.0), axis=1, keepdims=True)
            idx_ref[0, e] = _col_to_row(block * LANES + in_block).astype(jnp.int32) + row0
            gate_ref[0, e] = _col_to_row(gate)


def _route(aff_t, cap):
    b, n_e, n_b, _ = aff_t.shape
    spec = pl.BlockSpec((1, n_e, n_b, LANES), lambda i: (i, 0, 0, 0))
    slot_spec = pl.BlockSpec((1, n_e, 1, cap), lambda i: (i, 0, 0, 0))
    return pl.pallas_call(
        functools.partial(_route_kernel, cap=cap),
        grid=(b,), in_specs=[pl.BlockSpec(aff_t.shape, lambda i: (0, 0, 0, 0))],
        out_specs=[spec, slot_spec, slot_spec, pl.BlockSpec((1, n_e * n_b, LANES), lambda i: (i, 0, 0))],
        scratch_shapes=[pltpu.VMEM(aff_t.shape, jnp.int32)],
        out_shape=[jax.ShapeDtypeStruct(aff_t.shape, jnp.int32),
                   jax.ShapeDtypeStruct((b, n_e, 1, cap), jnp.int32),
                   jax.ShapeDtypeStruct((b, n_e, 1, cap), F32),
                   jax.ShapeDtypeStruct((b, n_e * n_b, LANES), jnp.int32)],
        compiler_params=_params("arbitrary"),
    )(aff_t)


GATHER_DMA_PRIORITY = 1


def _ffn_kernel(idx_ref, h_hbm, gate_ref, wg_ref, wu_ref, wd_ref, y_ref, stage_ref, x_ref, acc_ref, sem,
                *, m, per_step):
    e, f = pl.program_id(0), pl.program_id(1)
    n_e, n_f = pl.num_programs(0), pl.num_programs(1)
    rows_per_expert = stage_ref.shape[0]

    def row_copy(base, j):
        return pltpu.make_async_copy(h_hbm.at[pl.ds(idx_ref[base + j], 1)],
                                     stage_ref.at[pl.ds(j, 1)], sem.at[0])

    def wait_rows():
        pltpu.make_async_copy(h_hbm.at[pl.ds(0, rows_per_expert)], stage_ref, sem.at[0]).wait()

    @pl.when((e == 0) & (f == 0))
    def _():
        def start_row(j, carry):
            row_copy(0, j).start(priority=GATHER_DMA_PRIORITY)
            return carry
        lax.fori_loop(0, rows_per_expert, start_row, 0)

    @pl.when(f == 0)
    def _():
        wait_rows()
        x_ref[...] = stage_ref[:m].reshape(x_ref.shape).astype(BF16)
        acc_ref[...] = jnp.zeros_like(acc_ref)

    nxt = jnp.minimum(e + 1, n_e - 1) * rows_per_expert
    for i in range(per_step):
        row_copy(nxt, f * per_step + i).start(priority=GATHER_DMA_PRIORITY)

    x = x_ref[...]
    a = _dot(x, wg_ref[0].astype(BF16))
    u = _dot(x, wu_ref[0].astype(BF16))
    mid = (a * _sigmoid(a) * u).astype(BF16)
    acc_ref[...] += _dot(mid, wd_ref[0].astype(BF16))

    @pl.when(f == n_f - 1)
    def _():
        gate = _row_to_cols(gate_ref[0])
        for c in range(0, y_ref.shape[2], LANES):
            y_ref[0, :, c:c + LANES] = (acc_ref[:, c:c + LANES] * gate).astype(BF16)

    @pl.when((e == n_e - 1) & (f == n_f - 1))
    def _():
        wait_rows()


def _ffn(idx, gate, h_rows, wg, wu, wd):
    n_e, m = idx.shape
    d = h_rows.shape[1] * LANES
    ff = wg.shape[2]
    tf = _tile(ff, 256)
    n_f = ff // tf
    per_step = -(-m // (8 * n_f)) * 8
    idx = jnp.pad(idx, ((0, 0), (0, n_f * per_step - m))).reshape(-1)
    return pl.pallas_call(
        functools.partial(_ffn_kernel, m=m, per_step=per_step),
        grid_spec=pltpu.PrefetchScalarGridSpec(
            num_scalar_prefetch=1,
            grid=(n_e, n_f),
            in_specs=[pl.BlockSpec(memory_space=pl.ANY),
                      pl.BlockSpec((1, 1, m), lambda e, f, idx: (e, 0, 0)),
                      pl.BlockSpec((1, d, tf), lambda e, f, idx: (e, 0, f)),
                      pl.BlockSpec((1, d, tf), lambda e, f, idx: (e, 0, f)),
                      pl.BlockSpec((1, tf, d), lambda e, f, idx: (e, f, 0))],
            out_specs=pl.BlockSpec((1, m, d), lambda e, f, idx: (e, 0, 0)),
            scratch_shapes=[pltpu.VMEM((n_f * per_step, d // LANES, LANES), F32),
                            pltpu.VMEM((m, d), BF16),
                            pltpu.VMEM((m, d), F32),
                            pltpu.SemaphoreType.DMA((1,))]),
        out_shape=jax.ShapeDtypeStruct((n_e, m, d), BF16),
        compiler_params=_params("arbitrary", "arbitrary"),
    )(idx, h_rows, gate, wg, wu, wd)


def _combine_kernel(start_ref, y_ref, pos_ref, xl_ref, g2_ref, fw_ref, o_ref, yc_ref, acc_ref,
                    *, n_experts, cap, n_b, win):
    i, j = pl.program_id(0), pl.program_id(1)
    tt = xl_ref.shape[1]
    blocks_per_tile = tt // LANES
    lane = lax.broadcasted_iota(jnp.int32, (tt, win), 1).astype(F32)
    r_i = lax.broadcasted_iota(jnp.int32, (2 * n_experts, n_experts * win), 0)
    c_e = lax.broadcasted_iota(jnp.int32, (2 * n_experts, n_experts * win), 1) // win
    spread = jnp.where(r_i == c_e, 16.0, jnp.where(r_i == c_e + n_experts, 1.0, 0.0)).astype(BF16)
    slots = _dot(pos_ref[0], spread)
    last_tile = j == pl.num_programs(1) - 1
    first, n_pass = [], 0
    for e in range(n_experts):
        row = (i * n_experts + e) * n_b
        lo = start_ref[row + j * blocks_per_tile]
        hi = jnp.where(last_tile, cap, start_ref[row + jnp.minimum((j + 1) * blocks_per_tile, n_b - 1)])
        w0 = jnp.minimum((lo // 16) * 16, cap - win)
        first.append(w0)
        n_pass = jnp.maximum(n_pass, jnp.where(hi > lo, (hi - w0 + win - 1) // win, 0))
    group = min(n_experts, COMBINE_GROUP)

    def add_pass(p, acc):
        for g0 in range(0, n_experts, group):
            hits = []
            for e in range(g0, g0 + group):
                done = first[e] + p * win
                off = pl.multiple_of(jnp.minimum(done, cap - win), 16)
                yc_ref[e * win:(e + 1) * win, :] = y_ref[e, 0, pl.ds(off, win), :]
                pe = slots[:, e * win:(e + 1) * win]
                hits.append(((pe - off.astype(F32)) == lane) & (pe >= done.astype(F32)))
            onehot = jnp.concatenate(hits, axis=1).astype(F32).astype(BF16)
            acc = acc + _dot(onehot, yc_ref[g0 * win:(g0 + group) * win, :])
        return acc

    acc_ref[...] = add_pass(0, jnp.zeros(acc_ref.shape, F32))

    def extra_pass(p, carry):
        acc_ref[...] = add_pass(p, acc_ref[...])
        return carry

    lax.fori_loop(1, n_pass, extra_pass, 0)
    xl = xl_ref[0] + g2_ref[0] * acc_ref[...]
    ms = jnp.mean(xl * xl, axis=-1, keepdims=True)
    o_ref[0] = xl * lax.rsqrt(ms + EPS) * fw_ref[...]


def _combine(starts, y, pos_col, xl, g2, fw, cap):
    b, t, d = xl.shape
    n_e = y.shape[0]
    tt = _tile(t, 512)
    win = min(cap, COMBINE_WINDOW)
    assert cap % 16 == 0 and win % 16 == 0 and tt % LANES == 0 and cap + win <= UNSELECTED_SLOT
    tok = lambda i, j, st: (i, j, 0)
    return pl.pallas_call(
        functools.partial(_combine_kernel, n_experts=n_e, cap=cap, n_b=t // LANES, win=win),
        grid_spec=pltpu.PrefetchScalarGridSpec(
            num_scalar_prefetch=1,
            grid=(b, t // tt),
            in_specs=[pl.BlockSpec((n_e, 1, cap, d), lambda i, j, st: (0, i, 0, 0)),
                      pl.BlockSpec((1, tt, 2 * n_e), tok),
                      pl.BlockSpec((1, tt, d), tok),
                      pl.BlockSpec((1, 1, d), lambda i, j, st: (i, 0, 0)),
                      pl.BlockSpec((1, d), lambda i, j, st: (0, 0))],
            out_specs=pl.BlockSpec((1, tt, d), tok),
            scratch_shapes=[pltpu.VMEM((n_e * win, d), BF16), pltpu.VMEM((tt, d), F32)]),
        out_shape=jax.ShapeDtypeStruct((b, t, d), F32),
        compiler_params=_params("arbitrary", "arbitrary"),
    )(starts, y, pos_col, xl, g2, fw)


def _rope_tables(t):
    n_freq = HEAD_DIM // 4
    rows = jnp.repeat(jnp.arange(t // GRID_W, dtype=F32), GRID_W)
    cols = jnp.tile(jnp.arange(GRID_W, dtype=F32), t // GRID_W)
    inv_freq = ROPE_BASE ** (-jnp.arange(n_freq, dtype=F32) / n_freq)
    ang_r, ang_c = rows[:, None] * inv_freq, cols[:, None] * inv_freq
    cos = jnp.concatenate([jnp.cos(ang_r)] * 2 + [jnp.cos(ang_c)] * 2, axis=-1)
    sin = jnp.concatenate([-jnp.sin(ang_r), jnp.sin(ang_r), -jnp.sin(ang_c), jnp.sin(ang_c)], axis=-1)
    return jnp.tile(cos, (1, N_HEADS)), jnp.tile(sin, (1, N_HEADS))


def kernel(x, c, ctx, c_ctx, w_ada, b_ada, norm1_w, norm2_w, w_in, mlstm_gate_bias, ret_decay_logit,
           mlstm_norm_w, ret_norm_w, w_out, w_router, w_gate, w_up, w_down, final_norm_w):
    B, T, D = x.shape
    Tc = ctx.shape[1]
    H = N_HEADS
    width = H * HEAD_DIM
    assert D == 2 * width and w_ada.shape[0] == 1
    assert T % CHUNK == 0 and Tc % CHUNK == 0 and T % GRID_W == 0
    n_lat, n_ctx = T // CHUNK, Tc // CHUNK
    n_experts = w_router.shape[2]
    cap = EC_CAPACITY_FACTOR * T // n_experts
    n_gate = 4 * H

    pad = (-(B + 1)) % 8
    cc = jnp.concatenate([c, c_ctx[None], jnp.zeros((pad, D), F32)], axis=0)
    mod = _ada(cc, w_ada[0], b_ada[0])
    sh1, sc1, g1, sh2, sc2, g2 = [mod[:B, None, i * D:(i + 1) * D] for i in range(6)]
    csh1, csc1 = [jnp.broadcast_to(mod[B, i * D:(i + 1) * D], (B, 1, D)) for i in range(2)]

    w_in0 = w_in[0]
    assert (4 * width) % LANES == 0 and w_in0.shape[1] >= 4 * width + LANES
    wr = w_in0[:, 4 * width + n_gate:]
    gb = jnp.pad(mlstm_gate_bias[0], (0, LANES - n_gate)).reshape(1, LANES)
    nw1 = norm1_w[0].reshape(1, D)
    cos, sin = _rope_tables(T)
    pm, gl, pr = _inproj(x, nw1, sh1, sc1, w_in0, wr, gb, cos, sin, groups=(0, 1, 2, 3))
    pmc, gc, prc = _inproj(ctx, nw1, csh1, csc1, w_in0, wr, gb,
                           jnp.ones((Tc, width), F32), jnp.zeros((Tc, width), F32), groups=(1, 2))

    ym = _mlstm(pm, pmc, _gateprep(gl, gc), mlstm_norm_w[0].reshape(H, 1, HEAD_DIM))
    dl = jnp.broadcast_to(jnp.swapaxes(ret_decay_logit[0], 0, 1)[:, :, None], (H, 2, LANES))
    yr = _ret(pr, prc, dl, ret_norm_w[0].reshape(H, 1, HEAD_DIM))

    wr_hi = w_router[0].astype(BF16)
    wr_lo = (w_router[0] - wr_hi.astype(F32)).astype(BF16)
    wrt = jnp.pad(jnp.concatenate([wr_hi, wr_lo], axis=1), ((0, 0), (0, LANES - 2 * n_experts)))
    xl, h2, aff_t = _outproj(ym, yr, x, w_out[0].astype(BF16), g1, norm2_w[0].reshape(1, D), sh2, sc2,
                             wrt, n_experts)

    pos, idx, gate, starts = _route(aff_t, cap)
    by_expert = lambda a: jnp.swapaxes(a.reshape(B, n_experts, cap), 0, 1).reshape(n_experts, B * cap)
    y = _ffn(by_expert(idx), by_expert(gate)[:, None, :], h2.reshape(B * T, D // LANES, LANES),
             w_gate[0], w_up[0], w_down[0])
    pos_t = jnp.swapaxes(pos.reshape(B, n_experts, T), 1, 2)
    pos_t = jnp.where(pos_t < 0, UNSELECTED_SLOT, pos_t)
    pos_hl = jnp.concatenate([pos_t >> 4, pos_t & 15], axis=-1).astype(BF16)
    return _combine(starts[:, :, 0].reshape(-1), y.reshape(n_experts, B, cap, D), pos_hl, xl, g2,
                    final_norm_w.reshape(1, D), cap)
```

```python
import functools

import jax
import jax.numpy as jnp
from jax import lax
from jax.experimental import pallas as pl
from jax.experimental.pallas import tpu as pltpu

F32 = jnp.float32
BF16 = jnp.bfloat16

CHUNK = 128
GRID_W = 64
N_HEADS = 4
HEAD_DIM = 128
N_GATE = 4 * N_HEADS
ROPE_BASE = 10000.0
EPS = 1e-6
LOG2E = 1.4426950408889634
EC_CAPACITY_FACTOR = 2
LANES = 128
VMEM_LIMIT_BYTES = 56 * 1024 * 1024
COMBINE_WINDOW = 128
UNSELECTED_SLOT = 1023
COMBINE_GROUP = 4
MIXER_UNROLL = 16
GATE_UNROLL = 8


def _params(*sem):
    return pltpu.CompilerParams(dimension_semantics=sem, vmem_limit_bytes=VMEM_LIMIT_BYTES)


def _dot(a, b):
    return jnp.dot(a, b, preferred_element_type=F32)


def _dot_nt(a, b):
    return lax.dot_general(a, b, (((1,), (1,)), ((), ())), preferred_element_type=F32)


def _dot_tn(a, b):
    return lax.dot_general(a, b, (((0,), (0,)), ((), ())), preferred_element_type=F32)


def _sigmoid(x):
    return 1.0 / (1.0 + jnp.exp(-x))


def _log_sigmoid(x):
    return jnp.minimum(x, 0.0) - jnp.log1p(jnp.exp(-jnp.abs(x)))


def _col_to_row(col):
    return jnp.broadcast_to(col, (col.shape[0], LANES)).T[0:1, :]


def _row_to_cols(row):
    pieces = [jnp.broadcast_to(row[:, c:c + LANES], (LANES, LANES)).T for c in range(0, row.shape[1], LANES)]
    return pieces[0] if len(pieces) == 1 else jnp.concatenate(pieces, axis=0)


def _tile(n, pref):
    t = min(n, pref)
    assert n % t == 0, (n, t)
    return t


def _ada_kernel(c_ref, w_ref, b_ref, o_ref):
    cv = c_ref[...]
    s = (cv * _sigmoid(cv)).astype(BF16)
    o_ref[...] = _dot(s, w_ref[...].astype(BF16)) + b_ref[...]


def _ada(cc, w, b):
    rows, d = cc.shape
    n = w.shape[1]
    tn = _tile(n, 1024)
    return pl.pallas_call(
        _ada_kernel,
        grid=(n // tn,),
        in_specs=[pl.BlockSpec((rows, d), lambda j: (0, 0)),
                  pl.BlockSpec((d, tn), lambda j: (0, j)),
                  pl.BlockSpec((1, tn), lambda j: (0, j))],
        out_specs=pl.BlockSpec((rows, tn), lambda j: (0, j)),
        out_shape=jax.ShapeDtypeStruct((rows, n), F32),
        compiler_params=_params("arbitrary"),
    )(cc, w, b.reshape(1, n))


def _rms_mod(x, nw, sh, sc):
    ms = jnp.mean(x * x, axis=-1, keepdims=True)
    return (x * lax.rsqrt(ms + EPS) * nw) * (1.0 + sc) + sh


def _inproj_kernel(x_ref, nw_ref, sh_ref, sc_ref, wt_ref, gb_ref, cos_ref, sin_ref,
                   pm_ref, g_ref, pr_ref, *, width, k_scale, groups):
    hb = _rms_mod(x_ref[0], nw_ref[...], sh_ref[0], sc_ref[0]).astype(BF16)
    ret0 = 4 * width + g_ref.shape[1]
    for slot, j in enumerate(groups):
        acc = _dot_nt(hb, wt_ref[j * width:(j + 1) * width, :].astype(BF16))
        if j == 1:
            acc = acc * k_scale
        for h in range(N_HEADS):
            pm_ref[0, slot * N_HEADS + h] = acc[:, h * HEAD_DIM:(h + 1) * HEAD_DIM].astype(BF16)
    gates = _dot_nt(hb, wt_ref[4 * width:4 * width + LANES, :].astype(BF16)) + gb_ref[...]
    g_ref[0] = gates.T[:g_ref.shape[1]]
    cos = cos_ref[...]
    sin = sin_ref[...]
    lane = lax.broadcasted_iota(jnp.int32, cos.shape, 1)
    even = ((lane // 32) % 2) == 0
    for slot, j in enumerate(groups):
        acc = _dot_nt(hb, wt_ref[ret0 + j * width:ret0 + (j + 1) * width, :].astype(BF16))
        if j == 1:
            acc = acc * k_scale
        if j < 2:
            swapped = jnp.where(even, pltpu.roll(acc, width - 32, 1), pltpu.roll(acc, 32, 1))
            acc = acc * cos + swapped * sin
        for h in range(N_HEADS):
            pr_ref[0, slot * N_HEADS + h] = acc[:, h * HEAD_DIM:(h + 1) * HEAD_DIM].astype(BF16)


def _inproj(x, nw, sh, sc, w_t, gb, cos, sin, groups):
    b, t, d = x.shape
    width = (w_t.shape[1] - N_GATE) // 8
    tm = _tile(t, 512)
    slabs = len(groups) * N_HEADS
    kern = functools.partial(_inproj_kernel, width=width, k_scale=HEAD_DIM ** -0.5, groups=groups)
    const = lambda i, j: (0, 0)
    return pl.pallas_call(
        kern,
        grid=(b, t // tm),
        in_specs=[pl.BlockSpec((1, tm, d), lambda i, j: (i, j, 0)),
                  pl.BlockSpec((1, d), const),
                  pl.BlockSpec((1, 1, d), lambda i, j: (i, 0, 0)),
                  pl.BlockSpec((1, 1, d), lambda i, j: (i, 0, 0)),
                  pl.BlockSpec((None,) + w_t.shape[1:], lambda i, j: (0, 0, 0), pipeline_mode=pl.Buffered(1)),
                  pl.BlockSpec((1, LANES), const),
                  pl.BlockSpec((tm, width), lambda i, j: (j, 0)),
                  pl.BlockSpec((tm, width), lambda i, j: (j, 0))],
        out_specs=[pl.BlockSpec((1, slabs, tm, HEAD_DIM), lambda i, j: (i, 0, j, 0)),
                   pl.BlockSpec((1, N_GATE, tm), lambda i, j: (i, 0, j)),
                   pl.BlockSpec((1, slabs, tm, HEAD_DIM), lambda i, j: (i, 0, j, 0))],
        out_shape=[jax.ShapeDtypeStruct((b, slabs, t, HEAD_DIM), BF16),
                   jax.ShapeDtypeStruct((b, N_GATE, t), F32),
                   jax.ShapeDtypeStruct((b, slabs, t, HEAD_DIM), BF16)],
        compiler_params=_params("arbitrary", "arbitrary"),
    )(x, nw, sh, sc, w_t, gb, cos, sin)


def _scan_lanes(x, op, fill, reverse):
    lane = lax.broadcasted_iota(jnp.int32, x.shape, 1)
    sh = 1
    while sh < LANES:
        if reverse:
            x = op(x, jnp.where(lane < LANES - sh, pltpu.roll(x, LANES - sh, 1), fill))
        else:
            x = op(x, jnp.where(lane >= sh, pltpu.roll(x, sh, 1), fill))
        sh *= 2
    return x


def _gateprep_kernel(gl_ref, gc_ref, o_ref, *, n_lat, n_ctx):
    n_b = gl_ref.shape[0]
    half = N_GATE // 2
    rows = n_b * half
    row = lax.broadcasted_iota(jnp.int32, (rows, LANES), 0)
    lane = lax.broadcasted_iota(jnp.int32, (rows, LANES), 1)
    is_fwd = (row % half) < N_HEADS
    last = lane == jnp.where(is_fwd, LANES - 1, 0)

    def chunk_rows(ref, lo, c_fwd, c_bwd):
        def at(c):
            cols = (slice(c * LANES, (c + 1) * LANES) if isinstance(c, int)
                    else pl.ds(pl.multiple_of(c * LANES, LANES), LANES))
            return jnp.concatenate([ref[s, lo:lo + half, cols] for s in range(n_b)], axis=0)
        return jnp.where(is_fwd, at(c_fwd), at(c_bwd))

    def step(c, ig, fg, m):
        lf = _log_sigmoid(fg)
        b = jnp.where(is_fwd, _scan_lanes(lf, jnp.add, 0.0, False), _scan_lanes(lf, jnp.add, 0.0, True))
        b_tot = jnp.sum(jnp.where(last, b, 0.0), axis=1, keepdims=True)
        a = ig - b
        m_new = jnp.maximum(b_tot + m, jnp.max(b_tot + a, axis=1, keepdims=True))
        w = jnp.exp(b_tot + a - m_new)
        decay = jnp.exp(b_tot + m - m_new)
        run = jnp.where(is_fwd, _scan_lanes(a, jnp.maximum, -jnp.inf, False),
                        _scan_lanes(a, jnp.maximum, -jnp.inf, True))
        mm = jnp.maximum(m, run)
        o_ref[c, 0] = a * LOG2E
        o_ref[c, 1] = w
        o_ref[c, 2] = mm * LOG2E
        o_ref[c, 3] = jnp.exp(-(b + mm))
        o_ref[c, 4] = jnp.broadcast_to(decay, (rows, LANES))
        o_ref[c, 5] = jnp.broadcast_to(m * LOG2E, (rows, LANES))
        return m_new

    m = jnp.zeros((rows, 1), F32)
    for c in range(n_ctx):
        m = step(c, chunk_rows(gc_ref, 0, c, n_ctx - 1 - c), chunk_rows(gc_ref, half, c, n_ctx - 1 - c), m)

    def latent(i, m):
        return step(n_ctx + i, chunk_rows(gl_ref, 0, i, n_lat - 1 - i),
                    chunk_rows(gl_ref, half, i, n_lat - 1 - i), m)

    lax.fori_loop(0, n_lat, latent, m, unroll=min(n_lat, GATE_UNROLL))


def _gateprep(gl, gc):
    n_lat, n_ctx = gl.shape[2] // CHUNK, gc.shape[2] // CHUNK
    return pl.pallas_call(
        functools.partial(_gateprep_kernel, n_lat=n_lat, n_ctx=n_ctx),
        out_shape=jax.ShapeDtypeStruct((n_ctx + n_lat, 6, gl.shape[0] * N_GATE // 2, LANES), F32),
        compiler_params=pltpu.CompilerParams(vmem_limit_bytes=VMEM_LIMIT_BYTES),
    )(gl, gc)


def _head_norm_gate(h, gate, nw):
    return gate * (h * lax.rsqrt(jnp.mean(h * h, axis=-1, keepdims=True) + EPS)) * nw


def _mlstm_kernel(q_ref, k_ref, v_ref, o_ref, kc_ref, vc_ref, st_ref, nw_ref, y_ref,
                  sf_ref, sb_ref, cf_ref, cb_ref, *, n_lat, n_ctx, unroll):
    L = CHUNK
    A, WT, MM, EN, DECAY, M0 = range(6)
    ones = jnp.ones((L, HEAD_DIM), BF16)
    stream_f = pl.program_id(0) * (N_GATE // 2) + pl.program_id(1)
    stream_b = stream_f + N_HEADS

    def stat(step, plane, stream):
        return st_ref[step, plane, pl.ds(stream, 1), :]

    def to_rows(row):
        return jnp.broadcast_to(row, (L, L)).T

    def update(state_ref, k, v, step, stream):
        wkt = (k.astype(F32).T * stat(step, WT, stream)).astype(BF16)
        u = _dot(wkt, jnp.concatenate([v, ones], axis=1))
        decay = stat(step, DECAY, stream)
        state_ref[...] = jnp.concatenate([decay, decay], axis=1) * state_ref[...] + u

    cf_ref[...] = jnp.zeros_like(cf_ref)
    cb_ref[...] = jnp.zeros_like(cb_ref)
    for c in range(n_ctx):
        rf = slice(c * L, (c + 1) * L)
        rb = slice((n_ctx - 1 - c) * L, (n_ctx - c) * L)
        update(cf_ref, kc_ref[0, rf, :], vc_ref[0, rf, :], c, stream_f)
        update(cb_ref, kc_ref[0, rb, :], vc_ref[0, rb, :], c, stream_b)

    def state_body(i, carry):
        jb = n_lat - 1 - i
        rf = pl.ds(pl.multiple_of(i * L, L), L)
        rb = pl.ds(pl.multiple_of(jb * L, L), L)
        sf_ref[i] = cf_ref[...].astype(BF16)
        sb_ref[jb] = cb_ref[...].astype(BF16)
        update(cf_ref, k_ref[0, rf, :], v_ref[0, rf, :], n_ctx + i, stream_f)
        update(cb_ref, k_ref[0, rb, :], v_ref[0, rb, :], n_ctx + i, stream_b)
        return carry

    lax.fori_loop(0, n_lat, state_body, 0, unroll=unroll)

    r_i = lax.broadcasted_iota(jnp.int32, (L, L), 0)
    c_i = lax.broadcasted_iota(jnp.int32, (L, L), 1)
    tril = c_i <= r_i
    triu = c_i >= r_i
    nw = nw_ref[0]

    def out_body(j, carry):
        rows = pl.ds(pl.multiple_of(j * L, L), L)
        step_f = n_ctx + j
        step_b = n_ctx + n_lat - 1 - j
        q = q_ref[0, rows, :]
        k = k_ref[0, rows, :]
        v = v_ref[0, rows, :]
        mm_f = to_rows(stat(step_f, MM, stream_f))
        mm_b = to_rows(stat(step_b, MM, stream_b))
        s = _dot_nt(q, k)
        d_f = jnp.where(tril, jnp.exp2(stat(step_f, A, stream_f) - mm_f), 0.0)
        d_b = jnp.where(triu, jnp.exp2(stat(step_b, A, stream_b) - mm_b), 0.0)
        iw_f = jnp.exp2(stat(step_f, M0, stream_f) - mm_f)
        iw_b = jnp.exp2(stat(step_b, M0, stream_b) - mm_b)
        qf = q.astype(F32)
        vext = jnp.concatenate([v, ones], axis=1)
        lhs_f = jnp.concatenate([(s * d_f).astype(BF16), (qf * iw_f).astype(BF16)], axis=1)
        lhs_b = jnp.concatenate([(s * d_b).astype(BF16), (qf * iw_b).astype(BF16)], axis=1)
        tot_f = _dot(lhs_f, jnp.concatenate([vext, sf_ref[j]], axis=0))
        tot_b = _dot(lhs_b, jnp.concatenate([vext, sb_ref[j]], axis=0))
        D = HEAD_DIM
        h = (tot_f[:, :D] / jnp.maximum(jnp.abs(tot_f[:, D:]), to_rows(stat(step_f, EN, stream_f)))
             + tot_b[:, :D] / jnp.maximum(jnp.abs(tot_b[:, D:]), to_rows(stat(step_b, EN, stream_b))))
        gate = _sigmoid(o_ref[0, rows, :].astype(F32))
        y_ref[0, rows, :] = _head_norm_gate(h, gate, nw).astype(BF16)
        return carry

    lax.fori_loop(0, n_lat, out_body, 0, unroll=unroll)


def _head_slab(t, group):
    return pl.BlockSpec((None, 1, t, HEAD_DIM), lambda i, h: (i, group * N_HEADS + h, 0, 0))


def _mlstm(pm, pmc, stats, nw):
    b, _, t, _ = pm.shape
    tc = pmc.shape[2]
    n_lat, n_ctx = t // CHUNK, tc // CHUNK
    H = N_HEADS
    blk = lambda off: _head_slab(t, off // H)
    cblk = lambda off: _head_slab(tc, off // H - 1)
    return pl.pallas_call(
        functools.partial(_mlstm_kernel, n_lat=n_lat, n_ctx=n_ctx, unroll=min(n_lat, MIXER_UNROLL)),
        grid=(b, H),
        in_specs=[blk(0), blk(H), blk(2 * H), blk(3 * H), cblk(H), cblk(2 * H),
                  pl.BlockSpec(stats.shape, lambda i, h: (0, 0, 0, 0)),
                  pl.BlockSpec((1, 1, HEAD_DIM), lambda i, h: (h, 0, 0))],
        out_specs=_head_slab(t, 0),
        out_shape=jax.ShapeDtypeStruct((b, H, t, HEAD_DIM), BF16),
        scratch_shapes=[pltpu.VMEM((n_lat, HEAD_DIM, HEAD_DIM + LANES), BF16),
                        pltpu.VMEM((n_lat, HEAD_DIM, HEAD_DIM + LANES), BF16),
                        pltpu.VMEM((HEAD_DIM, HEAD_DIM + LANES), F32),
                        pltpu.VMEM((HEAD_DIM, HEAD_DIM + LANES), F32)],
        compiler_params=_params("arbitrary", "arbitrary"),
    )(pm, pm, pm, pm, pmc, pmc, stats, nw)


def _ret_kernel(q_ref, k_ref, v_ref, o_ref, kc_ref, vc_ref, dl_ref, nw_ref, y_ref,
                sf_ref, sb_ref, rf_ref, rb_ref, *, n_lat, n_ctx, unroll):
    L = CHUNK
    lg = _log_sigmoid(dl_ref[0])
    lg_f, lg_b = lg[0:1, :], lg[1:2, :]
    r_i = lax.broadcasted_iota(jnp.int32, (L, L), 0)
    c_i = lax.broadcasted_iota(jnp.int32, (L, L), 1)
    r_f = r_i.astype(F32)
    rel = (r_i - c_i).astype(F32)
    kw_f = jnp.exp((L - 1.0 - r_f) * lg_f)
    kw_b = jnp.exp(r_f * lg_b)
    in_f = jnp.exp((r_f + 1.0) * lg_f)
    in_b = jnp.exp((L - r_f) * lg_b)
    dch_f = jnp.exp(L * lg_f)
    dch_b = jnp.exp(L * lg_b)
    d_mat = (jnp.where(rel >= 0, jnp.exp(jnp.maximum(rel, 0.0) * lg_f), 0.0)
             + jnp.where(rel <= 0, jnp.exp(jnp.maximum(-rel, 0.0) * lg_b), 0.0))

    def update(state_ref, k, v, kw, dch):
        wk = (k.astype(F32) * kw).astype(BF16)
        state_ref[...] = dch * state_ref[...] + _dot_tn(wk, v)

    rf_ref[...] = jnp.zeros_like(rf_ref)
    rb_ref[...] = jnp.zeros_like(rb_ref)
    for c in range(n_ctx):
        rows = slice(c * L, (c + 1) * L)
        update(rf_ref, kc_ref[0, rows, :], vc_ref[0, rows, :], kw_f, dch_f)
    for c in reversed(range(n_ctx)):
        rows = slice(c * L, (c + 1) * L)
        update(rb_ref, kc_ref[0, rows, :], vc_ref[0, rows, :], kw_b, dch_b)

    def state_body(i, carry):
        jb = n_lat - 1 - i
        rowf = pl.ds(pl.multiple_of(i * L, L), L)
        rowb = pl.ds(pl.multiple_of(jb * L, L), L)
        sf_ref[i] = rf_ref[...].astype(BF16)
        sb_ref[jb] = rb_ref[...].astype(BF16)
        update(rf_ref, k_ref[0, rowf, :], v_ref[0, rowf, :], kw_f, dch_f)
        update(rb_ref, k_ref[0, rowb, :], v_ref[0, rowb, :], kw_b, dch_b)
        return carry

    lax.fori_loop(0, n_lat, state_body, 0, unroll=unroll)
    nw = nw_ref[0]

    def out_body(j, carry):
        rows = pl.ds(pl.multiple_of(j * L, L), L)
        q = q_ref[0, rows, :]
        s = _dot_nt(q, k_ref[0, rows, :])
        intra = _dot((s * d_mat).astype(BF16), v_ref[0, rows, :])
        inter = _dot(q, jnp.concatenate([sf_ref[j], sb_ref[j]], axis=1))
        out = intra + in_f * inter[:, :HEAD_DIM] + in_b * inter[:, HEAD_DIM:]
        og = o_ref[0, rows, :].astype(F32)
        y_ref[0, rows, :] = _head_norm_gate(out, og * _sigmoid(og), nw).astype(BF16)
        return carry

    lax.fori_loop(0, n_lat, out_body, 0, unroll=unroll)


def _ret(pr, prc, dl, nw):
    b, _, t, _ = pr.shape
    tc = prc.shape[2]
    n_lat, n_ctx = t // CHUNK, tc // CHUNK
    H = N_HEADS
    blk = lambda off: _head_slab(t, off // H)
    cblk = lambda off: _head_slab(tc, off // H - 1)
    return pl.pallas_call(
        functools.partial(_ret_kernel, n_lat=n_lat, n_ctx=n_ctx, unroll=min(n_lat, MIXER_UNROLL)),
        grid=(b, H),
        in_specs=[blk(0), blk(H), blk(2 * H), blk(3 * H), cblk(H), cblk(2 * H),
                  pl.BlockSpec((1, 2, LANES), lambda i, h: (h, 0, 0)),
                  pl.BlockSpec((1, 1, HEAD_DIM), lambda i, h: (h, 0, 0))],
        out_specs=_head_slab(t, 0),
        out_shape=jax.ShapeDtypeStruct((b, H, t, HEAD_DIM), BF16),
        scratch_shapes=[pltpu.VMEM((n_lat, HEAD_DIM, HEAD_DIM), BF16),
                        pltpu.VMEM((n_lat, HEAD_DIM, HEAD_DIM), BF16),
                        pltpu.VMEM((HEAD_DIM, HEAD_DIM), F32),
                        pltpu.VMEM((HEAD_DIM, HEAD_DIM), F32)],
        compiler_params=_params("arbitrary", "arbitrary"),
    )(pr, pr, pr, pr, prc, prc, dl, nw)


def _outproj_kernel(ym_ref, yr_ref, x_ref, wo_ref, g1_ref, nw_ref, sh_ref, sc_ref, wrt_ref,
                    xl_ref, h2_ref, aff_ref, *, n_experts):
    heads = [ym_ref[0, h] for h in range(N_HEADS)] + [yr_ref[0, h] for h in range(N_HEADS)]
    y = _dot(jnp.concatenate(heads, axis=1), wo_ref[...])
    xl = x_ref[0] + g1_ref[0] * y
    xl_ref[0] = xl
    h2 = _rms_mod(xl, nw_ref[...], sh_ref[0], sc_ref[0])
    h2_ref[0] = h2.reshape(h2.shape[0], h2.shape[1] // LANES, LANES)
    h_hi = h2.astype(BF16)
    h_lo = (h2 - h_hi.astype(F32)).astype(BF16)
    p_hi = _dot(h_hi, wrt_ref[...])
    p_lo = _dot(h_lo, wrt_ref[...])
    logits = p_hi + pltpu.roll(p_hi, LANES - n_experts, 1) + p_lo
    lane = lax.broadcasted_iota(jnp.int32, logits.shape, 1)
    logits = jnp.where(lane < n_experts, logits, -jnp.inf)
    e = jnp.exp(logits - jnp.max(logits, axis=-1, keepdims=True))
    aff_t = (e / jnp.sum(e, axis=-1, keepdims=True)).T[:n_experts]
    aff_ref[0] = aff_t.reshape(n_experts, aff_t.shape[1] // LANES, LANES)


def _outproj(ym, yr, x, wo, g1, nw, sh, sc, wrt, n_experts):
    b, t, d = x.shape
    tm = _tile(t, 8 * LANES)
    slab = pl.BlockSpec((1, N_HEADS, tm, HEAD_DIM), lambda i, j: (i, 0, j, 0))
    const = lambda i, j: (0, 0)
    per_b = lambda i, j: (i, 0, 0)
    tok = lambda i, j: (i, j, 0)
    return pl.pallas_call(
        functools.partial(_outproj_kernel, n_experts=n_experts),
        grid=(b, t // tm),
        in_specs=[slab, slab,
                  pl.BlockSpec((1, tm, d), tok), pl.BlockSpec(wo.shape, const),
                  pl.BlockSpec((1, 1, d), per_b), pl.BlockSpec((1, d), const),
                  pl.BlockSpec((1, 1, d), per_b), pl.BlockSpec((1, 1, d), per_b),
                  pl.BlockSpec(wrt.shape, const)],
        out_specs=[pl.BlockSpec((1, tm, d), tok),
                   pl.BlockSpec((1, tm, d // LANES, LANES), lambda i, j: (i, j, 0, 0)),
                   pl.BlockSpec((1, n_experts, tm // LANES, LANES), lambda i, j: (i, 0, j, 0))],
        out_shape=[jax.ShapeDtypeStruct((b, t, d), F32),
                   jax.ShapeDtypeStruct((b, t, d // LANES, LANES), F32),
                   jax.ShapeDtypeStruct((b, n_experts, t // LANES, LANES), F32)],
        compiler_params=_params("arbitrary", "arbitrary"),
    )(ym, yr, x, wo, g1, nw, sh, sc, wrt)


def _select_top(a, cap):
    n_e, n_b, _ = a.shape

    ones = jnp.ones((LANES, LANES), BF16)

    def count(mask):
        per_block = _dot(mask.astype(F32).astype(BF16).reshape(n_e * n_b, LANES), ones)
        return jnp.sum(per_block.reshape(n_e, n_b, LANES), axis=1, keepdims=True).astype(jnp.int32)

    def search(i, thr):
        cand = thr | jnp.left_shift(jnp.int32(1), 30 - i)
        return jnp.where(count(a >= pltpu.bitcast(cand, F32)) >= cap, cand, thr)

    thr = lax.fori_loop(0, 31, search, jnp.zeros((n_e, 1, LANES), jnp.int32))
    above = a >= pltpu.bitcast(thr + 1, F32)
    band = (a >= pltpu.bitcast(thr, F32)) & jnp.logical_not(above)
    need = cap - count(above)[:, :, 0:1]
    tok = (lax.broadcasted_iota(jnp.int32, a.shape, 1) * LANES
           + lax.broadcasted_iota(jnp.int32, a.shape, 2))

    def take_one(_, carry):
        sel, band, need = carry
        in_band = band > 0
        best = jnp.max(jnp.max(jnp.where(in_band, a, -1.0), axis=2, keepdims=True), axis=1, keepdims=True)
        cand = jnp.where(in_band & (a == best), tok, n_b * LANES)
        first = jnp.min(jnp.min(cand, axis=2, keepdims=True), axis=1, keepdims=True)
        take = ((tok == first) & (need > 0)).astype(jnp.int32)
        return sel | take, band - take, need - 1

    sel, _, _ = lax.fori_loop(0, jnp.max(need), take_one,
                              (above.astype(jnp.int32), band.astype(jnp.int32), need))
    return sel


def _route_kernel(aff_ref, pos_ref, idx_ref, gate_ref, start_ref, sel_ref, *, cap):
    i = pl.program_id(0)
    n_s, n_e, n_b, _ = aff_ref.shape

    @pl.when(i == 0)
    def _():
        sel_all = _select_top(aff_ref[...].reshape(n_s * n_e, n_b, LANES), cap)
        sel_ref[...] = sel_all.reshape(n_s, n_e, n_b, LANES)

    a = aff_ref[i]
    sel = sel_ref[i] > 0

    rows = n_e * n_b
    u_i = lax.broadcasted_iota(jnp.int32, (LANES, LANES), 0)
    s_i = lax.broadcasted_iota(jnp.int32, (LANES, LANES), 1)
    incl = (u_i <= s_i).astype(BF16)
    ones = jnp.ones((LANES, LANES), BF16)
    r_i = lax.broadcasted_iota(jnp.int32, (rows, rows), 0)
    c_i = lax.broadcasted_iota(jnp.int32, (rows, rows), 1)
    before = ((r_i // n_b == c_i // n_b) & (c_i < r_i)).astype(BF16)

    x = sel.astype(F32).reshape(rows, LANES)
    xb = x.astype(BF16)
    block_tot = _dot(xb, ones).astype(BF16)
    start = _dot(before, block_tot)
    pos = jnp.where(x > 0, start + _dot(xb, incl) - x, -1.0).astype(jnp.int32)
    pos_ref[0] = pos.reshape(n_e, n_b, LANES)
    start_ref[0] = start.astype(jnp.int32)

    pad = (-rows) % LANES
    def pad_rows(m):
        return m if pad == 0 else jnp.concatenate([m, jnp.zeros((pad, LANES), m.dtype)], axis=0)
    start_p = pad_rows(start)
    end_p = pad_rows(start + block_tot.astype(F32))
    pos_p = pad_rows(pos)
    pos_p = jnp.where(pos_p < 0, 1023, pos_p)
    a_p = pad_rows(a.reshape(rows, LANES))
    a_hi = a_p.astype(BF16)
    a_mid = (a_p - a_hi.astype(F32)).astype(BF16)
    a_lo = (a_p - a_hi.astype(F32) - a_mid.astype(F32)).astype(BF16)
    per_group = LANES // n_b
    j_col = lax.broadcasted_iota(jnp.int32, (cap, LANES), 0).astype(F32)
    lane = lax.broadcasted_iota(jnp.int32, (cap, LANES), 1)
    lane_f = lane.astype(F32)
    block_f = (lane % n_b).astype(F32)
    row0 = pl.program_id(0) * (n_b * LANES)
    for g in range((rows + pad) // LANES):
        rs = slice(g * LANES, (g + 1) * LANES)
        s_row = start_p[rs].T[0:1, :]
        e_row = end_p[rs].T[0:1, :]
        hi = (pos_p[rs] >> 4).astype(F32).astype(BF16)
        lo = (pos_p[rs] & 15).astype(F32).astype(BF16)
        for el in range(per_group):
            e = g * per_group + el
            if e >= n_e:
                break
            mine = (j_col >= s_row) & (j_col < e_row) & ((lane // n_b) == el)
            mine_b = mine.astype(F32).astype(BF16)
            slots = 16.0 * _dot(mine_b, hi) + _dot(mine_b, lo)
            hit = slots == j_col
            in_block = jnp.sum(jnp.where(hit, lane_f, 0.0), axis=1, keepdims=True)
            block = jnp.sum(jnp.where(mine, block_f, 0.0), axis=1, keepdims=True)
            a_blk = _dot(mine_b, a_hi[rs]) + _dot(mine_b, a_mid[rs]) + _dot(mine_b, a_lo[rs])
            gate = jnp.sum(jnp.where(hit, a_blk, 0.0), axis=1, keepdims=True)
            idx_ref[0, e] = _col_to_row(block * LANES + in_block).astype(jnp.int32) + row0
            gate_ref[0, e] = _col_to_row(gate)


def _route(aff_t, cap):
    b, n_e, n_b, _ = aff_t.shape
    spec = pl.BlockSpec((1, n_e, n_b, LANES), lambda i: (i, 0, 0, 0))
    slot_spec = pl.BlockSpec((1, n_e, 1, cap), lambda i: (i, 0, 0, 0))
    return pl.pallas_call(
        functools.partial(_route_kernel, cap=cap),
        grid=(b,), in_specs=[pl.BlockSpec(aff_t.shape, lambda i: (0, 0, 0, 0))],
        out_specs=[spec, slot_spec, slot_spec, pl.BlockSpec((1, n_e * n_b, LANES), lambda i: (i, 0, 0))],
        scratch_shapes=[pltpu.VMEM(aff_t.shape, jnp.int32)],
        out_shape=[jax.ShapeDtypeStruct(aff_t.shape, jnp.int32),
                   jax.ShapeDtypeStruct((b, n_e, 1, cap), jnp.int32),
                   jax.ShapeDtypeStruct((b, n_e, 1, cap), F32),
                   jax.ShapeDtypeStruct((b, n_e * n_b, LANES), jnp.int32)],
        compiler_params=_params("arbitrary"),
    )(aff_t)


GATHER_DMA_PRIORITY = 1


def _ffn_kernel(idx_ref, h_hbm, gate_ref, wg_ref, wu_ref, wd_ref, y_ref, stage_ref, x_ref, acc_ref, sem,
                *, m, per_step):
    e, f = pl.program_id(0), pl.program_id(1)
    n_e, n_f = pl.num_programs(0), pl.num_programs(1)
    rows_per_expert = stage_ref.shape[0]

    def row_copy(base, j):
        return pltpu.make_async_copy(h_hbm.at[pl.ds(idx_ref[base + j], 1)],
                                     stage_ref.at[pl.ds(j, 1)], sem.at[0])

    def wait_rows():
        pltpu.make_async_copy(h_hbm.at[pl.ds(0, rows_per_expert)], stage_ref, sem.at[0]).wait()

    @pl.when((e == 0) & (f == 0))
    def _():
        def start_row(j, carry):
            row_copy(0, j).start(priority=GATHER_DMA_PRIORITY)
            return carry
        lax.fori_loop(0, rows_per_expert, start_row, 0)

    @pl.when(f == 0)
    def _():
        wait_rows()
        x_ref[...] = stage_ref[:m].reshape(x_ref.shape).astype(BF16)
        acc_ref[...] = jnp.zeros_like(acc_ref)

    nxt = jnp.minimum(e + 1, n_e - 1) * rows_per_expert
    for i in range(per_step):
        row_copy(nxt, f * per_step + i).start(priority=GATHER_DMA_PRIORITY)

    x = x_ref[...]
    a = _dot(x, wg_ref[0].astype(BF16))
    u = _dot(x, wu_ref[0].astype(BF16))
    mid = (a * _sigmoid(a) * u).astype(BF16)
    acc_ref[...] += _dot(mid, wd_ref[0].astype(BF16))

    @pl.when(f == n_f - 1)
    def _():
        gate = _row_to_cols(gate_ref[0])
        for c in range(0, y_ref.shape[2], LANES):
            y_ref[0, :, c:c + LANES] = (acc_ref[:, c:c + LANES] * gate).astype(BF16)

    @pl.when((e == n_e - 1) & (f == n_f - 1))
    def _():
        wait_rows()


def _ffn(idx, gate, h_rows, wg, wu, wd):
    n_e, m = idx.shape
    d = h_rows.shape[1] * LANES
    ff = wg.shape[2]
    tf = _tile(ff, 256)
    n_f = ff // tf
    per_step = -(-m // (8 * n_f)) * 8
    idx = jnp.pad(idx, ((0, 0), (0, n_f * per_step - m))).reshape(-1)
    return pl.pallas_call(
        functools.partial(_ffn_kernel, m=m, per_step=per_step),
        grid_spec=pltpu.PrefetchScalarGridSpec(
            num_scalar_prefetch=1,
            grid=(n_e, n_f),
            in_specs=[pl.BlockSpec(memory_space=pl.ANY),
                      pl.BlockSpec((1, 1, m), lambda e, f, idx: (e, 0, 0)),
                      pl.BlockSpec((1, d, tf), lambda e, f, idx: (e, 0, f)),
                      pl.BlockSpec((1, d, tf), lambda e, f, idx: (e, 0, f)),
                      pl.BlockSpec((1, tf, d), lambda e, f, idx: (e, f, 0))],
            out_specs=pl.BlockSpec((1, m, d), lambda e, f, idx: (e, 0, 0)),
            scratch_shapes=[pltpu.VMEM((n_f * per_step, d // LANES, LANES), F32),
                            pltpu.VMEM((m, d), BF16),
                            pltpu.VMEM((m, d), F32),
                            pltpu.SemaphoreType.DMA((1,))]),
        out_shape=jax.ShapeDtypeStruct((n_e, m, d), BF16),
        compiler_params=_params("arbitrary", "arbitrary"),
    )(idx, h_rows, gate, wg, wu, wd)


def _combine_kernel(start_ref, y_ref, pos_ref, xl_ref, g2_ref, fw_ref, o_ref, yc_ref, acc_ref,
                    *, n_experts, cap, n_b, win):
    i, j = pl.program_id(0), pl.program_id(1)
    tt = xl_ref.shape[1]
    blocks_per_tile = tt // LANES
    lane = lax.broadcasted_iota(jnp.int32, (tt, win), 1).astype(F32)
    r_i = lax.broadcasted_iota(jnp.int32, (2 * n_experts, n_experts * win), 0)
    c_e = lax.broadcasted_iota(jnp.int32, (2 * n_experts, n_experts * win), 1) // win
    spread = jnp.where(r_i == c_e, 16.0, jnp.where(r_i == c_e + n_experts, 1.0, 0.0)).astype(BF16)
    slots = _dot(pos_ref[0], spread)
    last_tile = j == pl.num_programs(1) - 1
    first, n_pass = [], 0
    for e in range(n_experts):
        row = (i * n_experts + e) * n_b
        lo = start_ref[row + j * blocks_per_tile]
        hi = jnp.where(last_tile, cap, start_ref[row + jnp.minimum((j + 1) * blocks_per_tile, n_b - 1)])
        w0 = jnp.minimum((lo // 16) * 16, cap - win)
        first.append(w0)
        n_pass = jnp.maximum(n_pass, jnp.where(hi > lo, (hi - w0 + win - 1) // win, 0))
    group = min(n_experts, COMBINE_GROUP)

    def add_pass(p, acc):
        for g0 in range(0, n_experts, group):
            hits = []
            for e in range(g0, g0 + group):
                done = first[e] + p * win
                off = pl.multiple_of(jnp.minimum(done, cap - win), 16)
                yc_ref[e * win:(e + 1) * win, :] = y_ref[e, 0, pl.ds(off, win), :]
                pe = slots[:, e * win:(e + 1) * win]
                hits.append(((pe - off.astype(F32)) == lane) & (pe >= done.astype(F32)))
            onehot = jnp.concatenate(hits, axis=1).astype(F32).astype(BF16)
            acc = acc + _dot(onehot, yc_ref[g0 * win:(g0 + group) * win, :])
        return acc

    acc_ref[...] = add_pass(0, jnp.zeros(acc_ref.shape, F32))

    def extra_pass(p, carry):
        acc_ref[...] = add_pass(p, acc_ref[...])
        return carry

    lax.fori_loop(1, n_pass, extra_pass, 0)
    xl = xl_ref[0] + g2_ref[0] * acc_ref[...]
    ms = jnp.mean(xl * xl, axis=-1, keepdims=True)
    o_ref[0] = xl * lax.rsqrt(ms + EPS) * fw_ref[...]


def _combine(starts, y, pos_col, xl, g2, fw, cap):
    b, t, d = xl.shape
    n_e = y.shape[0]
    tt = _tile(t, 512)
    win = min(cap, COMBINE_WINDOW)
    assert cap % 16 == 0 and win % 16 == 0 and tt % LANES == 0 and cap + win <= UNSELECTED_SLOT
    tok = lambda i, j, st: (i, j, 0)
    return pl.pallas_call(
        functools.partial(_combine_kernel, n_experts=n_e, cap=cap, n_b=t // LANES, win=win),
        grid_spec=pltpu.PrefetchScalarGridSpec(
            num_scalar_prefetch=1,
            grid=(b, t // tt),
            in_specs=[pl.BlockSpec((n_e, 1, cap, d), lambda i, j, st: (0, i, 0, 0)),
                      pl.BlockSpec((1, tt, 2 * n_e), tok),
                      pl.BlockSpec((1, tt, d), tok),
                      pl.BlockSpec((1, 1, d), lambda i, j, st: (i, 0, 0)),
                      pl.BlockSpec((1, d), lambda i, j, st: (0, 0))],
            out_specs=pl.BlockSpec((1, tt, d), tok),
            scratch_shapes=[pltpu.VMEM((n_e * win, d), BF16), pltpu.VMEM((tt, d), F32)]),
        out_shape=jax.ShapeDtypeStruct((b, t, d), F32),
        compiler_params=_params("arbitrary", "arbitrary"),
    )(starts, y, pos_col, xl, g2, fw)


def _rope_tables(t):
    n_freq = HEAD_DIM // 4
    rows = jnp.repeat(jnp.arange(t // GRID_W, dtype=F32), GRID_W)
    cols = jnp.tile(jnp.arange(GRID_W, dtype=F32), t // GRID_W)
    inv_freq = ROPE_BASE ** (-jnp.arange(n_freq, dtype=F32) / n_freq)
    ang_r, ang_c = rows[:, None] * inv_freq, cols[:, None] * inv_freq
    cos = jnp.concatenate([jnp.cos(ang_r)] * 2 + [jnp.cos(ang_c)] * 2, axis=-1)
    sin = jnp.concatenate([-jnp.sin(ang_r), jnp.sin(ang_r), -jnp.sin(ang_c), jnp.sin(ang_c)], axis=-1)
    return jnp.tile(cos, (1, N_HEADS)), jnp.tile(sin, (1, N_HEADS))


def kernel(x, c, ctx, c_ctx, w_ada, b_ada, norm1_w, norm2_w, w_in, mlstm_gate_bias, ret_decay_logit,
           mlstm_norm_w, ret_norm_w, w_out, w_router, w_gate, w_up, w_down, final_norm_w):
    B, T, D = x.shape
    Tc = ctx.shape[1]
    H = N_HEADS
    width = H * HEAD_DIM
    assert D == 2 * width and w_ada.shape[0] == 1
    assert T % CHUNK == 0 and Tc % CHUNK == 0 and T % GRID_W == 0
    n_lat, n_ctx = T // CHUNK, Tc // CHUNK
    n_experts = w_router.shape[2]
    cap = EC_CAPACITY_FACTOR * T // n_experts
    n_gate = 4 * H

    pad = (-(B + 1)) % 8
    cc = jnp.concatenate([c, c_ctx[None], jnp.zeros((pad, D), F32)], axis=0)
    mod = _ada(cc, w_ada[0], b_ada[0])
    sh1, sc1, g1, sh2, sc2, g2 = [mod[:B, None, i * D:(i + 1) * D] for i in range(6)]
    csh1, csc1 = [jnp.broadcast_to(mod[B, i * D:(i + 1) * D], (B, 1, D)) for i in range(2)]

    assert (4 * width + n_gate) % 8 == 0 and w_in.shape[2] == 8 * width + n_gate
    w_t = jnp.swapaxes(w_in, 1, 2)
    gb = jnp.pad(mlstm_gate_bias[0], (0, LANES - n_gate)).reshape(1, LANES)
    nw1 = norm1_w[0].reshape(1, D)
    cos, sin = _rope_tables(T)
    pm, gl, pr = _inproj(x, nw1, sh1, sc1, w_t, gb, cos, sin, groups=(0, 1, 2, 3))
    pmc, gc, prc = _inproj(ctx, nw1, csh1, csc1, w_t, gb,
                           jnp.ones((Tc, width), F32), jnp.zeros((Tc, width), F32), groups=(1, 2))

    ym = _mlstm(pm, pmc, _gateprep(gl, gc), mlstm_norm_w[0].reshape(H, 1, HEAD_DIM))
    dl = jnp.broadcast_to(jnp.swapaxes(ret_decay_logit[0], 0, 1)[:, :, None], (H, 2, LANES))
    yr = _ret(pr, prc, dl, ret_norm_w[0].reshape(H, 1, HEAD_DIM))

    wr_hi = w_router[0].astype(BF16)
    wr_lo = (w_router[0] - wr_hi.astype(F32)).astype(BF16)
    wrt = jnp.pad(jnp.concatenate([wr_hi, wr_lo], axis=1), ((0, 0), (0, LANES - 2 * n_experts)))
    xl, h2, aff_t = _outproj(ym, yr, x, w_out[0].astype(BF16), g1, norm2_w[0].reshape(1, D), sh2, sc2,
                             wrt, n_experts)

    pos, idx, gate, starts = _route(aff_t, cap)
    by_expert = lambda a: jnp.swapaxes(a.reshape(B, n_experts, cap), 0, 1).reshape(n_experts, B * cap)
    y = _ffn(by_expert(idx), by_expert(gate)[:, None, :], h2.reshape(B * T, D // LANES, LANES),
             w_gate[0], w_up[0], w_down[0])
    pos_t = jnp.swapaxes(pos.reshape(B, n_experts, T), 1, 2)
    pos_t = jnp.where(pos_t < 0, UNSELECTED_SLOT, pos_t)
    pos_hl = jnp.concatenate([pos_t >> 4, pos_t & 15], axis=-1).astype(BF16)
    return _combine(starts[:, :, 0].reshape(-1), y.reshape(n_experts, B, cap, D), pos_hl, xl, g2,
                    final_norm_w.reshape(1, D), cap)
```

```python
import functools

import jax
import jax.numpy as jnp
from jax import lax
from jax.experimental import pallas as pl
from jax.experimental.pallas import tpu as pltpu

F32 = jnp.float32
BF16 = jnp.bfloat16

CHUNK = 128
GRID_W = 64
N_HEADS = 4
HEAD_DIM = 128
N_GATE = 4 * N_HEADS
ROPE_BASE = 10000.0
EPS = 1e-6
LOG2E = 1.4426950408889634
EC_CAPACITY_FACTOR = 2
LANES = 128
VMEM_LIMIT_BYTES = 56 * 1024 * 1024
COMBINE_WINDOW = 128
UNSELECTED_SLOT = 1023
COMBINE_GROUP = 4
MIXER_UNROLL = 16
GATE_UNROLL = 8


def _params(*sem):
    return pltpu.CompilerParams(dimension_semantics=sem, vmem_limit_bytes=VMEM_LIMIT_BYTES)


def _dot(a, b):
    return jnp.dot(a, b, preferred_element_type=F32)


def _dot_nt(a, b):
    return lax.dot_general(a, b, (((1,), (1,)), ((), ())), preferred_element_type=F32)


def _dot_tn(a, b):
    return lax.dot_general(a, b, (((0,), (0,)), ((), ())), preferred_element_type=F32)


def _sigmoid(x):
    return 1.0 / (1.0 + jnp.exp(-x))


def _log_sigmoid(x):
    return jnp.minimum(x, 0.0) - jnp.log1p(jnp.exp(-jnp.abs(x)))


def _col_to_row(col):
    return jnp.broadcast_to(col, (col.shape[0], LANES)).T[0:1, :]


def _row_to_cols(row):
    pieces = [jnp.broadcast_to(row[:, c:c + LANES], (LANES, LANES)).T for c in range(0, row.shape[1], LANES)]
    return pieces[0] if len(pieces) == 1 else jnp.concatenate(pieces, axis=0)


def _tile(n, pref):
    t = min(n, pref)
    assert n % t == 0, (n, t)
    return t


def _ada_kernel(c_ref, w_ref, b_ref, o_ref):
    cv = c_ref[...]
    s = (cv * _sigmoid(cv)).astype(BF16)
    o_ref[...] = _dot(s, w_ref[...].astype(BF16)) + b_ref[...]


def _ada(cc, w, b):
    rows, d = cc.shape
    n = w.shape[1]
    tn = _tile(n, 1024)
    return pl.pallas_call(
        _ada_kernel,
        grid=(n // tn,),
        in_specs=[pl.BlockSpec((rows, d), lambda j: (0, 0)),
                  pl.BlockSpec((d, tn), lambda j: (0, j)),
                  pl.BlockSpec((1, tn), lambda j: (0, j))],
        out_specs=pl.BlockSpec((rows, tn), lambda j: (0, j)),
        out_shape=jax.ShapeDtypeStruct((rows, n), F32),
        compiler_params=_params("arbitrary"),
    )(cc, w, b.reshape(1, n))


def _rms_mod(x, nw, sh, sc):
    ms = jnp.mean(x * x, axis=-1, keepdims=True)
    return (x * lax.rsqrt(ms + EPS) * nw) * (1.0 + sc) + sh


def _inproj_kernel(x_ref, nw_ref, sh_ref, sc_ref, wt_ref, gb_ref, cos_ref, sin_ref,
                   pm_ref, g_ref, pr_ref, w_ref, *, width, k_scale, groups):
    n_g = len(groups)
    ret0 = 4 * width + g_ref.shape[1]
    gate_col = n_g * width
    ret_col = gate_col + LANES

    @pl.when((pl.program_id(0) == 0) & (pl.program_id(1) == 0))
    def _():
        for slot, j in enumerate(groups):
            w_ref[:, slot * width:(slot + 1) * width] = wt_ref[j * width:(j + 1) * width, :].T.astype(BF16)
            w_ref[:, ret_col + slot * width:ret_col + (slot + 1) * width] = (
                wt_ref[ret0 + j * width:ret0 + (j + 1) * width, :].T.astype(BF16))
        w_ref[:, gate_col:ret_col] = wt_ref[4 * width:4 * width + LANES, :].T.astype(BF16)

    hb = _rms_mod(x_ref[0], nw_ref[...], sh_ref[0], sc_ref[0]).astype(BF16)
    for slot, j in enumerate(groups):
        acc = _dot(hb, w_ref[:, slot * width:(slot + 1) * width])
        if j == 1:
            acc = acc * k_scale
        for h in range(N_HEADS):
            pm_ref[0, slot * N_HEADS + h] = acc[:, h * HEAD_DIM:(h + 1) * HEAD_DIM].astype(BF16)
    g_ref[0] = (_dot(hb, w_ref[:, gate_col:ret_col]) + gb_ref[...]).T[:g_ref.shape[1]]
    cos = cos_ref[...]
    sin = sin_ref[...]
    lane = lax.broadcasted_iota(jnp.int32, cos.shape, 1)
    even = ((lane // 32) % 2) == 0
    for slot, j in enumerate(groups):
        acc = _dot(hb, w_ref[:, ret_col + slot * width:ret_col + (slot + 1) * width])
        if j == 1:
            acc = acc * k_scale
        if j < 2:
            swapped = jnp.where(even, pltpu.roll(acc, width - 32, 1), pltpu.roll(acc, 32, 1))
            acc = acc * cos + swapped * sin
        for h in range(N_HEADS):
            pr_ref[0, slot * N_HEADS + h] = acc[:, h * HEAD_DIM:(h + 1) * HEAD_DIM].astype(BF16)


def _inproj(x, nw, sh, sc, w_t, gb, cos, sin, groups):
    b, t, d = x.shape
    width = (w_t.shape[1] - N_GATE) // 8
    tm = _tile(t, 512)
    slabs = len(groups) * N_HEADS
    kern = functools.partial(_inproj_kernel, width=width, k_scale=HEAD_DIM ** -0.5, groups=groups)
    const = lambda i, j: (0, 0)
    return pl.pallas_call(
        kern,
        grid=(b, t // tm),
        in_specs=[pl.BlockSpec((1, tm, d), lambda i, j: (i, j, 0)),
                  pl.BlockSpec((1, d), const),
                  pl.BlockSpec((1, 1, d), lambda i, j: (i, 0, 0)),
                  pl.BlockSpec((1, 1, d), lambda i, j: (i, 0, 0)),
                  pl.BlockSpec((None,) + w_t.shape[1:], lambda i, j: (0, 0, 0), pipeline_mode=pl.Buffered(1)),
                  pl.BlockSpec((1, LANES), const),
                  pl.BlockSpec((tm, width), lambda i, j: (j, 0)),
                  pl.BlockSpec((tm, width), lambda i, j: (j, 0))],
        out_specs=[pl.BlockSpec((1, slabs, tm, HEAD_DIM), lambda i, j: (i, 0, j, 0)),
                   pl.BlockSpec((1, N_GATE, tm), lambda i, j: (i, 0, j)),
                   pl.BlockSpec((1, slabs, tm, HEAD_DIM), lambda i, j: (i, 0, j, 0))],
        out_shape=[jax.ShapeDtypeStruct((b, slabs, t, HEAD_DIM), BF16),
                   jax.ShapeDtypeStruct((b, N_GATE, t), F32),
                   jax.ShapeDtypeStruct((b, slabs, t, HEAD_DIM), BF16)],
        scratch_shapes=[pltpu.VMEM((d, 2 * len(groups) * width + LANES), BF16)],
        compiler_params=_params("arbitrary", "arbitrary"),
    )(x, nw, sh, sc, w_t, gb, cos, sin)


def _scan_lanes(x, op, fill, reverse):
    lane = lax.broadcasted_iota(jnp.int32, x.shape, 1)
    sh = 1
    while sh < LANES:
        if reverse:
            x = op(x, jnp.where(lane < LANES - sh, pltpu.roll(x, LANES - sh, 1), fill))
        else:
            x = op(x, jnp.where(lane >= sh, pltpu.roll(x, sh, 1), fill))
        sh *= 2
    return x


def _gateprep_kernel(gl_ref, gc_ref, o_ref, *, n_lat, n_ctx):
    n_b = gl_ref.shape[0]
    half = N_GATE // 2
    rows = n_b * half
    row = lax.broadcasted_iota(jnp.int32, (rows, LANES), 0)
    lane = lax.broadcasted_iota(jnp.int32, (rows, LANES), 1)
    is_fwd = (row % half) < N_HEADS
    last = lane == jnp.where(is_fwd, LANES - 1, 0)

    def chunk_rows(ref, lo, c_fwd, c_bwd):
        def at(c):
            cols = (slice(c * LANES, (c + 1) * LANES) if isinstance(c, int)
                    else pl.ds(pl.multiple_of(c * LANES, LANES), LANES))
            return jnp.concatenate([ref[s, lo:lo + half, cols] for s in range(n_b)], axis=0)
        return jnp.where(is_fwd, at(c_fwd), at(c_bwd))

    def step(c, ig, fg, m):
        lf = _log_sigmoid(fg)
        b = jnp.where(is_fwd, _scan_lanes(lf, jnp.add, 0.0, False), _scan_lanes(lf, jnp.add, 0.0, True))
        b_tot = jnp.sum(jnp.where(last, b, 0.0), axis=1, keepdims=True)
        a = ig - b
        m_new = jnp.maximum(b_tot + m, jnp.max(b_tot + a, axis=1, keepdims=True))
        w = jnp.exp(b_tot + a - m_new)
        decay = jnp.exp(b_tot + m - m_new)
        run = jnp.where(is_fwd, _scan_lanes(a, jnp.maximum, -jnp.inf, False),
                        _scan_lanes(a, jnp.maximum, -jnp.inf, True))
        mm = jnp.maximum(m, run)
        o_ref[c, 0] = a * LOG2E
        o_ref[c, 1] = w
        o_ref[c, 2] = mm * LOG2E
        o_ref[c, 3] = jnp.exp(-(b + mm))
        o_ref[c, 4] = jnp.broadcast_to(decay, (rows, LANES))
        o_ref[c, 5] = jnp.broadcast_to(m * LOG2E, (rows, LANES))
        return m_new

    m = jnp.zeros((rows, 1), F32)
    for c in range(n_ctx):
        m = step(c, chunk_rows(gc_ref, 0, c, n_ctx - 1 - c), chunk_rows(gc_ref, half, c, n_ctx - 1 - c), m)

    def latent(i, m):
        return step(n_ctx + i, chunk_rows(gl_ref, 0, i, n_lat - 1 - i),
                    chunk_rows(gl_ref, half, i, n_lat - 1 - i), m)

    lax.fori_loop(0, n_lat, latent, m, unroll=min(n_lat, GATE_UNROLL))


def _gateprep(gl, gc):
    n_lat, n_ctx = gl.shape[2] // CHUNK, gc.shape[2] // CHUNK
    return pl.pallas_call(
        functools.partial(_gateprep_kernel, n_lat=n_lat, n_ctx=n_ctx),
        out_shape=jax.ShapeDtypeStruct((n_ctx + n_lat, 6, gl.shape[0] * N_GATE // 2, LANES), F32),
        compiler_params=pltpu.CompilerParams(vmem_limit_bytes=VMEM_LIMIT_BYTES),
    )(gl, gc)


def _head_norm_gate(h, gate, nw):
    return gate * (h * lax.rsqrt(jnp.mean(h * h, axis=-1, keepdims=True) + EPS)) * nw


def _mlstm_kernel(q_ref, k_ref, v_ref, o_ref, kc_ref, vc_ref, st_ref, nw_ref, y_ref,
                  sf_ref, sb_ref, cf_ref, cb_ref, *, n_lat, n_ctx, unroll):
    L = CHUNK
    A, WT, MM, EN, DECAY, M0 = range(6)
    ones = jnp.ones((L, HEAD_DIM), BF16)
    stream_f = pl.program_id(0) * (N_GATE // 2) + pl.program_id(1)
    stream_b = stream_f + N_HEADS

    def stat(step, plane, stream):
        return st_ref[step, plane, pl.ds(stream, 1), :]

    def to_rows(row):
        return jnp.broadcast_to(row, (L, L)).T

    def update(state_ref, k, v, step, stream):
        wkt = (k.astype(F32).T * stat(step, WT, stream)).astype(BF16)
        u = _dot(wkt, jnp.concatenate([v, ones], axis=1))
        decay = stat(step, DECAY, stream)
        state_ref[...] = jnp.concatenate([decay, decay], axis=1) * state_ref[...] + u

    cf_ref[...] = jnp.zeros_like(cf_ref)
    cb_ref[...] = jnp.zeros_like(cb_ref)
    for c in range(n_ctx):
        rf = slice(c * L, (c + 1) * L)
        rb = slice((n_ctx - 1 - c) * L, (n_ctx - c) * L)
        update(cf_ref, kc_ref[0, rf, :], vc_ref[0, rf, :], c, stream_f)
        update(cb_ref, kc_ref[0, rb, :], vc_ref[0, rb, :], c, stream_b)

    def state_body(i, carry):
        jb = n_lat - 1 - i
        rf = pl.ds(pl.multiple_of(i * L, L), L)
        rb = pl.ds(pl.multiple_of(jb * L, L), L)
        sf_ref[i] = cf_ref[...].astype(BF16)
        sb_ref[jb] = cb_ref[...].astype(BF16)
        update(cf_ref, k_ref[0, rf, :], v_ref[0, rf, :], n_ctx + i, stream_f)
        update(cb_ref, k_ref[0, rb, :], v_ref[0, rb, :], n_ctx + i, stream_b)
        return carry

    lax.fori_loop(0, n_lat, state_body, 0, unroll=unroll)

    r_i = lax.broadcasted_iota(jnp.int32, (L, L), 0)
    c_i = lax.broadcasted_iota(jnp.int32, (L, L), 1)
    tril = c_i <= r_i
    triu = c_i >= r_i
    nw = nw_ref[0]

    def out_body(j, carry):
        rows = pl.ds(pl.multiple_of(j * L, L), L)
        step_f = n_ctx + j
        step_b = n_ctx + n_lat - 1 - j
        q = q_ref[0, rows, :]
        k = k_ref[0, rows, :]
        v = v_ref[0, rows, :]
        mm_f = to_rows(stat(step_f, MM, stream_f))
        mm_b = to_rows(stat(step_b, MM, stream_b))
        s = _dot_nt(q, k)
        d_f = jnp.where(tril, jnp.exp2(stat(step_f, A, stream_f) - mm_f), 0.0)
        d_b = jnp.where(triu, jnp.exp2(stat(step_b, A, stream_b) - mm_b), 0.0)
        iw_f = jnp.exp2(stat(step_f, M0, stream_f) - mm_f)
        iw_b = jnp.exp2(stat(step_b, M0, stream_b) - mm_b)
        qf = q.astype(F32)
        vext = jnp.concatenate([v, ones], axis=1)
        lhs_f = jnp.concatenate([(s * d_f).astype(BF16), (qf * iw_f).astype(BF16)], axis=1)
        lhs_b = jnp.concatenate([(s * d_b).astype(BF16), (qf * iw_b).astype(BF16)], axis=1)
        tot_f = _dot(lhs_f, jnp.concatenate([vext, sf_ref[j]], axis=0))
        tot_b = _dot(lhs_b, jnp.concatenate([vext, sb_ref[j]], axis=0))
        D = HEAD_DIM
        h = (tot_f[:, :D] / jnp.maximum(jnp.abs(tot_f[:, D:]), to_rows(stat(step_f, EN, stream_f)))
             + tot_b[:, :D] / jnp.maximum(jnp.abs(tot_b[:, D:]), to_rows(stat(step_b, EN, stream_b))))
        gate = _sigmoid(o_ref[0, rows, :].astype(F32))
        y_ref[0, rows, :] = _head_norm_gate(h, gate, nw).astype(BF16)
        return carry

    lax.fori_loop(0, n_lat, out_body, 0, unroll=unroll)


def _head_slab(t, group):
    return pl.BlockSpec((None, 1, t, HEAD_DIM), lambda i, h: (i, group * N_HEADS + h, 0, 0))


def _mlstm(pm, pmc, stats, nw):
    b, _, t, _ = pm.shape
    tc = pmc.shape[2]
    n_lat, n_ctx = t // CHUNK, tc // CHUNK
    H = N_HEADS
    blk = lambda off: _head_slab(t, off // H)
    cblk = lambda off: _head_slab(tc, off // H - 1)
    return pl.pallas_call(
        functools.partial(_mlstm_kernel, n_lat=n_lat, n_ctx=n_ctx, unroll=min(n_lat, MIXER_UNROLL)),
        grid=(b, H),
        in_specs=[blk(0), blk(H), blk(2 * H), blk(3 * H), cblk(H), cblk(2 * H),
                  pl.BlockSpec(stats.shape, lambda i, h: (0, 0, 0, 0)),
                  pl.BlockSpec((1, 1, HEAD_DIM), lambda i, h: (h, 0, 0))],
        out_specs=_head_slab(t, 0),
        out_shape=jax.ShapeDtypeStruct((b, H, t, HEAD_DIM), BF16),
        scratch_shapes=[pltpu.VMEM((n_lat, HEAD_DIM, HEAD_DIM + LANES), BF16),
                        pltpu.VMEM((n_lat, HEAD_DIM, HEAD_DIM + LANES), BF16),
                        pltpu.VMEM((HEAD_DIM, HEAD_DIM + LANES), F32),
                        pltpu.VMEM((HEAD_DIM, HEAD_DIM + LANES), F32)],
        compiler_params=_params("arbitrary", "arbitrary"),
    )(pm, pm, pm, pm, pmc, pmc, stats, nw)


def _ret_kernel(q_ref, k_ref, v_ref, o_ref, kc_ref, vc_ref, dl_ref, nw_ref, y_ref,
                sf_ref, sb_ref, rf_ref, rb_ref, *, n_lat, n_ctx, unroll):
    L = CHUNK
    lg = _log_sigmoid(dl_ref[0])
    lg_f, lg_b = lg[0:1, :], lg[1:2, :]
    r_i = lax.broadcasted_iota(jnp.int32, (L, L), 0)
    c_i = lax.broadcasted_iota(jnp.int32, (L, L), 1)
    r_f = r_i.astype(F32)
    rel = (r_i - c_i).astype(F32)
    kw_f = jnp.exp((L - 1.0 - r_f) * lg_f)
    kw_b = jnp.exp(r_f * lg_b)
    in_f = jnp.exp((r_f + 1.0) * lg_f)
    in_b = jnp.exp((L - r_f) * lg_b)
    dch_f = jnp.exp(L * lg_f)
    dch_b = jnp.exp(L * lg_b)
    d_mat = (jnp.where(rel >= 0, jnp.exp(jnp.maximum(rel, 0.0) * lg_f), 0.0)
             + jnp.where(rel <= 0, jnp.exp(jnp.maximum(-rel, 0.0) * lg_b), 0.0))

    def update(state_ref, k, v, kw, dch):
        wk = (k.astype(F32) * kw).astype(BF16)
        state_ref[...] = dch * state_ref[...] + _dot_tn(wk, v)

    rf_ref[...] = jnp.zeros_like(rf_ref)
    rb_ref[...] = jnp.zeros_like(rb_ref)
    for c in range(n_ctx):
        rows = slice(c * L, (c + 1) * L)
        update(rf_ref, kc_ref[0, rows, :], vc_ref[0, rows, :], kw_f, dch_f)
    for c in reversed(range(n_ctx)):
        rows = slice(c * L, (c + 1) * L)
        update(rb_ref, kc_ref[0, rows, :], vc_ref[0, rows, :], kw_b, dch_b)

    def state_body(i, carry):
        jb = n_lat - 1 - i
        rowf = pl.ds(pl.multiple_of(i * L, L), L)
        rowb = pl.ds(pl.multiple_of(jb * L, L), L)
        sf_ref[i] = rf_ref[...].astype(BF16)
        sb_ref[jb] = rb_ref[...].astype(BF16)
        update(rf_ref, k_ref[0, rowf, :], v_ref[0, rowf, :], kw_f, dch_f)
        update(rb_ref, k_ref[0, rowb, :], v_ref[0, rowb, :], kw_b, dch_b)
        return carry

    lax.fori_loop(0, n_lat, state_body, 0, unroll=unroll)
    nw = nw_ref[0]

    def out_body(j, carry):
        rows = pl.ds(pl.multiple_of(j * L, L), L)
        q = q_ref[0, rows, :]
        s = _dot_nt(q, k_ref[0, rows, :])
        intra = _dot((s * d_mat).astype(BF16), v_ref[0, rows, :])
        inter = _dot(q, jnp.concatenate([sf_ref[j], sb_ref[j]], axis=1))
        out = intra + in_f * inter[:, :HEAD_DIM] + in_b * inter[:, HEAD_DIM:]
        og = o_ref[0, rows, :].astype(F32)
        y_ref[0, rows, :] = _head_norm_gate(out, og * _sigmoid(og), nw).astype(BF16)
        return carry

    lax.fori_loop(0, n_lat, out_body, 0, unroll=unroll)


def _ret(pr, prc, dl, nw):
    b, _, t, _ = pr.shape
    tc = prc.shape[2]
    n_lat, n_ctx = t // CHUNK, tc // CHUNK
    H = N_HEADS
    blk = lambda off: _head_slab(t, off // H)
    cblk = lambda off: _head_slab(tc, off // H - 1)
    return pl.pallas_call(
        functools.partial(_ret_kernel, n_lat=n_lat, n_ctx=n_ctx, unroll=min(n_lat, MIXER_UNROLL)),
        grid=(b, H),
        in_specs=[blk(0), blk(H), blk(2 * H), blk(3 * H), cblk(H), cblk(2 * H),
                  pl.BlockSpec((1, 2, LANES), lambda i, h: (h, 0, 0)),
                  pl.BlockSpec((1, 1, HEAD_DIM), lambda i, h: (h, 0, 0))],
        out_specs=_head_slab(t, 0),
        out_shape=jax.ShapeDtypeStruct((b, H, t, HEAD_DIM), BF16),
        scratch_shapes=[pltpu.VMEM((n_lat, HEAD_DIM, HEAD_DIM), BF16),
                        pltpu.VMEM((n_lat, HEAD_DIM, HEAD_DIM), BF16),
                        pltpu.VMEM((HEAD_DIM, HEAD_DIM), F32),
                        pltpu.VMEM((HEAD_DIM, HEAD_DIM), F32)],
        compiler_params=_params("arbitrary", "arbitrary"),
    )(pr, pr, pr, pr, prc, prc, dl, nw)


def _outproj_kernel(ym_ref, yr_ref, x_ref, wo_ref, g1_ref, nw_ref, sh_ref, sc_ref, wrt_ref,
                    xl_ref, h2_ref, aff_ref, *, n_experts):
    heads = [ym_ref[0, h] for h in range(N_HEADS)] + [yr_ref[0, h] for h in range(N_HEADS)]
    y = _dot(jnp.concatenate(heads, axis=1), wo_ref[...])
    xl = x_ref[0] + g1_ref[0] * y
    xl_ref[0] = xl
    h2 = _rms_mod(xl, nw_ref[...], sh_ref[0], sc_ref[0])
    h2_ref[0] = h2.reshape(h2.shape[0], h2.shape[1] // LANES, LANES)
    h_hi = h2.astype(BF16)
    h_lo = (h2 - h_hi.astype(F32)).astype(BF16)
    p_hi = _dot(h_hi, wrt_ref[...])
    p_lo = _dot(h_lo, wrt_ref[...])
    logits = p_hi + pltpu.roll(p_hi, LANES - n_experts, 1) + p_lo
    lane = lax.broadcasted_iota(jnp.int32, logits.shape, 1)
    logits = jnp.where(lane < n_experts, logits, -jnp.inf)
    e = jnp.exp(logits - jnp.max(logits, axis=-1, keepdims=True))
    aff_t = (e / jnp.sum(e, axis=-1, keepdims=True)).T[:n_experts]
    aff_ref[0] = aff_t.reshape(n_experts, aff_t.shape[1] // LANES, LANES)


def _outproj(ym, yr, x, wo, g1, nw, sh, sc, wrt, n_experts):
    b, t, d = x.shape
    tm = _tile(t, 8 * LANES)
    slab = pl.BlockSpec((1, N_HEADS, tm, HEAD_DIM), lambda i, j: (i, 0, j, 0))
    const = lambda i, j: (0, 0)
    per_b = lambda i, j: (i, 0, 0)
    tok = lambda i, j: (i, j, 0)
    return pl.pallas_call(
        functools.partial(_outproj_kernel, n_experts=n_experts),
        grid=(b, t // tm),
        in_specs=[slab, slab,
                  pl.BlockSpec((1, tm, d), tok), pl.BlockSpec(wo.shape, const),
                  pl.BlockSpec((1, 1, d), per_b), pl.BlockSpec((1, d), const),
                  pl.BlockSpec((1, 1, d), per_b), pl.BlockSpec((1, 1, d), per_b),
                  pl.BlockSpec(wrt.shape, const)],
        out_specs=[pl.BlockSpec((1, tm, d), tok),
                   pl.BlockSpec((1, tm, d // LANES, LANES), lambda i, j: (i, j, 0, 0)),
                   pl.BlockSpec((1, n_experts, tm // LANES, LANES), lambda i, j: (i, 0, j, 0))],
        out_shape=[jax.ShapeDtypeStruct((b, t, d), F32),
                   jax.ShapeDtypeStruct((b, t, d // LANES, LANES), F32),
                   jax.ShapeDtypeStruct((b, n_experts, t // LANES, LANES), F32)],
        compiler_params=_params("arbitrary", "arbitrary"),
    )(ym, yr, x, wo, g1, nw, sh, sc, wrt)


def _select_top(a, cap):
    n_e, n_b, _ = a.shape

    ones = jnp.ones((LANES, LANES), BF16)

    def count(mask):
        per_block = _dot(mask.astype(F32).astype(BF16).reshape(n_e * n_b, LANES), ones)
        return jnp.sum(per_block.reshape(n_e, n_b, LANES), axis=1, keepdims=True).astype(jnp.int32)

    def search(i, thr):
        cand = thr | jnp.left_shift(jnp.int32(1), 30 - i)
        return jnp.where(count(a >= pltpu.bitcast(cand, F32)) >= cap, cand, thr)

    thr = lax.fori_loop(0, 31, search, jnp.zeros((n_e, 1, LANES), jnp.int32))
    above = a >= pltpu.bitcast(thr + 1, F32)
    band = (a >= pltpu.bitcast(thr, F32)) & jnp.logical_not(above)
    need = cap - count(above)[:, :, 0:1]
    tok = (lax.broadcasted_iota(jnp.int32, a.shape, 1) * LANES
           + lax.broadcasted_iota(jnp.int32, a.shape, 2))

    def take_one(_, carry):
        sel, band, need = carry
        in_band = band > 0
        best = jnp.max(jnp.max(jnp.where(in_band, a, -1.0), axis=2, keepdims=True), axis=1, keepdims=True)
        cand = jnp.where(in_band & (a == best), tok, n_b * LANES)
        first = jnp.min(jnp.min(cand, axis=2, keepdims=True), axis=1, keepdims=True)
        take = ((tok == first) & (need > 0)).astype(jnp.int32)
        return sel | take, band - take, need - 1

    sel, _, _ = lax.fori_loop(0, jnp.max(need), take_one,
                              (above.astype(jnp.int32), band.astype(jnp.int32), need))
    return sel


def _route_kernel(aff_ref, pos_ref, idx_ref, gate_ref, start_ref, sel_ref, *, cap):
    i = pl.program_id(0)
    n_s, n_e, n_b, _ = aff_ref.shape

    @pl.when(i == 0)
    def _():
        sel_all = _select_top(aff_ref[...].reshape(n_s * n_e, n_b, LANES), cap)
        sel_ref[...] = sel_all.reshape(n_s, n_e, n_b, LANES)

    a = aff_ref[i]
    sel = sel_ref[i] > 0

    rows = n_e * n_b
    u_i = lax.broadcasted_iota(jnp.int32, (LANES, LANES), 0)
    s_i = lax.broadcasted_iota(jnp.int32, (LANES, LANES), 1)
    incl = (u_i <= s_i).astype(BF16)
    ones = jnp.ones((LANES, LANES), BF16)
    r_i = lax.broadcasted_iota(jnp.int32, (rows, rows), 0)
    c_i = lax.broadcasted_iota(jnp.int32, (rows, rows), 1)
    before = ((r_i // n_b == c_i // n_b) & (c_i < r_i)).astype(BF16)

    x = sel.astype(F32).reshape(rows, LANES)
    xb = x.astype(BF16)
    block_tot = _dot(xb, ones).astype(BF16)
    start = _dot(before, block_tot)
    pos = jnp.where(x > 0, start + _dot(xb, incl) - x, -1.0).astype(jnp.int32)
    pos_ref[0] = pos.reshape(n_e, n_b, LANES)
    start_ref[0] = start.astype(jnp.int32)

    pad = (-rows) % LANES
    def pad_rows(m):
        return m if pad == 0 else jnp.concatenate([m, jnp.zeros((pad, LANES), m.dtype)], axis=0)
    start_p = pad_rows(start)
    end_p = pad_rows(start + block_tot.astype(F32))
    pos_p = pad_rows(pos)
    pos_p = jnp.where(pos_p < 0, 1023, pos_p)
    a_p = pad_rows(a.reshape(rows, LANES))
    a_hi = a_p.astype(BF16)
    a_mid = (a_p - a_hi.astype(F32)).astype(BF16)
    a_lo = (a_p - a_hi.astype(F32) - a_mid.astype(F32)).astype(BF16)
    per_group = LANES // n_b
    j_col = lax.broadcasted_iota(jnp.int32, (cap, LANES), 0).astype(F32)
    lane = lax.broadcasted_iota(jnp.int32, (cap, LANES), 1)
    lane_f = lane.astype(F32)
    block_f = (lane % n_b).astype(F32)
    row0 = pl.program_id(0) * (n_b * LANES)
    for g in range((rows + pad) // LANES):
        rs = slice(g * LANES, (g + 1) * LANES)
        s_row = start_p[rs].T[0:1, :]
        e_row = end_p[rs].T[0:1, :]
        hi = (pos_p[rs] >> 4).astype(F32).astype(BF16)
        lo = (pos_p[rs] & 15).astype(F32).astype(BF16)
        for el in range(per_group):
            e = g * per_group + el
            if e >= n_e:
                break
            mine = (j_col >= s_row) & (j_col < e_row) & ((lane // n_b) == el)
            mine_b = mine.astype(F32).astype(BF16)
            slots = 16.0 * _dot(mine_b, hi) + _dot(mine_b, lo)
            hit = slots == j_col
            in_block = jnp.sum(jnp.where(hit, lane_f, 0.0), axis=1, keepdims=True)
            block = jnp.sum(jnp.where(mine, block_f, 0.0), axis=1, keepdims=True)
            a_blk = _dot(mine_b, a_hi[rs]) + _dot(mine_b, a_mid[rs]) + _dot(mine_b, a_lo[rs])
            gate = jnp.sum(jnp.where(hit, a_blk, 0.0), axis=1, keepdims=True)
            idx_ref[0, e] = _col_to_row(block * LANES + in_block).astype(jnp.int32) + row0
            gate_ref[0, e] = _col_to_row(gate)


def _route(aff_t, cap):
    b, n_e, n_b, _ = aff_t.shape
    spec = pl.BlockSpec((1, n_e, n_b, LANES), lambda i: (i, 0, 0, 0))
    slot_spec = pl.BlockSpec((1, n_e, 1, cap), lambda i: (i, 0, 0, 0))
    return pl.pallas_call(
        functools.partial(_route_kernel, cap=cap),
        grid=(b,), in_specs=[pl.BlockSpec(aff_t.shape, lambda i: (0, 0, 0, 0))],
        out_specs=[spec, slot_spec, slot_spec, pl.BlockSpec((1, n_e * n_b, LANES), lambda i: (i, 0, 0))],
        scratch_shapes=[pltpu.VMEM(aff_t.shape, jnp.int32)],
        out_shape=[jax.ShapeDtypeStruct(aff_t.shape, jnp.int32),
                   jax.ShapeDtypeStruct((b, n_e, 1, cap), jnp.int32),
                   jax.ShapeDtypeStruct((b, n_e, 1, cap), F32),
                   jax.ShapeDtypeStruct((b, n_e * n_b, LANES), jnp.int32)],
        compiler_params=_params("arbitrary"),
    )(aff_t)


def _ffn_kernel(idx_ref, h_hbm, gate_ref, wg_ref, wu_ref, wd_ref, y_ref, stage_ref, x_ref, acc_ref, sem,
                *, m, per_step):
    e, f = pl.program_id(0), pl.program_id(1)
    n_e, n_f = pl.num_programs(0), pl.num_programs(1)
    rows_per_expert = stage_ref.shape[0]

    def row_copy(base, j):
        return pltpu.make_async_copy(h_hbm.at[pl.ds(idx_ref[base + j], 1)],
                                     stage_ref.at[pl.ds(j, 1)], sem.at[0])

    def wait_rows():
        pltpu.make_async_copy(h_hbm.at[pl.ds(0, rows_per_expert)], stage_ref, sem.at[0]).wait()

    @pl.when((e == 0) & (f == 0))
    def _():
        def start_row(j, carry):
            row_copy(0, j).start()
            return carry
        lax.fori_loop(0, rows_per_expert, start_row, 0)

    @pl.when(f == 0)
    def _():
        wait_rows()
        x_ref[...] = stage_ref[:m].reshape(x_ref.shape).astype(BF16)
        acc_ref[...] = jnp.zeros_like(acc_ref)

    nxt = jnp.minimum(e + 1, n_e - 1) * rows_per_expert
    for i in range(per_step):
        row_copy(nxt, f * per_step + i).start()

    x = x_ref[...]
    a = _dot(x, wg_ref[0].astype(BF16))
    u = _dot(x, wu_ref[0].astype(BF16))
    mid = (a * _sigmoid(a) * u).astype(BF16)
    acc_ref[...] += _dot(mid, wd_ref[0].astype(BF16))

    @pl.when(f == n_f - 1)
    def _():
        gate = _row_to_cols(gate_ref[0])
        for c in range(0, y_ref.shape[2], LANES):
            y_ref[0, :, c:c + LANES] = (acc_ref[:, c:c + LANES] * gate).astype(BF16)

    @pl.when((e == n_e - 1) & (f == n_f - 1))
    def _():
        wait_rows()


def _ffn(idx, gate, h_rows, wg, wu, wd):
    n_e, m = idx.shape
    d = h_rows.shape[1] * LANES
    ff = wg.shape[2]
    tf = _tile(ff, 256)
    n_f = ff // tf
    per_step = -(-m // (8 * n_f)) * 8
    idx = jnp.pad(idx, ((0, 0), (0, n_f * per_step - m))).reshape(-1)
    return pl.pallas_call(
        functools.partial(_ffn_kernel, m=m, per_step=per_step),
        grid_spec=pltpu.PrefetchScalarGridSpec(
            num_scalar_prefetch=1,
            grid=(n_e, n_f),
            in_specs=[pl.BlockSpec(memory_space=pl.ANY),
                      pl.BlockSpec((1, 1, m), lambda e, f, idx: (e, 0, 0)),
                      pl.BlockSpec((1, d, tf), lambda e, f, idx: (e, 0, f)),
                      pl.BlockSpec((1, d, tf), lambda e, f, idx: (e, 0, f)),
                      pl.BlockSpec((1, tf, d), lambda e, f, idx: (e, f, 0))],
            out_specs=pl.BlockSpec((1, m, d), lambda e, f, idx: (e, 0, 0)),
            scratch_shapes=[pltpu.VMEM((n_f * per_step, d // LANES, LANES), F32),
                            pltpu.VMEM((m, d), BF16),
                            pltpu.VMEM((m, d), F32),
                            pltpu.SemaphoreType.DMA((1,))]),
        out_shape=jax.ShapeDtypeStruct((n_e, m, d), BF16),
        compiler_params=_params("arbitrary", "arbitrary"),
    )(idx, h_rows, gate, wg, wu, wd)


def _combine_kernel(start_ref, y_ref, pos_ref, xl_ref, g2_ref, fw_ref, o_ref, yc_ref, acc_ref,
                    *, n_experts, cap, n_b, win):
    i, j = pl.program_id(0), pl.program_id(1)
    tt = xl_ref.shape[1]
    blocks_per_tile = tt // LANES
    lane = lax.broadcasted_iota(jnp.int32, (tt, win), 1).astype(F32)
    r_i = lax.broadcasted_iota(jnp.int32, (2 * n_experts, n_experts * win), 0)
    c_e = lax.broadcasted_iota(jnp.int32, (2 * n_experts, n_experts * win), 1) // win
    spread = jnp.where(r_i == c_e, 16.0, jnp.where(r_i == c_e + n_experts, 1.0, 0.0)).astype(BF16)
    slots = _dot(pos_ref[0], spread)
    last_tile = j == pl.num_programs(1) - 1
    first, n_pass = [], 0
    for e in range(n_experts):
        row = (i * n_experts + e) * n_b
        lo = start_ref[row + j * blocks_per_tile]
        hi = jnp.where(last_tile, cap, start_ref[row + jnp.minimum((j + 1) * blocks_per_tile, n_b - 1)])
        w0 = jnp.minimum((lo // 16) * 16, cap - win)
        first.append(w0)
        n_pass = jnp.maximum(n_pass, jnp.where(hi > lo, (hi - w0 + win - 1) // win, 0))
    group = min(n_experts, COMBINE_GROUP)

    def add_pass(p, acc):
        for g0 in range(0, n_experts, group):
            hits = []
            for e in range(g0, g0 + group):
                done = first[e] + p * win
                off = pl.multiple_of(jnp.minimum(done, cap - win), 16)
                yc_ref[e * win:(e + 1) * win, :] = y_ref[e, 0, pl.ds(off, win), :]
                pe = slots[:, e * win:(e + 1) * win]
                hits.append(((pe - off.astype(F32)) == lane) & (pe >= done.astype(F32)))
            onehot = jnp.concatenate(hits, axis=1).astype(F32).astype(BF16)
            acc = acc + _dot(onehot, yc_ref[g0 * win:(g0 + group) * win, :])
        return acc

    acc_ref[...] = add_pass(0, jnp.zeros(acc_ref.shape, F32))

    def extra_pass(p, carry):
        acc_ref[...] = add_pass(p, acc_ref[...])
        return carry

    lax.fori_loop(1, n_pass, extra_pass, 0)
    xl = xl_ref[0] + g2_ref[0] * acc_ref[...]
    ms = jnp.mean(xl * xl, axis=-1, keepdims=True)
    o_ref[0] = xl * lax.rsqrt(ms + EPS) * fw_ref[...]


def _combine(starts, y, pos_col, xl, g2, fw, cap):
    b, t, d = xl.shape
    n_e = y.shape[0]
    tt = _tile(t, 512)
    win = min(cap, COMBINE_WINDOW)
    assert cap % 16 == 0 and win % 16 == 0 and tt % LANES == 0 and cap + win <= UNSELECTED_SLOT
    tok = lambda i, j, st: (i, j, 0)
    return pl.pallas_call(
        functools.partial(_combine_kernel, n_experts=n_e, cap=cap, n_b=t // LANES, win=win),
        grid_spec=pltpu.PrefetchScalarGridSpec(
            num_scalar_prefetch=1,
            grid=(b, t // tt),
            in_specs=[pl.BlockSpec((n_e, 1, cap, d), lambda i, j, st: (0, i, 0, 0)),
                      pl.BlockSpec((1, tt, 2 * n_e), tok),
                      pl.BlockSpec((1, tt, d), tok),
                      pl.BlockSpec((1, 1, d), lambda i, j, st: (i, 0, 0)),
                      pl.BlockSpec((1, d), lambda i, j, st: (0, 0))],
            out_specs=pl.BlockSpec((1, tt, d), tok),
            scratch_shapes=[pltpu.VMEM((n_e * win, d), BF16), pltpu.VMEM((tt, d), F32)]),
        out_shape=jax.ShapeDtypeStruct((b, t, d), F32),
        compiler_params=_params("arbitrary", "arbitrary"),
    )(starts, y, pos_col, xl, g2, fw)


def _rope_tables(t):
    n_freq = HEAD_DIM // 4
    rows = jnp.repeat(jnp.arange(t // GRID_W, dtype=F32), GRID_W)
    cols = jnp.tile(jnp.arange(GRID_W, dtype=F32), t // GRID_W)
    inv_freq = ROPE_BASE ** (-jnp.arange(n_freq, dtype=F32) / n_freq)
    ang_r, ang_c = rows[:, None] * inv_freq, cols[:, None] * inv_freq
    cos = jnp.concatenate([jnp.cos(ang_r)] * 2 + [jnp.cos(ang_c)] * 2, axis=-1)
    sin = jnp.concatenate([-jnp.sin(ang_r), jnp.sin(ang_r), -jnp.sin(ang_c), jnp.sin(ang_c)], axis=-1)
    return jnp.tile(cos, (1, N_HEADS)), jnp.tile(sin, (1, N_HEADS))


def kernel(x, c, ctx, c_ctx, w_ada, b_ada, norm1_w, norm2_w, w_in, mlstm_gate_bias, ret_decay_logit,
           mlstm_norm_w, ret_norm_w, w_out, w_router, w_gate, w_up, w_down, final_norm_w):
    B, T, D = x.shape
    Tc = ctx.shape[1]
    H = N_HEADS
    width = H * HEAD_DIM
    assert D == 2 * width and w_ada.shape[0] == 1
    assert T % CHUNK == 0 and Tc % CHUNK == 0 and T % GRID_W == 0
    n_lat, n_ctx = T // CHUNK, Tc // CHUNK
    n_experts = w_router.shape[2]
    cap = EC_CAPACITY_FACTOR * T // n_experts
    n_gate = 4 * H

    pad = (-(B + 1)) % 8
    cc = jnp.concatenate([c, c_ctx[None], jnp.zeros((pad, D), F32)], axis=0)
    mod = _ada(cc, w_ada[0], b_ada[0])
    sh1, sc1, g1, sh2, sc2, g2 = [mod[:B, None, i * D:(i + 1) * D] for i in range(6)]
    csh1, csc1 = [jnp.broadcast_to(mod[B, i * D:(i + 1) * D], (B, 1, D)) for i in range(2)]

    assert (4 * width + n_gate) % 8 == 0 and w_in.shape[2] == 8 * width + n_gate
    w_t = jnp.swapaxes(w_in, 1, 2)
    gb = jnp.pad(mlstm_gate_bias[0], (0, LANES - n_gate)).reshape(1, LANES)
    nw1 = norm1_w[0].reshape(1, D)
    cos, sin = _rope_tables(T)
    pm, gl, pr = _inproj(x, nw1, sh1, sc1, w_t, gb, cos, sin, groups=(0, 1, 2, 3))
    pmc, gc, prc = _inproj(ctx, nw1, csh1, csc1, w_t, gb,
                           jnp.ones((Tc, width), F32), jnp.zeros((Tc, width), F32), groups=(1, 2))

    ym = _mlstm(pm, pmc, _gateprep(gl, gc), mlstm_norm_w[0].reshape(H, 1, HEAD_DIM))
    dl = jnp.broadcast_to(jnp.swapaxes(ret_decay_logit[0], 0, 1)[:, :, None], (H, 2, LANES))
    yr = _ret(pr, prc, dl, ret_norm_w[0].reshape(H, 1, HEAD_DIM))

    wr_hi = w_router[0].astype(BF16)
    wr_lo = (w_router[0] - wr_hi.astype(F32)).astype(BF16)
    wrt = jnp.pad(jnp.concatenate([wr_hi, wr_lo], axis=1), ((0, 0), (0, LANES - 2 * n_experts)))
    xl, h2, aff_t = _outproj(ym, yr, x, w_out[0].astype(BF16), g1, norm2_w[0].reshape(1, D), sh2, sc2,
                             wrt, n_experts)

    pos, idx, gate, starts = _route(aff_t, cap)
    by_expert = lambda a: jnp.swapaxes(a.reshape(B, n_experts, cap), 0, 1).reshape(n_experts, B * cap)
    y = _ffn(by_expert(idx), by_expert(gate)[:, None, :], h2.reshape(B * T, D // LANES, LANES),
             w_gate[0], w_up[0], w_down[0])
    pos_t = jnp.swapaxes(pos.reshape(B, n_experts, T), 1, 2)
    pos_t = jnp.where(pos_t < 0, UNSELECTED_SLOT, pos_t)
    pos_hl = jnp.concatenate([pos_t >> 4, pos_t & 15], axis=-1).astype(BF16)
    return _combine(starts[:, :, 0].reshape(-1), y.reshape(n_experts, B, cap, D), pos_hl, xl, g2,
                    final_norm_w.reshape(1, D), cap)
```

```python
import functools

import jax
import jax.numpy as jnp
from jax import lax
from jax.experimental import pallas as pl
from jax.experimental.pallas import tpu as pltpu

F32 = jnp.float32
BF16 = jnp.bfloat16

CHUNK = 128
GRID_W = 64
N_HEADS = 4
HEAD_DIM = 128
N_GATE = 4 * N_HEADS
ROPE_BASE = 10000.0
EPS = 1e-6
LOG2E = 1.4426950408889634
EC_CAPACITY_FACTOR = 2
LANES = 128
VMEM_LIMIT_BYTES = 56 * 1024 * 1024
COMBINE_WINDOW = 128
UNSELECTED_SLOT = 1023
COMBINE_GROUP = 4
MIXER_UNROLL = 32
GATE_UNROLL = 8


def _params(*sem):
    return pltpu.CompilerParams(dimension_semantics=sem, vmem_limit_bytes=VMEM_LIMIT_BYTES)


def _dot(a, b):
    return jnp.dot(a, b, preferred_element_type=F32)


def _dot_nt(a, b):
    return lax.dot_general(a, b, (((1,), (1,)), ((), ())), preferred_element_type=F32)


def _dot_tn(a, b):
    return lax.dot_general(a, b, (((0,), (0,)), ((), ())), preferred_element_type=F32)


def _sigmoid(x):
    return 1.0 / (1.0 + jnp.exp(-x))


def _log_sigmoid(x):
    return jnp.minimum(x, 0.0) - jnp.log1p(jnp.exp(-jnp.abs(x)))


def _col_to_row(col):
    return jnp.broadcast_to(col, (col.shape[0], LANES)).T[0:1, :]


def _row_to_cols(row):
    pieces = [jnp.broadcast_to(row[:, c:c + LANES], (LANES, LANES)).T for c in range(0, row.shape[1], LANES)]
    return pieces[0] if len(pieces) == 1 else jnp.concatenate(pieces, axis=0)


def _tile(n, pref):
    t = min(n, pref)
    assert n % t == 0, (n, t)
    return t


def _ada_kernel(c_ref, w_ref, b_ref, o_ref):
    cv = c_ref[...]
    s = (cv * _sigmoid(cv)).astype(BF16)
    o_ref[...] = _dot(s, w_ref[...].astype(BF16)) + b_ref[...]


def _ada(cc, w, b):
    rows, d = cc.shape
    n = w.shape[1]
    tn = _tile(n, 1024)
    return pl.pallas_call(
        _ada_kernel,
        grid=(n // tn,),
        in_specs=[pl.BlockSpec((rows, d), lambda j: (0, 0)),
                  pl.BlockSpec((d, tn), lambda j: (0, j)),
                  pl.BlockSpec((1, tn), lambda j: (0, j))],
        out_specs=pl.BlockSpec((rows, tn), lambda j: (0, j)),
        out_shape=jax.ShapeDtypeStruct((rows, n), F32),
        compiler_params=_params("arbitrary"),
    )(cc, w, b.reshape(1, n))


def _rms_mod(x, nw, sh, sc):
    ms = jnp.mean(x * x, axis=-1, keepdims=True)
    return (x * lax.rsqrt(ms + EPS) * nw) * (1.0 + sc) + sh


def _inproj_kernel(x_ref, nw_ref, sh_ref, sc_ref, wt_ref, gb_ref, cos_ref, sin_ref,
                   pm_ref, g_ref, pr_ref, w_ref, *, width, k_scale, groups):
    n_g = len(groups)
    ret0 = 4 * width + g_ref.shape[1]
    gate_col = n_g * width
    ret_col = gate_col + LANES

    @pl.when((pl.program_id(0) == 0) & (pl.program_id(1) == 0))
    def _():
        for slot, j in enumerate(groups):
            w_ref[:, slot * width:(slot + 1) * width] = wt_ref[j * width:(j + 1) * width, :].T.astype(BF16)
            w_ref[:, ret_col + slot * width:ret_col + (slot + 1) * width] = (
                wt_ref[ret0 + j * width:ret0 + (j + 1) * width, :].T.astype(BF16))
        w_ref[:, gate_col:ret_col] = wt_ref[4 * width:4 * width + LANES, :].T.astype(BF16)

    hb = _rms_mod(x_ref[0], nw_ref[...], sh_ref[0], sc_ref[0]).astype(BF16)
    for slot, j in enumerate(groups):
        acc = _dot(hb, w_ref[:, slot * width:(slot + 1) * width])
        if j == 1:
            acc = acc * k_scale
        for h in range(N_HEADS):
            pm_ref[0, slot * N_HEADS + h] = acc[:, h * HEAD_DIM:(h + 1) * HEAD_DIM].astype(BF16)
    g_ref[0] = (_dot(hb, w_ref[:, gate_col:ret_col]) + gb_ref[...]).T[:g_ref.shape[1]]
    cos = cos_ref[...]
    sin = sin_ref[...]
    lane = lax.broadcasted_iota(jnp.int32, cos.shape, 1)
    even = ((lane // 32) % 2) == 0
    for slot, j in enumerate(groups):
        acc = _dot(hb, w_ref[:, ret_col + slot * width:ret_col + (slot + 1) * width])
        if j == 1:
            acc = acc * k_scale
        if j < 2:
            swapped = jnp.where(even, pltpu.roll(acc, width - 32, 1), pltpu.roll(acc, 32, 1))
            acc = acc * cos + swapped * sin
        for h in range(N_HEADS):
            pr_ref[0, slot * N_HEADS + h] = acc[:, h * HEAD_DIM:(h + 1) * HEAD_DIM].astype(BF16)


def _inproj(x, nw, sh, sc, w_t, gb, cos, sin, groups):
    b, t, d = x.shape
    width = (w_t.shape[1] - N_GATE) // 8
    tm = _tile(t, 512)
    slabs = len(groups) * N_HEADS
    kern = functools.partial(_inproj_kernel, width=width, k_scale=HEAD_DIM ** -0.5, groups=groups)
    const = lambda i, j: (0, 0)
    return pl.pallas_call(
        kern,
        grid=(b, t // tm),
        in_specs=[pl.BlockSpec((1, tm, d), lambda i, j: (i, j, 0)),
                  pl.BlockSpec((1, d), const),
                  pl.BlockSpec((1, 1, d), lambda i, j: (i, 0, 0)),
                  pl.BlockSpec((1, 1, d), lambda i, j: (i, 0, 0)),
                  pl.BlockSpec((None,) + w_t.shape[1:], lambda i, j: (0, 0, 0), pipeline_mode=pl.Buffered(1)),
                  pl.BlockSpec((1, LANES), const),
                  pl.BlockSpec((tm, width), lambda i, j: (j, 0)),
                  pl.BlockSpec((tm, width), lambda i, j: (j, 0))],
        out_specs=[pl.BlockSpec((1, slabs, tm, HEAD_DIM), lambda i, j: (i, 0, j, 0)),
                   pl.BlockSpec((1, N_GATE, tm), lambda i, j: (i, 0, j)),
                   pl.BlockSpec((1, slabs, tm, HEAD_DIM), lambda i, j: (i, 0, j, 0))],
        out_shape=[jax.ShapeDtypeStruct((b, slabs, t, HEAD_DIM), BF16),
                   jax.ShapeDtypeStruct((b, N_GATE, t), F32),
                   jax.ShapeDtypeStruct((b, slabs, t, HEAD_DIM), BF16)],
        scratch_shapes=[pltpu.VMEM((d, 2 * len(groups) * width + LANES), BF16)],
        compiler_params=_params("arbitrary", "arbitrary"),
    )(x, nw, sh, sc, w_t, gb, cos, sin)


def _scan_lanes(x, op, fill, reverse):
    lane = lax.broadcasted_iota(jnp.int32, x.shape, 1)
    sh = 1
    while sh < LANES:
        if reverse:
            x = op(x, jnp.where(lane < LANES - sh, pltpu.roll(x, LANES - sh, 1), fill))
        else:
            x = op(x, jnp.where(lane >= sh, pltpu.roll(x, sh, 1), fill))
        sh *= 2
    return x


def _gateprep_kernel(gl_ref, gc_ref, o_ref, *, n_lat, n_ctx):
    n_b = gl_ref.shape[0]
    half = N_GATE // 2
    rows = n_b * half
    row = lax.broadcasted_iota(jnp.int32, (rows, LANES), 0)
    lane = lax.broadcasted_iota(jnp.int32, (rows, LANES), 1)
    is_fwd = (row % half) < N_HEADS
    last = lane == jnp.where(is_fwd, LANES - 1, 0)

    def chunk_rows(ref, lo, c_fwd, c_bwd):
        def at(c):
            cols = (slice(c * LANES, (c + 1) * LANES) if isinstance(c, int)
                    else pl.ds(pl.multiple_of(c * LANES, LANES), LANES))
            return jnp.concatenate([ref[s, lo:lo + half, cols] for s in range(n_b)], axis=0)
        return jnp.where(is_fwd, at(c_fwd), at(c_bwd))

    def step(c, ig, fg, m):
        lf = _log_sigmoid(fg)
        b = jnp.where(is_fwd, _scan_lanes(lf, jnp.add, 0.0, False), _scan_lanes(lf, jnp.add, 0.0, True))
        b_tot = jnp.sum(jnp.where(last, b, 0.0), axis=1, keepdims=True)
        a = ig - b
        m_new = jnp.maximum(b_tot + m, jnp.max(b_tot + a, axis=1, keepdims=True))
        w = jnp.exp(b_tot + a - m_new)
        decay = jnp.exp(b_tot + m - m_new)
        run = jnp.where(is_fwd, _scan_lanes(a, jnp.maximum, -jnp.inf, False),
                        _scan_lanes(a, jnp.maximum, -jnp.inf, True))
        mm = jnp.maximum(m, run)
        o_ref[c, 0] = a * LOG2E
        o_ref[c, 1] = w
        o_ref[c, 2] = mm * LOG2E
        o_ref[c, 3] = jnp.exp(-(b + mm))
        o_ref[c, 4] = jnp.broadcast_to(decay, (rows, LANES))
        o_ref[c, 5] = jnp.broadcast_to(m * LOG2E, (rows, LANES))
        return m_new

    m = jnp.zeros((rows, 1), F32)
    for c in range(n_ctx):
        m = step(c, chunk_rows(gc_ref, 0, c, n_ctx - 1 - c), chunk_rows(gc_ref, half, c, n_ctx - 1 - c), m)

    def latent(i, m):
        return step(n_ctx + i, chunk_rows(gl_ref, 0, i, n_lat - 1 - i),
                    chunk_rows(gl_ref, half, i, n_lat - 1 - i), m)

    lax.fori_loop(0, n_lat, latent, m, unroll=min(n_lat, GATE_UNROLL))


def _gateprep(gl, gc):
    n_lat, n_ctx = gl.shape[2] // CHUNK, gc.shape[2] // CHUNK
    return pl.pallas_call(
        functools.partial(_gateprep_kernel, n_lat=n_lat, n_ctx=n_ctx),
        out_shape=jax.ShapeDtypeStruct((n_ctx + n_lat, 6, gl.shape[0] * N_GATE // 2, LANES), F32),
        compiler_params=pltpu.CompilerParams(vmem_limit_bytes=VMEM_LIMIT_BYTES),
    )(gl, gc)


def _head_norm_gate(h, gate, nw):
    return gate * (h * lax.rsqrt(jnp.mean(h * h, axis=-1, keepdims=True) + EPS)) * nw


def _mlstm_kernel(q_ref, k_ref, v_ref, o_ref, kc_ref, vc_ref, st_ref, nw_ref, y_ref,
                  sf_ref, sb_ref, cf_ref, cb_ref, *, n_lat, n_ctx, unroll):
    L = CHUNK
    A, WT, MM, EN, DECAY, M0 = range(6)
    ones = jnp.ones((L, HEAD_DIM), BF16)
    stream_f = pl.program_id(0) * (N_GATE // 2) + pl.program_id(1)
    stream_b = stream_f + N_HEADS

    def stat(step, plane, stream):
        return st_ref[step, plane, pl.ds(stream, 1), :]

    def to_rows(row):
        return jnp.broadcast_to(row, (L, L)).T

    def update(state_ref, k, v, step, stream):
        wkt = (k.astype(F32).T * stat(step, WT, stream)).astype(BF16)
        u = _dot(wkt, jnp.concatenate([v, ones], axis=1))
        decay = stat(step, DECAY, stream)
        state_ref[...] = jnp.concatenate([decay, decay], axis=1) * state_ref[...] + u

    cf_ref[...] = jnp.zeros_like(cf_ref)
    cb_ref[...] = jnp.zeros_like(cb_ref)
    for c in range(n_ctx):
        rf = slice(c * L, (c + 1) * L)
        rb = slice((n_ctx - 1 - c) * L, (n_ctx - c) * L)
        update(cf_ref, kc_ref[0, rf, :], vc_ref[0, rf, :], c, stream_f)
        update(cb_ref, kc_ref[0, rb, :], vc_ref[0, rb, :], c, stream_b)

    def state_body(i, carry):
        jb = n_lat - 1 - i
        rf = pl.ds(pl.multiple_of(i * L, L), L)
        rb = pl.ds(pl.multiple_of(jb * L, L), L)
        sf_ref[i] = cf_ref[...].astype(BF16)
        sb_ref[jb] = cb_ref[...].astype(BF16)
        update(cf_ref, k_ref[0, rf, :], v_ref[0, rf, :], n_ctx + i, stream_f)
        update(cb_ref, k_ref[0, rb, :], v_ref[0, rb, :], n_ctx + i, stream_b)
        return carry

    lax.fori_loop(0, n_lat, state_body, 0, unroll=unroll)

    r_i = lax.broadcasted_iota(jnp.int32, (L, L), 0)
    c_i = lax.broadcasted_iota(jnp.int32, (L, L), 1)
    tril = c_i <= r_i
    triu = c_i >= r_i
    nw = nw_ref[0]

    def out_body(j, carry):
        rows = pl.ds(pl.multiple_of(j * L, L), L)
        step_f = n_ctx + j
        step_b = n_ctx + n_lat - 1 - j
        q = q_ref[0, rows, :]
        k = k_ref[0, rows, :]
        v = v_ref[0, rows, :]
        mm_f = to_rows(stat(step_f, MM, stream_f))
        mm_b = to_rows(stat(step_b, MM, stream_b))
        s = _dot_nt(q, k)
        d_f = jnp.where(tril, jnp.exp2(stat(step_f, A, stream_f) - mm_f), 0.0)
        d_b = jnp.where(triu, jnp.exp2(stat(step_b, A, stream_b) - mm_b), 0.0)
        iw_f = jnp.exp2(stat(step_f, M0, stream_f) - mm_f)
        iw_b = jnp.exp2(stat(step_b, M0, stream_b) - mm_b)
        qf = q.astype(F32)
        vext = jnp.concatenate([v, ones], axis=1)
        lhs_f = jnp.concatenate([(s * d_f).astype(BF16), (qf * iw_f).astype(BF16)], axis=1)
        lhs_b = jnp.concatenate([(s * d_b).astype(BF16), (qf * iw_b).astype(BF16)], axis=1)
        tot_f = _dot(lhs_f, jnp.concatenate([vext, sf_ref[j]], axis=0))
        tot_b = _dot(lhs_b, jnp.concatenate([vext, sb_ref[j]], axis=0))
        D = HEAD_DIM
        h = (tot_f[:, :D] / jnp.maximum(jnp.abs(tot_f[:, D:]), to_rows(stat(step_f, EN, stream_f)))
             + tot_b[:, :D] / jnp.maximum(jnp.abs(tot_b[:, D:]), to_rows(stat(step_b, EN, stream_b))))
        gate = _sigmoid(o_ref[0, rows, :].astype(F32))
        y_ref[0, rows, :] = _head_norm_gate(h, gate, nw).astype(BF16)
        return carry

    lax.fori_loop(0, n_lat, out_body, 0, unroll=unroll)


def _head_slab(t, group):
    return pl.BlockSpec((None, 1, t, HEAD_DIM), lambda i, h: (i, group * N_HEADS + h, 0, 0))


def _mlstm(pm, pmc, stats, nw):
    b, _, t, _ = pm.shape
    tc = pmc.shape[2]
    n_lat, n_ctx = t // CHUNK, tc // CHUNK
    H = N_HEADS
    blk = lambda off: _head_slab(t, off // H)
    cblk = lambda off: _head_slab(tc, off // H - 1)
    return pl.pallas_call(
        functools.partial(_mlstm_kernel, n_lat=n_lat, n_ctx=n_ctx, unroll=min(n_lat, MIXER_UNROLL)),
        grid=(b, H),
        in_specs=[blk(0), blk(H), blk(2 * H), blk(3 * H), cblk(H), cblk(2 * H),
                  pl.BlockSpec(stats.shape, lambda i, h: (0, 0, 0, 0)),
                  pl.BlockSpec((1, 1, HEAD_DIM), lambda i, h: (h, 0, 0))],
        out_specs=_head_slab(t, 0),
        out_shape=jax.ShapeDtypeStruct((b, H, t, HEAD_DIM), BF16),
        scratch_shapes=[pltpu.VMEM((n_lat, HEAD_DIM, HEAD_DIM + LANES), BF16),
                        pltpu.VMEM((n_lat, HEAD_DIM, HEAD_DIM + LANES), BF16),
                        pltpu.VMEM((HEAD_DIM, HEAD_DIM + LANES), F32),
                        pltpu.VMEM((HEAD_DIM, HEAD_DIM + LANES), F32)],
        compiler_params=_params("arbitrary", "arbitrary"),
    )(pm, pm, pm, pm, pmc, pmc, stats, nw)


def _ret_kernel(q_ref, k_ref, v_ref, o_ref, kc_ref, vc_ref, dl_ref, nw_ref, y_ref,
                sf_ref, sb_ref, rf_ref, rb_ref, *, n_lat, n_ctx, unroll):
    L = CHUNK
    lg = _log_sigmoid(dl_ref[0])
    lg_f, lg_b = lg[0:1, :], lg[1:2, :]
    r_i = lax.broadcasted_iota(jnp.int32, (L, L), 0)
    c_i = lax.broadcasted_iota(jnp.int32, (L, L), 1)
    r_f = r_i.astype(F32)
    rel = (r_i - c_i).astype(F32)
    kw_f = jnp.exp((L - 1.0 - r_f) * lg_f)
    kw_b = jnp.exp(r_f * lg_b)
    in_f = jnp.exp((r_f + 1.0) * lg_f)
    in_b = jnp.exp((L - r_f) * lg_b)
    dch_f = jnp.exp(L * lg_f)
    dch_b = jnp.exp(L * lg_b)
    d_mat = (jnp.where(rel >= 0, jnp.exp(jnp.maximum(rel, 0.0) * lg_f), 0.0)
             + jnp.where(rel <= 0, jnp.exp(jnp.maximum(-rel, 0.0) * lg_b), 0.0))

    def update(state_ref, k, v, kw, dch):
        wk = (k.astype(F32) * kw).astype(BF16)
        state_ref[...] = dch * state_ref[...] + _dot_tn(wk, v)

    rf_ref[...] = jnp.zeros_like(rf_ref)
    rb_ref[...] = jnp.zeros_like(rb_ref)
    for c in range(n_ctx):
        rows = slice(c * L, (c + 1) * L)
        update(rf_ref, kc_ref[0, rows, :], vc_ref[0, rows, :], kw_f, dch_f)
    for c in reversed(range(n_ctx)):
        rows = slice(c * L, (c + 1) * L)
        update(rb_ref, kc_ref[0, rows, :], vc_ref[0, rows, :], kw_b, dch_b)

    def state_body(i, carry):
        jb = n_lat - 1 - i
        rowf = pl.ds(pl.multiple_of(i * L, L), L)
        rowb = pl.ds(pl.multiple_of(jb * L, L), L)
        sf_ref[i] = rf_ref[...].astype(BF16)
        sb_ref[jb] = rb_ref[...].astype(BF16)
        update(rf_ref, k_ref[0, rowf, :], v_ref[0, rowf, :], kw_f, dch_f)
        update(rb_ref, k_ref[0, rowb, :], v_ref[0, rowb, :], kw_b, dch_b)
        return carry

    lax.fori_loop(0, n_lat, state_body, 0, unroll=unroll)
    nw = nw_ref[0]

    def out_body(j, carry):
        rows = pl.ds(pl.multiple_of(j * L, L), L)
        q = q_ref[0, rows, :]
        s = _dot_nt(q, k_ref[0, rows, :])
        intra = _dot((s * d_mat).astype(BF16), v_ref[0, rows, :])
        inter = _dot(q, jnp.concatenate([sf_ref[j], sb_ref[j]], axis=1))
        out = intra + in_f * inter[:, :HEAD_DIM] + in_b * inter[:, HEAD_DIM:]
        og = o_ref[0, rows, :].astype(F32)
        y_ref[0, rows, :] = _head_norm_gate(out, og * _sigmoid(og), nw).astype(BF16)
        return carry

    lax.fori_loop(0, n_lat, out_body, 0, unroll=unroll)


def _ret(pr, prc, dl, nw):
    b, _, t, _ = pr.shape
    tc = prc.shape[2]
    n_lat, n_ctx = t // CHUNK, tc // CHUNK
    H = N_HEADS
    blk = lambda off: _head_slab(t, off // H)
    cblk = lambda off: _head_slab(tc, off // H - 1)
    return pl.pallas_call(
        functools.partial(_ret_kernel, n_lat=n_lat, n_ctx=n_ctx, unroll=min(n_lat, MIXER_UNROLL)),
        grid=(b, H),
        in_specs=[blk(0), blk(H), blk(2 * H), blk(3 * H), cblk(H), cblk(2 * H),
                  pl.BlockSpec((1, 2, LANES), lambda i, h: (h, 0, 0)),
                  pl.BlockSpec((1, 1, HEAD_DIM), lambda i, h: (h, 0, 0))],
        out_specs=_head_slab(t, 0),
        out_shape=jax.ShapeDtypeStruct((b, H, t, HEAD_DIM), BF16),
        scratch_shapes=[pltpu.VMEM((n_lat, HEAD_DIM, HEAD_DIM), BF16),
                        pltpu.VMEM((n_lat, HEAD_DIM, HEAD_DIM), BF16),
                        pltpu.VMEM((HEAD_DIM, HEAD_DIM), F32),
                        pltpu.VMEM((HEAD_DIM, HEAD_DIM), F32)],
        compiler_params=_params("arbitrary", "arbitrary"),
    )(pr, pr, pr, pr, prc, prc, dl, nw)


def _outproj_kernel(ym_ref, yr_ref, x_ref, wo_ref, g1_ref, nw_ref, sh_ref, sc_ref, wrt_ref,
                    xl_ref, h2_ref, aff_ref, *, n_experts):
    heads = [ym_ref[0, h] for h in range(N_HEADS)] + [yr_ref[0, h] for h in range(N_HEADS)]
    y = _dot(jnp.concatenate(heads, axis=1), wo_ref[...])
    xl = x_ref[0] + g1_ref[0] * y
    xl_ref[0] = xl
    h2 = _rms_mod(xl, nw_ref[...], sh_ref[0], sc_ref[0])
    h2_ref[0] = h2.reshape(h2.shape[0], h2.shape[1] // LANES, LANES)
    h_hi = h2.astype(BF16)
    h_lo = (h2 - h_hi.astype(F32)).astype(BF16)
    p_hi = _dot(h_hi, wrt_ref[...])
    p_lo = _dot(h_lo, wrt_ref[...])
    logits = p_hi + pltpu.roll(p_hi, LANES - n_experts, 1) + p_lo
    lane = lax.broadcasted_iota(jnp.int32, logits.shape, 1)
    logits = jnp.where(lane < n_experts, logits, -jnp.inf)
    e = jnp.exp(logits - jnp.max(logits, axis=-1, keepdims=True))
    aff_t = (e / jnp.sum(e, axis=-1, keepdims=True)).T[:n_experts]
    aff_ref[0] = aff_t.reshape(n_experts, aff_t.shape[1] // LANES, LANES)


def _outproj(ym, yr, x, wo, g1, nw, sh, sc, wrt, n_experts):
    b, t, d = x.shape
    tm = _tile(t, 8 * LANES)
    slab = pl.BlockSpec((1, N_HEADS, tm, HEAD_DIM), lambda i, j: (i, 0, j, 0))
    const = lambda i, j: (0, 0)
    per_b = lambda i, j: (i, 0, 0)
    tok = lambda i, j: (i, j, 0)
    return pl.pallas_call(
        functools.partial(_outproj_kernel, n_experts=n_experts),
        grid=(b, t // tm),
        in_specs=[slab, slab,
                  pl.BlockSpec((1, tm, d), tok), pl.BlockSpec(wo.shape, const),
                  pl.BlockSpec((1, 1, d), per_b), pl.BlockSpec((1, d), const),
                  pl.BlockSpec((1, 1, d), per_b), pl.BlockSpec((1, 1, d), per_b),
                  pl.BlockSpec(wrt.shape, const)],
        out_specs=[pl.BlockSpec((1, tm, d), tok),
                   pl.BlockSpec((1, tm, d // LANES, LANES), lambda i, j: (i, j, 0, 0)),
                   pl.BlockSpec((1, n_experts, tm // LANES, LANES), lambda i, j: (i, 0, j, 0))],
        out_shape=[jax.ShapeDtypeStruct((b, t, d), F32),
                   jax.ShapeDtypeStruct((b, t, d // LANES, LANES), F32),
                   jax.ShapeDtypeStruct((b, n_experts, t // LANES, LANES), F32)],
        compiler_params=_params("arbitrary", "arbitrary"),
    )(ym, yr, x, wo, g1, nw, sh, sc, wrt)


def _select_top(a, cap):
    n_e, n_b, _ = a.shape

    ones = jnp.ones((LANES, LANES), BF16)

    def count(mask):
        per_block = _dot(mask.astype(F32).astype(BF16).reshape(n_e * n_b, LANES), ones)
        return jnp.sum(per_block.reshape(n_e, n_b, LANES), axis=1, keepdims=True).astype(jnp.int32)

    def search(i, thr):
        cand = thr | jnp.left_shift(jnp.int32(1), 30 - i)
        return jnp.where(count(a >= pltpu.bitcast(cand, F32)) >= cap, cand, thr)

    thr = lax.fori_loop(0, 31, search, jnp.zeros((n_e, 1, LANES), jnp.int32))
    above = a >= pltpu.bitcast(thr + 1, F32)
    band = (a >= pltpu.bitcast(thr, F32)) & jnp.logical_not(above)
    need = cap - count(above)[:, :, 0:1]
    tok = (lax.broadcasted_iota(jnp.int32, a.shape, 1) * LANES
           + lax.broadcasted_iota(jnp.int32, a.shape, 2))

    def take_one(_, carry):
        sel, band, need = carry
        in_band = band > 0
        best = jnp.max(jnp.max(jnp.where(in_band, a, -1.0), axis=2, keepdims=True), axis=1, keepdims=True)
        cand = jnp.where(in_band & (a == best), tok, n_b * LANES)
        first = jnp.min(jnp.min(cand, axis=2, keepdims=True), axis=1, keepdims=True)
        take = ((tok == first) & (need > 0)).astype(jnp.int32)
        return sel | take, band - take, need - 1

    sel, _, _ = lax.fori_loop(0, jnp.max(need), take_one,
                              (above.astype(jnp.int32), band.astype(jnp.int32), need))
    return sel


def _route_kernel(aff_ref, pos_ref, idx_ref, gate_ref, start_ref, sel_ref, *, cap):
    i = pl.program_id(0)
    n_s, n_e, n_b, _ = aff_ref.shape

    @pl.when(i == 0)
    def _():
        sel_all = _select_top(aff_ref[...].reshape(n_s * n_e, n_b, LANES), cap)
        sel_ref[...] = sel_all.reshape(n_s, n_e, n_b, LANES)

    a = aff_ref[i]
    sel = sel_ref[i] > 0

    rows = n_e * n_b
    u_i = lax.broadcasted_iota(jnp.int32, (LANES, LANES), 0)
    s_i = lax.broadcasted_iota(jnp.int32, (LANES, LANES), 1)
    incl = (u_i <= s_i).astype(BF16)
    ones = jnp.ones((LANES, LANES), BF16)
    r_i = lax.broadcasted_iota(jnp.int32, (rows, rows), 0)
    c_i = lax.broadcasted_iota(jnp.int32, (rows, rows), 1)
    before = ((r_i // n_b == c_i // n_b) & (c_i < r_i)).astype(BF16)

    x = sel.astype(F32).reshape(rows, LANES)
    xb = x.astype(BF16)
    block_tot = _dot(xb, ones).astype(BF16)
    start = _dot(before, block_tot)
    pos = jnp.where(x > 0, start + _dot(xb, incl) - x, -1.0).astype(jnp.int32)
    pos_ref[0] = pos.reshape(n_e, n_b, LANES)
    start_ref[0] = start.astype(jnp.int32)

    pad = (-rows) % LANES
    def pad_rows(m):
        return m if pad == 0 else jnp.concatenate([m, jnp.zeros((pad, LANES), m.dtype)], axis=0)
    start_p = pad_rows(start)
    end_p = pad_rows(start + block_tot.astype(F32))
    pos_p = pad_rows(pos)
    pos_p = jnp.where(pos_p < 0, 1023, pos_p)
    a_p = pad_rows(a.reshape(rows, LANES))
    a_hi = a_p.astype(BF16)
    a_mid = (a_p - a_hi.astype(F32)).astype(BF16)
    a_lo = (a_p - a_hi.astype(F32) - a_mid.astype(F32)).astype(BF16)
    per_group = LANES // n_b
    j_col = lax.broadcasted_iota(jnp.int32, (cap, LANES), 0).astype(F32)
    lane = lax.broadcasted_iota(jnp.int32, (cap, LANES), 1)
    lane_f = lane.astype(F32)
    block_f = (lane % n_b).astype(F32)
    row0 = pl.program_id(0) * (n_b * LANES)
    for g in range((rows + pad) // LANES):
        rs = slice(g * LANES, (g + 1) * LANES)
        s_row = start_p[rs].T[0:1, :]
        e_row = end_p[rs].T[0:1, :]
        hi = (pos_p[rs] >> 4).astype(F32).astype(BF16)
        lo = (pos_p[rs] & 15).astype(F32).astype(BF16)
        for el in range(per_group):
            e = g * per_group + el
            if e >= n_e:
                break
            mine = (j_col >= s_row) & (j_col < e_row) & ((lane // n_b) == el)
            mine_b = mine.astype(F32).astype(BF16)
            slots = 16.0 * _dot(mine_b, hi) + _dot(mine_b, lo)
            hit = slots == j_col
            in_block = jnp.sum(jnp.where(hit, lane_f, 0.0), axis=1, keepdims=True)
            block = jnp.sum(jnp.where(mine, block_f, 0.0), axis=1, keepdims=True)
            a_blk = _dot(mine_b, a_hi[rs]) + _dot(mine_b, a_mid[rs]) + _dot(mine_b, a_lo[rs])
            gate = jnp.sum(jnp.where(hit, a_blk, 0.0), axis=1, keepdims=True)
            idx_ref[0, e] = _col_to_row(block * LANES + in_block).astype(jnp.int32) + row0
            gate_ref[0, e] = _col_to_row(gate)


def _route(aff_t, cap):
    b, n_e, n_b, _ = aff_t.shape
    spec = pl.BlockSpec((1, n_e, n_b, LANES), lambda i: (i, 0, 0, 0))
    slot_spec = pl.BlockSpec((1, n_e, 1, cap), lambda i: (i, 0, 0, 0))
    return pl.pallas_call(
        functools.partial(_route_kernel, cap=cap),
        grid=(b,), in_specs=[pl.BlockSpec(aff_t.shape, lambda i: (0, 0, 0, 0))],
        out_specs=[spec, slot_spec, slot_spec, pl.BlockSpec((1, n_e * n_b, LANES), lambda i: (i, 0, 0))],
        scratch_shapes=[pltpu.VMEM(aff_t.shape, jnp.int32)],
        out_shape=[jax.ShapeDtypeStruct(aff_t.shape, jnp.int32),
                   jax.ShapeDtypeStruct((b, n_e, 1, cap), jnp.int32),
                   jax.ShapeDtypeStruct((b, n_e, 1, cap), F32),
                   jax.ShapeDtypeStruct((b, n_e * n_b, LANES), jnp.int32)],
        compiler_params=_params("arbitrary"),
    )(aff_t)


def _ffn_kernel(idx_ref, h_hbm, gate_ref, wg_ref, wu_ref, wd_ref, y_ref, stage_ref, x_ref, acc_ref, sem,
                *, m, per_step):
    e, f = pl.program_id(0), pl.program_id(1)
    n_e, n_f = pl.num_programs(0), pl.num_programs(1)
    rows_per_expert = stage_ref.shape[0]

    def row_copy(base, j):
        return pltpu.make_async_copy(h_hbm.at[pl.ds(idx_ref[base + j], 1)],
                                     stage_ref.at[pl.ds(j, 1)], sem.at[0])

    def wait_rows():
        pltpu.make_async_copy(h_hbm.at[pl.ds(0, rows_per_expert)], stage_ref, sem.at[0]).wait()

    @pl.when((e == 0) & (f == 0))
    def _():
        def start_row(j, carry):
            row_copy(0, j).start()
            return carry
        lax.fori_loop(0, rows_per_expert, start_row, 0)

    @pl.when(f == 0)
    def _():
        wait_rows()
        x_ref[...] = stage_ref[:m].reshape(x_ref.shape).astype(BF16)
        acc_ref[...] = jnp.zeros_like(acc_ref)

    nxt = jnp.minimum(e + 1, n_e - 1) * rows_per_expert
    for i in range(per_step):
        row_copy(nxt, f * per_step + i).start()

    x = x_ref[...]
    a = _dot(x, wg_ref[0].astype(BF16))
    u = _dot(x, wu_ref[0].astype(BF16))
    mid = (a * _sigmoid(a) * u).astype(BF16)
    acc_ref[...] += _dot(mid, wd_ref[0].astype(BF16))

    @pl.when(f == n_f - 1)
    def _():
        gate = _row_to_cols(gate_ref[0])
        for c in range(0, y_ref.shape[2], LANES):
            y_ref[0, :, c:c + LANES] = (acc_ref[:, c:c + LANES] * gate).astype(BF16)

    @pl.when((e == n_e - 1) & (f == n_f - 1))
    def _():
        wait_rows()


def _ffn(idx, gate, h_rows, wg, wu, wd):
    n_e, m = idx.shape
    d = h_rows.shape[1] * LANES
    ff = wg.shape[2]
    tf = _tile(ff, 256)
    n_f = ff // tf
    per_step = -(-m // (8 * n_f)) * 8
    idx = jnp.pad(idx, ((0, 0), (0, n_f * per_step - m))).reshape(-1)
    return pl.pallas_call(
        functools.partial(_ffn_kernel, m=m, per_step=per_step),
        grid_spec=pltpu.PrefetchScalarGridSpec(
            num_scalar_prefetch=1,
            grid=(n_e, n_f),
            in_specs=[pl.BlockSpec(memory_space=pl.ANY),
                      pl.BlockSpec((1, 1, m), lambda e, f, idx: (e, 0, 0)),
                      pl.BlockSpec((1, d, tf), lambda e, f, idx: (e, 0, f)),
                      pl.BlockSpec((1, d, tf), lambda e, f, idx: (e, 0, f)),
                      pl.BlockSpec((1, tf, d), lambda e, f, idx: (e, f, 0))],
            out_specs=pl.BlockSpec((1, m, d), lambda e, f, idx: (e, 0, 0)),
            scratch_shapes=[pltpu.VMEM((n_f * per_step, d // LANES, LANES), F32),
                            pltpu.VMEM((m, d), BF16),
                            pltpu.VMEM((m, d), F32),
                            pltpu.SemaphoreType.DMA((1,))]),
        out_shape=jax.ShapeDtypeStruct((n_e, m, d), BF16),
        compiler_params=_params("arbitrary", "arbitrary"),
    )(idx, h_rows, gate, wg, wu, wd)


def _combine_kernel(start_ref, y_ref, pos_ref, xl_ref, g2_ref, fw_ref, o_ref, yc_ref, acc_ref,
                    *, n_experts, cap, n_b, win):
    i, j = pl.program_id(0), pl.program_id(1)
    tt = xl_ref.shape[1]
    blocks_per_tile = tt // LANES
    lane = lax.broadcasted_iota(jnp.int32, (tt, win), 1).astype(F32)
    r_i = lax.broadcasted_iota(jnp.int32, (2 * n_experts, n_experts * win), 0)
    c_e = lax.broadcasted_iota(jnp.int32, (2 * n_experts, n_experts * win), 1) // win
    spread = jnp.where(r_i == c_e, 16.0, jnp.where(r_i == c_e + n_experts, 1.0, 0.0)).astype(BF16)
    slots = _dot(pos_ref[0], spread)
    last_tile = j == pl.num_programs(1) - 1
    first, n_pass = [], 0
    for e in range(n_experts):
        row = (i * n_experts + e) * n_b
        lo = start_ref[row + j * blocks_per_tile]
        hi = jnp.where(last_tile, cap, start_ref[row + jnp.minimum((j + 1) * blocks_per_tile, n_b - 1)])
        w0 = jnp.minimum((lo // 16) * 16, cap - win)
        first.append(w0)
        n_pass = jnp.maximum(n_pass, jnp.where(hi > lo, (hi - w0 + win - 1) // win, 0))
    group = min(n_experts, COMBINE_GROUP)

    def add_pass(p, acc):
        for g0 in range(0, n_experts, group):
            hits = []
            for e in range(g0, g0 + group):
                done = first[e] + p * win
                off = pl.multiple_of(jnp.minimum(done, cap - win), 16)
                yc_ref[e * win:(e + 1) * win, :] = y_ref[e, 0, pl.ds(off, win), :]
                pe = slots[:, e * win:(e + 1) * win]
                hits.append(((pe - off.astype(F32)) == lane) & (pe >= done.astype(F32)))
            onehot = jnp.concatenate(hits, axis=1).astype(F32).astype(BF16)
            acc = acc + _dot(onehot, yc_ref[g0 * win:(g0 + group) * win, :])
        return acc

    acc_ref[...] = add_pass(0, jnp.zeros(acc_ref.shape, F32))

    def extra_pass(p, carry):
        acc_ref[...] = add_pass(p, acc_ref[...])
        return carry

    lax.fori_loop(1, n_pass, extra_pass, 0)
    xl = xl_ref[0] + g2_ref[0] * acc_ref[...]
    ms = jnp.mean(xl * xl, axis=-1, keepdims=True)
    o_ref[0] = xl * lax.rsqrt(ms + EPS) * fw_ref[...]


def _combine(starts, y, pos_col, xl, g2, fw, cap):
    b, t, d = xl.shape
    n_e = y.shape[0]
    tt = _tile(t, 512)
    win = min(cap, COMBINE_WINDOW)
    assert cap % 16 == 0 and win % 16 == 0 and tt % LANES == 0 and cap + win <= UNSELECTED_SLOT
    tok = lambda i, j, st: (i, j, 0)
    return pl.pallas_call(
        functools.partial(_combine_kernel, n_experts=n_e, cap=cap, n_b=t // LANES, win=win),
        grid_spec=pltpu.PrefetchScalarGridSpec(
            num_scalar_prefetch=1,
            grid=(b, t // tt),
            in_specs=[pl.BlockSpec((n_e, 1, cap, d), lambda i, j, st: (0, i, 0, 0)),
                      pl.BlockSpec((1, tt, 2 * n_e), tok),
                      pl.BlockSpec((1, tt, d), tok),
                      pl.BlockSpec((1, 1, d), lambda i, j, st: (i, 0, 0)),
                      pl.BlockSpec((1, d), lambda i, j, st: (0, 0))],
            out_specs=pl.BlockSpec((1, tt, d), tok),
            scratch_shapes=[pltpu.VMEM((n_e * win, d), BF16), pltpu.VMEM((tt, d), F32)]),
        out_shape=jax.ShapeDtypeStruct((b, t, d), F32),
        compiler_params=_params("arbitrary", "arbitrary"),
    )(starts, y, pos_col, xl, g2, fw)


def _rope_tables(t):
    n_freq = HEAD_DIM // 4
    rows = jnp.repeat(jnp.arange(t // GRID_W, dtype=F32), GRID_W)
    cols = jnp.tile(jnp.arange(GRID_W, dtype=F32), t // GRID_W)
    inv_freq = ROPE_BASE ** (-jnp.arange(n_freq, dtype=F32) / n_freq)
    ang_r, ang_c = rows[:, None] * inv_freq, cols[:, None] * inv_freq
    cos = jnp.concatenate([jnp.cos(ang_r)] * 2 + [jnp.cos(ang_c)] * 2, axis=-1)
    sin = jnp.concatenate([-jnp.sin(ang_r), jnp.sin(ang_r), -jnp.sin(ang_c), jnp.sin(ang_c)], axis=-1)
    return jnp.tile(cos, (1, N_HEADS)), jnp.tile(sin, (1, N_HEADS))


def kernel(x, c, ctx, c_ctx, w_ada, b_ada, norm1_w, norm2_w, w_in, mlstm_gate_bias, ret_decay_logit,
           mlstm_norm_w, ret_norm_w, w_out, w_router, w_gate, w_up, w_down, final_norm_w):
    B, T, D = x.shape
    Tc = ctx.shape[1]
    H = N_HEADS
    width = H * HEAD_DIM
    assert D == 2 * width and w_ada.shape[0] == 1
    assert T % CHUNK == 0 and Tc % CHUNK == 0 and T % GRID_W == 0
    n_lat, n_ctx = T // CHUNK, Tc // CHUNK
    n_experts = w_router.shape[2]
    cap = EC_CAPACITY_FACTOR * T // n_experts
    n_gate = 4 * H

    pad = (-(B + 1)) % 8
    cc = jnp.concatenate([c, c_ctx[None], jnp.zeros((pad, D), F32)], axis=0)
    mod = _ada(cc, w_ada[0], b_ada[0])
    sh1, sc1, g1, sh2, sc2, g2 = [mod[:B, None, i * D:(i + 1) * D] for i in range(6)]
    csh1, csc1 = [jnp.broadcast_to(mod[B, i * D:(i + 1) * D], (B, 1, D)) for i in range(2)]

    assert (4 * width + n_gate) % 8 == 0 and w_in.shape[2] == 8 * width + n_gate
    w_t = jnp.swapaxes(w_in, 1, 2)
    gb = jnp.pad(mlstm_gate_bias[0], (0, LANES - n_gate)).reshape(1, LANES)
    nw1 = norm1_w[0].reshape(1, D)
    cos, sin = _rope_tables(T)
    pm, gl, pr = _inproj(x, nw1, sh1, sc1, w_t, gb, cos, sin, groups=(0, 1, 2, 3))
    pmc, gc, prc = _inproj(ctx, nw1, csh1, csc1, w_t, gb,
                           jnp.ones((Tc, width), F32), jnp.zeros((Tc, width), F32), groups=(1, 2))

    ym = _mlstm(pm, pmc, _gateprep(gl, gc), mlstm_norm_w[0].reshape(H, 1, HEAD_DIM))
    dl = jnp.broadcast_to(jnp.swapaxes(ret_decay_logit[0], 0, 1)[:, :, None], (H, 2, LANES))
    yr = _ret(pr, prc, dl, ret_norm_w[0].reshape(H, 1, HEAD_DIM))

    wr_hi = w_router[0].astype(BF16)
    wr_lo = (w_router[0] - wr_hi.astype(F32)).astype(BF16)
    wrt = jnp.pad(jnp.concatenate([wr_hi, wr_lo], axis=1), ((0, 0), (0, LANES - 2 * n_experts)))
    xl, h2, aff_t = _outproj(ym, yr, x, w_out[0].astype(BF16), g1, norm2_w[0].reshape(1, D), sh2, sc2,
                             wrt, n_experts)

    pos, idx, gate, starts = _route(aff_t, cap)
    by_expert = lambda a: jnp.swapaxes(a.reshape(B, n_experts, cap), 0, 1).reshape(n_experts, B * cap)
    y = _ffn(by_expert(idx), by_expert(gate)[:, None, :], h2.reshape(B * T, D // LANES, LANES),
             w_gate[0], w_up[0], w_down[0])
    pos_t = jnp.swapaxes(pos.reshape(B, n_experts, T), 1, 2)
    pos_t = jnp.where(pos_t < 0, UNSELECTED_SLOT, pos_t)
    pos_hl = jnp.concatenate([pos_t >> 4, pos_t & 15], axis=-1).astype(BF16)
    return _combine(starts[:, :, 0].reshape(-1), y.reshape(n_experts, B, cap, D), pos_hl, xl, g2,
                    final_norm_w.reshape(1, D), cap)
```

```python
import functools

import jax
import jax.numpy as jnp
from jax import lax
from jax.experimental import pallas as pl
from jax.experimental.pallas import tpu as pltpu

F32 = jnp.float32
BF16 = jnp.bfloat16

CHUNK = 128
GRID_W = 64
N_HEADS = 4
HEAD_DIM = 128
N_GATE = 4 * N_HEADS
ROPE_BASE = 10000.0
EPS = 1e-6
LOG2E = 1.4426950408889634
EC_CAPACITY_FACTOR = 2
LANES = 128
VMEM_LIMIT_BYTES = 56 * 1024 * 1024
COMBINE_WINDOW = 128
UNSELECTED_SLOT = 1023
COMBINE_GROUP = 4
HEADS_PER_STEP = 2
MIXER_UNROLL = 32
GATE_UNROLL = 8


def _params(*sem):
    return pltpu.CompilerParams(dimension_semantics=sem, vmem_limit_bytes=VMEM_LIMIT_BYTES)


def _dot(a, b):
    return jnp.dot(a, b, preferred_element_type=F32)


def _dot_nt(a, b):
    return lax.dot_general(a, b, (((1,), (1,)), ((), ())), preferred_element_type=F32)


def _dot_tn(a, b):
    return lax.dot_general(a, b, (((0,), (0,)), ((), ())), preferred_element_type=F32)


def _sigmoid(x):
    return 1.0 / (1.0 + jnp.exp(-x))


def _log_sigmoid(x):
    return jnp.minimum(x, 0.0) - jnp.log1p(jnp.exp(-jnp.abs(x)))


def _col_to_row(col):
    return jnp.broadcast_to(col, (col.shape[0], LANES)).T[0:1, :]


def _row_to_cols(row):
    pieces = [jnp.broadcast_to(row[:, c:c + LANES], (LANES, LANES)).T for c in range(0, row.shape[1], LANES)]
    return pieces[0] if len(pieces) == 1 else jnp.concatenate(pieces, axis=0)


def _tile(n, pref):
    t = min(n, pref)
    assert n % t == 0, (n, t)
    return t


def _ada_kernel(c_ref, w_ref, b_ref, o_ref):
    cv = c_ref[...]
    s = (cv * _sigmoid(cv)).astype(BF16)
    o_ref[...] = _dot(s, w_ref[...].astype(BF16)) + b_ref[...]


def _ada(cc, w, b):
    rows, d = cc.shape
    n = w.shape[1]
    tn = _tile(n, 1024)
    return pl.pallas_call(
        _ada_kernel,
        grid=(n // tn,),
        in_specs=[pl.BlockSpec((rows, d), lambda j: (0, 0)),
                  pl.BlockSpec((d, tn), lambda j: (0, j)),
                  pl.BlockSpec((1, tn), lambda j: (0, j))],
        out_specs=pl.BlockSpec((rows, tn), lambda j: (0, j)),
        out_shape=jax.ShapeDtypeStruct((rows, n), F32),
        compiler_params=_params("arbitrary"),
    )(cc, w, b.reshape(1, n))


def _rms_mod(x, nw, sh, sc):
    ms = jnp.mean(x * x, axis=-1, keepdims=True)
    return (x * lax.rsqrt(ms + EPS) * nw) * (1.0 + sc) + sh


def _inproj_kernel(x_ref, nw_ref, sh_ref, sc_ref, wt_ref, gb_ref, cos_ref, sin_ref,
                   pm_ref, g_ref, pr_ref, w_ref, *, width, k_scale, groups):
    n_g = len(groups)
    ret0 = 4 * width + g_ref.shape[1]
    gate_col = n_g * width
    ret_col = gate_col + LANES

    @pl.when((pl.program_id(0) == 0) & (pl.program_id(1) == 0))
    def _():
        for slot, j in enumerate(groups):
            w_ref[:, slot * width:(slot + 1) * width] = wt_ref[j * width:(j + 1) * width, :].T.astype(BF16)
            w_ref[:, ret_col + slot * width:ret_col + (slot + 1) * width] = (
                wt_ref[ret0 + j * width:ret0 + (j + 1) * width, :].T.astype(BF16))
        w_ref[:, gate_col:ret_col] = wt_ref[4 * width:4 * width + LANES, :].T.astype(BF16)

    hb = _rms_mod(x_ref[0], nw_ref[...], sh_ref[0], sc_ref[0]).astype(BF16)
    for slot, j in enumerate(groups):
        acc = _dot(hb, w_ref[:, slot * width:(slot + 1) * width])
        if j == 1:
            acc = acc * k_scale
        for h in range(N_HEADS):
            pm_ref[0, slot * N_HEADS + h] = acc[:, h * HEAD_DIM:(h + 1) * HEAD_DIM].astype(BF16)
    g_ref[0] = (_dot(hb, w_ref[:, gate_col:ret_col]) + gb_ref[...]).T[:g_ref.shape[1]]
    cos = cos_ref[...]
    sin = sin_ref[...]
    lane = lax.broadcasted_iota(jnp.int32, cos.shape, 1)
    even = ((lane // 32) % 2) == 0
    for slot, j in enumerate(groups):
        acc = _dot(hb, w_ref[:, ret_col + slot * width:ret_col + (slot + 1) * width])
        if j == 1:
            acc = acc * k_scale
        if j < 2:
            swapped = jnp.where(even, pltpu.roll(acc, width - 32, 1), pltpu.roll(acc, 32, 1))
            acc = acc * cos + swapped * sin
        for h in range(N_HEADS):
            pr_ref[0, slot * N_HEADS + h] = acc[:, h * HEAD_DIM:(h + 1) * HEAD_DIM].astype(BF16)


def _inproj(x, nw, sh, sc, w_t, gb, cos, sin, groups):
    b, t, d = x.shape
    width = (w_t.shape[1] - N_GATE) // 8
    tm = _tile(t, 512)
    slabs = len(groups) * N_HEADS
    kern = functools.partial(_inproj_kernel, width=width, k_scale=HEAD_DIM ** -0.5, groups=groups)
    const = lambda i, j: (0, 0)
    return pl.pallas_call(
        kern,
        grid=(b, t // tm),
        in_specs=[pl.BlockSpec((1, tm, d), lambda i, j: (i, j, 0)),
                  pl.BlockSpec((1, d), const),
                  pl.BlockSpec((1, 1, d), lambda i, j: (i, 0, 0)),
                  pl.BlockSpec((1, 1, d), lambda i, j: (i, 0, 0)),
                  pl.BlockSpec((None,) + w_t.shape[1:], lambda i, j: (0, 0, 0), pipeline_mode=pl.Buffered(1)),
                  pl.BlockSpec((1, LANES), const),
                  pl.BlockSpec((tm, width), lambda i, j: (j, 0)),
                  pl.BlockSpec((tm, width), lambda i, j: (j, 0))],
        out_specs=[pl.BlockSpec((1, slabs, tm, HEAD_DIM), lambda i, j: (i, 0, j, 0)),
                   pl.BlockSpec((1, N_GATE, tm), lambda i, j: (i, 0, j)),
                   pl.BlockSpec((1, slabs, tm, HEAD_DIM), lambda i, j: (i, 0, j, 0))],
        out_shape=[jax.ShapeDtypeStruct((b, slabs, t, HEAD_DIM), BF16),
                   jax.ShapeDtypeStruct((b, N_GATE, t), F32),
                   jax.ShapeDtypeStruct((b, slabs, t, HEAD_DIM), BF16)],
        scratch_shapes=[pltpu.VMEM((d, 2 * len(groups) * width + LANES), BF16)],
        compiler_params=_params("arbitrary", "arbitrary"),
    )(x, nw, sh, sc, w_t, gb, cos, sin)


def _scan_lanes(x, op, fill, reverse):
    lane = lax.broadcasted_iota(jnp.int32, x.shape, 1)
    sh = 1
    while sh < LANES:
        if reverse:
            x = op(x, jnp.where(lane < LANES - sh, pltpu.roll(x, LANES - sh, 1), fill))
        else:
            x = op(x, jnp.where(lane >= sh, pltpu.roll(x, sh, 1), fill))
        sh *= 2
    return x


def _gateprep_kernel(gl_ref, gc_ref, o_ref, *, n_lat, n_ctx):
    n_b = gl_ref.shape[0]
    half = N_GATE // 2
    rows = n_b * half
    row = lax.broadcasted_iota(jnp.int32, (rows, LANES), 0)
    lane = lax.broadcasted_iota(jnp.int32, (rows, LANES), 1)
    is_fwd = (row % half) < N_HEADS
    last = lane == jnp.where(is_fwd, LANES - 1, 0)

    def chunk_rows(ref, lo, c_fwd, c_bwd):
        def at(c):
            cols = (slice(c * LANES, (c + 1) * LANES) if isinstance(c, int)
                    else pl.ds(pl.multiple_of(c * LANES, LANES), LANES))
            return jnp.concatenate([ref[s, lo:lo + half, cols] for s in range(n_b)], axis=0)
        return jnp.where(is_fwd, at(c_fwd), at(c_bwd))

    def step(c, ig, fg, m):
        lf = _log_sigmoid(fg)
        b = jnp.where(is_fwd, _scan_lanes(lf, jnp.add, 0.0, False), _scan_lanes(lf, jnp.add, 0.0, True))
        b_tot = jnp.sum(jnp.where(last, b, 0.0), axis=1, keepdims=True)
        a = ig - b
        m_new = jnp.maximum(b_tot + m, jnp.max(b_tot + a, axis=1, keepdims=True))
        w = jnp.exp(b_tot + a - m_new)
        decay = jnp.exp(b_tot + m - m_new)
        run = jnp.where(is_fwd, _scan_lanes(a, jnp.maximum, -jnp.inf, False),
                        _scan_lanes(a, jnp.maximum, -jnp.inf, True))
        mm = jnp.maximum(m, run)
        o_ref[c, 0] = a * LOG2E
        o_ref[c, 1] = w
        o_ref[c, 2] = mm * LOG2E
        o_ref[c, 3] = jnp.exp(-(b + mm))
        o_ref[c, 4] = jnp.broadcast_to(decay, (rows, LANES))
        o_ref[c, 5] = jnp.broadcast_to(m * LOG2E, (rows, LANES))
        return m_new

    m = jnp.zeros((rows, 1), F32)
    for c in range(n_ctx):
        m = step(c, chunk_rows(gc_ref, 0, c, n_ctx - 1 - c), chunk_rows(gc_ref, half, c, n_ctx - 1 - c), m)

    def latent(i, m):
        return step(n_ctx + i, chunk_rows(gl_ref, 0, i, n_lat - 1 - i),
                    chunk_rows(gl_ref, half, i, n_lat - 1 - i), m)

    lax.fori_loop(0, n_lat, latent, m, unroll=min(n_lat, GATE_UNROLL))


def _gateprep(gl, gc):
    n_lat, n_ctx = gl.shape[2] // CHUNK, gc.shape[2] // CHUNK
    return pl.pallas_call(
        functools.partial(_gateprep_kernel, n_lat=n_lat, n_ctx=n_ctx),
        out_shape=jax.ShapeDtypeStruct((n_ctx + n_lat, 6, gl.shape[0] * N_GATE // 2, LANES), F32),
        compiler_params=pltpu.CompilerParams(vmem_limit_bytes=VMEM_LIMIT_BYTES),
    )(gl, gc)


def _head_norm_gate(h, gate, nw):
    return gate * (h * lax.rsqrt(jnp.mean(h * h, axis=-1, keepdims=True) + EPS)) * nw


def _per_head(head_fn, n_shared):
    def kern(*refs, heads_per_step, n_blocked_tail, **kw):
        blocked, shared = refs[:6], refs[6:6 + n_shared]
        tail = refs[6 + n_shared:6 + n_shared + n_blocked_tail]
        scratch = refs[6 + n_shared + n_blocked_tail:]
        for hh in range(heads_per_step):
            one = lambda r: r.at[pl.ds(hh, 1)]
            head_fn(pl.program_id(1) * heads_per_step + hh, *[one(r) for r in blocked], *shared,
                    *[one(r) for r in tail], *[r.at[hh] for r in scratch], **kw)
    return kern


def _mlstm_head(head, q_ref, k_ref, v_ref, o_ref, kc_ref, vc_ref, st_ref, nw_ref, y_ref,
                  sf_ref, sb_ref, cf_ref, cb_ref, *, n_lat, n_ctx, unroll):
    L = CHUNK
    A, WT, MM, EN, DECAY, M0 = range(6)
    ones = jnp.ones((L, HEAD_DIM), BF16)
    stream_f = pl.program_id(0) * (N_GATE // 2) + head
    stream_b = stream_f + N_HEADS

    def stat(step, plane, stream):
        return st_ref[step, plane, pl.ds(stream, 1), :]

    def to_rows(row):
        return jnp.broadcast_to(row, (L, L)).T

    def update(state_ref, k, v, step, stream):
        wkt = (k.astype(F32).T * stat(step, WT, stream)).astype(BF16)
        u = _dot(wkt, jnp.concatenate([v, ones], axis=1))
        decay = stat(step, DECAY, stream)
        state_ref[...] = jnp.concatenate([decay, decay], axis=1) * state_ref[...] + u

    cf_ref[...] = jnp.zeros_like(cf_ref)
    cb_ref[...] = jnp.zeros_like(cb_ref)
    for c in range(n_ctx):
        rf = slice(c * L, (c + 1) * L)
        rb = slice((n_ctx - 1 - c) * L, (n_ctx - c) * L)
        update(cf_ref, kc_ref[0, rf, :], vc_ref[0, rf, :], c, stream_f)
        update(cb_ref, kc_ref[0, rb, :], vc_ref[0, rb, :], c, stream_b)

    def state_body(i, carry):
        jb = n_lat - 1 - i
        rf = pl.ds(pl.multiple_of(i * L, L), L)
        rb = pl.ds(pl.multiple_of(jb * L, L), L)
        sf_ref[i] = cf_ref[...].astype(BF16)
        sb_ref[jb] = cb_ref[...].astype(BF16)
        update(cf_ref, k_ref[0, rf, :], v_ref[0, rf, :], n_ctx + i, stream_f)
        update(cb_ref, k_ref[0, rb, :], v_ref[0, rb, :], n_ctx + i, stream_b)
        return carry

    lax.fori_loop(0, n_lat, state_body, 0, unroll=unroll)

    r_i = lax.broadcasted_iota(jnp.int32, (L, L), 0)
    c_i = lax.broadcasted_iota(jnp.int32, (L, L), 1)
    tril = c_i <= r_i
    triu = c_i >= r_i
    nw = nw_ref[0]

    def out_body(j, carry):
        rows = pl.ds(pl.multiple_of(j * L, L), L)
        step_f = n_ctx + j
        step_b = n_ctx + n_lat - 1 - j
        q = q_ref[0, rows, :]
        k = k_ref[0, rows, :]
        v = v_ref[0, rows, :]
        mm_f = to_rows(stat(step_f, MM, stream_f))
        mm_b = to_rows(stat(step_b, MM, stream_b))
        s = _dot_nt(q, k)
        d_f = jnp.where(tril, jnp.exp2(stat(step_f, A, stream_f) - mm_f), 0.0)
        d_b = jnp.where(triu, jnp.exp2(stat(step_b, A, stream_b) - mm_b), 0.0)
        iw_f = jnp.exp2(stat(step_f, M0, stream_f) - mm_f)
        iw_b = jnp.exp2(stat(step_b, M0, stream_b) - mm_b)
        qf = q.astype(F32)
        vext = jnp.concatenate([v, ones], axis=1)
        lhs_f = jnp.concatenate([(s * d_f).astype(BF16), (qf * iw_f).astype(BF16)], axis=1)
        lhs_b = jnp.concatenate([(s * d_b).astype(BF16), (qf * iw_b).astype(BF16)], axis=1)
        tot_f = _dot(lhs_f, jnp.concatenate([vext, sf_ref[j]], axis=0))
        tot_b = _dot(lhs_b, jnp.concatenate([vext, sb_ref[j]], axis=0))
        D = HEAD_DIM
        h = (tot_f[:, :D] / jnp.maximum(jnp.abs(tot_f[:, D:]), to_rows(stat(step_f, EN, stream_f)))
             + tot_b[:, :D] / jnp.maximum(jnp.abs(tot_b[:, D:]), to_rows(stat(step_b, EN, stream_b))))
        gate = _sigmoid(o_ref[0, rows, :].astype(F32))
        y_ref[0, rows, :] = _head_norm_gate(h, gate, nw).astype(BF16)
        return carry

    lax.fori_loop(0, n_lat, out_body, 0, unroll=unroll)


def _head_slab(t, group):
    per_group = N_HEADS // HEADS_PER_STEP
    return pl.BlockSpec((None, HEADS_PER_STEP, t, HEAD_DIM), lambda i, h: (i, group * per_group + h, 0, 0))


def _head_param(*shape):
    return pl.BlockSpec((HEADS_PER_STEP,) + shape, lambda i, h: (h,) + (0,) * len(shape))


def _mlstm(pm, pmc, stats, nw):
    b, _, t, _ = pm.shape
    tc = pmc.shape[2]
    n_lat, n_ctx = t // CHUNK, tc // CHUNK
    H, hps = N_HEADS, HEADS_PER_STEP
    blk = lambda off: _head_slab(t, off // H)
    cblk = lambda off: _head_slab(tc, off // H - 1)
    return pl.pallas_call(
        functools.partial(_per_head(_mlstm_head, n_shared=1), heads_per_step=hps, n_blocked_tail=2,
                          n_lat=n_lat, n_ctx=n_ctx, unroll=min(n_lat, MIXER_UNROLL)),
        grid=(b, H // hps),
        in_specs=[blk(0), blk(H), blk(2 * H), blk(3 * H), cblk(H), cblk(2 * H),
                  pl.BlockSpec(stats.shape, lambda i, h: (0, 0, 0, 0)),
                  _head_param(1, HEAD_DIM)],
        out_specs=_head_slab(t, 0),
        out_shape=jax.ShapeDtypeStruct((b, H, t, HEAD_DIM), BF16),
        scratch_shapes=[pltpu.VMEM((hps, n_lat, HEAD_DIM, HEAD_DIM + LANES), BF16),
                        pltpu.VMEM((hps, n_lat, HEAD_DIM, HEAD_DIM + LANES), BF16),
                        pltpu.VMEM((hps, HEAD_DIM, HEAD_DIM + LANES), F32),
                        pltpu.VMEM((hps, HEAD_DIM, HEAD_DIM + LANES), F32)],
        compiler_params=_params("arbitrary", "arbitrary"),
    )(pm, pm, pm, pm, pmc, pmc, stats, nw)


def _ret_head(head, q_ref, k_ref, v_ref, o_ref, kc_ref, vc_ref, dl_ref, nw_ref, y_ref,
                sf_ref, sb_ref, rf_ref, rb_ref, *, n_lat, n_ctx, unroll):
    L = CHUNK
    lg = _log_sigmoid(dl_ref[0])
    lg_f, lg_b = lg[0:1, :], lg[1:2, :]
    r_i = lax.broadcasted_iota(jnp.int32, (L, L), 0)
    c_i = lax.broadcasted_iota(jnp.int32, (L, L), 1)
    r_f = r_i.astype(F32)
    rel = (r_i - c_i).astype(F32)
    kw_f = jnp.exp((L - 1.0 - r_f) * lg_f)
    kw_b = jnp.exp(r_f * lg_b)
    in_f = jnp.exp((r_f + 1.0) * lg_f)
    in_b = jnp.exp((L - r_f) * lg_b)
    dch_f = jnp.exp(L * lg_f)
    dch_b = jnp.exp(L * lg_b)
    d_mat = (jnp.where(rel >= 0, jnp.exp(jnp.maximum(rel, 0.0) * lg_f), 0.0)
             + jnp.where(rel <= 0, jnp.exp(jnp.maximum(-rel, 0.0) * lg_b), 0.0))

    def update(state_ref, k, v, kw, dch):
        wk = (k.astype(F32) * kw).astype(BF16)
        state_ref[...] = dch * state_ref[...] + _dot_tn(wk, v)

    rf_ref[...] = jnp.zeros_like(rf_ref)
    rb_ref[...] = jnp.zeros_like(rb_ref)
    for c in range(n_ctx):
        rows = slice(c * L, (c + 1) * L)
        update(rf_ref, kc_ref[0, rows, :], vc_ref[0, rows, :], kw_f, dch_f)
    for c in reversed(range(n_ctx)):
        rows = slice(c * L, (c + 1) * L)
        update(rb_ref, kc_ref[0, rows, :], vc_ref[0, rows, :], kw_b, dch_b)

    def state_body(i, carry):
        jb = n_lat - 1 - i
        rowf = pl.ds(pl.multiple_of(i * L, L), L)
        rowb = pl.ds(pl.multiple_of(jb * L, L), L)
        sf_ref[i] = rf_ref[...].astype(BF16)
        sb_ref[jb] = rb_ref[...].astype(BF16)
        update(rf_ref, k_ref[0, rowf, :], v_ref[0, rowf, :], kw_f, dch_f)
        update(rb_ref, k_ref[0, rowb, :], v_ref[0, rowb, :], kw_b, dch_b)
        return carry

    lax.fori_loop(0, n_lat, state_body, 0, unroll=unroll)
    nw = nw_ref[0]

    def out_body(j, carry):
        rows = pl.ds(pl.multiple_of(j * L, L), L)
        q = q_ref[0, rows, :]
        s = _dot_nt(q, k_ref[0, rows, :])
        intra = _dot((s * d_mat).astype(BF16), v_ref[0, rows, :])
        inter = _dot(q, jnp.concatenate([sf_ref[j], sb_ref[j]], axis=1))
        out = intra + in_f * inter[:, :HEAD_DIM] + in_b * inter[:, HEAD_DIM:]
        og = o_ref[0, rows, :].astype(F32)
        y_ref[0, rows, :] = _head_norm_gate(out, og * _sigmoid(og), nw).astype(BF16)
        return carry

    lax.fori_loop(0, n_lat, out_body, 0, unroll=unroll)


def _ret(pr, prc, dl, nw):
    b, _, t, _ = pr.shape
    tc = prc.shape[2]
    n_lat, n_ctx = t // CHUNK, tc // CHUNK
    H, hps = N_HEADS, HEADS_PER_STEP
    blk = lambda off: _head_slab(t, off // H)
    cblk = lambda off: _head_slab(tc, off // H - 1)
    return pl.pallas_call(
        functools.partial(_per_head(_ret_head, n_shared=0), heads_per_step=hps, n_blocked_tail=3,
                          n_lat=n_lat, n_ctx=n_ctx, unroll=min(n_lat, MIXER_UNROLL)),
        grid=(b, H // hps),
        in_specs=[blk(0), blk(H), blk(2 * H), blk(3 * H), cblk(H), cblk(2 * H),
                  _head_param(2, LANES), _head_param(1, HEAD_DIM)],
        out_specs=_head_slab(t, 0),
        out_shape=jax.ShapeDtypeStruct((b, H, t, HEAD_DIM), BF16),
        scratch_shapes=[pltpu.VMEM((hps, n_lat, HEAD_DIM, HEAD_DIM), BF16),
                        pltpu.VMEM((hps, n_lat, HEAD_DIM, HEAD_DIM), BF16),
                        pltpu.VMEM((hps, HEAD_DIM, HEAD_DIM), F32),
                        pltpu.VMEM((hps, HEAD_DIM, HEAD_DIM), F32)],
        compiler_params=_params("arbitrary", "arbitrary"),
    )(pr, pr, pr, pr, prc, prc, dl, nw)


def _outproj_kernel(ym_ref, yr_ref, x_ref, wo_ref, g1_ref, nw_ref, sh_ref, sc_ref, wrt_ref,
                    xl_ref, h2_ref, aff_ref, *, n_experts):
    heads = [ym_ref[0, h] for h in range(N_HEADS)] + [yr_ref[0, h] for h in range(N_HEADS)]
    y = _dot(jnp.concatenate(heads, axis=1), wo_ref[...])
    xl = x_ref[0] + g1_ref[0] * y
    xl_ref[0] = xl
    h2 = _rms_mod(xl, nw_ref[...], sh_ref[0], sc_ref[0])
    h2_ref[0] = h2.reshape(h2.shape[0], h2.shape[1] // LANES, LANES)
    h_hi = h2.astype(BF16)
    h_lo = (h2 - h_hi.astype(F32)).astype(BF16)
    p_hi = _dot(h_hi, wrt_ref[...])
    p_lo = _dot(h_lo, wrt_ref[...])
    logits = p_hi + pltpu.roll(p_hi, LANES - n_experts, 1) + p_lo
    lane = lax.broadcasted_iota(jnp.int32, logits.shape, 1)
    logits = jnp.where(lane < n_experts, logits, -jnp.inf)
    e = jnp.exp(logits - jnp.max(logits, axis=-1, keepdims=True))
    aff_t = (e / jnp.sum(e, axis=-1, keepdims=True)).T[:n_experts]
    aff_ref[0] = aff_t.reshape(n_experts, aff_t.shape[1] // LANES, LANES)


def _outproj(ym, yr, x, wo, g1, nw, sh, sc, wrt, n_experts):
    b, t, d = x.shape
    tm = _tile(t, 8 * LANES)
    slab = pl.BlockSpec((1, N_HEADS, tm, HEAD_DIM), lambda i, j: (i, 0, j, 0))
    const = lambda i, j: (0, 0)
    per_b = lambda i, j: (i, 0, 0)
    tok = lambda i, j: (i, j, 0)
    return pl.pallas_call(
        functools.partial(_outproj_kernel, n_experts=n_experts),
        grid=(b, t // tm),
        in_specs=[slab, slab,
                  pl.BlockSpec((1, tm, d), tok), pl.BlockSpec(wo.shape, const),
                  pl.BlockSpec((1, 1, d), per_b), pl.BlockSpec((1, d), const),
                  pl.BlockSpec((1, 1, d), per_b), pl.BlockSpec((1, 1, d), per_b),
                  pl.BlockSpec(wrt.shape, const)],
        out_specs=[pl.BlockSpec((1, tm, d), tok),
                   pl.BlockSpec((1, tm, d // LANES, LANES), lambda i, j: (i, j, 0, 0)),
                   pl.BlockSpec((1, n_experts, tm // LANES, LANES), lambda i, j: (i, 0, j, 0))],
        out_shape=[jax.ShapeDtypeStruct((b, t, d), F32),
                   jax.ShapeDtypeStruct((b, t, d // LANES, LANES), F32),
                   jax.ShapeDtypeStruct((b, n_experts, t // LANES, LANES), F32)],
        compiler_params=_params("arbitrary", "arbitrary"),
    )(ym, yr, x, wo, g1, nw, sh, sc, wrt)


def _select_top(a, cap):
    n_e, n_b, _ = a.shape

    ones = jnp.ones((LANES, LANES), BF16)

    def count(mask):
        per_block = _dot(mask.astype(F32).astype(BF16).reshape(n_e * n_b, LANES), ones)
        return jnp.sum(per_block.reshape(n_e, n_b, LANES), axis=1, keepdims=True).astype(jnp.int32)

    def search(i, thr):
        cand = thr | jnp.left_shift(jnp.int32(1), 30 - i)
        return jnp.where(count(a >= pltpu.bitcast(cand, F32)) >= cap, cand, thr)

    thr = lax.fori_loop(0, 31, search, jnp.zeros((n_e, 1, LANES), jnp.int32))
    above = a >= pltpu.bitcast(thr + 1, F32)
    band = (a >= pltpu.bitcast(thr, F32)) & jnp.logical_not(above)
    need = cap - count(above)[:, :, 0:1]
    tok = (lax.broadcasted_iota(jnp.int32, a.shape, 1) * LANES
           + lax.broadcasted_iota(jnp.int32, a.shape, 2))

    def take_one(_, carry):
        sel, band, need = carry
        in_band = band > 0
        best = jnp.max(jnp.max(jnp.where(in_band, a, -1.0), axis=2, keepdims=True), axis=1, keepdims=True)
        cand = jnp.where(in_band & (a == best), tok, n_b * LANES)
        first = jnp.min(jnp.min(cand, axis=2, keepdims=True), axis=1, keepdims=True)
        take = ((tok == first) & (need > 0)).astype(jnp.int32)
        return sel | take, band - take, need - 1

    sel, _, _ = lax.fori_loop(0, jnp.max(need), take_one,
                              (above.astype(jnp.int32), band.astype(jnp.int32), need))
    return sel


def _route_kernel(aff_ref, pos_ref, idx_ref, gate_ref, start_ref, sel_ref, *, cap):
    i = pl.program_id(0)
    n_s, n_e, n_b, _ = aff_ref.shape

    @pl.when(i == 0)
    def _():
        sel_all = _select_top(aff_ref[...].reshape(n_s * n_e, n_b, LANES), cap)
        sel_ref[...] = sel_all.reshape(n_s, n_e, n_b, LANES)

    a = aff_ref[i]
    sel = sel_ref[i] > 0

    rows = n_e * n_b
    u_i = lax.broadcasted_iota(jnp.int32, (LANES, LANES), 0)
    s_i = lax.broadcasted_iota(jnp.int32, (LANES, LANES), 1)
    incl = (u_i <= s_i).astype(BF16)
    ones = jnp.ones((LANES, LANES), BF16)
    r_i = lax.broadcasted_iota(jnp.int32, (rows, rows), 0)
    c_i = lax.broadcasted_iota(jnp.int32, (rows, rows), 1)
    before = ((r_i // n_b == c_i // n_b) & (c_i < r_i)).astype(BF16)

    x = sel.astype(F32).reshape(rows, LANES)
    xb = x.astype(BF16)
    block_tot = _dot(xb, ones).astype(BF16)
    start = _dot(before, block_tot)
    pos = jnp.where(x > 0, start + _dot(xb, incl) - x, -1.0).astype(jnp.int32)
    pos_ref[0] = pos.reshape(n_e, n_b, LANES)
    start_ref[0] = start.astype(jnp.int32)

    pad = (-rows) % LANES
    def pad_rows(m):
        return m if pad == 0 else jnp.concatenate([m, jnp.zeros((pad, LANES), m.dtype)], axis=0)
    start_p = pad_rows(start)
    end_p = pad_rows(start + block_tot.astype(F32))
    pos_p = pad_rows(pos)
    pos_p = jnp.where(pos_p < 0, 1023, pos_p)
    a_p = pad_rows(a.reshape(rows, LANES))
    a_hi = a_p.astype(BF16)
    a_mid = (a_p - a_hi.astype(F32)).astype(BF16)
    a_lo = (a_p - a_hi.astype(F32) - a_mid.astype(F32)).astype(BF16)
    per_group = LANES // n_b
    j_col = lax.broadcasted_iota(jnp.int32, (cap, LANES), 0).astype(F32)
    lane = lax.broadcasted_iota(jnp.int32, (cap, LANES), 1)
    lane_f = lane.astype(F32)
    block_f = (lane % n_b).astype(F32)
    row0 = pl.program_id(0) * (n_b * LANES)
    for g in range((rows + pad) // LANES):
        rs = slice(g * LANES, (g + 1) * LANES)
        s_row = start_p[rs].T[0:1, :]
        e_row = end_p[rs].T[0:1, :]
        hi = (pos_p[rs] >> 4).astype(F32).astype(BF16)
        lo = (pos_p[rs] & 15).astype(F32).astype(BF16)
        for el in range(per_group):
            e = g * per_group + el
            if e >= n_e:
                break
            mine = (j_col >= s_row) & (j_col < e_row) & ((lane // n_b) == el)
            mine_b = mine.astype(F32).astype(BF16)
            slots = 16.0 * _dot(mine_b, hi) + _dot(mine_b, lo)
            hit = slots == j_col
            in_block = jnp.sum(jnp.where(hit, lane_f, 0.0), axis=1, keepdims=True)
            block = jnp.sum(jnp.where(mine, block_f, 0.0), axis=1, keepdims=True)
            a_blk = _dot(mine_b, a_hi[rs]) + _dot(mine_b, a_mid[rs]) + _dot(mine_b, a_lo[rs])
            gate = jnp.sum(jnp.where(hit, a_blk, 0.0), axis=1, keepdims=True)
            idx_ref[0, e] = _col_to_row(block * LANES + in_block).astype(jnp.int32) + row0
            gate_ref[0, e] = _col_to_row(gate)


def _route(aff_t, cap):
    b, n_e, n_b, _ = aff_t.shape
    spec = pl.BlockSpec((1, n_e, n_b, LANES), lambda i: (i, 0, 0, 0))
    slot_spec = pl.BlockSpec((1, n_e, 1, cap), lambda i: (i, 0, 0, 0))
    return pl.pallas_call(
        functools.partial(_route_kernel, cap=cap),
        grid=(b,), in_specs=[pl.BlockSpec(aff_t.shape, lambda i: (0, 0, 0, 0))],
        out_specs=[spec, slot_spec, slot_spec, pl.BlockSpec((1, n_e * n_b, LANES), lambda i: (i, 0, 0))],
        scratch_shapes=[pltpu.VMEM(aff_t.shape, jnp.int32)],
        out_shape=[jax.ShapeDtypeStruct(aff_t.shape, jnp.int32),
                   jax.ShapeDtypeStruct((b, n_e, 1, cap), jnp.int32),
                   jax.ShapeDtypeStruct((b, n_e, 1, cap), F32),
                   jax.ShapeDtypeStruct((b, n_e * n_b, LANES), jnp.int32)],
        compiler_params=_params("arbitrary"),
    )(aff_t)


def _ffn_kernel(idx_ref, h_hbm, gate_ref, wg_ref, wu_ref, wd_ref, y_ref, stage_ref, x_ref, acc_ref, sem,
                *, m, per_step):
    e, f = pl.program_id(0), pl.program_id(1)
    n_e, n_f = pl.num_programs(0), pl.num_programs(1)
    rows_per_expert = stage_ref.shape[0]

    def row_copy(base, j):
        return pltpu.make_async_copy(h_hbm.at[pl.ds(idx_ref[base + j], 1)],
                                     stage_ref.at[pl.ds(j, 1)], sem.at[0])

    def wait_rows():
        pltpu.make_async_copy(h_hbm.at[pl.ds(0, rows_per_expert)], stage_ref, sem.at[0]).wait()

    @pl.when((e == 0) & (f == 0))
    def _():
        def start_row(j, carry):
            row_copy(0, j).start()
            return carry
        lax.fori_loop(0, rows_per_expert, start_row, 0)

    @pl.when(f == 0)
    def _():
        wait_rows()
        x_ref[...] = stage_ref[:m].reshape(x_ref.shape).astype(BF16)
        acc_ref[...] = jnp.zeros_like(acc_ref)

    nxt = jnp.minimum(e + 1, n_e - 1) * rows_per_expert
    for i in range(per_step):
        row_copy(nxt, f * per_step + i).start()

    x = x_ref[...]
    a = _dot(x, wg_ref[0].astype(BF16))
    u = _dot(x, wu_ref[0].astype(BF16))
    mid = (a * _sigmoid(a) * u).astype(BF16)
    acc_ref[...] += _dot(mid, wd_ref[0].astype(BF16))

    @pl.when(f == n_f - 1)
    def _():
        gate = _row_to_cols(gate_ref[0])
        for c in range(0, y_ref.shape[2], LANES):
            y_ref[0, :, c:c + LANES] = (acc_ref[:, c:c + LANES] * gate).astype(BF16)

    @pl.when((e == n_e - 1) & (f == n_f - 1))
    def _():
        wait_rows()


def _ffn(idx, gate, h_rows, wg, wu, wd):
    n_e, m = idx.shape
    d = h_rows.shape[1] * LANES
    ff = wg.shape[2]
    tf = _tile(ff, 256)
    n_f = ff // tf
    per_step = -(-m // (8 * n_f)) * 8
    idx = jnp.pad(idx, ((0, 0), (0, n_f * per_step - m))).reshape(-1)
    return pl.pallas_call(
        functools.partial(_ffn_kernel, m=m, per_step=per_step),
        grid_spec=pltpu.PrefetchScalarGridSpec(
            num_scalar_prefetch=1,
            grid=(n_e, n_f),
            in_specs=[pl.BlockSpec(memory_space=pl.ANY),
                      pl.BlockSpec((1, 1, m), lambda e, f, idx: (e, 0, 0)),
                      pl.BlockSpec((1, d, tf), lambda e, f, idx: (e, 0, f)),
                      pl.BlockSpec((1, d, tf), lambda e, f, idx: (e, 0, f)),
                      pl.BlockSpec((1, tf, d), lambda e, f, idx: (e, f, 0))],
            out_specs=pl.BlockSpec((1, m, d), lambda e, f, idx: (e, 0, 0)),
            scratch_shapes=[pltpu.VMEM((n_f * per_step, d // LANES, LANES), F32),
                            pltpu.VMEM((m, d), BF16),
                            pltpu.VMEM((m, d), F32),
                            pltpu.SemaphoreType.DMA((1,))]),
        out_shape=jax.ShapeDtypeStruct((n_e, m, d), BF16),
        compiler_params=_params("arbitrary", "arbitrary"),
    )(idx, h_rows, gate, wg, wu, wd)


def _combine_kernel(start_ref, y_ref, pos_ref, xl_ref, g2_ref, fw_ref, o_ref, yc_ref, acc_ref,
                    *, n_experts, cap, n_b, win):
    i, j = pl.program_id(0), pl.program_id(1)
    tt = xl_ref.shape[1]
    blocks_per_tile = tt // LANES
    lane = lax.broadcasted_iota(jnp.int32, (tt, win), 1).astype(F32)
    r_i = lax.broadcasted_iota(jnp.int32, (2 * n_experts, n_experts * win), 0)
    c_e = lax.broadcasted_iota(jnp.int32, (2 * n_experts, n_experts * win), 1) // win
    spread = jnp.where(r_i == c_e, 16.0, jnp.where(r_i == c_e + n_experts, 1.0, 0.0)).astype(BF16)
    slots = _dot(pos_ref[0], spread)
    last_tile = j == pl.num_programs(1) - 1
    first, n_pass = [], 0
    for e in range(n_experts):
        row = (i * n_experts + e) * n_b
        lo = start_ref[row + j * blocks_per_tile]
        hi = jnp.where(last_tile, cap, start_ref[row + jnp.minimum((j + 1) * blocks_per_tile, n_b - 1)])
        w0 = jnp.minimum((lo // 16) * 16, cap - win)
        first.append(w0)
        n_pass = jnp.maximum(n_pass, jnp.where(hi > lo, (hi - w0 + win - 1) // win, 0))
    group = min(n_experts, COMBINE_GROUP)

    def add_pass(p, acc):
        for g0 in range(0, n_experts, group):
            hits = []
            for e in range(g0, g0 + group):
                done = first[e] + p * win
                off = pl.multiple_of(jnp.minimum(done, cap - win), 16)
                yc_ref[e * win:(e + 1) * win, :] = y_ref[e, 0, pl.ds(off, win), :]
                pe = slots[:, e * win:(e + 1) * win]
                hits.append(((pe - off.astype(F32)) == lane) & (pe >= done.astype(F32)))
            onehot = jnp.concatenate(hits, axis=1).astype(F32).astype(BF16)
            acc = acc + _dot(onehot, yc_ref[g0 * win:(g0 + group) * win, :])
        return acc

    acc_ref[...] = add_pass(0, jnp.zeros(acc_ref.shape, F32))

    def extra_pass(p, carry):
        acc_ref[...] = add_pass(p, acc_ref[...])
        return carry

    lax.fori_loop(1, n_pass, extra_pass, 0)
    xl = xl_ref[0] + g2_ref[0] * acc_ref[...]
    ms = jnp.mean(xl * xl, axis=-1, keepdims=True)
    o_ref[0] = xl * lax.rsqrt(ms + EPS) * fw_ref[...]


def _combine(starts, y, pos_col, xl, g2, fw, cap):
    b, t, d = xl.shape
    n_e = y.shape[0]
    tt = _tile(t, 512)
    win = min(cap, COMBINE_WINDOW)
    assert cap % 16 == 0 and win % 16 == 0 and tt % LANES == 0 and cap + win <= UNSELECTED_SLOT
    tok = lambda i, j, st: (i, j, 0)
    return pl.pallas_call(
        functools.partial(_combine_kernel, n_experts=n_e, cap=cap, n_b=t // LANES, win=win),
        grid_spec=pltpu.PrefetchScalarGridSpec(
            num_scalar_prefetch=1,
            grid=(b, t // tt),
            in_specs=[pl.BlockSpec((n_e, 1, cap, d), lambda i, j, st: (0, i, 0, 0)),
                      pl.BlockSpec((1, tt, 2 * n_e), tok),
                      pl.BlockSpec((1, tt, d), tok),
                      pl.BlockSpec((1, 1, d), lambda i, j, st: (i, 0, 0)),
                      pl.BlockSpec((1, d), lambda i, j, st: (0, 0))],
            out_specs=pl.BlockSpec((1, tt, d), tok),
            scratch_shapes=[pltpu.VMEM((n_e * win, d), BF16), pltpu.VMEM((tt, d), F32)]),
        out_shape=jax.ShapeDtypeStruct((b, t, d), F32),
        compiler_params=_params("arbitrary", "arbitrary"),
    )(starts, y, pos_col, xl, g2, fw)


def _rope_tables(t):
    n_freq = HEAD_DIM // 4
    rows = jnp.repeat(jnp.arange(t // GRID_W, dtype=F32), GRID_W)
    cols = jnp.tile(jnp.arange(GRID_W, dtype=F32), t // GRID_W)
    inv_freq = ROPE_BASE ** (-jnp.arange(n_freq, dtype=F32) / n_freq)
    ang_r, ang_c = rows[:, None] * inv_freq, cols[:, None] * inv_freq
    cos = jnp.concatenate([jnp.cos(ang_r)] * 2 + [jnp.cos(ang_c)] * 2, axis=-1)
    sin = jnp.concatenate([-jnp.sin(ang_r), jnp.sin(ang_r), -jnp.sin(ang_c), jnp.sin(ang_c)], axis=-1)
    return jnp.tile(cos, (1, N_HEADS)), jnp.tile(sin, (1, N_HEADS))


def kernel(x, c, ctx, c_ctx, w_ada, b_ada, norm1_w, norm2_w, w_in, mlstm_gate_bias, ret_decay_logit,
           mlstm_norm_w, ret_norm_w, w_out, w_router, w_gate, w_up, w_down, final_norm_w):
    B, T, D = x.shape
    Tc = ctx.shape[1]
    H = N_HEADS
    width = H * HEAD_DIM
    assert D == 2 * width and w_ada.shape[0] == 1
    assert T % CHUNK == 0 and Tc % CHUNK == 0 and T % GRID_W == 0
    n_lat, n_ctx = T // CHUNK, Tc // CHUNK
    n_experts = w_router.shape[2]
    cap = EC_CAPACITY_FACTOR * T // n_experts
    n_gate = 4 * H

    pad = (-(B + 1)) % 8
    cc = jnp.concatenate([c, c_ctx[None], jnp.zeros((pad, D), F32)], axis=0)
    mod = _ada(cc, w_ada[0], b_ada[0])
    sh1, sc1, g1, sh2, sc2, g2 = [mod[:B, None, i * D:(i + 1) * D] for i in range(6)]
    csh1, csc1 = [jnp.broadcast_to(mod[B, i * D:(i + 1) * D], (B, 1, D)) for i in range(2)]

    assert (4 * width + n_gate) % 8 == 0 and w_in.shape[2] == 8 * width + n_gate
    w_t = jnp.swapaxes(w_in, 1, 2)
    gb = jnp.pad(mlstm_gate_bias[0], (0, LANES - n_gate)).reshape(1, LANES)
    nw1 = norm1_w[0].reshape(1, D)
    cos, sin = _rope_tables(T)
    pm, gl, pr = _inproj(x, nw1, sh1, sc1, w_t, gb, cos, sin, groups=(0, 1, 2, 3))
    pmc, gc, prc = _inproj(ctx, nw1, csh1, csc1, w_t, gb,
                           jnp.ones((Tc, width), F32), jnp.zeros((Tc, width), F32), groups=(1, 2))

    ym = _mlstm(pm, pmc, _gateprep(gl, gc), mlstm_norm_w[0].reshape(H, 1, HEAD_DIM))
    dl = jnp.broadcast_to(jnp.swapaxes(ret_decay_logit[0], 0, 1)[:, :, None], (H, 2, LANES))
    yr = _ret(pr, prc, dl, ret_norm_w[0].reshape(H, 1, HEAD_DIM))

    wr_hi = w_router[0].astype(BF16)
    wr_lo = (w_router[0] - wr_hi.astype(F32)).astype(BF16)
    wrt = jnp.pad(jnp.concatenate([wr_hi, wr_lo], axis=1), ((0, 0), (0, LANES - 2 * n_experts)))
    xl, h2, aff_t = _outproj(ym, yr, x, w_out[0].astype(BF16), g1, norm2_w[0].reshape(1, D), sh2, sc2,
                             wrt, n_experts)

    pos, idx, gate, starts = _route(aff_t, cap)
    by_expert = lambda a: jnp.swapaxes(a.reshape(B, n_experts, cap), 0, 1).reshape(n_experts, B * cap)
    y = _ffn(by_expert(idx), by_expert(gate)[:, None, :], h2.reshape(B * T, D // LANES, LANES),
             w_gate[0], w_up[0], w_down[0])
    pos_t = jnp.swapaxes(pos.reshape(B, n_experts, T), 1, 2)
    pos_t = jnp.where(pos_t < 0, UNSELECTED_SLOT, pos_t)
    pos_hl = jnp.concatenate([pos_t >> 4, pos_t & 15], axis=-1).astype(BF16)
    return _combine(starts[:, :, 0].reshape(-1), y.reshape(n_experts, B, cap, D), pos_hl, xl, g2,
                    final_norm_w.reshape(1, D), cap)
```

```python
import functools

import jax
import jax.numpy as jnp
from jax import lax
from jax.experimental import pallas as pl
from jax.experimental.pallas import tpu as pltpu

F32 = jnp.float32
BF16 = jnp.bfloat16

CHUNK = 128
GRID_W = 64
N_HEADS = 4
HEAD_DIM = 128
N_GATE = 4 * N_HEADS
ROPE_BASE = 10000.0
EPS = 1e-6
LOG2E = 1.4426950408889634
EC_CAPACITY_FACTOR = 2
LANES = 128
VMEM_LIMIT_BYTES = 56 * 1024 * 1024
COMBINE_WINDOW = 128
UNSELECTED_SLOT = 1023
COMBINE_GROUP = 4
HEADS_PER_STEP = 2
MIXER_UNROLL = 32
GATE_UNROLL = 8


def _params(*sem):
    return pltpu.CompilerParams(dimension_semantics=sem, vmem_limit_bytes=VMEM_LIMIT_BYTES)


def _dot(a, b):
    return jnp.dot(a, b, preferred_element_type=F32)


def _dot_nt(a, b):
    return lax.dot_general(a, b, (((1,), (1,)), ((), ())), preferred_element_type=F32)


def _dot_tn(a, b):
    return lax.dot_general(a, b, (((0,), (0,)), ((), ())), preferred_element_type=F32)


def _sigmoid(x):
    return 1.0 / (1.0 + jnp.exp(-x))


def _log_sigmoid(x):
    return jnp.minimum(x, 0.0) - jnp.log1p(jnp.exp(-jnp.abs(x)))


def _col_to_row(col):
    return jnp.broadcast_to(col, (col.shape[0], LANES)).T[0:1, :]


def _row_to_cols(row):
    pieces = [jnp.broadcast_to(row[:, c:c + LANES], (LANES, LANES)).T for c in range(0, row.shape[1], LANES)]
    return pieces[0] if len(pieces) == 1 else jnp.concatenate(pieces, axis=0)


def _tile(n, pref):
    t = min(n, pref)
    assert n % t == 0, (n, t)
    return t


def _ada_kernel(c_ref, w_ref, b_ref, o_ref):
    cv = c_ref[...]
    s = (cv * _sigmoid(cv)).astype(BF16)
    o_ref[...] = _dot(s, w_ref[...].astype(BF16)) + b_ref[...]


def _ada(cc, w, b):
    rows, d = cc.shape
    n = w.shape[1]
    tn = _tile(n, 1024)
    return pl.pallas_call(
        _ada_kernel,
        grid=(n // tn,),
        in_specs=[pl.BlockSpec((rows, d), lambda j: (0, 0)),
                  pl.BlockSpec((d, tn), lambda j: (0, j)),
                  pl.BlockSpec((1, tn), lambda j: (0, j))],
        out_specs=pl.BlockSpec((rows, tn), lambda j: (0, j)),
        out_shape=jax.ShapeDtypeStruct((rows, n), F32),
        compiler_params=_params("arbitrary"),
    )(cc, w, b.reshape(1, n))


def _rms_mod(x, nw, sh, sc):
    ms = jnp.mean(x * x, axis=-1, keepdims=True)
    return (x * lax.rsqrt(ms + EPS) * nw) * (1.0 + sc) + sh


def _inproj_kernel(x_ref, nw_ref, sh_ref, sc_ref, wt_ref, gb_ref, cos_ref, sin_ref,
                   pm_ref, g_ref, pr_ref, w_ref, *, width, k_scale, groups):
    n_g = len(groups)
    ret0 = 4 * width + g_ref.shape[1]
    gate_col = n_g * width
    ret_col = gate_col + LANES

    @pl.when((pl.program_id(0) == 0) & (pl.program_id(1) == 0))
    def _():
        for slot, j in enumerate(groups):
            w_ref[:, slot * width:(slot + 1) * width] = wt_ref[j * width:(j + 1) * width, :].T.astype(BF16)
            w_ref[:, ret_col + slot * width:ret_col + (slot + 1) * width] = (
                wt_ref[ret0 + j * width:ret0 + (j + 1) * width, :].T.astype(BF16))
        w_ref[:, gate_col:ret_col] = wt_ref[4 * width:4 * width + LANES, :].T.astype(BF16)

    hb = _rms_mod(x_ref[0], nw_ref[...], sh_ref[0], sc_ref[0]).astype(BF16)
    for slot, j in enumerate(groups):
        acc = _dot(hb, w_ref[:, slot * width:(slot + 1) * width])
        if j == 1:
            acc = acc * k_scale
        for h in range(N_HEADS):
            pm_ref[0, slot * N_HEADS + h] = acc[:, h * HEAD_DIM:(h + 1) * HEAD_DIM].astype(BF16)
    g_ref[0] = (_dot(hb, w_ref[:, gate_col:ret_col]) + gb_ref[...]).T[:g_ref.shape[1]]
    cos = cos_ref[...]
    sin = sin_ref[...]
    lane = lax.broadcasted_iota(jnp.int32, cos.shape, 1)
    even = ((lane // 32) % 2) == 0
    for slot, j in enumerate(groups):
        acc = _dot(hb, w_ref[:, ret_col + slot * width:ret_col + (slot + 1) * width])
        if j == 1:
            acc = acc * k_scale
        if j < 2:
            swapped = jnp.where(even, pltpu.roll(acc, width - 32, 1), pltpu.roll(acc, 32, 1))
            acc = acc * cos + swapped * sin
        for h in range(N_HEADS):
            pr_ref[0, slot * N_HEADS + h] = acc[:, h * HEAD_DIM:(h + 1) * HEAD_DIM].astype(BF16)


def _inproj(x, nw, sh, sc, w_t, gb, cos, sin, groups):
    b, t, d = x.shape
    width = (w_t.shape[1] - N_GATE) // 8
    tm = _tile(t, 512)
    slabs = len(groups) * N_HEADS
    kern = functools.partial(_inproj_kernel, width=width, k_scale=HEAD_DIM ** -0.5, groups=groups)
    const = lambda i, j: (0, 0)
    return pl.pallas_call(
        kern,
        grid=(b, t // tm),
        in_specs=[pl.BlockSpec((1, tm, d), lambda i, j: (i, j, 0)),
                  pl.BlockSpec((1, d), const),
                  pl.BlockSpec((1, 1, d), lambda i, j: (i, 0, 0)),
                  pl.BlockSpec((1, 1, d), lambda i, j: (i, 0, 0)),
                  pl.BlockSpec((None,) + w_t.shape[1:], lambda i, j: (0, 0, 0), pipeline_mode=pl.Buffered(1)),
                  pl.BlockSpec((1, LANES), const),
                  pl.BlockSpec((tm, width), lambda i, j: (j, 0)),
                  pl.BlockSpec((tm, width), lambda i, j: (j, 0))],
        out_specs=[pl.BlockSpec((1, slabs, tm, HEAD_DIM), lambda i, j: (i, 0, j, 0)),
                   pl.BlockSpec((1, N_GATE, tm), lambda i, j: (i, 0, j)),
                   pl.BlockSpec((1, slabs, tm, HEAD_DIM), lambda i, j: (i, 0, j, 0))],
        out_shape=[jax.ShapeDtypeStruct((b, slabs, t, HEAD_DIM), BF16),
                   jax.ShapeDtypeStruct((b, N_GATE, t), F32),
                   jax.ShapeDtypeStruct((b, slabs, t, HEAD_DIM), BF16)],
        scratch_shapes=[pltpu.VMEM((d, 2 * len(groups) * width + LANES), BF16)],
        compiler_params=_params("arbitrary", "arbitrary"),
    )(x, nw, sh, sc, w_t, gb, cos, sin)


def _scan_lanes(x, op, fill, reverse):
    lane = lax.broadcasted_iota(jnp.int32, x.shape, 1)
    sh = 1
    while sh < LANES:
        if reverse:
            x = op(x, jnp.where(lane < LANES - sh, pltpu.roll(x, LANES - sh, 1), fill))
        else:
            x = op(x, jnp.where(lane >= sh, pltpu.roll(x, sh, 1), fill))
        sh *= 2
    return x


def _gateprep_kernel(gl_ref, gc_ref, o_ref, *, n_lat, n_ctx):
    n_b = gl_ref.shape[0]
    half = N_GATE // 2
    rows = n_b * half
    row = lax.broadcasted_iota(jnp.int32, (rows, LANES), 0)
    lane = lax.broadcasted_iota(jnp.int32, (rows, LANES), 1)
    is_fwd = (row % half) < N_HEADS
    last = lane == jnp.where(is_fwd, LANES - 1, 0)

    def chunk_rows(ref, lo, c_fwd, c_bwd):
        def at(c):
            cols = (slice(c * LANES, (c + 1) * LANES) if isinstance(c, int)
                    else pl.ds(pl.multiple_of(c * LANES, LANES), LANES))
            return jnp.concatenate([ref[s, lo:lo + half, cols] for s in range(n_b)], axis=0)
        return jnp.where(is_fwd, at(c_fwd), at(c_bwd))

    def step(c, ig, fg, m):
        lf = _log_sigmoid(fg)
        b = jnp.where(is_fwd, _scan_lanes(lf, jnp.add, 0.0, False), _scan_lanes(lf, jnp.add, 0.0, True))
        b_tot = jnp.sum(jnp.where(last, b, 0.0), axis=1, keepdims=True)
        a = ig - b
        m_new = jnp.maximum(b_tot + m, jnp.max(b_tot + a, axis=1, keepdims=True))
        w = jnp.exp(b_tot + a - m_new)
        decay = jnp.exp(b_tot + m - m_new)
        run = jnp.where(is_fwd, _scan_lanes(a, jnp.maximum, -jnp.inf, False),
                        _scan_lanes(a, jnp.maximum, -jnp.inf, True))
        mm = jnp.maximum(m, run)
        o_ref[c, 0] = a * LOG2E
        o_ref[c, 1] = w
        o_ref[c, 2] = mm * LOG2E
        o_ref[c, 3] = jnp.exp(-(b + mm))
        o_ref[c, 4] = jnp.broadcast_to(decay, (rows, LANES))
        o_ref[c, 5] = jnp.broadcast_to(m * LOG2E, (rows, LANES))
        return m_new

    m = jnp.zeros((rows, 1), F32)
    for c in range(n_ctx):
        m = step(c, chunk_rows(gc_ref, 0, c, n_ctx - 1 - c), chunk_rows(gc_ref, half, c, n_ctx - 1 - c), m)

    def latent(i, m):
        return step(n_ctx + i, chunk_rows(gl_ref, 0, i, n_lat - 1 - i),
                    chunk_rows(gl_ref, half, i, n_lat - 1 - i), m)

    lax.fori_loop(0, n_lat, latent, m, unroll=min(n_lat, GATE_UNROLL))


def _gateprep(gl, gc):
    n_lat, n_ctx = gl.shape[2] // CHUNK, gc.shape[2] // CHUNK
    return pl.pallas_call(
        functools.partial(_gateprep_kernel, n_lat=n_lat, n_ctx=n_ctx),
        out_shape=jax.ShapeDtypeStruct((n_ctx + n_lat, 6, gl.shape[0] * N_GATE // 2, LANES), F32),
        compiler_params=pltpu.CompilerParams(vmem_limit_bytes=VMEM_LIMIT_BYTES),
    )(gl, gc)


def _head_norm_gate(h, gate, nw):
    return gate * (h * lax.rsqrt(jnp.mean(h * h, axis=-1, keepdims=True) + EPS)) * nw


def _per_head(head_fn, n_shared):
    def kern(*refs, heads_per_step, n_blocked_tail, **kw):
        blocked, shared = refs[:6], refs[6:6 + n_shared]
        tail = refs[6 + n_shared:6 + n_shared + n_blocked_tail]
        scratch = refs[6 + n_shared + n_blocked_tail:]
        for hh in range(heads_per_step):
            one = lambda r: r.at[pl.ds(hh, 1)]
            head_fn(pl.program_id(1) * heads_per_step + hh, *[one(r) for r in blocked], *shared,
                    *[one(r) for r in tail], *[r.at[hh] for r in scratch], **kw)
    return kern


def _mlstm_head(head, q_ref, k_ref, v_ref, o_ref, kc_ref, vc_ref, st_ref, nw_ref, y_ref,
                  sf_ref, sb_ref, cf_ref, cb_ref, *, n_lat, n_ctx, unroll):
    L = CHUNK
    A, WT, MM, EN, DECAY, M0 = range(6)
    ones = jnp.ones((L, HEAD_DIM), BF16)
    stream_f = pl.program_id(0) * (N_GATE // 2) + head
    stream_b = stream_f + N_HEADS

    def stat(step, plane, stream):
        return st_ref[step, plane, pl.ds(stream, 1), :]

    def to_rows(row):
        return jnp.broadcast_to(row, (L, L)).T

    def update(state_ref, k, v, step, stream):
        wkt = (k.astype(F32).T * stat(step, WT, stream)).astype(BF16)
        u = _dot(wkt, jnp.concatenate([v, ones], axis=1))
        decay = stat(step, DECAY, stream)
        state_ref[...] = jnp.concatenate([decay, decay], axis=1) * state_ref[...] + u

    cf_ref[...] = jnp.zeros_like(cf_ref)
    cb_ref[...] = jnp.zeros_like(cb_ref)
    for c in range(n_ctx):
        rf = slice(c * L, (c + 1) * L)
        rb = slice((n_ctx - 1 - c) * L, (n_ctx - c) * L)
        update(cf_ref, kc_ref[0, rf, :], vc_ref[0, rf, :], c, stream_f)
        update(cb_ref, kc_ref[0, rb, :], vc_ref[0, rb, :], c, stream_b)

    def state_body(i, carry):
        jb = n_lat - 1 - i
        rf = pl.ds(pl.multiple_of(i * L, L), L)
        rb = pl.ds(pl.multiple_of(jb * L, L), L)
        sf_ref[i] = cf_ref[...].astype(BF16)
        sb_ref[jb] = cb_ref[...].astype(BF16)
        update(cf_ref, k_ref[0, rf, :], v_ref[0, rf, :], n_ctx + i, stream_f)
        update(cb_ref, k_ref[0, rb, :], v_ref[0, rb, :], n_ctx + i, stream_b)
        return carry

    lax.fori_loop(0, n_lat, state_body, 0, unroll=unroll)

    r_i = lax.broadcasted_iota(jnp.int32, (L, L), 0)
    c_i = lax.broadcasted_iota(jnp.int32, (L, L), 1)
    tril = c_i <= r_i
    triu = c_i >= r_i
    nw = nw_ref[0]

    def out_body(j, carry):
        rows = pl.ds(pl.multiple_of(j * L, L), L)
        step_f = n_ctx + j
        step_b = n_ctx + n_lat - 1 - j
        q = q_ref[0, rows, :]
        k = k_ref[0, rows, :]
        v = v_ref[0, rows, :]
        mm_f = to_rows(stat(step_f, MM, stream_f))
        mm_b = to_rows(stat(step_b, MM, stream_b))
        s = _dot_nt(q, k)
        d_f = jnp.where(tril, jnp.exp2(stat(step_f, A, stream_f) - mm_f), 0.0)
        d_b = jnp.where(triu, jnp.exp2(stat(step_b, A, stream_b) - mm_b), 0.0)
        iw_f = jnp.exp2(stat(step_f, M0, stream_f) - mm_f)
        iw_b = jnp.exp2(stat(step_b, M0, stream_b) - mm_b)
        qf = q.astype(F32)
        vext = jnp.concatenate([v, ones], axis=1)
        lhs_f = jnp.concatenate([(s * d_f).astype(BF16), (qf * iw_f).astype(BF16)], axis=1)
        lhs_b = jnp.concatenate([(s * d_b).astype(BF16), (qf * iw_b).astype(BF16)], axis=1)
        tot_f = _dot(lhs_f, jnp.concatenate([vext, sf_ref[j]], axis=0))
        tot_b = _dot(lhs_b, jnp.concatenate([vext, sb_ref[j]], axis=0))
        D = HEAD_DIM
        h = (tot_f[:, :D] / jnp.maximum(jnp.abs(tot_f[:, D:]), to_rows(stat(step_f, EN, stream_f)))
             + tot_b[:, :D] / jnp.maximum(jnp.abs(tot_b[:, D:]), to_rows(stat(step_b, EN, stream_b))))
        gate = _sigmoid(o_ref[0, rows, :].astype(F32))
        y_ref[0, rows, :] = _head_norm_gate(h, gate, nw).astype(BF16)
        return carry

    lax.fori_loop(0, n_lat, out_body, 0, unroll=unroll)


def _head_slab(t, group):
    per_group = N_HEADS // HEADS_PER_STEP
    return pl.BlockSpec((None, HEADS_PER_STEP, t, HEAD_DIM), lambda i, h: (i, group * per_group + h, 0, 0))


def _head_param(*shape):
    return pl.BlockSpec((HEADS_PER_STEP,) + shape, lambda i, h: (h,) + (0,) * len(shape))


def _mlstm(pm, pmc, stats, nw):
    b, _, t, _ = pm.shape
    tc = pmc.shape[2]
    n_lat, n_ctx = t // CHUNK, tc // CHUNK
    H, hps = N_HEADS, HEADS_PER_STEP
    blk = lambda off: _head_slab(t, off // H)
    cblk = lambda off: _head_slab(tc, off // H - 1)
    return pl.pallas_call(
        functools.partial(_per_head(_mlstm_head, n_shared=1), heads_per_step=hps, n_blocked_tail=2,
                          n_lat=n_lat, n_ctx=n_ctx, unroll=min(n_lat, MIXER_UNROLL)),
        grid=(b, H // hps),
        in_specs=[blk(0), blk(H), blk(2 * H), blk(3 * H), cblk(H), cblk(2 * H),
                  pl.BlockSpec(stats.shape, lambda i, h: (0, 0, 0, 0)),
                  _head_param(1, HEAD_DIM)],
        out_specs=_head_slab(t, 0),
        out_shape=jax.ShapeDtypeStruct((b, H, t, HEAD_DIM), BF16),
        scratch_shapes=[pltpu.VMEM((hps, n_lat, HEAD_DIM, HEAD_DIM + LANES), BF16),
                        pltpu.VMEM((hps, n_lat, HEAD_DIM, HEAD_DIM + LANES), BF16),
                        pltpu.VMEM((hps, HEAD_DIM, HEAD_DIM + LANES), F32),
                        pltpu.VMEM((hps, HEAD_DIM, HEAD_DIM + LANES), F32)],
        compiler_params=_params("arbitrary", "arbitrary"),
    )(pm, pm, pm, pm, pmc, pmc, stats, nw)


def _ret_head(head, q_ref, k_ref, v_ref, o_ref, kc_ref, vc_ref, dl_ref, nw_ref, y_ref,
                sf_ref, sb_ref, rf_ref, rb_ref, *, n_lat, n_ctx, unroll):
    L = CHUNK
    lg = _log_sigmoid(dl_ref[0])
    lg_f, lg_b = lg[0:1, :], lg[1:2, :]
    r_i = lax.broadcasted_iota(jnp.int32, (L, L), 0)
    c_i = lax.broadcasted_iota(jnp.int32, (L, L), 1)
    r_f = r_i.astype(F32)
    rel = (r_i - c_i).astype(F32)
    kw_f = jnp.exp((L - 1.0 - r_f) * lg_f)
    kw_b = jnp.exp(r_f * lg_b)
    in_f = jnp.exp((r_f + 1.0) * lg_f)
    in_b = jnp.exp((L - r_f) * lg_b)
    dch_f = jnp.exp(L * lg_f)
    dch_b = jnp.exp(L * lg_b)
    d_mat = (jnp.where(rel >= 0, jnp.exp(jnp.maximum(rel, 0.0) * lg_f), 0.0)
             + jnp.where(rel <= 0, jnp.exp(jnp.maximum(-rel, 0.0) * lg_b), 0.0))

    def update(state_ref, k, v, kw, dch):
        wk = (k.astype(F32) * kw).astype(BF16)
        state_ref[...] = dch * state_ref[...] + _dot_tn(wk, v)

    rf_ref[...] = jnp.zeros_like(rf_ref)
    rb_ref[...] = jnp.zeros_like(rb_ref)
    for c in range(n_ctx):
        rows = slice(c * L, (c + 1) * L)
        update(rf_ref, kc_ref[0, rows, :], vc_ref[0, rows, :], kw_f, dch_f)
    for c in reversed(range(n_ctx)):
        rows = slice(c * L, (c + 1) * L)
        update(rb_ref, kc_ref[0, rows, :], vc_ref[0, rows, :], kw_b, dch_b)

    def state_body(i, carry):
        jb = n_lat - 1 - i
        rowf = pl.ds(pl.multiple_of(i * L, L), L)
        rowb = pl.ds(pl.multiple_of(jb * L, L), L)
        sf_ref[i] = rf_ref[...].astype(BF16)
        sb_ref[jb] = rb_ref[...].astype(BF16)
        update(rf_ref, k_ref[0, rowf, :], v_ref[0, rowf, :], kw_f, dch_f)
        update(rb_ref, k_ref[0, rowb, :], v_ref[0, rowb, :], kw_b, dch_b)
        return carry

    lax.fori_loop(0, n_lat, state_body, 0, unroll=unroll)
    nw = nw_ref[0]

    def out_body(j, carry):
        rows = pl.ds(pl.multiple_of(j * L, L), L)
        q = q_ref[0, rows, :]
        s = _dot_nt(q, k_ref[0, rows, :])
        intra = _dot((s * d_mat).astype(BF16), v_ref[0, rows, :])
        inter = _dot(q, jnp.concatenate([sf_ref[j], sb_ref[j]], axis=1))
        out = intra + in_f * inter[:, :HEAD_DIM] + in_b * inter[:, HEAD_DIM:]
        og = o_ref[0, rows, :].astype(F32)
        y_ref[0, rows, :] = _head_norm_gate(out, og * _sigmoid(og), nw).astype(BF16)
        return carry

    lax.fori_loop(0, n_lat, out_body, 0, unroll=unroll)


def _ret(pr, prc, dl, nw):
    b, _, t, _ = pr.shape
    tc = prc.shape[2]
    n_lat, n_ctx = t // CHUNK, tc // CHUNK
    H, hps = N_HEADS, HEADS_PER_STEP
    blk = lambda off: _head_slab(t, off // H)
    cblk = lambda off: _head_slab(tc, off // H - 1)
    return pl.pallas_call(
        functools.partial(_per_head(_ret_head, n_shared=0), heads_per_step=hps, n_blocked_tail=3,
                          n_lat=n_lat, n_ctx=n_ctx, unroll=min(n_lat, MIXER_UNROLL)),
        grid=(b, H // hps),
        in_specs=[blk(0), blk(H), blk(2 * H), blk(3 * H), cblk(H), cblk(2 * H),
                  _head_param(2, LANES), _head_param(1, HEAD_DIM)],
        out_specs=_head_slab(t, 0),
        out_shape=jax.ShapeDtypeStruct((b, H, t, HEAD_DIM), BF16),
        scratch_shapes=[pltpu.VMEM((hps, n_lat, HEAD_DIM, HEAD_DIM), BF16),
                        pltpu.VMEM((hps, n_lat, HEAD_DIM, HEAD_DIM), BF16),
                        pltpu.VMEM((hps, HEAD_DIM, HEAD_DIM), F32),
                        pltpu.VMEM((hps, HEAD_DIM, HEAD_DIM), F32)],
        compiler_params=_params("arbitrary", "arbitrary"),
    )(pr, pr, pr, pr, prc, prc, dl, nw)


def _outproj_kernel(ym_ref, yr_ref, x_ref, wo_ref, g1_ref, nw_ref, sh_ref, sc_ref, wrt_ref,
                    xl_ref, h2_ref, aff_ref, *, n_experts):
    heads = [ym_ref[0, h] for h in range(N_HEADS)] + [yr_ref[0, h] for h in range(N_HEADS)]
    y = _dot(jnp.concatenate(heads, axis=1), wo_ref[...])
    xl = x_ref[0] + g1_ref[0] * y
    xl_ref[0] = xl
    h2 = _rms_mod(xl, nw_ref[...], sh_ref[0], sc_ref[0])
    h2_ref[0] = h2.reshape(h2.shape[0], h2.shape[1] // LANES, LANES)
    h_hi = h2.astype(BF16)
    h_lo = (h2 - h_hi.astype(F32)).astype(BF16)
    p_hi = _dot(h_hi, wrt_ref[...])
    p_lo = _dot(h_lo, wrt_ref[...])
    logits = p_hi + pltpu.roll(p_hi, LANES - n_experts, 1) + p_lo
    lane = lax.broadcasted_iota(jnp.int32, logits.shape, 1)
    logits = jnp.where(lane < n_experts, logits, -jnp.inf)
    e = jnp.exp(logits - jnp.max(logits, axis=-1, keepdims=True))
    aff_t = (e / jnp.sum(e, axis=-1, keepdims=True)).T[:n_experts]
    aff_ref[0] = aff_t.reshape(n_experts, aff_t.shape[1] // LANES, LANES)


def _outproj(ym, yr, x, wo, g1, nw, sh, sc, wrt, n_experts):
    b, t, d = x.shape
    tm = _tile(t, 8 * LANES)
    slab = pl.BlockSpec((1, N_HEADS, tm, HEAD_DIM), lambda i, j: (i, 0, j, 0))
    const = lambda i, j: (0, 0)
    per_b = lambda i, j: (i, 0, 0)
    tok = lambda i, j: (i, j, 0)
    return pl.pallas_call(
        functools.partial(_outproj_kernel, n_experts=n_experts),
        grid=(b, t // tm),
        in_specs=[slab, slab,
                  pl.BlockSpec((1, tm, d), tok), pl.BlockSpec(wo.shape, const),
                  pl.BlockSpec((1, 1, d), per_b), pl.BlockSpec((1, d), const),
                  pl.BlockSpec((1, 1, d), per_b), pl.BlockSpec((1, 1, d), per_b),
                  pl.BlockSpec(wrt.shape, const)],
        out_specs=[pl.BlockSpec((1, tm, d), tok),
                   pl.BlockSpec((1, tm, d // LANES, LANES), lambda i, j: (i, j, 0, 0)),
                   pl.BlockSpec((1, n_experts, tm // LANES, LANES), lambda i, j: (i, 0, j, 0))],
        out_shape=[jax.ShapeDtypeStruct((b, t, d), F32),
                   jax.ShapeDtypeStruct((b, t, d // LANES, LANES), F32),
                   jax.ShapeDtypeStruct((b, n_experts, t // LANES, LANES), F32)],
        compiler_params=_params("arbitrary", "arbitrary"),
    )(ym, yr, x, wo, g1, nw, sh, sc, wrt)


def _select_top(a, cap):
    n_e, n_b, _ = a.shape

    ones = jnp.ones((LANES, LANES), BF16)

    def count(mask):
        per_block = _dot(mask.astype(F32).astype(BF16).reshape(n_e * n_b, LANES), ones)
        return jnp.sum(per_block.reshape(n_e, n_b, LANES), axis=1, keepdims=True).astype(jnp.int32)

    def search(i, thr):
        cand = thr | jnp.left_shift(jnp.int32(1), 30 - i)
        return jnp.where(count(a >= pltpu.bitcast(cand, F32)) >= cap, cand, thr)

    thr = lax.fori_loop(0, 31, search, jnp.zeros((n_e, 1, LANES), jnp.int32))
    above = a >= pltpu.bitcast(thr + 1, F32)
    band = (a >= pltpu.bitcast(thr, F32)) & jnp.logical_not(above)
    need = cap - count(above)[:, :, 0:1]
    tok = (lax.broadcasted_iota(jnp.int32, a.shape, 1) * LANES
           + lax.broadcasted_iota(jnp.int32, a.shape, 2))

    def take_one(_, carry):
        sel, band, need = carry
        in_band = band > 0
        best = jnp.max(jnp.max(jnp.where(in_band, a, -1.0), axis=2, keepdims=True), axis=1, keepdims=True)
        cand = jnp.where(in_band & (a == best), tok, n_b * LANES)
        first = jnp.min(jnp.min(cand, axis=2, keepdims=True), axis=1, keepdims=True)
        take = ((tok == first) & (need > 0)).astype(jnp.int32)
        return sel | take, band - take, need - 1

    sel, _, _ = lax.fori_loop(0, jnp.max(need), take_one,
                              (above.astype(jnp.int32), band.astype(jnp.int32), need))
    return sel


def _route_kernel(aff_ref, pos_ref, idx_ref, gate_ref, start_ref, sel_ref, *, cap):
    i = pl.program_id(0)
    n_s, n_e, n_b, _ = aff_ref.shape

    @pl.when(i == 0)
    def _():
        sel_all = _select_top(aff_ref[...].reshape(n_s * n_e, n_b, LANES), cap)
        sel_ref[...] = sel_all.reshape(n_s, n_e, n_b, LANES)

    a = aff_ref[i]
    sel = sel_ref[i] > 0

    rows = n_e * n_b
    u_i = lax.broadcasted_iota(jnp.int32, (LANES, LANES), 0)
    s_i = lax.broadcasted_iota(jnp.int32, (LANES, LANES), 1)
    incl = (u_i <= s_i).astype(BF16)
    ones = jnp.ones((LANES, LANES), BF16)
    r_i = lax.broadcasted_iota(jnp.int32, (rows, rows), 0)
    c_i = lax.broadcasted_iota(jnp.int32, (rows, rows), 1)
    before = ((r_i // n_b == c_i // n_b) & (c_i < r_i)).astype(BF16)

    x = sel.astype(F32).reshape(rows, LANES)
    xb = x.astype(BF16)
    block_tot = _dot(xb, ones).astype(BF16)
    start = _dot(before, block_tot)
    pos = jnp.where(x > 0, start + _dot(xb, incl) - x, -1.0).astype(jnp.int32)
    pos_ref[0] = pos.reshape(n_e, n_b, LANES)
    start_ref[0] = start.astype(jnp.int32)

    pad = (-rows) % LANES
    def pad_rows(m):
        return m if pad == 0 else jnp.concatenate([m, jnp.zeros((pad, LANES), m.dtype)], axis=0)
    start_p = pad_rows(start)
    end_p = pad_rows(start + block_tot.astype(F32))
    pos_p = pad_rows(pos)
    pos_p = jnp.where(pos_p < 0, 1023, pos_p)
    a_p = pad_rows(a.reshape(rows, LANES))
    a_hi = a_p.astype(BF16)
    a_mid = (a_p - a_hi.astype(F32)).astype(BF16)
    a_lo = (a_p - a_hi.astype(F32) - a_mid.astype(F32)).astype(BF16)
    per_group = LANES // n_b
    j_col = lax.broadcasted_iota(jnp.int32, (cap, LANES), 0).astype(F32)
    lane = lax.broadcasted_iota(jnp.int32, (cap, LANES), 1)
    lane_f = lane.astype(F32)
    block_f = (lane % n_b).astype(F32)
    row0 = pl.program_id(0) * (n_b * LANES)
    for g in range((rows + pad) // LANES):
        rs = slice(g * LANES, (g + 1) * LANES)
        s_row = start_p[rs].T[0:1, :]
        e_row = end_p[rs].T[0:1, :]
        hi = (pos_p[rs] >> 4).astype(F32).astype(BF16)
        lo = (pos_p[rs] & 15).astype(F32).astype(BF16)
        for el in range(per_group):
            e = g * per_group + el
            if e >= n_e:
                break
            mine = (j_col >= s_row) & (j_col < e_row) & ((lane // n_b) == el)
            mine_b = mine.astype(F32).astype(BF16)
            slots = 16.0 * _dot(mine_b, hi) + _dot(mine_b, lo)
            hit = slots == j_col
            in_block = jnp.sum(jnp.where(hit, lane_f, 0.0), axis=1, keepdims=True)
            block = jnp.sum(jnp.where(mine, block_f, 0.0), axis=1, keepdims=True)
            a_blk = _dot(mine_b, a_hi[rs]) + _dot(mine_b, a_mid[rs]) + _dot(mine_b, a_lo[rs])
            gate = jnp.sum(jnp.where(hit, a_blk, 0.0), axis=1, keepdims=True)
            idx_ref[0, e] = _col_to_row(block * LANES + in_block).astype(jnp.int32) + row0
            gate_ref[0, e] = _col_to_row(gate)


def _route(aff_t, cap):
    b, n_e, n_b, _ = aff_t.shape
    spec = pl.BlockSpec((1, n_e, n_b, LANES), lambda i: (i, 0, 0, 0))
    slot_spec = pl.BlockSpec((1, n_e, 1, cap), lambda i: (i, 0, 0, 0))
    return pl.pallas_call(
        functools.partial(_route_kernel, cap=cap),
        grid=(b,), in_specs=[pl.BlockSpec(aff_t.shape, lambda i: (0, 0, 0, 0))],
        out_specs=[spec, slot_spec, slot_spec, pl.BlockSpec((1, n_e * n_b, LANES), lambda i: (i, 0, 0))],
        scratch_shapes=[pltpu.VMEM(aff_t.shape, jnp.int32)],
        out_shape=[jax.ShapeDtypeStruct(aff_t.shape, jnp.int32),
                   jax.ShapeDtypeStruct((b, n_e, 1, cap), jnp.int32),
                   jax.ShapeDtypeStruct((b, n_e, 1, cap), F32),
                   jax.ShapeDtypeStruct((b, n_e * n_b, LANES), jnp.int32)],
        compiler_params=_params("arbitrary"),
    )(aff_t)


def _ffn_kernel(idx_ref, h_hbm, gate_ref, wg_ref, wu_ref, wd_ref, wg2_ref, wu2_ref, wd2_ref, y_ref,
                stage_ref, x_ref, acc_ref, sem, *, m, per_step, n_tiles):
    e, f = pl.program_id(0), pl.program_id(1)
    n_e, n_f = pl.num_programs(0), pl.num_programs(1)
    rows_per_expert = stage_ref.shape[0]

    def row_copy(base, j):
        return pltpu.make_async_copy(h_hbm.at[pl.ds(idx_ref[base + j], 1)],
                                     stage_ref.at[pl.ds(j, 1)], sem.at[0])

    def wait_rows():
        pltpu.make_async_copy(h_hbm.at[pl.ds(0, rows_per_expert)], stage_ref, sem.at[0]).wait()

    @pl.when((e == 0) & (f == 0))
    def _():
        def start_row(j, carry):
            row_copy(0, j).start()
            return carry
        lax.fori_loop(0, rows_per_expert, start_row, 0)

    @pl.when(f == 0)
    def _():
        wait_rows()
        x_ref[...] = stage_ref[:m].reshape(x_ref.shape).astype(BF16)
        acc_ref[...] = jnp.zeros_like(acc_ref)

    nxt = jnp.minimum(e + 1, n_e - 1) * rows_per_expert
    for i in range(per_step):
        row_copy(nxt, f * per_step + i).start()

    def ff_tile(wg, wu, wd):
        x = x_ref[...]
        a = _dot(x, wg[0].astype(BF16))
        u = _dot(x, wu[0].astype(BF16))
        mid = (a * _sigmoid(a) * u).astype(BF16)
        acc_ref[...] += _dot(mid, wd[0].astype(BF16))

    ff_tile(wg_ref, wu_ref, wd_ref)

    @pl.when(2 * f + 1 < n_tiles)
    def _():
        ff_tile(wg2_ref, wu2_ref, wd2_ref)

    @pl.when(f == n_f - 1)
    def _():
        gate = _row_to_cols(gate_ref[0])
        for c in range(0, y_ref.shape[2], LANES):
            y_ref[0, :, c:c + LANES] = (acc_ref[:, c:c + LANES] * gate).astype(BF16)

    @pl.when((e == n_e - 1) & (f == n_f - 1))
    def _():
        wait_rows()


def _ffn(idx, gate, h_rows, wg, wu, wd):
    n_e, m = idx.shape
    d = h_rows.shape[1] * LANES
    ff = wg.shape[2]
    tf = _tile(ff, 256)
    n_tiles = ff // tf
    n_f = -(-n_tiles // 2)
    per_step = -(-m // (8 * n_f)) * 8
    idx = jnp.pad(idx, ((0, 0), (0, n_f * per_step - m))).reshape(-1)
    second = lambda f: jnp.minimum(2 * f + 1, n_tiles - 1)
    return pl.pallas_call(
        functools.partial(_ffn_kernel, m=m, per_step=per_step, n_tiles=n_tiles),
        grid_spec=pltpu.PrefetchScalarGridSpec(
            num_scalar_prefetch=1,
            grid=(n_e, n_f),
            in_specs=[pl.BlockSpec(memory_space=pl.ANY),
                      pl.BlockSpec((1, 1, m), lambda e, f, idx: (e, 0, 0)),
                      pl.BlockSpec((1, d, tf), lambda e, f, idx: (e, 0, 2 * f)),
                      pl.BlockSpec((1, d, tf), lambda e, f, idx: (e, 0, 2 * f)),
                      pl.BlockSpec((1, tf, d), lambda e, f, idx: (e, 2 * f, 0)),
                      pl.BlockSpec((1, d, tf), lambda e, f, idx: (e, 0, second(f))),
                      pl.BlockSpec((1, d, tf), lambda e, f, idx: (e, 0, second(f))),
                      pl.BlockSpec((1, tf, d), lambda e, f, idx: (e, second(f), 0))],
            out_specs=pl.BlockSpec((1, m, d), lambda e, f, idx: (e, 0, 0)),
            scratch_shapes=[pltpu.VMEM((n_f * per_step, d // LANES, LANES), F32),
                            pltpu.VMEM((m, d), BF16),
                            pltpu.VMEM((m, d), F32),
                            pltpu.SemaphoreType.DMA((1,))]),
        out_shape=jax.ShapeDtypeStruct((n_e, m, d), BF16),
        compiler_params=_params("arbitrary", "arbitrary"),
    )(idx, h_rows, gate, wg, wu, wd, wg, wu, wd)


def _combine_kernel(start_ref, y_ref, pos_ref, xl_ref, g2_ref, fw_ref, o_ref, yc_ref, acc_ref,
                    *, n_experts, cap, n_b, win):
    i, j = pl.program_id(0), pl.program_id(1)
    tt = xl_ref.shape[1]
    blocks_per_tile = tt // LANES
    lane = lax.broadcasted_iota(jnp.int32, (tt, win), 1).astype(F32)
    r_i = lax.broadcasted_iota(jnp.int32, (2 * n_experts, n_experts * win), 0)
    c_e = lax.broadcasted_iota(jnp.int32, (2 * n_experts, n_experts * win), 1) // win
    spread = jnp.where(r_i == c_e, 16.0, jnp.where(r_i == c_e + n_experts, 1.0, 0.0)).astype(BF16)
    slots = _dot(pos_ref[0], spread)
    last_tile = j == pl.num_programs(1) - 1
    first, n_pass = [], 0
    for e in range(n_experts):
        row = (i * n_experts + e) * n_b
        lo = start_ref[row + j * blocks_per_tile]
        hi = jnp.where(last_tile, cap, start_ref[row + jnp.minimum((j + 1) * blocks_per_tile, n_b - 1)])
        w0 = jnp.minimum((lo // 16) * 16, cap - win)
        first.append(w0)
        n_pass = jnp.maximum(n_pass, jnp.where(hi > lo, (hi - w0 + win - 1) // win, 0))
    group = min(n_experts, COMBINE_GROUP)

    def add_pass(p, acc):
        for g0 in range(0, n_experts, group):
            hits = []
            for e in range(g0, g0 + group):
                done = first[e] + p * win
                off = pl.multiple_of(jnp.minimum(done, cap - win), 16)
                yc_ref[e * win:(e + 1) * win, :] = y_ref[e, 0, pl.ds(off, win), :]
                pe = slots[:, e * win:(e + 1) * win]
                hits.append(((pe - off.astype(F32)) == lane) & (pe >= done.astype(F32)))
            onehot = jnp.concatenate(hits, axis=1).astype(F32).astype(BF16)
            acc = acc + _dot(onehot, yc_ref[g0 * win:(g0 + group) * win, :])
        return acc

    acc_ref[...] = add_pass(0, jnp.zeros(acc_ref.shape, F32))

    def extra_pass(p, carry):
        acc_ref[...] = add_pass(p, acc_ref[...])
        return carry

    lax.fori_loop(1, n_pass, extra_pass, 0)
    xl = xl_ref[0] + g2_ref[0] * acc_ref[...]
    ms = jnp.mean(xl * xl, axis=-1, keepdims=True)
    o_ref[0] = xl * lax.rsqrt(ms + EPS) * fw_ref[...]


def _combine(starts, y, pos_col, xl, g2, fw, cap):
    b, t, d = xl.shape
    n_e = y.shape[0]
    tt = _tile(t, 512)
    win = min(cap, COMBINE_WINDOW)
    assert cap % 16 == 0 and win % 16 == 0 and tt % LANES == 0 and cap + win <= UNSELECTED_SLOT
    tok = lambda i, j, st: (i, j, 0)
    return pl.pallas_call(
        functools.partial(_combine_kernel, n_experts=n_e, cap=cap, n_b=t // LANES, win=win),
        grid_spec=pltpu.PrefetchScalarGridSpec(
            num_scalar_prefetch=1,
            grid=(b, t // tt),
            in_specs=[pl.BlockSpec((n_e, 1, cap, d), lambda i, j, st: (0, i, 0, 0)),
                      pl.BlockSpec((1, tt, 2 * n_e), tok),
                      pl.BlockSpec((1, tt, d), tok),
                      pl.BlockSpec((1, 1, d), lambda i, j, st: (i, 0, 0)),
                      pl.BlockSpec((1, d), lambda i, j, st: (0, 0))],
            out_specs=pl.BlockSpec((1, tt, d), tok),
            scratch_shapes=[pltpu.VMEM((n_e * win, d), BF16), pltpu.VMEM((tt, d), F32)]),
        out_shape=jax.ShapeDtypeStruct((b, t, d), F32),
        compiler_params=_params("arbitrary", "arbitrary"),
    )(starts, y, pos_col, xl, g2, fw)


def _rope_tables(t):
    n_freq = HEAD_DIM // 4
    rows = jnp.repeat(jnp.arange(t // GRID_W, dtype=F32), GRID_W)
    cols = jnp.tile(jnp.arange(GRID_W, dtype=F32), t // GRID_W)
    inv_freq = ROPE_BASE ** (-jnp.arange(n_freq, dtype=F32) / n_freq)
    ang_r, ang_c = rows[:, None] * inv_freq, cols[:, None] * inv_freq
    cos = jnp.concatenate([jnp.cos(ang_r)] * 2 + [jnp.cos(ang_c)] * 2, axis=-1)
    sin = jnp.concatenate([-jnp.sin(ang_r), jnp.sin(ang_r), -jnp.sin(ang_c), jnp.sin(ang_c)], axis=-1)
    return jnp.tile(cos, (1, N_HEADS)), jnp.tile(sin, (1, N_HEADS))


def kernel(x, c, ctx, c_ctx, w_ada, b_ada, norm1_w, norm2_w, w_in, mlstm_gate_bias, ret_decay_logit,
           mlstm_norm_w, ret_norm_w, w_out, w_router, w_gate, w_up, w_down, final_norm_w):
    B, T, D = x.shape
    Tc = ctx.shape[1]
    H = N_HEADS
    width = H * HEAD_DIM
    assert D == 2 * width and w_ada.shape[0] == 1
    assert T % CHUNK == 0 and Tc % CHUNK == 0 and T % GRID_W == 0
    n_lat, n_ctx = T // CHUNK, Tc // CHUNK
    n_experts = w_router.shape[2]
    cap = EC_CAPACITY_FACTOR * T // n_experts
    n_gate = 4 * H

    pad = (-(B + 1)) % 8
    cc = jnp.concatenate([c, c_ctx[None], jnp.zeros((pad, D), F32)], axis=0)
    mod = _ada(cc, w_ada[0], b_ada[0])
    sh1, sc1, g1, sh2, sc2, g2 = [mod[:B, None, i * D:(i + 1) * D] for i in range(6)]
    csh1, csc1 = [jnp.broadcast_to(mod[B, i * D:(i + 1) * D], (B, 1, D)) for i in range(2)]

    assert (4 * width + n_gate) % 8 == 0 and w_in.shape[2] == 8 * width + n_gate
    w_t = jnp.swapaxes(w_in, 1, 2)
    gb = jnp.pad(mlstm_gate_bias[0], (0, LANES - n_gate)).reshape(1, LANES)
    nw1 = norm1_w[0].reshape(1, D)
    cos, sin = _rope_tables(T)
    pm, gl, pr = _inproj(x, nw1, sh1, sc1, w_t, gb, cos, sin, groups=(0, 1, 2, 3))
    pmc, gc, prc = _inproj(ctx, nw1, csh1, csc1, w_t, gb,
                           jnp.ones((Tc, width), F32), jnp.zeros((Tc, width), F32), groups=(1, 2))

    ym = _mlstm(pm, pmc, _gateprep(gl, gc), mlstm_norm_w[0].reshape(H, 1, HEAD_DIM))
    dl = jnp.broadcast_to(jnp.swapaxes(ret_decay_logit[0], 0, 1)[:, :, None], (H, 2, LANES))
    yr = _ret(pr, prc, dl, ret_norm_w[0].reshape(H, 1, HEAD_DIM))

    wr_hi = w_router[0].astype(BF16)
    wr_lo = (w_router[0] - wr_hi.astype(F32)).astype(BF16)
    wrt = jnp.pad(jnp.concatenate([wr_hi, wr_lo], axis=1), ((0, 0), (0, LANES - 2 * n_experts)))
    xl, h2, aff_t = _outproj(ym, yr, x, w_out[0].astype(BF16), g1, norm2_w[0].reshape(1, D), sh2, sc2,
                             wrt, n_experts)

    pos, idx, gate, starts = _route(aff_t, cap)
    by_expert = lambda a: jnp.swapaxes(a.reshape(B, n_experts, cap), 0, 1).reshape(n_experts, B * cap)
    y = _ffn(by_expert(idx), by_expert(gate)[:, None, :], h2.reshape(B * T, D // LANES, LANES),
             w_gate[0], w_up[0], w_down[0])
    pos_t = jnp.swapaxes(pos.reshape(B, n_experts, T), 1, 2)
    pos_t = jnp.where(pos_t < 0, UNSELECTED_SLOT, pos_t)
    pos_hl = jnp.concatenate([pos_t >> 4, pos_t & 15], axis=-1).astype(BF16)
    return _combine(starts[:, :, 0].reshape(-1), y.reshape(n_experts, B, cap, D), pos_hl, xl, g2,
                    final_norm_w.reshape(1, D), cap)
```

```python
import functools

import jax
import jax.numpy as jnp
from jax import lax
from jax.experimental import pallas as pl
from jax.experimental.pallas import tpu as pltpu

F32 = jnp.float32
BF16 = jnp.bfloat16

CHUNK = 128
GRID_W = 64
N_HEADS = 4
HEAD_DIM = 128
N_GATE = 4 * N_HEADS
ROPE_BASE = 10000.0
EPS = 1e-6
LOG2E = 1.4426950408889634
EC_CAPACITY_FACTOR = 2
LANES = 128
VMEM_LIMIT_BYTES = 56 * 1024 * 1024
COMBINE_WINDOW = 128
UNSELECTED_SLOT = 1023
COMBINE_GROUP = 4
FF_TILES_PER_STEP = 3
HEADS_PER_STEP = 2
MIXER_UNROLL = 32
GATE_UNROLL = 8


def _params(*sem):
    return pltpu.CompilerParams(dimension_semantics=sem, vmem_limit_bytes=VMEM_LIMIT_BYTES)


def _dot(a, b):
    return jnp.dot(a, b, preferred_element_type=F32)


def _dot_nt(a, b):
    return lax.dot_general(a, b, (((1,), (1,)), ((), ())), preferred_element_type=F32)


def _dot_tn(a, b):
    return lax.dot_general(a, b, (((0,), (0,)), ((), ())), preferred_element_type=F32)


def _sigmoid(x):
    return 1.0 / (1.0 + jnp.exp(-x))


def _log_sigmoid(x):
    return jnp.minimum(x, 0.0) - jnp.log1p(jnp.exp(-jnp.abs(x)))


def _col_to_row(col):
    return jnp.broadcast_to(col, (col.shape[0], LANES)).T[0:1, :]


def _row_to_cols(row):
    pieces = [jnp.broadcast_to(row[:, c:c + LANES], (LANES, LANES)).T for c in range(0, row.shape[1], LANES)]
    return pieces[0] if len(pieces) == 1 else jnp.concatenate(pieces, axis=0)


def _tile(n, pref):
    t = min(n, pref)
    assert n % t == 0, (n, t)
    return t


def _ada_kernel(c_ref, w_ref, b_ref, o_ref):
    cv = c_ref[...]
    s = (cv * _sigmoid(cv)).astype(BF16)
    o_ref[...] = _dot(s, w_ref[...].astype(BF16)) + b_ref[...]


def _ada(cc, w, b):
    rows, d = cc.shape
    n = w.shape[1]
    tn = _tile(n, 1024)
    return pl.pallas_call(
        _ada_kernel,
        grid=(n // tn,),
        in_specs=[pl.BlockSpec((rows, d), lambda j: (0, 0)),
                  pl.BlockSpec((d, tn), lambda j: (0, j)),
                  pl.BlockSpec((1, tn), lambda j: (0, j))],
        out_specs=pl.BlockSpec((rows, tn), lambda j: (0, j)),
        out_shape=jax.ShapeDtypeStruct((rows, n), F32),
        compiler_params=_params("arbitrary"),
    )(cc, w, b.reshape(1, n))


def _rms_mod(x, nw, sh, sc):
    ms = jnp.mean(x * x, axis=-1, keepdims=True)
    return (x * lax.rsqrt(ms + EPS) * nw) * (1.0 + sc) + sh


def _inproj_kernel(x_ref, nw_ref, sh_ref, sc_ref, wt_ref, gb_ref, cos_ref, sin_ref,
                   pm_ref, g_ref, pr_ref, w_ref, *, width, k_scale, groups):
    n_g = len(groups)
    ret0 = 4 * width + g_ref.shape[1]
    gate_col = n_g * width
    ret_col = gate_col + LANES

    @pl.when((pl.program_id(0) == 0) & (pl.program_id(1) == 0))
    def _():
        for slot, j in enumerate(groups):
            w_ref[:, slot * width:(slot + 1) * width] = wt_ref[j * width:(j + 1) * width, :].T.astype(BF16)
            w_ref[:, ret_col + slot * width:ret_col + (slot + 1) * width] = (
                wt_ref[ret0 + j * width:ret0 + (j + 1) * width, :].T.astype(BF16))
        w_ref[:, gate_col:ret_col] = wt_ref[4 * width:4 * width + LANES, :].T.astype(BF16)

    hb = _rms_mod(x_ref[0], nw_ref[...], sh_ref[0], sc_ref[0]).astype(BF16)
    for slot, j in enumerate(groups):
        acc = _dot(hb, w_ref[:, slot * width:(slot + 1) * width])
        if j == 1:
            acc = acc * k_scale
        for h in range(N_HEADS):
            pm_ref[0, slot * N_HEADS + h] = acc[:, h * HEAD_DIM:(h + 1) * HEAD_DIM].astype(BF16)
    g_ref[0] = (_dot(hb, w_ref[:, gate_col:ret_col]) + gb_ref[...]).T[:g_ref.shape[1]]
    cos = cos_ref[...]
    sin = sin_ref[...]
    lane = lax.broadcasted_iota(jnp.int32, cos.shape, 1)
    even = ((lane // 32) % 2) == 0
    for slot, j in enumerate(groups):
        acc = _dot(hb, w_ref[:, ret_col + slot * width:ret_col + (slot + 1) * width])
        if j == 1:
            acc = acc * k_scale
        if j < 2:
            swapped = jnp.where(even, pltpu.roll(acc, width - 32, 1), pltpu.roll(acc, 32, 1))
            acc = acc * cos + swapped * sin
        for h in range(N_HEADS):
            pr_ref[0, slot * N_HEADS + h] = acc[:, h * HEAD_DIM:(h + 1) * HEAD_DIM].astype(BF16)


def _inproj(x, nw, sh, sc, w_t, gb, cos, sin, groups):
    b, t, d = x.shape
    width = (w_t.shape[1] - N_GATE) // 8
    tm = _tile(t, 512)
    slabs = len(groups) * N_HEADS
    kern = functools.partial(_inproj_kernel, width=width, k_scale=HEAD_DIM ** -0.5, groups=groups)
    const = lambda i, j: (0, 0)
    return pl.pallas_call(
        kern,
        grid=(b, t // tm),
        in_specs=[pl.BlockSpec((1, tm, d), lambda i, j: (i, j, 0)),
                  pl.BlockSpec((1, d), const),
                  pl.BlockSpec((1, 1, d), lambda i, j: (i, 0, 0)),
                  pl.BlockSpec((1, 1, d), lambda i, j: (i, 0, 0)),
                  pl.BlockSpec((None,) + w_t.shape[1:], lambda i, j: (0, 0, 0), pipeline_mode=pl.Buffered(1)),
                  pl.BlockSpec((1, LANES), const),
                  pl.BlockSpec((tm, width), lambda i, j: (j, 0)),
                  pl.BlockSpec((tm, width), lambda i, j: (j, 0))],
        out_specs=[pl.BlockSpec((1, slabs, tm, HEAD_DIM), lambda i, j: (i, 0, j, 0)),
                   pl.BlockSpec((1, N_GATE, tm), lambda i, j: (i, 0, j)),
                   pl.BlockSpec((1, slabs, tm, HEAD_DIM), lambda i, j: (i, 0, j, 0))],
        out_shape=[jax.ShapeDtypeStruct((b, slabs, t, HEAD_DIM), BF16),
                   jax.ShapeDtypeStruct((b, N_GATE, t), F32),
                   jax.ShapeDtypeStruct((b, slabs, t, HEAD_DIM), BF16)],
        scratch_shapes=[pltpu.VMEM((d, 2 * len(groups) * width + LANES), BF16)],
        compiler_params=_params("arbitrary", "arbitrary"),
    )(x, nw, sh, sc, w_t, gb, cos, sin)


def _scan_lanes(x, op, fill, reverse):
    lane = lax.broadcasted_iota(jnp.int32, x.shape, 1)
    sh = 1
    while sh < LANES:
        if reverse:
            x = op(x, jnp.where(lane < LANES - sh, pltpu.roll(x, LANES - sh, 1), fill))
        else:
            x = op(x, jnp.where(lane >= sh, pltpu.roll(x, sh, 1), fill))
        sh *= 2
    return x


def _gateprep_kernel(gl_ref, gc_ref, o_ref, *, n_lat, n_ctx):
    n_b = gl_ref.shape[0]
    half = N_GATE // 2
    rows = n_b * half
    row = lax.broadcasted_iota(jnp.int32, (rows, LANES), 0)
    lane = lax.broadcasted_iota(jnp.int32, (rows, LANES), 1)
    is_fwd = (row % half) < N_HEADS
    last = lane == jnp.where(is_fwd, LANES - 1, 0)

    def chunk_rows(ref, lo, c_fwd, c_bwd):
        def at(c):
            cols = (slice(c * LANES, (c + 1) * LANES) if isinstance(c, int)
                    else pl.ds(pl.multiple_of(c * LANES, LANES), LANES))
            return jnp.concatenate([ref[s, lo:lo + half, cols] for s in range(n_b)], axis=0)
        return jnp.where(is_fwd, at(c_fwd), at(c_bwd))

    def step(c, ig, fg, m):
        lf = _log_sigmoid(fg)
        b = jnp.where(is_fwd, _scan_lanes(lf, jnp.add, 0.0, False), _scan_lanes(lf, jnp.add, 0.0, True))
        b_tot = jnp.sum(jnp.where(last, b, 0.0), axis=1, keepdims=True)
        a = ig - b
        m_new = jnp.maximum(b_tot + m, jnp.max(b_tot + a, axis=1, keepdims=True))
        w = jnp.exp(b_tot + a - m_new)
        decay = jnp.exp(b_tot + m - m_new)
        run = jnp.where(is_fwd, _scan_lanes(a, jnp.maximum, -jnp.inf, False),
                        _scan_lanes(a, jnp.maximum, -jnp.inf, True))
        mm = jnp.maximum(m, run)
        o_ref[c, 0] = a * LOG2E
        o_ref[c, 1] = w
        o_ref[c, 2] = mm * LOG2E
        o_ref[c, 3] = jnp.exp(-(b + mm))
        o_ref[c, 4] = jnp.broadcast_to(decay, (rows, LANES))
        o_ref[c, 5] = jnp.broadcast_to(m * LOG2E, (rows, LANES))
        return m_new

    m = jnp.zeros((rows, 1), F32)
    for c in range(n_ctx):
        m = step(c, chunk_rows(gc_ref, 0, c, n_ctx - 1 - c), chunk_rows(gc_ref, half, c, n_ctx - 1 - c), m)

    def latent(i, m):
        return step(n_ctx + i, chunk_rows(gl_ref, 0, i, n_lat - 1 - i),
                    chunk_rows(gl_ref, half, i, n_lat - 1 - i), m)

    lax.fori_loop(0, n_lat, latent, m, unroll=min(n_lat, GATE_UNROLL))


def _gateprep(gl, gc):
    n_lat, n_ctx = gl.shape[2] // CHUNK, gc.shape[2] // CHUNK
    return pl.pallas_call(
        functools.partial(_gateprep_kernel, n_lat=n_lat, n_ctx=n_ctx),
        out_shape=jax.ShapeDtypeStruct((n_ctx + n_lat, 6, gl.shape[0] * N_GATE // 2, LANES), F32),
        compiler_params=pltpu.CompilerParams(vmem_limit_bytes=VMEM_LIMIT_BYTES),
    )(gl, gc)


def _head_norm_gate(h, gate, nw):
    return gate * (h * lax.rsqrt(jnp.mean(h * h, axis=-1, keepdims=True) + EPS)) * nw


def _per_head(head_fn, n_shared):
    def kern(*refs, heads_per_step, n_blocked_tail, **kw):
        blocked, shared = refs[:6], refs[6:6 + n_shared]
        tail = refs[6 + n_shared:6 + n_shared + n_blocked_tail]
        scratch = refs[6 + n_shared + n_blocked_tail:]
        for hh in range(heads_per_step):
            one = lambda r: r.at[pl.ds(hh, 1)]
            head_fn(pl.program_id(1) * heads_per_step + hh, *[one(r) for r in blocked], *shared,
                    *[one(r) for r in tail], *[r.at[hh] for r in scratch], **kw)
    return kern


def _mlstm_head(head, q_ref, k_ref, v_ref, o_ref, kc_ref, vc_ref, st_ref, nw_ref, y_ref,
                  sf_ref, sb_ref, cf_ref, cb_ref, *, n_lat, n_ctx, unroll):
    L = CHUNK
    A, WT, MM, EN, DECAY, M0 = range(6)
    ones = jnp.ones((L, HEAD_DIM), BF16)
    stream_f = pl.program_id(0) * (N_GATE // 2) + head
    stream_b = stream_f + N_HEADS

    def stat(step, plane, stream):
        return st_ref[step, plane, pl.ds(stream, 1), :]

    def to_rows(row):
        return jnp.broadcast_to(row, (L, L)).T

    def update(state_ref, k, v, step, stream):
        wkt = (k.astype(F32).T * stat(step, WT, stream)).astype(BF16)
        u = _dot(wkt, jnp.concatenate([v, ones], axis=1))
        decay = stat(step, DECAY, stream)
        state_ref[...] = jnp.concatenate([decay, decay], axis=1) * state_ref[...] + u

    cf_ref[...] = jnp.zeros_like(cf_ref)
    cb_ref[...] = jnp.zeros_like(cb_ref)
    for c in range(n_ctx):
        rf = slice(c * L, (c + 1) * L)
        rb = slice((n_ctx - 1 - c) * L, (n_ctx - c) * L)
        update(cf_ref, kc_ref[0, rf, :], vc_ref[0, rf, :], c, stream_f)
        update(cb_ref, kc_ref[0, rb, :], vc_ref[0, rb, :], c, stream_b)

    def state_body(i, carry):
        jb = n_lat - 1 - i
        rf = pl.ds(pl.multiple_of(i * L, L), L)
        rb = pl.ds(pl.multiple_of(jb * L, L), L)
        sf_ref[i] = cf_ref[...].astype(BF16)
        sb_ref[jb] = cb_ref[...].astype(BF16)
        update(cf_ref, k_ref[0, rf, :], v_ref[0, rf, :], n_ctx + i, stream_f)
        update(cb_ref, k_ref[0, rb, :], v_ref[0, rb, :], n_ctx + i, stream_b)
        return carry

    lax.fori_loop(0, n_lat, state_body, 0, unroll=unroll)

    r_i = lax.broadcasted_iota(jnp.int32, (L, L), 0)
    c_i = lax.broadcasted_iota(jnp.int32, (L, L), 1)
    tril = c_i <= r_i
    triu = c_i >= r_i
    nw = nw_ref[0]

    def out_body(j, carry):
        rows = pl.ds(pl.multiple_of(j * L, L), L)
        step_f = n_ctx + j
        step_b = n_ctx + n_lat - 1 - j
        q = q_ref[0, rows, :]
        k = k_ref[0, rows, :]
        v = v_ref[0, rows, :]
        mm_f = to_rows(stat(step_f, MM, stream_f))
        mm_b = to_rows(stat(step_b, MM, stream_b))
        s = _dot_nt(q, k)
        d_f = jnp.where(tril, jnp.exp2(stat(step_f, A, stream_f) - mm_f), 0.0)
        d_b = jnp.where(triu, jnp.exp2(stat(step_b, A, stream_b) - mm_b), 0.0)
        iw_f = jnp.exp2(stat(step_f, M0, stream_f) - mm_f)
        iw_b = jnp.exp2(stat(step_b, M0, stream_b) - mm_b)
        qf = q.astype(F32)
        vext = jnp.concatenate([v, ones], axis=1)
        lhs_f = jnp.concatenate([(s * d_f).astype(BF16), (qf * iw_f).astype(BF16)], axis=1)
        lhs_b = jnp.concatenate([(s * d_b).astype(BF16), (qf * iw_b).astype(BF16)], axis=1)
        tot_f = _dot(lhs_f, jnp.concatenate([vext, sf_ref[j]], axis=0))
        tot_b = _dot(lhs_b, jnp.concatenate([vext, sb_ref[j]], axis=0))
        D = HEAD_DIM
        h = (tot_f[:, :D] / jnp.maximum(jnp.abs(tot_f[:, D:]), to_rows(stat(step_f, EN, stream_f)))
             + tot_b[:, :D] / jnp.maximum(jnp.abs(tot_b[:, D:]), to_rows(stat(step_b, EN, stream_b))))
        gate = _sigmoid(o_ref[0, rows, :].astype(F32))
        y_ref[0, rows, :] = _head_norm_gate(h, gate, nw).astype(BF16)
        return carry

    lax.fori_loop(0, n_lat, out_body, 0, unroll=unroll)


def _head_slab(t, group):
    per_group = N_HEADS // HEADS_PER_STEP
    return pl.BlockSpec((None, HEADS_PER_STEP, t, HEAD_DIM), lambda i, h: (i, group * per_group + h, 0, 0))


def _head_param(*shape):
    return pl.BlockSpec((HEADS_PER_STEP,) + shape, lambda i, h: (h,) + (0,) * len(shape))


def _mlstm(pm, pmc, stats, nw):
    b, _, t, _ = pm.shape
    tc = pmc.shape[2]
    n_lat, n_ctx = t // CHUNK, tc // CHUNK
    H, hps = N_HEADS, HEADS_PER_STEP
    blk = lambda off: _head_slab(t, off // H)
    cblk = lambda off: _head_slab(tc, off // H - 1)
    return pl.pallas_call(
        functools.partial(_per_head(_mlstm_head, n_shared=1), heads_per_step=hps, n_blocked_tail=2,
                          n_lat=n_lat, n_ctx=n_ctx, unroll=min(n_lat, MIXER_UNROLL)),
        grid=(b, H // hps),
        in_specs=[blk(0), blk(H), blk(2 * H), blk(3 * H), cblk(H), cblk(2 * H),
                  pl.BlockSpec(stats.shape, lambda i, h: (0, 0, 0, 0)),
                  _head_param(1, HEAD_DIM)],
        out_specs=_head_slab(t, 0),
        out_shape=jax.ShapeDtypeStruct((b, H, t, HEAD_DIM), BF16),
        scratch_shapes=[pltpu.VMEM((hps, n_lat, HEAD_DIM, HEAD_DIM + LANES), BF16),
                        pltpu.VMEM((hps, n_lat, HEAD_DIM, HEAD_DIM + LANES), BF16),
                        pltpu.VMEM((hps, HEAD_DIM, HEAD_DIM + LANES), F32),
                        pltpu.VMEM((hps, HEAD_DIM, HEAD_DIM + LANES), F32)],
        compiler_params=_params("arbitrary", "arbitrary"),
    )(pm, pm, pm, pm, pmc, pmc, stats, nw)


def _ret_head(head, q_ref, k_ref, v_ref, o_ref, kc_ref, vc_ref, dl_ref, nw_ref, y_ref,
                sf_ref, sb_ref, rf_ref, rb_ref, *, n_lat, n_ctx, unroll):
    L = CHUNK
    lg = _log_sigmoid(dl_ref[0])
    lg_f, lg_b = lg[0:1, :], lg[1:2, :]
    r_i = lax.broadcasted_iota(jnp.int32, (L, L), 0)
    c_i = lax.broadcasted_iota(jnp.int32, (L, L), 1)
    r_f = r_i.astype(F32)
    rel = (r_i - c_i).astype(F32)
    kw_f = jnp.exp((L - 1.0 - r_f) * lg_f)
    kw_b = jnp.exp(r_f * lg_b)
    in_f = jnp.exp((r_f + 1.0) * lg_f)
    in_b = jnp.exp((L - r_f) * lg_b)
    dch_f = jnp.exp(L * lg_f)
    dch_b = jnp.exp(L * lg_b)
    d_mat = (jnp.where(rel >= 0, jnp.exp(jnp.maximum(rel, 0.0) * lg_f), 0.0)
             + jnp.where(rel <= 0, jnp.exp(jnp.maximum(-rel, 0.0) * lg_b), 0.0))

    def update(state_ref, k, v, kw, dch):
        wk = (k.astype(F32) * kw).astype(BF16)
        state_ref[...] = dch * state_ref[...] + _dot_tn(wk, v)

    rf_ref[...] = jnp.zeros_like(rf_ref)
    rb_ref[...] = jnp.zeros_like(rb_ref)
    for c in range(n_ctx):
        rows = slice(c * L, (c + 1) * L)
        update(rf_ref, kc_ref[0, rows, :], vc_ref[0, rows, :], kw_f, dch_f)
    for c in reversed(range(n_ctx)):
        rows = slice(c * L, (c + 1) * L)
        update(rb_ref, kc_ref[0, rows, :], vc_ref[0, rows, :], kw_b, dch_b)

    def state_body(i, carry):
        jb = n_lat - 1 - i
        rowf = pl.ds(pl.multiple_of(i * L, L), L)
        rowb = pl.ds(pl.multiple_of(jb * L, L), L)
        sf_ref[i] = rf_ref[...].astype(BF16)
        sb_ref[jb] = rb_ref[...].astype(BF16)
        update(rf_ref, k_ref[0, rowf, :], v_ref[0, rowf, :], kw_f, dch_f)
        update(rb_ref, k_ref[0, rowb, :], v_ref[0, rowb, :], kw_b, dch_b)
        return carry

    lax.fori_loop(0, n_lat, state_body, 0, unroll=unroll)
    nw = nw_ref[0]

    def out_body(j, carry):
        rows = pl.ds(pl.multiple_of(j * L, L), L)
        q = q_ref[0, rows, :]
        s = _dot_nt(q, k_ref[0, rows, :])
        intra = _dot((s * d_mat).astype(BF16), v_ref[0, rows, :])
        inter = _dot(q, jnp.concatenate([sf_ref[j], sb_ref[j]], axis=1))
        out = intra + in_f * inter[:, :HEAD_DIM] + in_b * inter[:, HEAD_DIM:]
        og = o_ref[0, rows, :].astype(F32)
        y_ref[0, rows, :] = _head_norm_gate(out, og * _sigmoid(og), nw).astype(BF16)
        return carry

    lax.fori_loop(0, n_lat, out_body, 0, unroll=unroll)


def _ret(pr, prc, dl, nw):
    b, _, t, _ = pr.shape
    tc = prc.shape[2]
    n_lat, n_ctx = t // CHUNK, tc // CHUNK
    H, hps = N_HEADS, HEADS_PER_STEP
    blk = lambda off: _head_slab(t, off // H)
    cblk = lambda off: _head_slab(tc, off // H - 1)
    return pl.pallas_call(
        functools.partial(_per_head(_ret_head, n_shared=0), heads_per_step=hps, n_blocked_tail=3,
                          n_lat=n_lat, n_ctx=n_ctx, unroll=min(n_lat, MIXER_UNROLL)),
        grid=(b, H // hps),
        in_specs=[blk(0), blk(H), blk(2 * H), blk(3 * H), cblk(H), cblk(2 * H),
                  _head_param(2, LANES), _head_param(1, HEAD_DIM)],
        out_specs=_head_slab(t, 0),
        out_shape=jax.ShapeDtypeStruct((b, H, t, HEAD_DIM), BF16),
        scratch_shapes=[pltpu.VMEM((hps, n_lat, HEAD_DIM, HEAD_DIM), BF16),
                        pltpu.VMEM((hps, n_lat, HEAD_DIM, HEAD_DIM), BF16),
                        pltpu.VMEM((hps, HEAD_DIM, HEAD_DIM), F32),
                        pltpu.VMEM((hps, HEAD_DIM, HEAD_DIM), F32)],
        compiler_params=_params("arbitrary", "arbitrary"),
    )(pr, pr, pr, pr, prc, prc, dl, nw)


def _outproj_kernel(ym_ref, yr_ref, x_ref, wo_ref, g1_ref, nw_ref, sh_ref, sc_ref, wrt_ref,
                    xl_ref, h2_ref, aff_ref, *, n_experts):
    heads = [ym_ref[0, h] for h in range(N_HEADS)] + [yr_ref[0, h] for h in range(N_HEADS)]
    y = _dot(jnp.concatenate(heads, axis=1), wo_ref[...])
    xl = x_ref[0] + g1_ref[0] * y
    xl_ref[0] = xl
    h2 = _rms_mod(xl, nw_ref[...], sh_ref[0], sc_ref[0])
    h2_ref[0] = h2.reshape(h2.shape[0], h2.shape[1] // LANES, LANES)
    h_hi = h2.astype(BF16)
    h_lo = (h2 - h_hi.astype(F32)).astype(BF16)
    p_hi = _dot(h_hi, wrt_ref[...])
    p_lo = _dot(h_lo, wrt_ref[...])
    logits = p_hi + pltpu.roll(p_hi, LANES - n_experts, 1) + p_lo
    lane = lax.broadcasted_iota(jnp.int32, logits.shape, 1)
    logits = jnp.where(lane < n_experts, logits, -jnp.inf)
    e = jnp.exp(logits - jnp.max(logits, axis=-1, keepdims=True))
    aff_t = (e / jnp.sum(e, axis=-1, keepdims=True)).T[:n_experts]
    aff_ref[0] = aff_t.reshape(n_experts, aff_t.shape[1] // LANES, LANES)


def _outproj(ym, yr, x, wo, g1, nw, sh, sc, wrt, n_experts):
    b, t, d = x.shape
    tm = _tile(t, 8 * LANES)
    slab = pl.BlockSpec((1, N_HEADS, tm, HEAD_DIM), lambda i, j: (i, 0, j, 0))
    const = lambda i, j: (0, 0)
    per_b = lambda i, j: (i, 0, 0)
    tok = lambda i, j: (i, j, 0)
    return pl.pallas_call(
        functools.partial(_outproj_kernel, n_experts=n_experts),
        grid=(b, t // tm),
        in_specs=[slab, slab,
                  pl.BlockSpec((1, tm, d), tok), pl.BlockSpec(wo.shape, const),
                  pl.BlockSpec((1, 1, d), per_b), pl.BlockSpec((1, d), const),
                  pl.BlockSpec((1, 1, d), per_b), pl.BlockSpec((1, 1, d), per_b),
                  pl.BlockSpec(wrt.shape, const)],
        out_specs=[pl.BlockSpec((1, tm, d), tok),
                   pl.BlockSpec((1, tm, d // LANES, LANES), lambda i, j: (i, j, 0, 0)),
                   pl.BlockSpec((1, n_experts, tm // LANES, LANES), lambda i, j: (i, 0, j, 0))],
        out_shape=[jax.ShapeDtypeStruct((b, t, d), F32),
                   jax.ShapeDtypeStruct((b, t, d // LANES, LANES), F32),
                   jax.ShapeDtypeStruct((b, n_experts, t // LANES, LANES), F32)],
        compiler_params=_params("arbitrary", "arbitrary"),
    )(ym, yr, x, wo, g1, nw, sh, sc, wrt)


def _select_top(a, cap):
    n_e, n_b, _ = a.shape

    ones = jnp.ones((LANES, LANES), BF16)

    def count(mask):
        per_block = _dot(mask.astype(F32).astype(BF16).reshape(n_e * n_b, LANES), ones)
        return jnp.sum(per_block.reshape(n_e, n_b, LANES), axis=1, keepdims=True).astype(jnp.int32)

    def search(i, thr):
        cand = thr | jnp.left_shift(jnp.int32(1), 30 - i)
        return jnp.where(count(a >= pltpu.bitcast(cand, F32)) >= cap, cand, thr)

    thr = lax.fori_loop(0, 31, search, jnp.zeros((n_e, 1, LANES), jnp.int32))
    above = a >= pltpu.bitcast(thr + 1, F32)
    band = (a >= pltpu.bitcast(thr, F32)) & jnp.logical_not(above)
    need = cap - count(above)[:, :, 0:1]
    tok = (lax.broadcasted_iota(jnp.int32, a.shape, 1) * LANES
           + lax.broadcasted_iota(jnp.int32, a.shape, 2))

    def take_one(_, carry):
        sel, band, need = carry
        in_band = band > 0
        best = jnp.max(jnp.max(jnp.where(in_band, a, -1.0), axis=2, keepdims=True), axis=1, keepdims=True)
        cand = jnp.where(in_band & (a == best), tok, n_b * LANES)
        first = jnp.min(jnp.min(cand, axis=2, keepdims=True), axis=1, keepdims=True)
        take = ((tok == first) & (need > 0)).astype(jnp.int32)
        return sel | take, band - take, need - 1

    sel, _, _ = lax.fori_loop(0, jnp.max(need), take_one,
                              (above.astype(jnp.int32), band.astype(jnp.int32), need))
    return sel


def _route_kernel(aff_ref, pos_ref, idx_ref, gate_ref, start_ref, sel_ref, *, cap):
    i = pl.program_id(0)
    n_s, n_e, n_b, _ = aff_ref.shape

    @pl.when(i == 0)
    def _():
        sel_all = _select_top(aff_ref[...].reshape(n_s * n_e, n_b, LANES), cap)
        sel_ref[...] = sel_all.reshape(n_s, n_e, n_b, LANES)

    a = aff_ref[i]
    sel = sel_ref[i] > 0

    rows = n_e * n_b
    u_i = lax.broadcasted_iota(jnp.int32, (LANES, LANES), 0)
    s_i = lax.broadcasted_iota(jnp.int32, (LANES, LANES), 1)
    incl = (u_i <= s_i).astype(BF16)
    ones = jnp.ones((LANES, LANES), BF16)
    r_i = lax.broadcasted_iota(jnp.int32, (rows, rows), 0)
    c_i = lax.broadcasted_iota(jnp.int32, (rows, rows), 1)
    before = ((r_i // n_b == c_i // n_b) & (c_i < r_i)).astype(BF16)

    x = sel.astype(F32).reshape(rows, LANES)
    xb = x.astype(BF16)
    block_tot = _dot(xb, ones).astype(BF16)
    start = _dot(before, block_tot)
    pos = jnp.where(x > 0, start + _dot(xb, incl) - x, -1.0).astype(jnp.int32)
    pos_ref[0] = pos.reshape(n_e, n_b, LANES)
    start_ref[0] = start.astype(jnp.int32)

    pad = (-rows) % LANES
    def pad_rows(m):
        return m if pad == 0 else jnp.concatenate([m, jnp.zeros((pad, LANES), m.dtype)], axis=0)
    start_p = pad_rows(start)
    end_p = pad_rows(start + block_tot.astype(F32))
    pos_p = pad_rows(pos)
    pos_p = jnp.where(pos_p < 0, 1023, pos_p)
    a_p = pad_rows(a.reshape(rows, LANES))
    a_hi = a_p.astype(BF16)
    a_mid = (a_p - a_hi.astype(F32)).astype(BF16)
    a_lo = (a_p - a_hi.astype(F32) - a_mid.astype(F32)).astype(BF16)
    per_group = LANES // n_b
    j_col = lax.broadcasted_iota(jnp.int32, (cap, LANES), 0).astype(F32)
    lane = lax.broadcasted_iota(jnp.int32, (cap, LANES), 1)
    lane_f = lane.astype(F32)
    block_f = (lane % n_b).astype(F32)
    row0 = pl.program_id(0) * (n_b * LANES)
    for g in range((rows + pad) // LANES):
        rs = slice(g * LANES, (g + 1) * LANES)
        s_row = start_p[rs].T[0:1, :]
        e_row = end_p[rs].T[0:1, :]
        hi = (pos_p[rs] >> 4).astype(F32).astype(BF16)
        lo = (pos_p[rs] & 15).astype(F32).astype(BF16)
        for el in range(per_group):
            e = g * per_group + el
            if e >= n_e:
                break
            mine = (j_col >= s_row) & (j_col < e_row) & ((lane // n_b) == el)
            mine_b = mine.astype(F32).astype(BF16)
            slots = 16.0 * _dot(mine_b, hi) + _dot(mine_b, lo)
            hit = slots == j_col
            in_block = jnp.sum(jnp.where(hit, lane_f, 0.0), axis=1, keepdims=True)
            block = jnp.sum(jnp.where(mine, block_f, 0.0), axis=1, keepdims=True)
            a_blk = _dot(mine_b, a_hi[rs]) + _dot(mine_b, a_mid[rs]) + _dot(mine_b, a_lo[rs])
            gate = jnp.sum(jnp.where(hit, a_blk, 0.0), axis=1, keepdims=True)
            idx_ref[0, e] = _col_to_row(block * LANES + in_block).astype(jnp.int32) + row0
            gate_ref[0, e] = _col_to_row(gate)


def _route(aff_t, cap):
    b, n_e, n_b, _ = aff_t.shape
    spec = pl.BlockSpec((1, n_e, n_b, LANES), lambda i: (i, 0, 0, 0))
    slot_spec = pl.BlockSpec((1, n_e, 1, cap), lambda i: (i, 0, 0, 0))
    return pl.pallas_call(
        functools.partial(_route_kernel, cap=cap),
        grid=(b,), in_specs=[pl.BlockSpec(aff_t.shape, lambda i: (0, 0, 0, 0))],
        out_specs=[spec, slot_spec, slot_spec, pl.BlockSpec((1, n_e * n_b, LANES), lambda i: (i, 0, 0))],
        scratch_shapes=[pltpu.VMEM(aff_t.shape, jnp.int32)],
        out_shape=[jax.ShapeDtypeStruct(aff_t.shape, jnp.int32),
                   jax.ShapeDtypeStruct((b, n_e, 1, cap), jnp.int32),
                   jax.ShapeDtypeStruct((b, n_e, 1, cap), F32),
                   jax.ShapeDtypeStruct((b, n_e * n_b, LANES), jnp.int32)],
        compiler_params=_params("arbitrary"),
    )(aff_t)


def _ffn_kernel(idx_ref, h_hbm, gate_ref, *refs, m, per_step, n_tiles, tiles_per_step):
    w_refs = refs[:3 * tiles_per_step]
    y_ref, stage_ref, x_ref, acc_ref, sem = refs[3 * tiles_per_step:]
    e, f = pl.program_id(0), pl.program_id(1)
    n_e, n_f = pl.num_programs(0), pl.num_programs(1)
    rows_per_expert = stage_ref.shape[0]

    def row_copy(base, j):
        return pltpu.make_async_copy(h_hbm.at[pl.ds(idx_ref[base + j], 1)],
                                     stage_ref.at[pl.ds(j, 1)], sem.at[0])

    def wait_rows():
        pltpu.make_async_copy(h_hbm.at[pl.ds(0, rows_per_expert)], stage_ref, sem.at[0]).wait()

    @pl.when((e == 0) & (f == 0))
    def _():
        def start_row(j, carry):
            row_copy(0, j).start()
            return carry
        lax.fori_loop(0, rows_per_expert, start_row, 0)

    @pl.when(f == 0)
    def _():
        wait_rows()
        x_ref[...] = stage_ref[:m].reshape(x_ref.shape).astype(BF16)
        acc_ref[...] = jnp.zeros_like(acc_ref)

    nxt = jnp.minimum(e + 1, n_e - 1) * rows_per_expert
    for i in range(per_step):
        row_copy(nxt, f * per_step + i).start()

    def ff_tile(wg, wu, wd):
        x = x_ref[...]
        a = _dot(x, wg[0].astype(BF16))
        u = _dot(x, wu[0].astype(BF16))
        mid = (a * _sigmoid(a) * u).astype(BF16)
        acc_ref[...] += _dot(mid, wd[0].astype(BF16))

    ff_tile(*w_refs[:3])
    for k in range(1, tiles_per_step):
        @pl.when(tiles_per_step * f + k < n_tiles)
        def _(k=k):
            ff_tile(*w_refs[3 * k:3 * k + 3])

    @pl.when(f == n_f - 1)
    def _():
        gate = _row_to_cols(gate_ref[0])
        for c in range(0, y_ref.shape[2], LANES):
            y_ref[0, :, c:c + LANES] = (acc_ref[:, c:c + LANES] * gate).astype(BF16)

    @pl.when((e == n_e - 1) & (f == n_f - 1))
    def _():
        wait_rows()


def _ffn(idx, gate, h_rows, wg, wu, wd):
    n_e, m = idx.shape
    d = h_rows.shape[1] * LANES
    ff = wg.shape[2]
    tf = _tile(ff, 256)
    n_tiles = ff // tf
    tps = min(n_tiles, FF_TILES_PER_STEP)
    n_f = -(-n_tiles // tps)
    per_step = -(-m // (8 * n_f)) * 8
    idx = jnp.pad(idx, ((0, 0), (0, n_f * per_step - m))).reshape(-1)
    w_specs = []
    for k in range(tps):
        tile = lambda f, k=k: jnp.minimum(tps * f + k, n_tiles - 1)
        w_specs += [pl.BlockSpec((1, d, tf), lambda e, f, idx, tile=tile: (e, 0, tile(f))),
                    pl.BlockSpec((1, d, tf), lambda e, f, idx, tile=tile: (e, 0, tile(f))),
                    pl.BlockSpec((1, tf, d), lambda e, f, idx, tile=tile: (e, tile(f), 0))]
    return pl.pallas_call(
        functools.partial(_ffn_kernel, m=m, per_step=per_step, n_tiles=n_tiles, tiles_per_step=tps),
        grid_spec=pltpu.PrefetchScalarGridSpec(
            num_scalar_prefetch=1,
            grid=(n_e, n_f),
            in_specs=[pl.BlockSpec(memory_space=pl.ANY),
                      pl.BlockSpec((1, 1, m), lambda e, f, idx: (e, 0, 0))] + w_specs,
            out_specs=pl.BlockSpec((1, m, d), lambda e, f, idx: (e, 0, 0)),
            scratch_shapes=[pltpu.VMEM((n_f * per_step, d // LANES, LANES), F32),
                            pltpu.VMEM((m, d), BF16),
                            pltpu.VMEM((m, d), F32),
                            pltpu.SemaphoreType.DMA((1,))]),
        out_shape=jax.ShapeDtypeStruct((n_e, m, d), BF16),
        compiler_params=_params("arbitrary", "arbitrary"),
    )(idx, h_rows, gate, *([wg, wu, wd] * tps))


def _combine_kernel(start_ref, y_ref, pos_ref, xl_ref, g2_ref, fw_ref, o_ref, yc_ref, acc_ref,
                    *, n_experts, cap, n_b, win):
    i, j = pl.program_id(0), pl.program_id(1)
    tt = xl_ref.shape[1]
    blocks_per_tile = tt // LANES
    lane = lax.broadcasted_iota(jnp.int32, (tt, win), 1).astype(F32)
    r_i = lax.broadcasted_iota(jnp.int32, (2 * n_experts, n_experts * win), 0)
    c_e = lax.broadcasted_iota(jnp.int32, (2 * n_experts, n_experts * win), 1) // win
    spread = jnp.where(r_i == c_e, 16.0, jnp.where(r_i == c_e + n_experts, 1.0, 0.0)).astype(BF16)
    slots = _dot(pos_ref[0], spread)
    last_tile = j == pl.num_programs(1) - 1
    first, n_pass = [], 0
    for e in range(n_experts):
        row = (i * n_experts + e) * n_b
        lo = start_ref[row + j * blocks_per_tile]
        hi = jnp.where(last_tile, cap, start_ref[row + jnp.minimum((j + 1) * blocks_per_tile, n_b - 1)])
        w0 = jnp.minimum((lo // 16) * 16, cap - win)
        first.append(w0)
        n_pass = jnp.maximum(n_pass, jnp.where(hi > lo, (hi - w0 + win - 1) // win, 0))
    group = min(n_experts, COMBINE_GROUP)

    def add_pass(p, acc):
        for g0 in range(0, n_experts, group):
            hits = []
            for e in range(g0, g0 + group):
                done = first[e] + p * win
                off = pl.multiple_of(jnp.minimum(done, cap - win), 16)
                yc_ref[e * win:(e + 1) * win, :] = y_ref[e, 0, pl.ds(off, win), :]
                pe = slots[:, e * win:(e + 1) * win]
                hits.append(((pe - off.astype(F32)) == lane) & (pe >= done.astype(F32)))
            onehot = jnp.concatenate(hits, axis=1).astype(F32).astype(BF16)
            acc = acc + _dot(onehot, yc_ref[g0 * win:(g0 + group) * win, :])
        return acc

    acc_ref[...] = add_pass(0, jnp.zeros(acc_ref.shape, F32))

    def extra_pass(p, carry):
        acc_ref[...] = add_pass(p, acc_ref[...])
        return carry

    lax.fori_loop(1, n_pass, extra_pass, 0)
    xl = xl_ref[0] + g2_ref[0] * acc_ref[...]
    ms = jnp.mean(xl * xl, axis=-1, keepdims=True)
    o_ref[0] = xl * lax.rsqrt(ms + EPS) * fw_ref[...]


def _combine(starts, y, pos_col, xl, g2, fw, cap):
    b, t, d = xl.shape
    n_e = y.shape[0]
    tt = _tile(t, 512)
    win = min(cap, COMBINE_WINDOW)
    assert cap % 16 == 0 and win % 16 == 0 and tt % LANES == 0 and cap + win <= UNSELECTED_SLOT
    tok = lambda i, j, st: (i, j, 0)
    return pl.pallas_call(
        functools.partial(_combine_kernel, n_experts=n_e, cap=cap, n_b=t // LANES, win=win),
        grid_spec=pltpu.PrefetchScalarGridSpec(
            num_scalar_prefetch=1,
            grid=(b, t // tt),
            in_specs=[pl.BlockSpec((n_e, 1, cap, d), lambda i, j, st: (0, i, 0, 0)),
                      pl.BlockSpec((1, tt, 2 * n_e), tok),
                      pl.BlockSpec((1, tt, d), tok),
                      pl.BlockSpec((1, 1, d), lambda i, j, st: (i, 0, 0)),
                      pl.BlockSpec((1, d), lambda i, j, st: (0, 0))],
            out_specs=pl.BlockSpec((1, tt, d), tok),
            scratch_shapes=[pltpu.VMEM((n_e * win, d), BF16), pltpu.VMEM((tt, d), F32)]),
        out_shape=jax.ShapeDtypeStruct((b, t, d), F32),
        compiler_params=_params("arbitrary", "arbitrary"),
    )(starts, y, pos_col, xl, g2, fw)


def _rope_tables(t):
    n_freq = HEAD_DIM // 4
    rows = jnp.repeat(jnp.arange(t // GRID_W, dtype=F32), GRID_W)
    cols = jnp.tile(jnp.arange(GRID_W, dtype=F32), t // GRID_W)
    inv_freq = ROPE_BASE ** (-jnp.arange(n_freq, dtype=F32) / n_freq)
    ang_r, ang_c = rows[:, None] * inv_freq, cols[:, None] * inv_freq
    cos = jnp.concatenate([jnp.cos(ang_r)] * 2 + [jnp.cos(ang_c)] * 2, axis=-1)
    sin = jnp.concatenate([-jnp.sin(ang_r), jnp.sin(ang_r), -jnp.sin(ang_c), jnp.sin(ang_c)], axis=-1)
    return jnp.tile(cos, (1, N_HEADS)), jnp.tile(sin, (1, N_HEADS))


def kernel(x, c, ctx, c_ctx, w_ada, b_ada, norm1_w, norm2_w, w_in, mlstm_gate_bias, ret_decay_logit,
           mlstm_norm_w, ret_norm_w, w_out, w_router, w_gate, w_up, w_down, final_norm_w):
    B, T, D = x.shape
    Tc = ctx.shape[1]
    H = N_HEADS
    width = H * HEAD_DIM
    assert D == 2 * width and w_ada.shape[0] == 1
    assert T % CHUNK == 0 and Tc % CHUNK == 0 and T % GRID_W == 0
    n_lat, n_ctx = T // CHUNK, Tc // CHUNK
    n_experts = w_router.shape[2]
    cap = EC_CAPACITY_FACTOR * T // n_experts
    n_gate = 4 * H

    pad = (-(B + 1)) % 8
    cc = jnp.concatenate([c, c_ctx[None], jnp.zeros((pad, D), F32)], axis=0)
    mod = _ada(cc, w_ada[0], b_ada[0])
    sh1, sc1, g1, sh2, sc2, g2 = [mod[:B, None, i * D:(i + 1) * D] for i in range(6)]
    csh1, csc1 = [jnp.broadcast_to(mod[B, i * D:(i + 1) * D], (B, 1, D)) for i in range(2)]

    assert (4 * width + n_gate) % 8 == 0 and w_in.shape[2] == 8 * width + n_gate
    w_t = jnp.swapaxes(w_in, 1, 2)
    gb = jnp.pad(mlstm_gate_bias[0], (0, LANES - n_gate)).reshape(1, LANES)
    nw1 = norm1_w[0].reshape(1, D)
    cos, sin = _rope_tables(T)
    pm, gl, pr = _inproj(x, nw1, sh1, sc1, w_t, gb, cos, sin, groups=(0, 1, 2, 3))
    pmc, gc, prc = _inproj(ctx, nw1, csh1, csc1, w_t, gb,
                           jnp.ones((Tc, width), F32), jnp.zeros((Tc, width), F32), groups=(1, 2))

    ym = _mlstm(pm, pmc, _gateprep(gl, gc), mlstm_norm_w[0].reshape(H, 1, HEAD_DIM))
    dl = jnp.broadcast_to(jnp.swapaxes(ret_decay_logit[0], 0, 1)[:, :, None], (H, 2, LANES))
    yr = _ret(pr, prc, dl, ret_norm_w[0].reshape(H, 1, HEAD_DIM))

    wr_hi = w_router[0].astype(BF16)
    wr_lo = (w_router[0] - wr_hi.astype(F32)).astype(BF16)
    wrt = jnp.pad(jnp.concatenate([wr_hi, wr_lo], axis=1), ((0, 0), (0, LANES - 2 * n_experts)))
    xl, h2, aff_t = _outproj(ym, yr, x, w_out[0].astype(BF16), g1, norm2_w[0].reshape(1, D), sh2, sc2,
                             wrt, n_experts)

    pos, idx, gate, starts = _route(aff_t, cap)
    by_expert = lambda a: jnp.swapaxes(a.reshape(B, n_experts, cap), 0, 1).reshape(n_experts, B * cap)
    y = _ffn(by_expert(idx), by_expert(gate)[:, None, :], h2.reshape(B * T, D // LANES, LANES),
             w_gate[0], w_up[0], w_down[0])
    pos_t = jnp.swapaxes(pos.reshape(B, n_experts, T), 1, 2)
    pos_t = jnp.where(pos_t < 0, UNSELECTED_SLOT, pos_t)
    pos_hl = jnp.concatenate([pos_t >> 4, pos_t & 15], axis=-1).astype(BF16)
    return _combine(starts[:, :, 0].reshape(-1), y.reshape(n_experts, B, cap, D), pos_hl, xl, g2,
                    final_norm_w.reshape(1, D), cap)
```

```python
import functools

import jax
import jax.numpy as jnp
from jax import lax
from jax.experimental import pallas as pl
from jax.experimental.pallas import tpu as pltpu

F32 = jnp.float32
BF16 = jnp.bfloat16

CHUNK = 128
GRID_W = 64
N_HEADS = 4
HEAD_DIM = 128
N_GATE = 4 * N_HEADS
ROPE_BASE = 10000.0
EPS = 1e-6
LOG2E = 1.4426950408889634
EC_CAPACITY_FACTOR = 2
LANES = 128
VMEM_LIMIT_BYTES = 56 * 1024 * 1024
COMBINE_WINDOW = 128
UNSELECTED_SLOT = 1023
COMBINE_GROUP = 4
FF_TILES_PER_STEP = 3
HEADS_PER_STEP = 2
MIXER_UNROLL = 32
GATE_UNROLL = 8


def _params(*sem):
    return pltpu.CompilerParams(dimension_semantics=sem, vmem_limit_bytes=VMEM_LIMIT_BYTES)


def _dot(a, b):
    return jnp.dot(a, b, preferred_element_type=F32)


def _dot_nt(a, b):
    return lax.dot_general(a, b, (((1,), (1,)), ((), ())), preferred_element_type=F32)


def _dot_tn(a, b):
    return lax.dot_general(a, b, (((0,), (0,)), ((), ())), preferred_element_type=F32)


def _sigmoid(x):
    return 1.0 / (1.0 + jnp.exp(-x))


def _log_sigmoid(x):
    return jnp.minimum(x, 0.0) - jnp.log1p(jnp.exp(-jnp.abs(x)))


def _col_to_row(col):
    return jnp.broadcast_to(col, (col.shape[0], LANES)).T[0:1, :]


def _row_to_cols(row):
    pieces = [jnp.broadcast_to(row[:, c:c + LANES], (LANES, LANES)).T for c in range(0, row.shape[1], LANES)]
    return pieces[0] if len(pieces) == 1 else jnp.concatenate(pieces, axis=0)


def _tile(n, pref):
    t = min(n, pref)
    assert n % t == 0, (n, t)
    return t


def _ada_kernel(c_ref, w_ref, b_ref, o_ref):
    cv = c_ref[...]
    s = (cv * _sigmoid(cv)).astype(BF16)
    o_ref[...] = _dot(s, w_ref[...].astype(BF16)) + b_ref[...]


def _ada(cc, w, b):
    rows, d = cc.shape
    n = w.shape[1]
    tn = _tile(n, 1024)
    return pl.pallas_call(
        _ada_kernel,
        grid=(n // tn,),
        in_specs=[pl.BlockSpec((rows, d), lambda j: (0, 0)),
                  pl.BlockSpec((d, tn), lambda j: (0, j)),
                  pl.BlockSpec((1, tn), lambda j: (0, j))],
        out_specs=pl.BlockSpec((rows, tn), lambda j: (0, j)),
        out_shape=jax.ShapeDtypeStruct((rows, n), F32),
        compiler_params=_params("arbitrary"),
    )(cc, w, b.reshape(1, n))


def _rms_mod(x, nw, sh, sc):
    ms = jnp.mean(x * x, axis=-1, keepdims=True)
    return (x * lax.rsqrt(ms + EPS) * nw) * (1.0 + sc) + sh


def _inproj_kernel(x_ref, nw_ref, sh_ref, sc_ref, wt_ref, gb_ref, cos_ref, sin_ref,
                   pm_ref, g_ref, pr_ref, w_ref, *, width, k_scale, groups):
    n_g = len(groups)
    ret0 = 4 * width + g_ref.shape[1]
    gate_col = n_g * width
    ret_col = gate_col + LANES

    @pl.when((pl.program_id(0) == 0) & (pl.program_id(1) == 0))
    def _():
        for slot, j in enumerate(groups):
            w_ref[:, slot * width:(slot + 1) * width] = wt_ref[j * width:(j + 1) * width, :].T.astype(BF16)
            w_ref[:, ret_col + slot * width:ret_col + (slot + 1) * width] = (
                wt_ref[ret0 + j * width:ret0 + (j + 1) * width, :].T.astype(BF16))
        w_ref[:, gate_col:ret_col] = wt_ref[4 * width:4 * width + LANES, :].T.astype(BF16)

    hb = _rms_mod(x_ref[0], nw_ref[...], sh_ref[0], sc_ref[0]).astype(BF16)
    for slot, j in enumerate(groups):
        acc = _dot(hb, w_ref[:, slot * width:(slot + 1) * width])
        if j == 1:
            acc = acc * k_scale
        for h in range(N_HEADS):
            pm_ref[0, slot * N_HEADS + h] = acc[:, h * HEAD_DIM:(h + 1) * HEAD_DIM].astype(BF16)
    g_ref[0] = (_dot(hb, w_ref[:, gate_col:ret_col]) + gb_ref[...]).T[:g_ref.shape[1]]
    cos = cos_ref[...]
    sin = sin_ref[...]
    lane = lax.broadcasted_iota(jnp.int32, cos.shape, 1)
    even = ((lane // 32) % 2) == 0
    for slot, j in enumerate(groups):
        acc = _dot(hb, w_ref[:, ret_col + slot * width:ret_col + (slot + 1) * width])
        if j == 1:
            acc = acc * k_scale
        if j < 2:
            swapped = jnp.where(even, pltpu.roll(acc, width - 32, 1), pltpu.roll(acc, 32, 1))
            acc = acc * cos + swapped * sin
        for h in range(N_HEADS):
            pr_ref[0, slot * N_HEADS + h] = acc[:, h * HEAD_DIM:(h + 1) * HEAD_DIM].astype(BF16)


def _inproj(x, nw, sh, sc, w_t, gb, cos, sin, groups):
    b, t, d = x.shape
    width = (w_t.shape[1] - N_GATE) // 8
    tm = _tile(t, 512)
    slabs = len(groups) * N_HEADS
    kern = functools.partial(_inproj_kernel, width=width, k_scale=HEAD_DIM ** -0.5, groups=groups)
    const = lambda i, j: (0, 0)
    return pl.pallas_call(
        kern,
        grid=(b, t // tm),
        in_specs=[pl.BlockSpec((1, tm, d), lambda i, j: (i, j, 0)),
                  pl.BlockSpec((1, d), const),
                  pl.BlockSpec((1, 1, d), lambda i, j: (i, 0, 0)),
                  pl.BlockSpec((1, 1, d), lambda i, j: (i, 0, 0)),
                  pl.BlockSpec((None,) + w_t.shape[1:], lambda i, j: (0, 0, 0), pipeline_mode=pl.Buffered(1)),
                  pl.BlockSpec((1, LANES), const),
                  pl.BlockSpec((tm, width), lambda i, j: (j, 0)),
                  pl.BlockSpec((tm, width), lambda i, j: (j, 0))],
        out_specs=[pl.BlockSpec((1, slabs, tm, HEAD_DIM), lambda i, j: (i, 0, j, 0)),
                   pl.BlockSpec((1, N_GATE, tm), lambda i, j: (i, 0, j)),
                   pl.BlockSpec((1, slabs, tm, HEAD_DIM), lambda i, j: (i, 0, j, 0))],
        out_shape=[jax.ShapeDtypeStruct((b, slabs, t, HEAD_DIM), BF16),
                   jax.ShapeDtypeStruct((b, N_GATE, t), F32),
                   jax.ShapeDtypeStruct((b, slabs, t, HEAD_DIM), BF16)],
        scratch_shapes=[pltpu.VMEM((d, 2 * len(groups) * width + LANES), BF16)],
        compiler_params=_params("arbitrary", "arbitrary"),
    )(x, nw, sh, sc, w_t, gb, cos, sin)


def _scan_lanes(x, op, fill, reverse):
    lane = lax.broadcasted_iota(jnp.int32, x.shape, 1)
    sh = 1
    while sh < LANES:
        if reverse:
            x = op(x, jnp.where(lane < LANES - sh, pltpu.roll(x, LANES - sh, 1), fill))
        else:
            x = op(x, jnp.where(lane >= sh, pltpu.roll(x, sh, 1), fill))
        sh *= 2
    return x


def _gateprep_kernel(gl_ref, gc_ref, o_ref, *, n_lat, n_ctx):
    n_b = gl_ref.shape[0]
    half = N_GATE // 2
    rows = n_b * half
    row = lax.broadcasted_iota(jnp.int32, (rows, LANES), 0)
    lane = lax.broadcasted_iota(jnp.int32, (rows, LANES), 1)
    is_fwd = (row % half) < N_HEADS
    last = lane == jnp.where(is_fwd, LANES - 1, 0)

    def chunk_rows(ref, lo, c_fwd, c_bwd):
        def at(c):
            cols = (slice(c * LANES, (c + 1) * LANES) if isinstance(c, int)
                    else pl.ds(pl.multiple_of(c * LANES, LANES), LANES))
            return jnp.concatenate([ref[s, lo:lo + half, cols] for s in range(n_b)], axis=0)
        return jnp.where(is_fwd, at(c_fwd), at(c_bwd))

    def step(c, ig, fg, m):
        lf = _log_sigmoid(fg)
        b = jnp.where(is_fwd, _scan_lanes(lf, jnp.add, 0.0, False), _scan_lanes(lf, jnp.add, 0.0, True))
        b_tot = jnp.sum(jnp.where(last, b, 0.0), axis=1, keepdims=True)
        a = ig - b
        m_new = jnp.maximum(b_tot + m, jnp.max(b_tot + a, axis=1, keepdims=True))
        w = jnp.exp(b_tot + a - m_new)
        decay = jnp.exp(b_tot + m - m_new)
        run = jnp.where(is_fwd, _scan_lanes(a, jnp.maximum, -jnp.inf, False),
                        _scan_lanes(a, jnp.maximum, -jnp.inf, True))
        mm = jnp.maximum(m, run)
        o_ref[c, 0] = a * LOG2E
        o_ref[c, 1] = w
        o_ref[c, 2] = mm * LOG2E
        o_ref[c, 3] = jnp.exp(-(b + mm))
        o_ref[c, 4] = jnp.broadcast_to(decay, (rows, LANES))
        o_ref[c, 5] = jnp.broadcast_to(m * LOG2E, (rows, LANES))
        return m_new

    m = jnp.zeros((rows, 1), F32)
    for c in range(n_ctx):
        m = step(c, chunk_rows(gc_ref, 0, c, n_ctx - 1 - c), chunk_rows(gc_ref, half, c, n_ctx - 1 - c), m)

    def latent(i, m):
        return step(n_ctx + i, chunk_rows(gl_ref, 0, i, n_lat - 1 - i),
                    chunk_rows(gl_ref, half, i, n_lat - 1 - i), m)

    lax.fori_loop(0, n_lat, latent, m, unroll=min(n_lat, GATE_UNROLL))


def _gateprep(gl, gc):
    n_lat, n_ctx = gl.shape[2] // CHUNK, gc.shape[2] // CHUNK
    return pl.pallas_call(
        functools.partial(_gateprep_kernel, n_lat=n_lat, n_ctx=n_ctx),
        out_shape=jax.ShapeDtypeStruct((n_ctx + n_lat, 6, gl.shape[0] * N_GATE // 2, LANES), F32),
        compiler_params=pltpu.CompilerParams(vmem_limit_bytes=VMEM_LIMIT_BYTES),
    )(gl, gc)


def _head_norm_gate(h, gate, nw):
    return gate * (h * lax.rsqrt(jnp.mean(h * h, axis=-1, keepdims=True) + EPS)) * nw


def _per_head(head_fn, n_shared):
    def kern(*refs, heads_per_step, n_blocked_tail, **kw):
        blocked, shared = refs[:6], refs[6:6 + n_shared]
        tail = refs[6 + n_shared:6 + n_shared + n_blocked_tail]
        scratch = refs[6 + n_shared + n_blocked_tail:]
        for hh in range(heads_per_step):
            one = lambda r: r.at[pl.ds(hh, 1)]
            head_fn(pl.program_id(1) * heads_per_step + hh, *[one(r) for r in blocked], *shared,
                    *[one(r) for r in tail], *[r.at[hh] for r in scratch], **kw)
    return kern


def _mlstm_head(head, q_ref, k_ref, v_ref, o_ref, kc_ref, vc_ref, st_ref, nw_ref, y_ref,
                  sf_ref, sb_ref, cf_ref, cb_ref, *, n_lat, n_ctx, unroll):
    L = CHUNK
    A, WT, MM, EN, DECAY, M0 = range(6)
    ones = jnp.ones((L, HEAD_DIM), BF16)
    stream_f = pl.program_id(0) * (N_GATE // 2) + head
    stream_b = stream_f + N_HEADS

    def stat(step, plane, stream):
        return st_ref[step, plane, pl.ds(stream, 1), :]

    def to_rows(row):
        return jnp.broadcast_to(row, (L, L)).T

    def update(state_ref, k, v, step, stream):
        wkt = (k.astype(F32).T * stat(step, WT, stream)).astype(BF16)
        u = _dot(wkt, jnp.concatenate([v, ones], axis=1))
        decay = stat(step, DECAY, stream)
        state_ref[...] = jnp.concatenate([decay, decay], axis=1) * state_ref[...] + u

    cf_ref[...] = jnp.zeros_like(cf_ref)
    cb_ref[...] = jnp.zeros_like(cb_ref)
    for c in range(n_ctx):
        rf = slice(c * L, (c + 1) * L)
        rb = slice((n_ctx - 1 - c) * L, (n_ctx - c) * L)
        update(cf_ref, kc_ref[0, rf, :], vc_ref[0, rf, :], c, stream_f)
        update(cb_ref, kc_ref[0, rb, :], vc_ref[0, rb, :], c, stream_b)

    def state_body(i, carry):
        jb = n_lat - 1 - i
        rf = pl.ds(pl.multiple_of(i * L, L), L)
        rb = pl.ds(pl.multiple_of(jb * L, L), L)
        sf_ref[i] = cf_ref[...].astype(BF16)
        sb_ref[jb] = cb_ref[...].astype(BF16)
        update(cf_ref, k_ref[0, rf, :], v_ref[0, rf, :], n_ctx + i, stream_f)
        update(cb_ref, k_ref[0, rb, :], v_ref[0, rb, :], n_ctx + i, stream_b)
        return carry

    lax.fori_loop(0, n_lat, state_body, 0, unroll=unroll)

    r_i = lax.broadcasted_iota(jnp.int32, (L, L), 0)
    c_i = lax.broadcasted_iota(jnp.int32, (L, L), 1)
    tril = c_i <= r_i
    triu = c_i >= r_i
    nw = nw_ref[0]

    def out_body(j, carry):
        rows = pl.ds(pl.multiple_of(j * L, L), L)
        step_f = n_ctx + j
        step_b = n_ctx + n_lat - 1 - j
        q = q_ref[0, rows, :]
        k = k_ref[0, rows, :]
        v = v_ref[0, rows, :]
        mm_f = to_rows(stat(step_f, MM, stream_f))
        mm_b = to_rows(stat(step_b, MM, stream_b))
        s = _dot_nt(q, k)
        d_f = jnp.where(tril, jnp.exp2(stat(step_f, A, stream_f) - mm_f), 0.0)
        d_b = jnp.where(triu, jnp.exp2(stat(step_b, A, stream_b) - mm_b), 0.0)
        iw_f = jnp.exp2(stat(step_f, M0, stream_f) - mm_f)
        iw_b = jnp.exp2(stat(step_b, M0, stream_b) - mm_b)
        qf = q.astype(F32)
        vext = jnp.concatenate([v, ones], axis=1)
        lhs_f = jnp.concatenate([(s * d_f).astype(BF16), (qf * iw_f).astype(BF16)], axis=1)
        lhs_b = jnp.concatenate([(s * d_b).astype(BF16), (qf * iw_b).astype(BF16)], axis=1)
        tot_f = _dot(lhs_f, jnp.concatenate([vext, sf_ref[j]], axis=0))
        tot_b = _dot(lhs_b, jnp.concatenate([vext, sb_ref[j]], axis=0))
        D = HEAD_DIM
        h = (tot_f[:, :D] / jnp.maximum(jnp.abs(tot_f[:, D:]), to_rows(stat(step_f, EN, stream_f)))
             + tot_b[:, :D] / jnp.maximum(jnp.abs(tot_b[:, D:]), to_rows(stat(step_b, EN, stream_b))))
        gate = _sigmoid(o_ref[0, rows, :].astype(F32))
        y_ref[0, rows, :] = _head_norm_gate(h, gate, nw).astype(BF16)
        return carry

    lax.fori_loop(0, n_lat, out_body, 0, unroll=unroll)


def _head_slab(t, group):
    per_group = N_HEADS // HEADS_PER_STEP
    return pl.BlockSpec((None, HEADS_PER_STEP, t, HEAD_DIM), lambda i, h: (i, group * per_group + h, 0, 0))


def _head_param(*shape):
    return pl.BlockSpec((HEADS_PER_STEP,) + shape, lambda i, h: (h,) + (0,) * len(shape))


def _mlstm(pm, pmc, stats, nw):
    b, _, t, _ = pm.shape
    tc = pmc.shape[2]
    n_lat, n_ctx = t // CHUNK, tc // CHUNK
    H, hps = N_HEADS, HEADS_PER_STEP
    blk = lambda off: _head_slab(t, off // H)
    cblk = lambda off: _head_slab(tc, off // H - 1)
    return pl.pallas_call(
        functools.partial(_per_head(_mlstm_head, n_shared=1), heads_per_step=hps, n_blocked_tail=2,
                          n_lat=n_lat, n_ctx=n_ctx, unroll=min(n_lat, MIXER_UNROLL)),
        grid=(b, H // hps),
        in_specs=[blk(0), blk(H), blk(2 * H), blk(3 * H), cblk(H), cblk(2 * H),
                  pl.BlockSpec(stats.shape, lambda i, h: (0, 0, 0, 0)),
                  _head_param(1, HEAD_DIM)],
        out_specs=_head_slab(t, 0),
        out_shape=jax.ShapeDtypeStruct((b, H, t, HEAD_DIM), BF16),
        scratch_shapes=[pltpu.VMEM((hps, n_lat, HEAD_DIM, HEAD_DIM + LANES), BF16),
                        pltpu.VMEM((hps, n_lat, HEAD_DIM, HEAD_DIM + LANES), BF16),
                        pltpu.VMEM((hps, HEAD_DIM, HEAD_DIM + LANES), F32),
                        pltpu.VMEM((hps, HEAD_DIM, HEAD_DIM + LANES), F32)],
        compiler_params=_params("arbitrary", "arbitrary"),
    )(pm, pm, pm, pm, pmc, pmc, stats, nw)


def _ret_head(head, q_ref, k_ref, v_ref, o_ref, kc_ref, vc_ref, dl_ref, nw_ref, y_ref,
                sf_ref, sb_ref, rf_ref, rb_ref, *, n_lat, n_ctx, unroll):
    L = CHUNK
    lg = _log_sigmoid(dl_ref[0])
    lg_f, lg_b = lg[0:1, :], lg[1:2, :]
    r_i = lax.broadcasted_iota(jnp.int32, (L, L), 0)
    c_i = lax.broadcasted_iota(jnp.int32, (L, L), 1)
    r_f = r_i.astype(F32)
    rel = (r_i - c_i).astype(F32)
    kw_f = jnp.exp((L - 1.0 - r_f) * lg_f)
    kw_b = jnp.exp(r_f * lg_b)
    in_f = jnp.exp((r_f + 1.0) * lg_f)
    in_b = jnp.exp((L - r_f) * lg_b)
    dch_f = jnp.exp(L * lg_f)
    dch_b = jnp.exp(L * lg_b)
    d_mat = (jnp.where(rel >= 0, jnp.exp(jnp.maximum(rel, 0.0) * lg_f), 0.0)
             + jnp.where(rel <= 0, jnp.exp(jnp.maximum(-rel, 0.0) * lg_b), 0.0))

    def update(state_ref, k, v, kw, dch):
        wk = (k.astype(F32) * kw).astype(BF16)
        state_ref[...] = dch * state_ref[...] + _dot_tn(wk, v)

    rf_ref[...] = jnp.zeros_like(rf_ref)
    rb_ref[...] = jnp.zeros_like(rb_ref)
    for c in range(n_ctx):
        rows = slice(c * L, (c + 1) * L)
        update(rf_ref, kc_ref[0, rows, :], vc_ref[0, rows, :], kw_f, dch_f)
    for c in reversed(range(n_ctx)):
        rows = slice(c * L, (c + 1) * L)
        update(rb_ref, kc_ref[0, rows, :], vc_ref[0, rows, :], kw_b, dch_b)

    def state_body(i, carry):
        jb = n_lat - 1 - i
        rowf = pl.ds(pl.multiple_of(i * L, L), L)
        rowb = pl.ds(pl.multiple_of(jb * L, L), L)
        sf_ref[i] = rf_ref[...].astype(BF16)
        sb_ref[jb] = rb_ref[...].astype(BF16)
        update(rf_ref, k_ref[0, rowf, :], v_ref[0, rowf, :], kw_f, dch_f)
        update(rb_ref, k_ref[0, rowb, :], v_ref[0, rowb, :], kw_b, dch_b)
        return carry

    lax.fori_loop(0, n_lat, state_body, 0, unroll=unroll)
    nw = nw_ref[0]

    def out_body(j, carry):
        rows = pl.ds(pl.multiple_of(j * L, L), L)
        q = q_ref[0, rows, :]
        s = _dot_nt(q, k_ref[0, rows, :])
        intra = _dot((s * d_mat).astype(BF16), v_ref[0, rows, :])
        inter = _dot(q, jnp.concatenate([sf_ref[j], sb_ref[j]], axis=1))
        out = intra + in_f * inter[:, :HEAD_DIM] + in_b * inter[:, HEAD_DIM:]
        og = o_ref[0, rows, :].astype(F32)
        y_ref[0, rows, :] = _head_norm_gate(out, og * _sigmoid(og), nw).astype(BF16)
        return carry

    lax.fori_loop(0, n_lat, out_body, 0, unroll=unroll)


def _ret(pr, prc, dl, nw):
    b, _, t, _ = pr.shape
    tc = prc.shape[2]
    n_lat, n_ctx = t // CHUNK, tc // CHUNK
    H, hps = N_HEADS, HEADS_PER_STEP
    blk = lambda off: _head_slab(t, off // H)
    cblk = lambda off: _head_slab(tc, off // H - 1)
    return pl.pallas_call(
        functools.partial(_per_head(_ret_head, n_shared=0), heads_per_step=hps, n_blocked_tail=3,
                          n_lat=n_lat, n_ctx=n_ctx, unroll=min(n_lat, MIXER_UNROLL)),
        grid=(b, H // hps),
        in_specs=[blk(0), blk(H), blk(2 * H), blk(3 * H), cblk(H), cblk(2 * H),
                  _head_param(2, LANES), _head_param(1, HEAD_DIM)],
        out_specs=_head_slab(t, 0),
        out_shape=jax.ShapeDtypeStruct((b, H, t, HEAD_DIM), BF16),
        scratch_shapes=[pltpu.VMEM((hps, n_lat, HEAD_DIM, HEAD_DIM), BF16),
                        pltpu.VMEM((hps, n_lat, HEAD_DIM, HEAD_DIM), BF16),
                        pltpu.VMEM((hps, HEAD_DIM, HEAD_DIM), F32),
                        pltpu.VMEM((hps, HEAD_DIM, HEAD_DIM), F32)],
        compiler_params=_params("arbitrary", "arbitrary"),
    )(pr, pr, pr, pr, prc, prc, dl, nw)


def _outproj_kernel(ym_ref, yr_ref, x_ref, wo_ref, g1_ref, nw_ref, sh_ref, sc_ref, wrt_ref,
                    xl_ref, h2_ref, aff_ref, *, n_experts):
    heads = [ym_ref[0, h] for h in range(N_HEADS)] + [yr_ref[0, h] for h in range(N_HEADS)]
    y = _dot(jnp.concatenate(heads, axis=1), wo_ref[...])
    xl = x_ref[0] + g1_ref[0] * y
    xl_ref[0] = xl
    h2 = _rms_mod(xl, nw_ref[...], sh_ref[0], sc_ref[0])
    h2_ref[0] = h2.reshape(h2.shape[0], h2.shape[1] // LANES, LANES)
    h_hi = h2.astype(BF16)
    h_lo = (h2 - h_hi.astype(F32)).astype(BF16)
    p_hi = _dot(h_hi, wrt_ref[...])
    p_lo = _dot(h_lo, wrt_ref[...])
    logits = p_hi + pltpu.roll(p_hi, LANES - n_experts, 1) + p_lo
    lane = lax.broadcasted_iota(jnp.int32, logits.shape, 1)
    logits = jnp.where(lane < n_experts, logits, -jnp.inf)
    e = jnp.exp(logits - jnp.max(logits, axis=-1, keepdims=True))
    aff_t = (e / jnp.sum(e, axis=-1, keepdims=True)).T[:n_experts]
    aff_ref[0] = aff_t.reshape(n_experts, aff_t.shape[1] // LANES, LANES)


def _outproj(ym, yr, x, wo, g1, nw, sh, sc, wrt, n_experts):
    b, t, d = x.shape
    tm = _tile(t, 8 * LANES)
    slab = pl.BlockSpec((1, N_HEADS, tm, HEAD_DIM), lambda i, j: (i, 0, j, 0))
    const = lambda i, j: (0, 0)
    per_b = lambda i, j: (i, 0, 0)
    tok = lambda i, j: (i, j, 0)
    return pl.pallas_call(
        functools.partial(_outproj_kernel, n_experts=n_experts),
        grid=(b, t // tm),
        in_specs=[slab, slab,
                  pl.BlockSpec((1, tm, d), tok), pl.BlockSpec(wo.shape, const),
                  pl.BlockSpec((1, 1, d), per_b), pl.BlockSpec((1, d), const),
                  pl.BlockSpec((1, 1, d), per_b), pl.BlockSpec((1, 1, d), per_b),
                  pl.BlockSpec(wrt.shape, const)],
        out_specs=[pl.BlockSpec((1, tm, d), tok),
                   pl.BlockSpec((1, tm, d // LANES, LANES), lambda i, j: (i, j, 0, 0)),
                   pl.BlockSpec((1, n_experts, tm // LANES, LANES), lambda i, j: (i, 0, j, 0))],
        out_shape=[jax.ShapeDtypeStruct((b, t, d), F32),
                   jax.ShapeDtypeStruct((b, t, d // LANES, LANES), F32),
                   jax.ShapeDtypeStruct((b, n_experts, t // LANES, LANES), F32)],
        compiler_params=_params("arbitrary", "arbitrary"),
    )(ym, yr, x, wo, g1, nw, sh, sc, wrt)


def _select_top(a, cap):
    n_e, n_b, _ = a.shape

    ones = jnp.ones((LANES, LANES), BF16)

    def count(mask):
        per_block = _dot(mask.astype(F32).astype(BF16).reshape(n_e * n_b, LANES), ones)
        return jnp.sum(per_block.reshape(n_e, n_b, LANES), axis=1, keepdims=True).astype(jnp.int32)

    def search(i, thr):
        cand = thr | jnp.left_shift(jnp.int32(1), 30 - i)
        return jnp.where(count(a >= pltpu.bitcast(cand, F32)) >= cap, cand, thr)

    thr = lax.fori_loop(0, 31, search, jnp.zeros((n_e, 1, LANES), jnp.int32))
    above = a >= pltpu.bitcast(thr + 1, F32)
    band = (a >= pltpu.bitcast(thr, F32)) & jnp.logical_not(above)
    need = cap - count(above)[:, :, 0:1]
    tok = (lax.broadcasted_iota(jnp.int32, a.shape, 1) * LANES
           + lax.broadcasted_iota(jnp.int32, a.shape, 2))

    def take_one(_, carry):
        sel, band, need = carry
        in_band = band > 0
        best = jnp.max(jnp.max(jnp.where(in_band, a, -1.0), axis=2, keepdims=True), axis=1, keepdims=True)
        cand = jnp.where(in_band & (a == best), tok, n_b * LANES)
        first = jnp.min(jnp.min(cand, axis=2, keepdims=True), axis=1, keepdims=True)
        take = ((tok == first) & (need > 0)).astype(jnp.int32)
        return sel | take, band - take, need - 1

    sel, _, _ = lax.fori_loop(0, jnp.max(need), take_one,
                              (above.astype(jnp.int32), band.astype(jnp.int32), need))
    return sel


def _route_kernel(aff_ref, pos_ref, idx_ref, gate_ref, start_ref, sel_ref, *, cap):
    i = pl.program_id(0)
    n_s, n_e, n_b, _ = aff_ref.shape

    @pl.when(i == 0)
    def _():
        sel_all = _select_top(aff_ref[...].reshape(n_s * n_e, n_b, LANES), cap)
        sel_ref[...] = sel_all.reshape(n_s, n_e, n_b, LANES)

    a = aff_ref[i]
    sel = sel_ref[i] > 0

    rows = n_e * n_b
    u_i = lax.broadcasted_iota(jnp.int32, (LANES, LANES), 0)
    s_i = lax.broadcasted_iota(jnp.int32, (LANES, LANES), 1)
    incl = (u_i <= s_i).astype(BF16)
    ones = jnp.ones((LANES, LANES), BF16)
    r_i = lax.broadcasted_iota(jnp.int32, (rows, rows), 0)
    c_i = lax.broadcasted_iota(jnp.int32, (rows, rows), 1)
    before = ((r_i // n_b == c_i // n_b) & (c_i < r_i)).astype(BF16)

    x = sel.astype(F32).reshape(rows, LANES)
    xb = x.astype(BF16)
    block_tot = _dot(xb, ones).astype(BF16)
    start = _dot(before, block_tot)
    pos = jnp.where(x > 0, start + _dot(xb, incl) - x, -1.0).astype(jnp.int32)
    pos_ref[0] = pos.reshape(n_e, n_b, LANES)
    start_ref[0] = start.astype(jnp.int32)

    pad = (-rows) % LANES
    def pad_rows(m):
        return m if pad == 0 else jnp.concatenate([m, jnp.zeros((pad, LANES), m.dtype)], axis=0)
    start_p = pad_rows(start)
    end_p = pad_rows(start + block_tot.astype(F32))
    pos_p = pad_rows(pos)
    pos_p = jnp.where(pos_p < 0, 1023, pos_p)
    a_p = pad_rows(a.reshape(rows, LANES))
    a_hi = a_p.astype(BF16)
    a_mid = (a_p - a_hi.astype(F32)).astype(BF16)
    a_lo = (a_p - a_hi.astype(F32) - a_mid.astype(F32)).astype(BF16)
    per_group = LANES // n_b
    j_col = lax.broadcasted_iota(jnp.int32, (cap, LANES), 0).astype(F32)
    lane = lax.broadcasted_iota(jnp.int32, (cap, LANES), 1)
    lane_f = lane.astype(F32)
    block_f = (lane % n_b).astype(F32)
    row0 = pl.program_id(0) * (n_b * LANES)
    for g in range((rows + pad) // LANES):
        rs = slice(g * LANES, (g + 1) * LANES)
        s_row = start_p[rs].T[0:1, :]
        e_row = end_p[rs].T[0:1, :]
        hi = (pos_p[rs] >> 4).astype(F32).astype(BF16)
        lo = (pos_p[rs] & 15).astype(F32).astype(BF16)
        for el in range(per_group):
            e = g * per_group + el
            if e >= n_e:
                break
            mine = (j_col >= s_row) & (j_col < e_row) & ((lane // n_b) == el)
            mine_b = mine.astype(F32).astype(BF16)
            slots = 16.0 * _dot(mine_b, hi) + _dot(mine_b, lo)
            hit = slots == j_col
            in_block = jnp.sum(jnp.where(hit, lane_f, 0.0), axis=1, keepdims=True)
            block = jnp.sum(jnp.where(mine, block_f, 0.0), axis=1, keepdims=True)
            a_blk = _dot(mine_b, a_hi[rs]) + _dot(mine_b, a_mid[rs]) + _dot(mine_b, a_lo[rs])
            gate = jnp.sum(jnp.where(hit, a_blk, 0.0), axis=1, keepdims=True)
            idx_ref[0, e] = _col_to_row(block * LANES + in_block).astype(jnp.int32) + row0
            gate_ref[0, e] = _col_to_row(gate)


def _route(aff_t, cap):
    b, n_e, n_b, _ = aff_t.shape
    spec = pl.BlockSpec((1, n_e, n_b, LANES), lambda i: (i, 0, 0, 0))
    slot_spec = pl.BlockSpec((1, n_e, 1, cap), lambda i: (i, 0, 0, 0))
    return pl.pallas_call(
        functools.partial(_route_kernel, cap=cap),
        grid=(b,), in_specs=[pl.BlockSpec(aff_t.shape, lambda i: (0, 0, 0, 0))],
        out_specs=[spec, slot_spec, slot_spec, pl.BlockSpec((1, n_e * n_b, LANES), lambda i: (i, 0, 0))],
        scratch_shapes=[pltpu.VMEM(aff_t.shape, jnp.int32)],
        out_shape=[jax.ShapeDtypeStruct(aff_t.shape, jnp.int32),
                   jax.ShapeDtypeStruct((b, n_e, 1, cap), jnp.int32),
                   jax.ShapeDtypeStruct((b, n_e, 1, cap), F32),
                   jax.ShapeDtypeStruct((b, n_e * n_b, LANES), jnp.int32)],
        compiler_params=_params("arbitrary"),
    )(aff_t)


def _ffn_kernel(idx_ref, h_hbm, gate_ref, *refs, m, per_step, n_tiles, tiles_per_step):
    w_refs = refs[:3 * tiles_per_step]
    y_ref, stage_ref, x_ref, acc_ref, sem = refs[3 * tiles_per_step:]
    e, f = pl.program_id(0), pl.program_id(1)
    n_e, n_f = pl.num_programs(0), pl.num_programs(1)
    rows_per_expert = stage_ref.shape[0]

    def row_copy(base, j):
        return pltpu.make_async_copy(h_hbm.at[pl.ds(idx_ref[base + j], 1)],
                                     stage_ref.at[pl.ds(j, 1)], sem.at[0])

    def wait_rows():
        pltpu.make_async_copy(h_hbm.at[pl.ds(0, rows_per_expert)], stage_ref, sem.at[0]).wait()

    @pl.when((e == 0) & (f == 0))
    def _():
        def start_row(j, carry):
            row_copy(0, j).start()
            return carry
        lax.fori_loop(0, rows_per_expert, start_row, 0)

    @pl.when(f == 0)
    def _():
        wait_rows()
        x_ref[...] = stage_ref[:m].reshape(x_ref.shape).astype(BF16)
        acc_ref[...] = jnp.zeros_like(acc_ref)

    nxt = jnp.minimum(e + 1, n_e - 1) * rows_per_expert
    for i in range(per_step):
        row_copy(nxt, f * per_step + i).start(priority=i % 2)

    def ff_tile(wg, wu, wd):
        x = x_ref[...]
        a = _dot(x, wg[0].astype(BF16))
        u = _dot(x, wu[0].astype(BF16))
        mid = (a * _sigmoid(a) * u).astype(BF16)
        acc_ref[...] += _dot(mid, wd[0].astype(BF16))

    ff_tile(*w_refs[:3])
    for k in range(1, tiles_per_step):
        @pl.when(tiles_per_step * f + k < n_tiles)
        def _(k=k):
            ff_tile(*w_refs[3 * k:3 * k + 3])

    @pl.when(f == n_f - 1)
    def _():
        gate = _row_to_cols(gate_ref[0])
        for c in range(0, y_ref.shape[2], LANES):
            y_ref[0, :, c:c + LANES] = (acc_ref[:, c:c + LANES] * gate).astype(BF16)

    @pl.when((e == n_e - 1) & (f == n_f - 1))
    def _():
        wait_rows()


def _ffn(idx, gate, h_rows, wg, wu, wd):
    n_e, m = idx.shape
    d = h_rows.shape[1] * LANES
    ff = wg.shape[2]
    tf = _tile(ff, 256)
    n_tiles = ff // tf
    tps = min(n_tiles, FF_TILES_PER_STEP)
    n_f = -(-n_tiles // tps)
    per_step = -(-m // (8 * n_f)) * 8
    idx = jnp.pad(idx, ((0, 0), (0, n_f * per_step - m))).reshape(-1)
    w_specs = []
    for k in range(tps):
        tile = lambda f, k=k: jnp.minimum(tps * f + k, n_tiles - 1)
        w_specs += [pl.BlockSpec((1, d, tf), lambda e, f, idx, tile=tile: (e, 0, tile(f))),
                    pl.BlockSpec((1, d, tf), lambda e, f, idx, tile=tile: (e, 0, tile(f))),
                    pl.BlockSpec((1, tf, d), lambda e, f, idx, tile=tile: (e, tile(f), 0))]
    return pl.pallas_call(
        functools.partial(_ffn_kernel, m=m, per_step=per_step, n_tiles=n_tiles, tiles_per_step=tps),
        grid_spec=pltpu.PrefetchScalarGridSpec(
            num_scalar_prefetch=1,
            grid=(n_e, n_f),
            in_specs=[pl.BlockSpec(memory_space=pl.ANY),
                      pl.BlockSpec((1, 1, m), lambda e, f, idx: (e, 0, 0))] + w_specs,
            out_specs=pl.BlockSpec((1, m, d), lambda e, f, idx: (e, 0, 0)),
            scratch_shapes=[pltpu.VMEM((n_f * per_step, d // LANES, LANES), F32),
                            pltpu.VMEM((m, d), BF16),
                            pltpu.VMEM((m, d), F32),
                            pltpu.SemaphoreType.DMA((1,))]),
        out_shape=jax.ShapeDtypeStruct((n_e, m, d), BF16),
        compiler_params=_params("arbitrary", "arbitrary"),
    )(idx, h_rows, gate, *([wg, wu, wd] * tps))


def _combine_kernel(start_ref, y_ref, pos_ref, xl_ref, g2_ref, fw_ref, o_ref, yc_ref, acc_ref,
                    *, n_experts, cap, n_b, win):
    i, j = pl.program_id(0), pl.program_id(1)
    tt = xl_ref.shape[1]
    blocks_per_tile = tt // LANES
    lane = lax.broadcasted_iota(jnp.int32, (tt, win), 1).astype(F32)
    r_i = lax.broadcasted_iota(jnp.int32, (2 * n_experts, n_experts * win), 0)
    c_e = lax.broadcasted_iota(jnp.int32, (2 * n_experts, n_experts * win), 1) // win
    spread = jnp.where(r_i == c_e, 16.0, jnp.where(r_i == c_e + n_experts, 1.0, 0.0)).astype(BF16)
    slots = _dot(pos_ref[0], spread)
    last_tile = j == pl.num_programs(1) - 1
    first, n_pass = [], 0
    for e in range(n_experts):
        row = (i * n_experts + e) * n_b
        lo = start_ref[row + j * blocks_per_tile]
        hi = jnp.where(last_tile, cap, start_ref[row + jnp.minimum((j + 1) * blocks_per_tile, n_b - 1)])
        w0 = jnp.minimum((lo // 16) * 16, cap - win)
        first.append(w0)
        n_pass = jnp.maximum(n_pass, jnp.where(hi > lo, (hi - w0 + win - 1) // win, 0))
    group = min(n_experts, COMBINE_GROUP)

    def add_pass(p, acc):
        for g0 in range(0, n_experts, group):
            hits = []
            for e in range(g0, g0 + group):
                done = first[e] + p * win
                off = pl.multiple_of(jnp.minimum(done, cap - win), 16)
                yc_ref[e * win:(e + 1) * win, :] = y_ref[e, 0, pl.ds(off, win), :]
                pe = slots[:, e * win:(e + 1) * win]
                hits.append(((pe - off.astype(F32)) == lane) & (pe >= done.astype(F32)))
            onehot = jnp.concatenate(hits, axis=1).astype(F32).astype(BF16)
            acc = acc + _dot(onehot, yc_ref[g0 * win:(g0 + group) * win, :])
        return acc

    acc_ref[...] = add_pass(0, jnp.zeros(acc_ref.shape, F32))

    def extra_pass(p, carry):
        acc_ref[...] = add_pass(p, acc_ref[...])
        return carry

    lax.fori_loop(1, n_pass, extra_pass, 0)
    xl = xl_ref[0] + g2_ref[0] * acc_ref[...]
    ms = jnp.mean(xl * xl, axis=-1, keepdims=True)
    o_ref[0] = xl * lax.rsqrt(ms + EPS) * fw_ref[...]


def _combine(starts, y, pos_col, xl, g2, fw, cap):
    b, t, d = xl.shape
    n_e = y.shape[0]
    tt = _tile(t, 512)
    win = min(cap, COMBINE_WINDOW)
    assert cap % 16 == 0 and win % 16 == 0 and tt % LANES == 0 and cap + win <= UNSELECTED_SLOT
    tok = lambda i, j, st: (i, j, 0)
    return pl.pallas_call(
        functools.partial(_combine_kernel, n_experts=n_e, cap=cap, n_b=t // LANES, win=win),
        grid_spec=pltpu.PrefetchScalarGridSpec(
            num_scalar_prefetch=1,
            grid=(b, t // tt),
            in_specs=[pl.BlockSpec((n_e, 1, cap, d), lambda i, j, st: (0, i, 0, 0)),
                      pl.BlockSpec((1, tt, 2 * n_e), tok),
                      pl.BlockSpec((1, tt, d), tok),
                      pl.BlockSpec((1, 1, d), lambda i, j, st: (i, 0, 0)),
                      pl.BlockSpec((1, d), lambda i, j, st: (0, 0))],
            out_specs=pl.BlockSpec((1, tt, d), tok),
            scratch_shapes=[pltpu.VMEM((n_e * win, d), BF16), pltpu.VMEM((tt, d), F32)]),
        out_shape=jax.ShapeDtypeStruct((b, t, d), F32),
        compiler_params=_params("arbitrary", "arbitrary"),
    )(starts, y, pos_col, xl, g2, fw)


def _rope_tables(t):
    n_freq = HEAD_DIM // 4
    rows = jnp.repeat(jnp.arange(t // GRID_W, dtype=F32), GRID_W)
    cols = jnp.tile(jnp.arange(GRID_W, dtype=F32), t // GRID_W)
    inv_freq = ROPE_BASE ** (-jnp.arange(n_freq, dtype=F32) / n_freq)
    ang_r, ang_c = rows[:, None] * inv_freq, cols[:, None] * inv_freq
    cos = jnp.concatenate([jnp.cos(ang_r)] * 2 + [jnp.cos(ang_c)] * 2, axis=-1)
    sin = jnp.concatenate([-jnp.sin(ang_r), jnp.sin(ang_r), -jnp.sin(ang_c), jnp.sin(ang_c)], axis=-1)
    return jnp.tile(cos, (1, N_HEADS)), jnp.tile(sin, (1, N_HEADS))


def kernel(x, c, ctx, c_ctx, w_ada, b_ada, norm1_w, norm2_w, w_in, mlstm_gate_bias, ret_decay_logit,
           mlstm_norm_w, ret_norm_w, w_out, w_router, w_gate, w_up, w_down, final_norm_w):
    B, T, D = x.shape
    Tc = ctx.shape[1]
    H = N_HEADS
    width = H * HEAD_DIM
    assert D == 2 * width and w_ada.shape[0] == 1
    assert T % CHUNK == 0 and Tc % CHUNK == 0 and T % GRID_W == 0
    n_lat, n_ctx = T // CHUNK, Tc // CHUNK
    n_experts = w_router.shape[2]
    cap = EC_CAPACITY_FACTOR * T // n_experts
    n_gate = 4 * H

    pad = (-(B + 1)) % 8
    cc = jnp.concatenate([c, c_ctx[None], jnp.zeros((pad, D), F32)], axis=0)
    mod = _ada(cc, w_ada[0], b_ada[0])
    sh1, sc1, g1, sh2, sc2, g2 = [mod[:B, None, i * D:(i + 1) * D] for i in range(6)]
    csh1, csc1 = [jnp.broadcast_to(mod[B, i * D:(i + 1) * D], (B, 1, D)) for i in range(2)]

    assert (4 * width + n_gate) % 8 == 0 and w_in.shape[2] == 8 * width + n_gate
    w_t = jnp.swapaxes(w_in, 1, 2)
    gb = jnp.pad(mlstm_gate_bias[0], (0, LANES - n_gate)).reshape(1, LANES)
    nw1 = norm1_w[0].reshape(1, D)
    cos, sin = _rope_tables(T)
    pm, gl, pr = _inproj(x, nw1, sh1, sc1, w_t, gb, cos, sin, groups=(0, 1, 2, 3))
    pmc, gc, prc = _inproj(ctx, nw1, csh1, csc1, w_t, gb,
                           jnp.ones((Tc, width), F32), jnp.zeros((Tc, width), F32), groups=(1, 2))

    ym = _mlstm(pm, pmc, _gateprep(gl, gc), mlstm_norm_w[0].reshape(H, 1, HEAD_DIM))
    dl = jnp.broadcast_to(jnp.swapaxes(ret_decay_logit[0], 0, 1)[:, :, None], (H, 2, LANES))
    yr = _ret(pr, prc, dl, ret_norm_w[0].reshape(H, 1, HEAD_DIM))

    wr_hi = w_router[0].astype(BF16)
    wr_lo = (w_router[0] - wr_hi.astype(F32)).astype(BF16)
    wrt = jnp.pad(jnp.concatenate([wr_hi, wr_lo], axis=1), ((0, 0), (0, LANES - 2 * n_experts)))
    xl, h2, aff_t = _outproj(ym, yr, x, w_out[0].astype(BF16), g1, norm2_w[0].reshape(1, D), sh2, sc2,
                             wrt, n_experts)

    pos, idx, gate, starts = _route(aff_t, cap)
    by_expert = lambda a: jnp.swapaxes(a.reshape(B, n_experts, cap), 0, 1).reshape(n_experts, B * cap)
    y = _ffn(by_expert(idx), by_expert(gate)[:, None, :], h2.reshape(B * T, D // LANES, LANES),
             w_gate[0], w_up[0], w_down[0])
    pos_t = jnp.swapaxes(pos.reshape(B, n_experts, T), 1, 2)
    pos_t = jnp.where(pos_t < 0, UNSELECTED_SLOT, pos_t)
    pos_hl = jnp.concatenate([pos_t >> 4, pos_t & 15], axis=-1).astype(BF16)
    return _combine(starts[:, :, 0].reshape(-1), y.reshape(n_experts, B, cap, D), pos_hl, xl, g2,
                    final_norm_w.reshape(1, D), cap)
```
